```python
import math
import numpy as np
import jax
import jax.numpy as jnp
from jax import lax

D_MODEL = 1024
BATCH = 2
SEQ = 8192
DEPTH = 4

HEAD_DIM = 64
ROT_DIM = HEAD_DIM // 4
ROPE_THETA = 500000.0
Q_BLOCK = 128
EPS = 1e-6
NEG = -1e30
BIG = 1e9

N_BRANCH = 4
BRANCH_HEADS = 4
BRANCH_W = BRANCH_HEADS * HEAD_DIM

DIL_PAIRS = ((128, 1), (512, 4), (2048, 16))
DIL_HEADS = len(DIL_PAIRS) * BRANCH_HEADS

MOBA_BLOCK = 256
MOBA_TOPK = 3

NSA_CMP_LEN = 32
NSA_CMP_STRIDE = 16
NSA_CMP_HID = HEAD_DIM
NSA_SEL_BLOCK = 64
NSA_SEL_TOPK = 16
NSA_WINDOW = 512
NSA_N_KV = 6

N_QK_NORMS = 8

COL_SIZES = (
    BRANCH_W, BRANCH_W, BRANCH_W, BRANCH_W,
    DIL_HEADS * HEAD_DIM, DIL_HEADS * HEAD_DIM, DIL_HEADS * HEAD_DIM, BRANCH_W,
    BRANCH_W, BRANCH_W, BRANCH_W, BRANCH_W,
    BRANCH_W, NSA_N_KV * HEAD_DIM, BRANCH_HEADS * 3, BRANCH_W,
    N_BRANCH * D_MODEL,
)
N_IN = sum(COL_SIZES)

kernel_name = 'hybrid_gated_sparse_mixers'


def _rms_norm(x, g):
    xf = x.astype(jnp.float32)
    y = xf * lax.rsqrt(jnp.mean(xf * xf, axis=-1, keepdims=True) + EPS)
    return (y * g.astype(jnp.float32)).astype(x.dtype)


def _rope_tables(S):
    pos = jnp.arange(S, dtype=jnp.float32)
    inv = ROPE_THETA ** (-jnp.arange(0, ROT_DIM, 2, dtype=jnp.float32) / ROT_DIM)
    ang = pos[:, None] * inv[None, :]
    return jnp.cos(ang), jnp.sin(ang)


def _partial_rope(x, cos, sin):
    half = ROT_DIM // 2
    x1 = x[..., :half].astype(jnp.float32)
    x2 = x[..., half:ROT_DIM].astype(jnp.float32)
    c = cos[:, None, :]
    s = sin[:, None, :]
    r = jnp.concatenate([x1 * c - x2 * s, x2 * c + x1 * s], axis=-1).astype(x.dtype)
    return jnp.concatenate([r, x[..., ROT_DIM:]], axis=-1)


def _heads(t, h):
    return t.reshape(t.shape[0], t.shape[1], h, HEAD_DIM)


def _unblock(t):
    nb, B, Q, H, Dh = t.shape
    return t.transpose(1, 0, 2, 3, 4).reshape(B, nb * Q, H, Dh)


def _stick_breaking_attn(q, k, v):
    B, S, H, Dh = q.shape
    scale = 1.0 / math.sqrt(Dh)
    kf = k.astype(jnp.float32)
    vf = v.astype(jnp.float32)
    kpos = jnp.arange(S)

    def block(i):
        q0 = i * Q_BLOCK
        qb = lax.dynamic_slice_in_dim(q, q0, Q_BLOCK, axis=1).astype(jnp.float32)
        qpos = q0 + jnp.arange(Q_BLOCK)
        z = jnp.einsum('bqhd,bshd->bhqs', qb, kf) * scale
        past = kpos[None, :] < qpos[:, None]
        log_1m = jnp.where(past, jax.nn.log_sigmoid(-z), 0.0)
        later = lax.cumsum(log_1m, axis=3, reverse=True) - log_1m
        w = jnp.where(past, jnp.exp(jax.nn.log_sigmoid(z) + later), 0.0)
        return jnp.einsum('bhqs,bshd->bqhd', w, vf)

    out = lax.map(block, jnp.arange(S // Q_BLOCK))
    return _unblock(out).astype(v.dtype)


def _banded_attn(q, k, v, max_dist):
    N, L, H, Dh = q.shape
    blk = Q_BLOCK
    nb = L // blk
    nprev = -(-max_dist // blk)
    pad = nprev * blk
    kp = jnp.pad(k, ((0, 0), (pad, 0), (0, 0), (0, 0))).reshape(N, nb + nprev, blk, H, Dh)
    vp = jnp.pad(v, ((0, 0), (pad, 0), (0, 0), (0, 0))).reshape(N, nb + nprev, blk, H, Dh)
    kb = jnp.concatenate([kp[:, p:p + nb] for p in range(nprev + 1)], axis=2)
    vb = jnp.concatenate([vp[:, p:p + nb] for p in range(nprev + 1)], axis=2)
    qb = q.reshape(N, nb, blk, H, Dh)
    z = jnp.einsum('nbqhd,nbkhd->nbhqk', qb.astype(jnp.float32), kb.astype(jnp.float32)) / math.sqrt(Dh)
    qi = jnp.arange(blk)
    ki = jnp.arange((nprev + 1) * blk)
    dist = pad + qi[:, None] - ki[None, :]
    kpos = jnp.arange(nb)[:, None] * blk - pad + ki[None, :]
    mask = ((dist >= 0) & (dist <= max_dist))[None, :, :] & (kpos >= 0)[:, None, :]
    z = jnp.where(mask[None, :, None], z, NEG)
    m = jnp.max(z, axis=-1, keepdims=True)
    e = jnp.exp(z - m)
    den = jnp.sum(e, axis=-1, keepdims=True)
    out = jnp.einsum('nbhqk,nbkhd->nbqhd', e / den, vb.astype(jnp.float32))
    lse = (m + jnp.log(den))[..., 0].transpose(0, 1, 3, 2).reshape(N, L, H)
    return out.reshape(N, L, H, Dh).astype(v.dtype), lse


def _dilated_attn(q, k, v):
    B, S, _, Dh = q.shape
    hg = BRANCH_HEADS
    outs, lses = [], []
    for g, (window, dil) in enumerate(DIL_PAIRS):
        unit = dil * Q_BLOCK
        Lp = -(-S // unit) * unit
        sub_len = Lp // dil

        def strided(t, g=g, dil=dil, Lp=Lp, sub_len=sub_len):
            t = jnp.pad(t[:, :, g * hg:(g + 1) * hg], ((0, 0), (0, Lp - S), (0, 0), (0, 0)))
            t = t.reshape(B, sub_len, dil, hg, Dh).transpose(0, 2, 1, 3, 4)
            return t.reshape(B * dil, sub_len, hg, Dh)

        o, lse = _banded_attn(strided(q), strided(k), strided(v), window // dil)
        o = o.reshape(B, dil, sub_len, hg, Dh).transpose(0, 2, 1, 3, 4).reshape(B, Lp, hg, Dh)[:, :S]
        lse = lse.reshape(B, dil, sub_len, hg).transpose(0, 2, 1, 3).reshape(B, Lp, hg)[:, :S]
        outs.append(o.astype(jnp.float32))
        lses.append(lse)
    alpha = jax.nn.softmax(jnp.stack(lses, axis=0), axis=0)
    out = jnp.sum(alpha[..., None] * jnp.stack(outs, axis=0), axis=0)
    return out.astype(v.dtype)


def _moba_attn(q, k, v):
    B, S, H, Dh = q.shape
    bs = MOBA_BLOCK
    scale = 1.0 / math.sqrt(Dh)
    Sp = -(-S // bs) * bs
    nblk = Sp // bs
    kp = jnp.pad(k, ((0, 0), (0, Sp - S), (0, 0), (0, 0))).astype(jnp.float32)
    vp = jnp.pad(v, ((0, 0), (0, Sp - S), (0, 0), (0, 0))).astype(jnp.float32)
    kblk = kp.reshape(B, nblk, bs, H, Dh).transpose(0, 3, 1, 2, 4)
    vblk = vp.reshape(B, nblk, bs, H, Dh).transpose(0, 3, 1, 2, 4)
    kmean = jnp.mean(kblk, axis=3)
    topk = min(MOBA_TOPK, max(nblk - 1, 1))
    bi = jnp.arange(B)[:, None, None, None]
    hi = jnp.arange(H)[None, :, None, None]
    blk_ids = jnp.arange(nblk)

    def block(i):
        q0 = i * Q_BLOCK
        qb = lax.dynamic_slice_in_dim(q, q0, Q_BLOCK, axis=1).astype(jnp.float32).transpose(0, 2, 1, 3)
        qpos = q0 + jnp.arange(Q_BLOCK)
        cur = q0 // bs
        gate = jnp.einsum('bhqd,bhnd->bhqn', qb, kmean)
        gate = jnp.where(blk_ids < cur, gate, NEG)
        _, idx = lax.top_k(gate, topk)
        valid = idx < cur
        ks = kblk[bi, hi, idx]
        vs = vblk[bi, hi, idx]
        z_sel = jnp.einsum('bhqd,bhqnkd->bhqnk', qb, ks) * scale
        z_sel = jnp.where(valid[..., None], z_sel, NEG).reshape(B, H, Q_BLOCK, topk * bs)
        k_own = lax.dynamic_slice_in_dim(kblk, cur, 1, axis=2)[:, :, 0]
        v_own = lax.dynamic_slice_in_dim(vblk, cur, 1, axis=2)[:, :, 0]
        own_pos = cur * bs + jnp.arange(bs)
        z_own = jnp.einsum('bhqd,bhkd->bhqk', qb, k_own) * scale
        z_own = jnp.where(own_pos[None, :] <= qpos[:, None], z_own, NEG)
        p = jax.nn.softmax(jnp.concatenate([z_sel, z_own], axis=-1), axis=-1)
        p_sel = p[..., :topk * bs].reshape(B, H, Q_BLOCK, topk, bs)
        out = (jnp.einsum('bhqnk,bhqnkd->bhqd', p_sel, vs)
               + jnp.einsum('bhqk,bhkd->bhqd', p[..., topk * bs:], v_own))
        return out.transpose(0, 2, 1, 3)

    out = lax.map(block, jnp.arange(S // Q_BLOCK))
    return _unblock(out).astype(v.dtype)


def _nsa_attn(q, kc_tok, vc_tok, ks, vs, kw, vw, gates, pe, w1, w2, g_kc):
    B, S, H, Dh = q.shape
    scale = 1.0 / math.sqrt(Dh)
    L, st, sb = NSA_CMP_LEN, NSA_CMP_STRIDE, NSA_SEL_BLOCK
    n_cmp = (S - L) // st + 1
    n_sel = S // sb
    top_n = min(NSA_SEL_TOPK, n_sel)
    tok_idx = np.arange(n_cmp)[:, None] * st + np.arange(L)[None, :]

    def compress(t, j):
        blocks = t[:, tok_idx] + pe[j]
        hid = jax.nn.silu(blocks.reshape(B, n_cmp, L * Dh) @ w1[j])
        return hid @ w2[j]

    kc = _rms_norm(compress(kc_tok, 0), g_kc).astype(jnp.float32)
    vc = compress(vc_tok, 1).astype(jnp.float32)
    c_end = jnp.arange(n_cmp) * st + (L - 1)
    c_start = np.arange(n_cmp) * st
    s_start = np.arange(n_sel) * sb
    overlap = jnp.asarray(((c_start[:, None] < s_start[None, :] + sb)
                           & (c_start[:, None] + L > s_start[None, :])).astype(np.float32))
    ksb = ks.astype(jnp.float32).reshape(B, n_sel, sb, Dh)
    vsb = vs.astype(jnp.float32).reshape(B, n_sel, sb, Dh)
    bi = jnp.arange(B)[:, None, None]
    sel_ids = jnp.arange(n_sel)

    def block(i):
        q0 = i * Q_BLOCK
        qb = lax.dynamic_slice_in_dim(q, q0, Q_BLOCK, axis=1).astype(jnp.float32)
        qpos = q0 + jnp.arange(Q_BLOCK)
        zc = jnp.einsum('bqhd,bcd->bhqc', qb, kc) * scale
        cmask = c_end[None, :] <= qpos[:, None]
        zc = jnp.where(cmask, zc, NEG)
        e = jnp.where(cmask, jnp.exp(zc - jnp.max(zc, axis=-1, keepdims=True)), 0.0)
        pc = e / jnp.maximum(jnp.sum(e, axis=-1, keepdims=True), 1.0)
        o_cmp = jnp.einsum('bhqc,bcd->bqhd', pc, vc)
        imp = jnp.einsum('bhqc,cn->bqn', pc, overlap)
        cur = (qpos // sb)[:, None]
        forced = (sel_ids == 0) | (sel_ids == cur) | (sel_ids == cur - 1)
        imp = jnp.where(forced, BIG, imp)
        imp = jnp.where(sel_ids > cur, NEG, imp)
        _, idx = lax.top_k(imp, top_n)
        kg = ksb[bi, idx]
        vg = vsb[bi, idx]
        tpos = idx[..., None] * sb + jnp.arange(sb)
        smask = tpos <= qpos[None, :, None, None]
        zs = jnp.einsum('bqhd,bqnkd->bhqnk', qb, kg) * scale
        zs = jnp.where(smask[:, None], zs, NEG)
        ps = jax.nn.softmax(zs.reshape(B, H, Q_BLOCK, top_n * sb), axis=-1).reshape(zs.shape)
        o_sel = jnp.einsum('bhqnk,bqnkd->bqhd', ps, vg)
        return o_cmp, o_sel

    o_cmp, o_sel = lax.map(block, jnp.arange(S // Q_BLOCK))
    o_cmp = _unblock(o_cmp)
    o_sel = _unblock(o_sel)
    o_win, _ = _banded_attn(q, jnp.broadcast_to(kw[:, :, None], q.shape),
                            jnp.broadcast_to(vw[:, :, None], q.shape), NSA_WINDOW - 1)
    g = gates.astype(jnp.float32)
    out = g[..., 0:1] * o_cmp + g[..., 1:2] * o_sel + g[..., 2:3] * o_win.astype(jnp.float32)
    return out.astype(q.dtype)


def _layer(x, norm_g, w_in, qk_g, cmp_pe, cmp_w1, cmp_w2, w_branch, w_out, cos, sin):
    B, S, _ = x.shape
    nh = BRANCH_HEADS
    xn = _rms_norm(x, norm_g)
    h = xn @ w_in
    (a_q, a_k, a_v, a_z, b_q, b_k, b_v, b_z, c_q, c_k, c_v, c_z,
     d_q, d_kv, d_g, d_z, m_g) = jnp.split(h, np.cumsum(COL_SIZES)[:-1].tolist(), axis=-1)

    def rope(t):
        return _partial_rope(t, cos, sin)

    y_a = _stick_breaking_attn(_heads(a_q, nh), _heads(a_k, nh), _heads(a_v, nh))
    y_b = _dilated_attn(rope(_rms_norm(_heads(b_q, DIL_HEADS), qk_g[0])),
                        rope(_rms_norm(_heads(b_k, DIL_HEADS), qk_g[1])),
                        _heads(b_v, DIL_HEADS))
    y_c = _moba_attn(rope(_rms_norm(_heads(c_q, nh), qk_g[2])),
                     rope(_rms_norm(_heads(c_k, nh), qk_g[3])),
                     _heads(c_v, nh))
    kv = d_kv.reshape(B, S, NSA_N_KV, HEAD_DIM)
    kc_tok = rope(kv[:, :, 0:1])[:, :, 0]
    ks = rope(_rms_norm(kv[:, :, 2:3], qk_g[6]))[:, :, 0]
    kw = rope(_rms_norm(kv[:, :, 4:5], qk_g[7]))[:, :, 0]
    y_d = _nsa_attn(rope(_rms_norm(_heads(d_q, nh), qk_g[4])), kc_tok, kv[:, :, 1], ks, kv[:, :, 3],
                    kw, kv[:, :, 5], jax.nn.sigmoid(d_g.reshape(B, S, nh, 3)),
                    cmp_pe, cmp_w1, cmp_w2, qk_g[5])

    merge = jax.nn.sigmoid(m_g.reshape(B, S, N_BRANCH, D_MODEL))
    ys = (y_a, y_b, y_c, y_d)
    zs = (a_z, b_z, c_z, d_z)
    merged = merge[:, :, 0] * ((ys[0].reshape(B, S, BRANCH_W) * jax.nn.silu(zs[0])) @ w_branch[0])
    for i in range(1, N_BRANCH):
        merged = merged + merge[:, :, i] * ((ys[i].reshape(B, S, BRANCH_W) * jax.nn.silu(zs[i])) @ w_branch[i])
    return x + merged @ w_out


def setup_inputs(seed: int = 0) -> dict:
    key = jax.random.key(seed)
    ks = jax.random.split(key, 9)
    f32 = jnp.float32
    x = jax.random.normal(ks[0], (BATCH, SEQ, D_MODEL), f32)
    norm_g = 1.0 + 0.05 * jax.random.normal(ks[1], (DEPTH, D_MODEL), f32)
    w_in = jax.random.normal(ks[2], (DEPTH, D_MODEL, N_IN), f32) * D_MODEL ** -0.5
    qk_g = 1.0 + 0.05 * jax.random.normal(ks[3], (DEPTH, N_QK_NORMS, HEAD_DIM), f32)
    cmp_pe = 0.1 * jax.random.normal(ks[4], (DEPTH, 2, NSA_CMP_LEN, HEAD_DIM), f32)
    cmp_w1 = jax.random.normal(ks[5], (DEPTH, 2, NSA_CMP_LEN * HEAD_DIM, NSA_CMP_HID), f32) * (NSA_CMP_LEN * HEAD_DIM) ** -0.5
    cmp_w2 = jax.random.normal(ks[6], (DEPTH, 2, NSA_CMP_HID, HEAD_DIM), f32) * NSA_CMP_HID ** -0.5
    w_branch = jax.random.normal(ks[7], (DEPTH, N_BRANCH, BRANCH_W, D_MODEL), f32) * BRANCH_W ** -0.5
    w_out = jax.random.normal(ks[8], (DEPTH, D_MODEL, D_MODEL), f32) * D_MODEL ** -0.5
    return {'x': x, 'norm_g': norm_g, 'w_in': w_in, 'qk_g': qk_g, 'cmp_pe': cmp_pe,
            'cmp_w1': cmp_w1, 'cmp_w2': cmp_w2, 'w_branch': w_branch, 'w_out': w_out}


def reference(x, norm_g, w_in, qk_g, cmp_pe, cmp_w1, cmp_w2, w_branch, w_out):
    cos, sin = _rope_tables(x.shape[1])
    for l in range(DEPTH):
        x = _layer(x, norm_g[l], w_in[l], qk_g[l], cmp_pe[l], cmp_w1[l], cmp_w2[l],
                   w_branch[l], w_out[l], cos, sin)
    return x
```

```python
import functools
import math

import numpy as np
import jax
import jax.numpy as jnp
from jax import lax
from jax.experimental import pallas as pl
from jax.experimental.pallas import tpu as pltpu

F32 = jnp.float32
BF16 = jnp.bfloat16

D_MODEL = 1024
DEPTH = 4
HEAD_DIM = 64
ROT_DIM = 16
ROPE_THETA = 500000.0
EPS = 1e-6
NEG = -1e30
BIG = 1e9
BRANCH_W = 256
DIL_PAIRS = ((128, 1), (512, 4), (2048, 16))
MOBA_BLOCK = 256
MOBA_TOPK = 3
NSA_CMP_LEN = 32
NSA_CMP_STRIDE = 16
NSA_SEL_BLOCK = 64
NSA_SEL_TOPK = 16
NSA_WINDOW = 512
COL_SIZES = (256, 256, 256, 256, 768, 768, 768, 256, 256, 256, 256, 256, 256, 384, 12, 256, 4096)
COL_OFF = tuple(int(v) for v in np.concatenate([[0], np.cumsum(COL_SIZES)]))

LANE = 128
CH = 256
VMEM_LIMIT = 56 * 1024 * 1024


def _cparams(sem):
    return pltpu.CompilerParams(dimension_semantics=sem, vmem_limit_bytes=VMEM_LIMIT)


def _dot(a, b):
    return jnp.dot(a, b, preferred_element_type=F32)


def _dot_nt(a, b):
    return lax.dot_general(a, b, (((1,), (1,)), ((), ())), preferred_element_type=F32)


def _split2(x):
    hi = x.astype(BF16)
    lo = (x - hi.astype(F32)).astype(BF16)
    return hi, lo


def _split3(x):
    hi = x.astype(BF16)
    r = x - hi.astype(F32)
    mid = r.astype(BF16)
    lo = (r - mid.astype(F32)).astype(BF16)
    return hi, mid, lo


def _proj_kernel(x_ref, ng_ref, wtok_ref, wT_ref, tokp_ref, gT_ref, rtok_ref, rT_ref, G_ref,
                 otok_ref, oT_ref, *, tok_kinds, T_kinds, tm):
    x = x_ref[...]
    ms = jnp.mean(x * x, axis=-1, keepdims=True)
    xn = (x * lax.rsqrt(ms + EPS) * ng_ref[...]).astype(BF16)

    for c, (has_norm, has_rope) in enumerate(tok_kinds):
        y = _dot(xn, wtok_ref[:, c * CH:(c + 1) * CH])
        prm = tokp_ref[c]
        if has_norm:
            ss = _dot((y * y).astype(BF16), G_ref[...])
            inv = lax.rsqrt(ss * (1.0 / HEAD_DIM) + EPS)
            y = y * jnp.where(prm[0:1, :] > 0.0, inv, 1.0)
        y = y * prm[2:3, :]
        if has_rope:
            halves = []
            for hf in range(CH // LANE):
                yh = y[:, hf * LANE:(hf + 1) * LANE]
                rf = prm[1:2, hf * LANE:(hf + 1) * LANE]
                cc = jnp.where(rf > 0.0, rtok_ref[0], 1.0)
                s1 = rtok_ref[1] * rf
                s2 = rtok_ref[2] * rf
                halves.append(yh * cc + pltpu.roll(yh, 8, 1) * s1 + pltpu.roll(yh, LANE - 8, 1) * s2)
            y = jnp.concatenate(halves, axis=1)
        otok_ref[:, c * CH:(c + 1) * CH] = y.astype(BF16)

    cosT = rT_ref[0]
    sinT = rT_ref[1]
    for c, heads in enumerate(T_kinds):
        y = _dot_nt(wT_ref[c * CH:(c + 1) * CH, :], xn)
        for h, (nrm, rope, scale, sigm) in enumerate(heads):
            r0 = c * CH + h * HEAD_DIM
            yh = y[h * HEAD_DIM:(h + 1) * HEAD_DIM, :]
            if nrm:
                msq = jnp.mean(yh * yh, axis=0, keepdims=True)
                yh = yh * lax.rsqrt(msq + EPS) * gT_ref[r0:r0 + HEAD_DIM, :]
            if rope:
                x1 = yh[0:8, :]
                x2 = yh[8:16, :]
                yh = jnp.concatenate([x1 * cosT - x2 * sinT, x2 * cosT + x1 * sinT, yh[16:, :]], axis=0)
            if scale != 1.0:
                yh = yh * scale
            if sigm:
                yh = jax.nn.sigmoid(yh)
            yb = yh.astype(BF16)
            for t in range(tm // LANE):
                oT_ref[t, r0:r0 + HEAD_DIM, :] = yb[:, t * LANE:(t + 1) * LANE]


def _proj_call(xv, ng, wtok, wT, tokp, gT, rtok, rT, G, *, tok_kinds, T_kinds, tm):
    B, L, wide = xv.shape
    dil = wide // D_MODEL
    ntok, nT = len(tok_kinds), len(T_kinds)
    kern = functools.partial(_proj_kernel, tok_kinds=tok_kinds, T_kinds=T_kinds, tm=tm)
    return pl.pallas_call(
        kern,
        grid=(B, dil, L // tm),
        in_specs=[
            pl.BlockSpec((None, tm, D_MODEL), lambda b, r, n: (b, n, r)),
            pl.BlockSpec((1, D_MODEL), lambda b, r, n: (0, 0)),
            pl.BlockSpec((D_MODEL, ntok * CH), lambda b, r, n: (0, 0)),
            pl.BlockSpec((nT * CH, D_MODEL), lambda b, r, n: (0, 0)),
            pl.BlockSpec((ntok, 8, CH), lambda b, r, n: (0, 0, 0)),
            pl.BlockSpec((nT * CH, 1), lambda b, r, n: (0, 0)),
            pl.BlockSpec((None, 3, tm, LANE), lambda b, r, n: (r, 0, n, 0)),
            pl.BlockSpec((None, 2, 8, tm), lambda b, r, n: (r, 0, 0, n)),
            pl.BlockSpec((CH, CH), lambda b, r, n: (0, 0)),
        ],
        out_specs=[
            pl.BlockSpec((None, None, tm, ntok * CH), lambda b, r, n: (b, r, n, 0)),
            pl.BlockSpec((None, None, tm // LANE, nT * CH, LANE), lambda b, r, n: (b, r, n, 0, 0)),
        ],
        out_shape=[
            jax.ShapeDtypeStruct((B, dil, L, ntok * CH), BF16),
            jax.ShapeDtypeStruct((B, dil, L // LANE, nT * CH, LANE), BF16),
        ],
        compiler_params=_cparams(("parallel", "parallel", "parallel")),
        name="proj",
    )(xv, ng, wtok, wT, tokp, gT, rtok, rT, G)


def _pair_masked_q(q_pair, h):
    rid = lax.broadcasted_iota(jnp.int32, q_pair.shape, 0)
    lo = (h % 2) * HEAD_DIM
    keep = jnp.where(rid >= lo, jnp.where(rid < lo + HEAD_DIM, 1.0, 0.0), 0.0).astype(BF16)
    return q_pair * keep


def _lane_tiles(ref, t0, nt, r0, nr):
    return jnp.concatenate([ref[t0 + t, r0:r0 + nr, :] for t in range(nt)], axis=1)


def _online(s, m, l, acc, vb):
    m_new = jnp.maximum(m, jnp.max(s, axis=0, keepdims=True))
    alpha = jnp.exp(m - m_new)
    p = jnp.exp(s - m_new)
    l = alpha * l + jnp.sum(p, axis=0, keepdims=True)
    acc = alpha * acc + _dot(vb, p.astype(BF16))
    return m_new, l, acc


def _sb_kernel(qT_ref, k_ref, vT_ref, U_ref, o_ref, *, tq):
    i = pl.program_id(1)
    nt = tq // LANE
    row = lax.broadcasted_iota(jnp.int32, (tq, tq), 0)
    col = lax.broadcasted_iota(jnp.int32, (tq, tq), 1)
    past = row < col
    U = U_ref[...]
    outs = []
    for h in range(4):
        p = h // 2
        qm = _pair_masked_q(_lane_tiles(qT_ref, 0, nt, p * LANE, LANE), h)

        def tile(j, carry, acc, masked, p=p, h=h, qm=qm):
            kb = k_ref[pl.ds(pl.multiple_of(j * tq, tq), tq), p * LANE:(p + 1) * LANE]
            s = _dot(kb, qm)
            sp = jnp.maximum(s, 0.0) + jnp.log(1.0 + jnp.exp(-jnp.abs(s)))
            lg = -sp
            if masked:
                lg = jnp.where(past, lg, 0.0)
            hi, lo = _split2(lg)
            later = _dot(U, hi) + _dot(U, lo) + carry
            w = jnp.exp(s - sp + later)
            if masked:
                w = jnp.where(past, w, 0.0)
            vb = _lane_tiles(vT_ref, j * nt, nt, h * HEAD_DIM, HEAD_DIM)
            acc = acc + _dot(vb, w.astype(BF16))
            carry = carry + jnp.sum(lg, axis=0, keepdims=True)
            return carry, acc

        carry0 = jnp.zeros((1, tq), F32)
        acc0 = jnp.zeros((HEAD_DIM, tq), F32)
        carry, acc = tile(i, carry0, acc0, True)

        def body(t, ca):
            return tile(i - 1 - t, ca[0], ca[1], False)

        carry, acc = lax.fori_loop(0, i, body, (carry, acc))
        outs.append(acc)
    o_ref[...] = jnp.concatenate(outs, axis=0).T.astype(o_ref.dtype)


def _sb_call(oT, otok, U, *, q_chunk, k_chunk, v_chunk, tq=256):
    B, _, nlt, _, _ = oT.shape
    S = nlt * LANE
    kern = functools.partial(_sb_kernel, tq=tq)
    return pl.pallas_call(
        kern,
        grid=(B, S // tq),
        in_specs=[
            pl.BlockSpec((None, None, tq // LANE, CH, LANE), lambda b, i: (b, 0, i, q_chunk, 0)),
            pl.BlockSpec((None, None, S, CH), lambda b, i: (b, 0, 0, k_chunk)),
            pl.BlockSpec((None, None, nlt, CH, LANE), lambda b, i: (b, 0, 0, v_chunk, 0)),
            pl.BlockSpec((tq, tq), lambda b, i: (0, 0)),
        ],
        out_specs=pl.BlockSpec((None, tq, BRANCH_W), lambda b, i: (b, i, 0)),
        out_shape=jax.ShapeDtypeStruct((B, S, BRANCH_W), BF16),
        compiler_params=_cparams(("parallel", "arbitrary")),
        name="stick_breaking",
    )(oT, otok, oT, U)


def _band_kernel(qT_ref, k_ref, vT_ref, o_ref, lse_ref, *, tqb, max_dist):
    n = pl.program_id(2)
    n_prev = -(-max_dist // LANE)
    nk = n_prev + 1
    nsub = tqb // LANE
    row = lax.broadcasted_iota(jnp.int32, (nk * LANE, LANE), 0)
    col = lax.broadcasted_iota(jnp.int32, (nk * LANE, LANE), 1)
    for u in range(nsub):
        qt = n * nsub + u
        kt0 = jnp.maximum(qt - n_prev, 0)
        dist = (qt - kt0) * LANE + col - row
        kwin = k_ref[pl.ds(pl.multiple_of(kt0 * LANE, LANE), nk * LANE), :]
        o_parts, l_parts = [], []
        for h in range(4):
            p = h // 2
            qm = _pair_masked_q(qT_ref[u, p * LANE:(p + 1) * LANE, :], h)
            s = _dot(kwin[:, p * LANE:(p + 1) * LANE], qm)
            s = jnp.where(dist >= 0, jnp.where(dist <= max_dist, s, NEG), NEG)
            m = jnp.max(s, axis=0, keepdims=True)
            e = jnp.exp(s - m)
            den = jnp.sum(e, axis=0, keepdims=True)
            vb = _lane_tiles(vT_ref, kt0, nk, h * HEAD_DIM, HEAD_DIM)
            o_parts.append(_dot(vb, e.astype(BF16)) / den)
            l_parts.append(jnp.broadcast_to(m + jnp.log(den), (HEAD_DIM, LANE)))
        o_ref[u * LANE:(u + 1) * LANE, :] = jnp.concatenate(o_parts, axis=0).T.astype(o_ref.dtype)
        lse_ref[u * LANE:(u + 1) * LANE, :] = jnp.concatenate(l_parts, axis=0).T


def _band_call(oT, otok, *, q_chunk, k_chunk, v_chunk, max_dist):
    B, dil, nlt, _, _ = oT.shape
    L = nlt * LANE
    tqb = min(512, L)
    kern = functools.partial(_band_kernel, tqb=tqb, max_dist=max_dist)
    return pl.pallas_call(
        kern,
        grid=(B, dil, L // tqb),
        in_specs=[
            pl.BlockSpec((None, None, tqb // LANE, CH, LANE), lambda b, r, n: (b, r, n, q_chunk, 0)),
            pl.BlockSpec((None, None, L, CH), lambda b, r, n: (b, r, 0, k_chunk)),
            pl.BlockSpec((None, None, nlt, CH, LANE), lambda b, r, n: (b, r, 0, v_chunk, 0)),
        ],
        out_specs=[
            pl.BlockSpec((None, tqb, BRANCH_W), lambda b, r, n: (b, n, r)),
            pl.BlockSpec((None, tqb, BRANCH_W), lambda b, r, n: (b, n, r)),
        ],
        out_shape=[
            jax.ShapeDtypeStruct((B, L, dil * BRANCH_W), BF16),
            jax.ShapeDtypeStruct((B, L, dil * BRANCH_W), F32),
        ],
        compiler_params=_cparams(("parallel", "parallel", "parallel")),
        name="banded",
    )(oT, otok, oT)


def _moba_kernel(qT_ref, k_ref, vT_ref, A_ref, o_ref, kmean_ref, sel_ref, *, tq):
    i = pl.program_id(1)
    nt = tq // LANE
    nblk = A_ref.shape[0]

    @pl.when(i == 0)
    def _():
        kmean_ref[...] = _dot(A_ref[...], k_ref[...])

    blk = lax.broadcasted_iota(jnp.int32, (nblk, tq), 0)
    row = lax.broadcasted_iota(jnp.int32, (tq, tq), 0)
    col = lax.broadcasted_iota(jnp.int32, (tq, tq), 1)
    outs = []
    for h in range(4):
        p = h // 2
        qm = _pair_masked_q(_lane_tiles(qT_ref, 0, nt, p * LANE, LANE), h)
        k1, k2, k3 = _split3(kmean_ref[:, p * LANE:(p + 1) * LANE])
        gate = _dot(k1, qm) + _dot(k2, qm) + _dot(k3, qm)
        g = jnp.where(blk < i, gate, NEG)
        sel = jnp.zeros((nblk, tq), F32)
        for _r in range(MOBA_TOPK):
            mx = jnp.max(g, axis=0, keepdims=True)
            idx = jnp.min(jnp.where(g == mx, blk, nblk), axis=0, keepdims=True)
            hit = blk == idx
            sel = jnp.where(hit, 1.0, sel)
            g = jnp.where(hit, -jnp.inf, g)
        sel_ref[h] = jnp.where(blk < i, sel, 0.0)

        kb = k_ref[pl.ds(pl.multiple_of(i * tq, tq), tq), p * LANE:(p + 1) * LANE]
        s = jnp.where(row <= col, _dot(kb, qm), NEG)
        m = jnp.max(s, axis=0, keepdims=True)
        pm = jnp.exp(s - m)
        l = jnp.sum(pm, axis=0, keepdims=True)
        acc = _dot(_lane_tiles(vT_ref, i * nt, nt, h * HEAD_DIM, HEAD_DIM), pm.astype(BF16))

        def body(j, mla, p=p, h=h, qm=qm):
            kbj = k_ref[pl.ds(pl.multiple_of(j * tq, tq), tq), p * LANE:(p + 1) * LANE]
            srow = sel_ref[h, pl.ds(j, 1), :]
            sj = jnp.where(srow > 0.0, _dot(kbj, qm), NEG)
            vb = _lane_tiles(vT_ref, j * nt, nt, h * HEAD_DIM, HEAD_DIM)
            return _online(sj, mla[0], mla[1], mla[2], vb)

        m, l, acc = lax.fori_loop(0, i, body, (m, l, acc))
        outs.append(acc / l)
    o_ref[...] = jnp.concatenate(outs, axis=0).T.astype(o_ref.dtype)


def _moba_call(oT, otok, A, *, q_chunk, k_chunk, v_chunk):
    B, _, nlt, _, _ = oT.shape
    S = nlt * LANE
    tq = MOBA_BLOCK
    nblk = S // MOBA_BLOCK
    kern = functools.partial(_moba_kernel, tq=tq)
    return pl.pallas_call(
        kern,
        grid=(B, S // tq),
        in_specs=[
            pl.BlockSpec((None, None, tq // LANE, CH, LANE), lambda b, i: (b, 0, i, q_chunk, 0)),
            pl.BlockSpec((None, None, S, CH), lambda b, i: (b, 0, 0, k_chunk)),
            pl.BlockSpec((None, None, nlt, CH, LANE), lambda b, i: (b, 0, 0, v_chunk, 0)),
            pl.BlockSpec((nblk, S), lambda b, i: (0, 0)),
        ],
        out_specs=pl.BlockSpec((None, tq, BRANCH_W), lambda b, i: (b, i, 0)),
        out_shape=jax.ShapeDtypeStruct((B, S, BRANCH_W), BF16),
        scratch_shapes=[pltpu.VMEM((nblk, CH), F32), pltpu.VMEM((4, nblk, tq), F32)],
        compiler_params=_cparams(("parallel", "arbitrary")),
        name="moba",
    )(oT, otok, oT, A)


def _compress_kernel(x_ref, w1_ref, pe_ref, w1f_ref, w2_ref, gk_ref, G_ref, o_ref, oT_ref, acc_ref):
    l = pl.program_id(1)

    @pl.when(l == 0)
    def _():
        acc_ref[...] = jnp.zeros_like(acc_ref)

    x = x_ref[...]
    w_hi, w_lo = _split2(w1_ref[...])
    acc_ref[...] += _dot(x, w_hi) + _dot(x, w_lo)

    @pl.when(l == pl.num_programs(1) - 1)
    def _():
        nc = acc_ref.shape[0]
        r = acc_ref[...]
        bias = jnp.dot(pe_ref[...], w1f_ref[...], preferred_element_type=F32,
                       precision=lax.Precision.HIGHEST)
        p1 = jnp.concatenate([r[:, 0:64], r[:, 128:192]], axis=1)
        p2 = jnp.concatenate([r[:, 64:128], r[:, 192:256]], axis=1)
        hid = p1 + pltpu.roll(p2, nc - 1, 0) + bias[0:1, :]
        hid = hid * jax.nn.sigmoid(hid)
        comp = jnp.dot(hid, w2_ref[...], preferred_element_type=F32, precision=lax.Precision.HIGHEST)
        ss = _dot((comp * comp).astype(BF16), G_ref[...])
        lane = lax.broadcasted_iota(jnp.int32, comp.shape, 1)
        inv = jnp.where(lane < HEAD_DIM, lax.rsqrt(ss * (1.0 / HEAD_DIM) + EPS), 1.0)
        comp = comp * inv * gk_ref[...]
        rowi = lax.broadcasted_iota(jnp.int32, comp.shape, 0)
        comp = jnp.where(rowi < nc - 1, comp, 0.0)
        o_ref[...] = comp.astype(o_ref.dtype)
        oT_ref[...] = comp.T.astype(oT_ref.dtype)


def _compress_call(otok, w1blk, pe2, w1f, w2blk, gk, G128, *, kv_chunk):
    B, _, S, width = otok.shape
    nc = S // NSA_CMP_STRIDE
    xv = otok.reshape(B, nc, NSA_CMP_STRIDE * width)
    cpb = width // LANE
    kvb = kv_chunk * (CH // LANE)
    return pl.pallas_call(
        _compress_kernel,
        grid=(B, NSA_CMP_STRIDE),
        in_specs=[
            pl.BlockSpec((None, nc, LANE), lambda b, l: (b, 0, l * cpb + kvb)),
            pl.BlockSpec((None, LANE, CH), lambda b, l: (l, 0, 0)),
            pl.BlockSpec((8, 2 * NSA_CMP_LEN * HEAD_DIM), lambda b, l: (0, 0)),
            pl.BlockSpec((2 * NSA_CMP_LEN * HEAD_DIM, LANE), lambda b, l: (0, 0)),
            pl.BlockSpec((LANE, LANE), lambda b, l: (0, 0)),
            pl.BlockSpec((1, LANE), lambda b, l: (0, 0)),
            pl.BlockSpec((LANE, LANE), lambda b, l: (0, 0)),
        ],
        out_specs=[
            pl.BlockSpec((None, nc, LANE), lambda b, l: (b, 0, 0)),
            pl.BlockSpec((None, LANE, nc), lambda b, l: (b, 0, 0)),
        ],
        out_shape=[
            jax.ShapeDtypeStruct((B, nc, LANE), BF16),
            jax.ShapeDtypeStruct((B, LANE, nc), BF16),
        ],
        scratch_shapes=[pltpu.VMEM((nc, CH), F32)],
        compiler_params=_cparams(("parallel", "arbitrary")),
        name="nsa_compress",
    )(xv, w1blk, pe2, w1f, w2blk, gk, G128)


def _nsa_kernel(qT_ref, k_ref, vT_ref, kc_ref, kcT_ref, ov_ref, o_ref, sel_ref, *, tq):
    i = pl.program_id(1)
    nt = tq // LANE
    nq = 4 * tq
    nc = kc_ref.shape[0]
    nsel = ov_ref.shape[0]
    spb = tq // NSA_SEL_BLOCK

    zeros = jnp.zeros((HEAD_DIM, tq), BF16)
    q_heads = [_lane_tiles(qT_ref, 0, nt, h * HEAD_DIM, HEAD_DIM) for h in range(4)]
    qm_lo = jnp.concatenate([jnp.concatenate([q, zeros], axis=0) for q in q_heads], axis=1)
    qm_hi = jnp.concatenate([jnp.concatenate([zeros, q], axis=0) for q in q_heads], axis=1)
    qpos = i * tq + lax.broadcasted_iota(jnp.int32, (1, tq), 1)
    qpos4 = jnp.concatenate([qpos] * 4, axis=1)

    zc = _dot(kc_ref[...], qm_lo)
    c_end = lax.broadcasted_iota(jnp.int32, (nc, nq), 0) * NSA_CMP_STRIDE + (NSA_CMP_LEN - 1)
    cmask = c_end <= qpos4
    zc = jnp.where(cmask, zc, NEG)
    e = jnp.where(cmask, jnp.exp(zc - jnp.max(zc, axis=0, keepdims=True)), 0.0)
    pc = e / jnp.maximum(jnp.sum(e, axis=0, keepdims=True), 1.0)
    o_cmp = _dot(kcT_ref[HEAD_DIM:2 * HEAD_DIM, :], pc.astype(BF16))

    psum = pc[:, 0:tq] + pc[:, tq:2 * tq] + pc[:, 2 * tq:3 * tq] + pc[:, 3 * tq:4 * tq]
    p_hi, p_lo = _split2(psum)
    imp = _dot(ov_ref[...], p_hi) + _dot(ov_ref[...], p_lo)
    nid = lax.broadcasted_iota(jnp.int32, (nsel, tq), 0)
    cur = qpos // NSA_SEL_BLOCK
    imp = jnp.where(nid == 0, BIG, imp)
    imp = jnp.where(nid == cur, BIG, imp)
    imp = jnp.where(nid == cur - 1, BIG, imp)
    imp = jnp.where(nid > cur, NEG, imp)
    sel = jnp.zeros((nsel, tq), F32)
    for _r in range(min(NSA_SEL_TOPK, nsel)):
        mx = jnp.max(imp, axis=0, keepdims=True)
        idx = jnp.min(jnp.where(imp == mx, nid, nsel), axis=0, keepdims=True)
        hit = nid == idx
        sel = jnp.where(hit, 1.0, sel)
        imp = jnp.where(hit, -jnp.inf, imp)
    sel = jnp.where(nid <= cur, sel, 0.0)
    sel_ref[...] = jnp.concatenate([sel] * 4, axis=1)

    def sel_scores(j):
        kb = k_ref[pl.ds(pl.multiple_of(j * tq, tq), tq), LANE:2 * LANE]
        s = _dot(kb, qm_lo)
        parts = []
        for u in range(spb):
            srow = sel_ref[pl.ds(j * spb + u, 1), :]
            parts.append(jnp.where(srow > 0.0, s[u * NSA_SEL_BLOCK:(u + 1) * NSA_SEL_BLOCK, :], NEG))
        return jnp.concatenate(parts, axis=0)

    kpos_d = i * tq + lax.broadcasted_iota(jnp.int32, (tq, nq), 0)
    s = jnp.where(kpos_d <= qpos4, sel_scores(i), NEG)
    m = jnp.max(s, axis=0, keepdims=True)
    pm = jnp.exp(s - m)
    l = jnp.sum(pm, axis=0, keepdims=True)
    acc = _dot(_lane_tiles(vT_ref, i * nt, nt, 0, HEAD_DIM), pm.astype(BF16))

    def body(j, mla):
        vb = _lane_tiles(vT_ref, j * nt, nt, 0, HEAD_DIM)
        return _online(sel_scores(j), mla[0], mla[1], mla[2], vb)

    m, l, acc = lax.fori_loop(0, i, body, (m, l, acc))
    o_sel = acc / l

    n_prev = -(-(NSA_WINDOW - 1) // tq)
    nk = n_prev + 1
    kt0 = jnp.maximum(i - n_prev, 0)
    kw = k_ref[pl.ds(pl.multiple_of(kt0 * tq, tq), nk * tq), LANE:2 * LANE]
    sw = _dot(kw, qm_hi)
    dist = qpos4 - (kt0 * tq + lax.broadcasted_iota(jnp.int32, (nk * tq, nq), 0))
    sw = jnp.where(dist >= 0, jnp.where(dist <= NSA_WINDOW - 1, sw, NEG), NEG)
    mw = jnp.max(sw, axis=0, keepdims=True)
    ew = jnp.exp(sw - mw)
    dw = jnp.sum(ew, axis=0, keepdims=True)
    vwin = _lane_tiles(vT_ref, kt0 * nt, nk * nt, HEAD_DIM, HEAD_DIM)
    o_win = _dot(vwin, ew.astype(BF16)) / dw

    gates = _lane_tiles(vT_ref, i * nt, nt, 2 * HEAD_DIM, 16).astype(F32)
    outs = []
    for h in range(4):
        sl = slice(h * tq, (h + 1) * tq)
        outs.append(gates[3 * h:3 * h + 1, :] * o_cmp[:, sl]
                    + gates[3 * h + 1:3 * h + 2, :] * o_sel[:, sl]
                    + gates[3 * h + 2:3 * h + 3, :] * o_win[:, sl])
    o_ref[...] = jnp.concatenate(outs, axis=0).T.astype(o_ref.dtype)


def _nsa_call(oT, otok, kc, kcT, ovT, *, q_chunk, kv_chunk, v_chunk):
    B, _, nlt, _, _ = oT.shape
    S = nlt * LANE
    tq = 256
    nc = S // NSA_CMP_STRIDE
    nsel = S // NSA_SEL_BLOCK
    kern = functools.partial(_nsa_kernel, tq=tq)
    return pl.pallas_call(
        kern,
        grid=(B, S // tq),
        in_specs=[
            pl.BlockSpec((None, None, tq // LANE, CH, LANE), lambda b, i: (b, 0, i, q_chunk, 0)),
            pl.BlockSpec((None, None, S, CH), lambda b, i: (b, 0, 0, kv_chunk)),
            pl.BlockSpec((None, None, nlt, CH, LANE), lambda b, i: (b, 0, 0, v_chunk, 0)),
            pl.BlockSpec((None, nc, LANE), lambda b, i: (b, 0, 0)),
            pl.BlockSpec((None, LANE, nc), lambda b, i: (b, 0, 0)),
            pl.BlockSpec((nsel, nc), lambda b, i: (0, 0)),
        ],
        out_specs=pl.BlockSpec((None, tq, BRANCH_W), lambda b, i: (b, i, 0)),
        out_shape=jax.ShapeDtypeStruct((B, S, BRANCH_W), BF16),
        scratch_shapes=[pltpu.VMEM((nsel, 4 * tq), F32)],
        compiler_params=_cparams(("parallel", "arbitrary")),
        name="nsa",
    )(oT, otok, oT, kc, kcT, ovT)


def _epi_kernel(x_ref, ng_ref, ya_ref, yc_ref, yd_ref, ob0_ref, ob1_ref, ob2_ref, l0_ref, l1_ref, l2_ref,
                za_ref, zb_ref, zc_ref, zd_ref, wmg_ref, wbr_ref, wout_ref, o_ref):
    x = x_ref[...]
    ms = jnp.mean(x * x, axis=-1, keepdims=True)
    xn = (x * lax.rsqrt(ms + EPS) * ng_ref[...]).astype(BF16)

    l0, l1, l2 = l0_ref[...], l1_ref[...], l2_ref[...]
    mx = jnp.maximum(jnp.maximum(l0, l1), l2)
    e0, e1, e2 = jnp.exp(l0 - mx), jnp.exp(l1 - mx), jnp.exp(l2 - mx)
    yb = (e0 * ob0_ref[...].astype(F32) + e1 * ob1_ref[...].astype(F32)
          + e2 * ob2_ref[...].astype(F32)) / (e0 + e1 + e2)

    ys = (ya_ref[...].astype(F32), yb, yc_ref[...].astype(F32), yd_ref[...].astype(F32))
    zs = (za_ref, zb_ref, zc_ref, zd_ref)
    merged = None
    for i in range(4):
        z = zs[i][...].astype(F32)
        gated = (ys[i] * (z * jax.nn.sigmoid(z))).astype(BF16)
        br = _dot(gated, wbr_ref[i])
        mg = _dot(xn, wmg_ref[:, i * D_MODEL:(i + 1) * D_MODEL])
        term = jax.nn.sigmoid(mg) * br
        merged = term if merged is None else merged + term
    o_ref[...] = x + _dot(merged.astype(BF16), wout_ref[...])


def _epi_call(x2, ng, ya, yc, yd, obs, lses, otok2, wmg, wbr, wout, *, z_chunks, tm=512):
    T = x2.shape[0]
    row = lambda i: (i, 0)
    full2 = lambda i: (0, 0)
    yspec = pl.BlockSpec((tm, BRANCH_W), row)
    zspecs = [pl.BlockSpec((tm, CH), (lambda i, c=c: (i, c))) for c in z_chunks]
    return pl.pallas_call(
        _epi_kernel,
        grid=(T // tm,),
        in_specs=[pl.BlockSpec((tm, D_MODEL), row), pl.BlockSpec((1, D_MODEL), full2)]
        + [yspec] * 3 + [yspec] * 3 + [yspec] * 3 + zspecs
        + [pl.BlockSpec((D_MODEL, 4 * D_MODEL), full2),
           pl.BlockSpec((4, BRANCH_W, D_MODEL), lambda i: (0, 0, 0)),
           pl.BlockSpec((D_MODEL, D_MODEL), full2)],
        out_specs=pl.BlockSpec((tm, D_MODEL), row),
        out_shape=jax.ShapeDtypeStruct((T, D_MODEL), F32),
        compiler_params=_cparams(("parallel",)),
        name="epilogue",
    )(x2, ng, ya, yc, yd, *obs, *lses, otok2, otok2, otok2, otok2, wmg, wbr, wout)


TOK_AK, TOK_AZ, TOK_BK, TOK_BZ, TOK_CK, TOK_CZ, TOK_DKV, TOK_DZ = range(8)
T_AQ, T_AV, T_BQ, T_BV, T_CQ, T_CV, T_DQ, T_DX = range(8)
SCALE = 1.0 / math.sqrt(HEAD_DIM)

_PLAIN = (False, False, 1.0, False)
_MAIN_TOK_KINDS = ((False, False), (False, False), (True, True), (False, False),
                   (True, True), (False, False), (True, True), (False, False))
_MAIN_T_KINDS = (
    ((False, False, SCALE, False),) * 4,
    (_PLAIN,) * 4,
    ((True, True, SCALE, False),) * 4,
    (_PLAIN,) * 4,
    ((True, True, SCALE, False),) * 4,
    (_PLAIN,) * 4,
    ((True, True, SCALE, False),) * 4,
    (_PLAIN, _PLAIN, (False, False, 1.0, True), _PLAIN),
)
_DIL_TOK_KINDS = ((True, True),)
_DIL_T_KINDS = (((True, True, SCALE, False),) * 4, (_PLAIN,) * 4)


def _rope_tables(S, dil):
    L = S // dil
    pos = (jnp.arange(dil, dtype=jnp.int32)[:, None] + dil * jnp.arange(L, dtype=jnp.int32)[None, :]).astype(F32)
    inv = ROPE_THETA ** (-jnp.arange(0, ROT_DIM, 2, dtype=F32) / ROT_DIM)
    ang = pos[:, :, None] * inv[None, None, :]
    cos, sin = jnp.cos(ang), jnp.sin(ang)
    one = jnp.ones((dil, L, HEAD_DIM - ROT_DIM), F32)
    zero8 = jnp.zeros((dil, L, 8), F32)
    zero = jnp.zeros_like(one)
    c_head = jnp.concatenate([cos, cos, one], axis=-1)
    s1_head = jnp.concatenate([zero8, sin, zero], axis=-1)
    s2_head = jnp.concatenate([-sin, zero8, zero], axis=-1)
    rtok = jnp.stack([jnp.tile(t, (1, 1, LANE // HEAD_DIM)) for t in (c_head, s1_head, s2_head)], axis=1)
    rT = jnp.stack([cos.transpose(0, 2, 1), sin.transpose(0, 2, 1)], axis=1)
    return rtok, rT


def _tok_params(rows):
    out = []
    for nf, rf, gains in rows:
        nrow = jnp.concatenate([jnp.full((HEAD_DIM,), float(f), F32) for f in nf])
        rrow = jnp.concatenate([jnp.full((HEAD_DIM,), float(f), F32) for f in rf])
        grow = jnp.concatenate([g.astype(F32) for g in gains])
        out.append(jnp.concatenate([jnp.stack([nrow, rrow, grow]), jnp.zeros((5, CH), F32)], axis=0))
    return jnp.stack(out)


def kernel(x, norm_g, w_in, qk_g, cmp_pe, cmp_w1, cmp_w2, w_branch, w_out):
    B, S, _ = x.shape
    T = B * S
    o = COL_OFF
    ones = jnp.ones((HEAD_DIM,), F32)

    r = np.arange(CH)
    G = jnp.asarray((r[:, None] // HEAD_DIM == r[None, :] // HEAD_DIM).astype(np.float32), BF16)
    G128 = G[:LANE, :LANE]
    t = np.arange(256)
    U = jnp.asarray((t[None, :] > t[:, None]).astype(np.float32), BF16)
    nblk = S // MOBA_BLOCK
    A = jnp.asarray(np.repeat(np.eye(nblk, dtype=np.float32), MOBA_BLOCK, axis=1) / MOBA_BLOCK, BF16)
    nc, nsel = S // NSA_CMP_STRIDE, S // NSA_SEL_BLOCK
    cs = np.arange(nc) * NSA_CMP_STRIDE
    ss = np.arange(nsel) * NSA_SEL_BLOCK
    ov = ((cs[None, :] < ss[:, None] + NSA_SEL_BLOCK) & (cs[None, :] + NSA_CMP_LEN > ss[:, None]))
    ov[:, nc - 1] = False
    ovT = jnp.asarray(ov.astype(np.float32), BF16)
    ropes = {dil: _rope_tables(S, dil) for _, dil in DIL_PAIRS}

    for l in range(DEPTH):
        W = w_in[l]
        col = lambda a, b: W[:, a:b]
        dkv = o[13]
        wtok = jnp.concatenate([
            col(o[1], o[2]), col(o[3], o[4]), col(o[5], o[5] + CH), col(o[7], o[8]),
            col(o[9], o[10]), col(o[11], o[12]),
            col(dkv, dkv + 64), col(dkv + 64, dkv + 128), col(dkv + 128, dkv + 192), col(dkv + 256, dkv + 320),
            col(o[15], o[16])], axis=1).astype(BF16)
        wT = jnp.concatenate([
            col(o[0], o[1]), col(o[2], o[3]), col(o[4], o[4] + CH), col(o[6], o[6] + CH),
            col(o[8], o[9]), col(o[10], o[11]), col(o[12], o[13]),
            col(dkv + 192, dkv + 256), col(dkv + 320, dkv + 384), col(o[14], o[15]),
            jnp.zeros((D_MODEL, CH - 140), F32)], axis=1).T.astype(BF16)
        g = qk_g[l]
        tokp = _tok_params([
            ((0,) * 4, (0,) * 4, (ones,) * 4), ((0,) * 4, (0,) * 4, (ones,) * 4),
            ((1,) * 4, (1,) * 4, (g[1],) * 4), ((0,) * 4, (0,) * 4, (ones,) * 4),
            ((1,) * 4, (1,) * 4, (g[3],) * 4), ((0,) * 4, (0,) * 4, (ones,) * 4),
            ((0, 0, 1, 1), (1, 0, 1, 1), (ones, ones, g[6], g[7])), ((0,) * 4, (0,) * 4, (ones,) * 4)])
        gT = jnp.concatenate([jnp.tile(ones, 8), jnp.tile(g[0], 4), jnp.tile(ones, 4), jnp.tile(g[2], 4),
                              jnp.tile(ones, 4), jnp.tile(g[4], 4), jnp.tile(ones, 4)])[:, None]
        ng = norm_g[l][None, :]
        rtok, rT = ropes[1]
        otok, oT = _proj_call(x, ng, wtok, wT, tokp, gT, rtok, rT, G,
                              tok_kinds=_MAIN_TOK_KINDS, T_kinds=_MAIN_T_KINDS, tm=512)

        ya = _sb_call(oT, otok, U, q_chunk=T_AQ, k_chunk=TOK_AK, v_chunk=T_AV)

        obs, lses = [], []
        for gi, (window, dil) in enumerate(DIL_PAIRS):
            if dil == 1:
                ob, lse = _band_call(oT, otok, q_chunk=T_BQ, k_chunk=TOK_BK, v_chunk=T_BV, max_dist=window // dil)
            else:
                wtok_g = col(o[5] + gi * CH, o[5] + (gi + 1) * CH).astype(BF16)
                wT_g = jnp.concatenate([col(o[4] + gi * CH, o[4] + (gi + 1) * CH),
                                        col(o[6] + gi * CH, o[6] + (gi + 1) * CH)], axis=1).T.astype(BF16)
                tokp_g = _tok_params([((1,) * 4, (1,) * 4, (g[1],) * 4)])
                gT_g = jnp.concatenate([jnp.tile(g[0], 4), jnp.tile(ones, 4)])[:, None]
                rtok_g, rT_g = ropes[dil]
                L = S // dil
                otok_g, oT_g = _proj_call(x.reshape(B, L, dil * D_MODEL), ng, wtok_g, wT_g, tokp_g, gT_g,
                                          rtok_g, rT_g, G, tok_kinds=_DIL_TOK_KINDS, T_kinds=_DIL_T_KINDS,
                                          tm=min(512, L))
                ob, lse = _band_call(oT_g, otok_g, q_chunk=0, k_chunk=0, v_chunk=1, max_dist=window // dil)
            obs.append(ob.reshape(T, BRANCH_W))
            lses.append(lse.reshape(T, BRANCH_W))

        yc = _moba_call(oT, otok, A, q_chunk=T_CQ, k_chunk=TOK_CK, v_chunk=T_CV)

        w1 = cmp_w1[l].reshape(2, NSA_CMP_LEN, HEAD_DIM, HEAD_DIM)
        z64 = jnp.zeros((NSA_CMP_STRIDE, HEAD_DIM, HEAD_DIM), F32)
        w1blk = jnp.concatenate([
            jnp.concatenate([w1[0, :16], w1[0, 16:], z64, z64], axis=2),
            jnp.concatenate([z64, z64, w1[1, :16], w1[1, 16:]], axis=2)], axis=1)
        pe2 = jnp.concatenate([cmp_pe[l].reshape(1, -1), jnp.zeros((7, 2 * NSA_CMP_LEN * HEAD_DIM), F32)], axis=0)
        zf = jnp.zeros((NSA_CMP_LEN * HEAD_DIM, HEAD_DIM), F32)
        w1f = jnp.concatenate([jnp.concatenate([cmp_w1[l, 0], zf], axis=1),
                               jnp.concatenate([zf, cmp_w1[l, 1]], axis=1)], axis=0)
        z2 = jnp.zeros((HEAD_DIM, HEAD_DIM), F32)
        w2blk = jnp.concatenate([jnp.concatenate([cmp_w2[l, 0], z2], axis=1),
                                 jnp.concatenate([z2, cmp_w2[l, 1]], axis=1)], axis=0)
        gk = jnp.concatenate([g[5], ones])[None, :]
        kc, kcT = _compress_call(otok, w1blk, pe2, w1f, w2blk, gk, G128, kv_chunk=TOK_DKV)
        yd = _nsa_call(oT, otok, kc, kcT, ovT, q_chunk=T_DQ, kv_chunk=TOK_DKV, v_chunk=T_DX)

        wmg = col(o[16], o[17]).astype(BF16)
        x2 = _epi_call(x.reshape(T, D_MODEL), ng, ya.reshape(T, BRANCH_W), yc.reshape(T, BRANCH_W),
                       yd.reshape(T, BRANCH_W), obs, lses, otok.reshape(T, -1), wmg,
                       w_branch[l].astype(BF16), w_out[l].astype(BF16),
                       z_chunks=(TOK_AZ, TOK_BZ, TOK_CZ, TOK_DZ))
        x = x2.reshape(B, S, D_MODEL)
    return x
```

```python
import functools
import math

import numpy as np
import jax
import jax.numpy as jnp
from jax import lax
from jax.experimental import pallas as pl
from jax.experimental.pallas import tpu as pltpu

F32 = jnp.float32
BF16 = jnp.bfloat16

D_MODEL = 1024
DEPTH = 4
HEAD_DIM = 64
ROT_DIM = 16
ROPE_THETA = 500000.0
EPS = 1e-6
NEG = -1e30
BIG = 1e9
BRANCH_W = 256
DIL_PAIRS = ((128, 1), (512, 4), (2048, 16))
MOBA_BLOCK = 256
MOBA_TOPK = 3
NSA_CMP_LEN = 32
NSA_CMP_STRIDE = 16
NSA_SEL_BLOCK = 64
NSA_SEL_TOPK = 16
NSA_WINDOW = 512
COL_SIZES = (256, 256, 256, 256, 768, 768, 768, 256, 256, 256, 256, 256, 256, 384, 12, 256, 4096)
COL_OFF = tuple(int(v) for v in np.concatenate([[0], np.cumsum(COL_SIZES)]))

LANE = 128
CH = 256
VMEM_LIMIT = 56 * 1024 * 1024


def _cparams(sem):
    return pltpu.CompilerParams(dimension_semantics=sem, vmem_limit_bytes=VMEM_LIMIT)


def _dot(a, b):
    return jnp.dot(a, b, preferred_element_type=F32)


def _dot_nt(a, b):
    return lax.dot_general(a, b, (((1,), (1,)), ((), ())), preferred_element_type=F32)


def _split2(x):
    hi = x.astype(BF16)
    lo = (x - hi.astype(F32)).astype(BF16)
    return hi, lo


def _split3(x):
    hi = x.astype(BF16)
    r = x - hi.astype(F32)
    mid = r.astype(BF16)
    lo = (r - mid.astype(F32)).astype(BF16)
    return hi, mid, lo


def _proj_kernel(x_ref, ng_ref, wtok_ref, wT_ref, tokp_ref, gT_ref, rtok_ref, rT_ref, G_ref,
                 otok_ref, oT_ref, *, tok_kinds, T_kinds, tm):
    x = x_ref[...]
    ms = jnp.mean(x * x, axis=-1, keepdims=True)
    xn = (x * lax.rsqrt(ms + EPS) * ng_ref[...]).astype(BF16)

    for c, (has_norm, has_rope) in enumerate(tok_kinds):
        y = _dot(xn, wtok_ref[:, c * CH:(c + 1) * CH])
        prm = tokp_ref[c]
        if has_norm:
            ss = _dot((y * y).astype(BF16), G_ref[...])
            inv = lax.rsqrt(ss * (1.0 / HEAD_DIM) + EPS)
            y = y * jnp.where(prm[0:1, :] > 0.0, inv, 1.0)
        y = y * prm[2:3, :]
        if has_rope:
            halves = []
            for hf in range(CH // LANE):
                yh = y[:, hf * LANE:(hf + 1) * LANE]
                rf = prm[1:2, hf * LANE:(hf + 1) * LANE]
                cc = jnp.where(rf > 0.0, rtok_ref[0], 1.0)
                s1 = rtok_ref[1] * rf
                s2 = rtok_ref[2] * rf
                halves.append(yh * cc + pltpu.roll(yh, 8, 1) * s1 + pltpu.roll(yh, LANE - 8, 1) * s2)
            y = jnp.concatenate(halves, axis=1)
        otok_ref[:, c * CH:(c + 1) * CH] = y.astype(BF16)

    cosT = rT_ref[0]
    sinT = rT_ref[1]
    for c, heads in enumerate(T_kinds):
        y = _dot_nt(wT_ref[c * CH:(c + 1) * CH, :], xn)
        for h, (nrm, rope, scale, sigm) in enumerate(heads):
            r0 = c * CH + h * HEAD_DIM
            yh = y[h * HEAD_DIM:(h + 1) * HEAD_DIM, :]
            if nrm:
                msq = jnp.mean(yh * yh, axis=0, keepdims=True)
                yh = yh * lax.rsqrt(msq + EPS) * gT_ref[r0:r0 + HEAD_DIM, :]
            if rope:
                x1 = yh[0:8, :]
                x2 = yh[8:16, :]
                yh = jnp.concatenate([x1 * cosT - x2 * sinT, x2 * cosT + x1 * sinT, yh[16:, :]], axis=0)
            if scale != 1.0:
                yh = yh * scale
            if sigm:
                yh = jax.nn.sigmoid(yh)
            yb = yh.astype(BF16)
            for t in range(tm // LANE):
                oT_ref[t, r0:r0 + HEAD_DIM, :] = yb[:, t * LANE:(t + 1) * LANE]


def _proj_call(xv, ng, wtok, wT, tokp, gT, rtok, rT, G, *, tok_kinds, T_kinds, tm):
    B, L, wide = xv.shape
    dil = wide // D_MODEL
    ntok, nT = len(tok_kinds), len(T_kinds)
    kern = functools.partial(_proj_kernel, tok_kinds=tok_kinds, T_kinds=T_kinds, tm=tm)
    return pl.pallas_call(
        kern,
        grid=(B, dil, L // tm),
        in_specs=[
            pl.BlockSpec((None, tm, D_MODEL), lambda b, r, n: (b, n, r)),
            pl.BlockSpec((1, D_MODEL), lambda b, r, n: (0, 0)),
            pl.BlockSpec((D_MODEL, ntok * CH), lambda b, r, n: (0, 0)),
            pl.BlockSpec((nT * CH, D_MODEL), lambda b, r, n: (0, 0)),
            pl.BlockSpec((ntok, 8, CH), lambda b, r, n: (0, 0, 0)),
            pl.BlockSpec((nT * CH, 1), lambda b, r, n: (0, 0)),
            pl.BlockSpec((None, 3, tm, LANE), lambda b, r, n: (r, 0, n, 0)),
            pl.BlockSpec((None, 2, 8, tm), lambda b, r, n: (r, 0, 0, n)),
            pl.BlockSpec((CH, CH), lambda b, r, n: (0, 0)),
        ],
        out_specs=[
            pl.BlockSpec((None, None, tm, ntok * CH), lambda b, r, n: (b, r, n, 0)),
            pl.BlockSpec((None, None, tm // LANE, nT * CH, LANE), lambda b, r, n: (b, r, n, 0, 0)),
        ],
        out_shape=[
            jax.ShapeDtypeStruct((B, dil, L, ntok * CH), BF16),
            jax.ShapeDtypeStruct((B, dil, L // LANE, nT * CH, LANE), BF16),
        ],
        compiler_params=_cparams(("parallel", "parallel", "parallel")),
        name="proj",
    )(xv, ng, wtok, wT, tokp, gT, rtok, rT, G)


def _pair_masked_q(q_pair, h):
    rid = lax.broadcasted_iota(jnp.int32, q_pair.shape, 0)
    lo = (h % 2) * HEAD_DIM
    keep = jnp.where(rid >= lo, jnp.where(rid < lo + HEAD_DIM, 1.0, 0.0), 0.0).astype(BF16)
    return q_pair * keep


def _lane_tiles(ref, t0, nt, r0, nr):
    return jnp.concatenate([ref[t0 + t, r0:r0 + nr, :] for t in range(nt)], axis=1)


def _online(s, m, l, acc, vb):
    m_new = jnp.maximum(m, jnp.max(s, axis=0, keepdims=True))
    alpha = jnp.exp(m - m_new)
    p = jnp.exp(s - m_new)
    l = alpha * l + jnp.sum(p, axis=0, keepdims=True)
    acc = alpha * acc + _dot(vb, p.astype(BF16))
    return m_new, l, acc


SB_LOG_CUTOFF = -100.0


def _sb_kernel(qT_ref, k_ref, vT_ref, U_ref, o_ref, qm_ref, acc_ref, carry_ref, *, tq):
    i = pl.program_id(1)
    nt = tq // LANE
    row = lax.broadcasted_iota(jnp.int32, (tq, tq), 0)
    col = lax.broadcasted_iota(jnp.int32, (tq, tq), 1)
    past = row < col
    for h in range(4):
        p = h // 2
        qm_ref[h] = _pair_masked_q(_lane_tiles(qT_ref, 0, nt, p * LANE, LANE), h)
    acc_ref[...] = jnp.zeros_like(acc_ref)
    carry_ref[...] = jnp.zeros_like(carry_ref)

    def tile(j, masked):
        scores, logsig, laters = [], [], []
        for h in range(4):
            p = h // 2
            kb = k_ref[pl.ds(pl.multiple_of(j * tq, tq), tq), p * LANE:(p + 1) * LANE]
            scores.append(_dot(kb, qm_ref[h]))
        worst = None
        for h in range(4):
            s = scores[h]
            sp = jnp.maximum(s, 0.0) + jnp.log(1.0 + jnp.exp(-jnp.abs(s)))
            lg = -sp
            if masked:
                lg = jnp.where(past, lg, 0.0)
            hi, lo = _split2(lg)
            carry = carry_ref[h:h + 1, :]
            laters.append(_dot(U_ref[...], hi) + _dot(U_ref[...], lo) + carry)
            logsig.append(s - sp)
            carry = carry + jnp.sum(lg, axis=0, keepdims=True)
            carry_ref[h:h + 1, :] = carry
            worst = carry if worst is None else jnp.maximum(worst, carry)
        for h in range(4):
            w = jnp.exp(logsig[h] + laters[h])
            if masked:
                w = jnp.where(past, w, 0.0)
            vb = _lane_tiles(vT_ref, j * nt, nt, h * HEAD_DIM, HEAD_DIM)
            acc_ref[h * HEAD_DIM:(h + 1) * HEAD_DIM, :] += _dot(vb, w.astype(BF16))
        return jnp.max(worst)

    worst0 = tile(i, True)

    def cond(st):
        return jnp.logical_and(st[0] >= 0, st[1] > SB_LOG_CUTOFF)

    def body(st):
        return st[0] - 1, tile(st[0], False)

    lax.while_loop(cond, body, (i - 1, worst0))
    o_ref[...] = acc_ref[...].T.astype(o_ref.dtype)


def _sb_call(oT, otok, U, *, q_chunk, k_chunk, v_chunk, tq=256):
    B, _, nlt, _, _ = oT.shape
    S = nlt * LANE
    kern = functools.partial(_sb_kernel, tq=tq)
    return pl.pallas_call(
        kern,
        grid=(B, S // tq),
        in_specs=[
            pl.BlockSpec((None, None, tq // LANE, CH, LANE), lambda b, i: (b, 0, i, q_chunk, 0)),
            pl.BlockSpec((None, None, S, CH), lambda b, i: (b, 0, 0, k_chunk)),
            pl.BlockSpec((None, None, nlt, CH, LANE), lambda b, i: (b, 0, 0, v_chunk, 0)),
            pl.BlockSpec((tq, tq), lambda b, i: (0, 0)),
        ],
        out_specs=pl.BlockSpec((None, tq, BRANCH_W), lambda b, i: (b, i, 0)),
        out_shape=jax.ShapeDtypeStruct((B, S, BRANCH_W), BF16),
        scratch_shapes=[pltpu.VMEM((4, LANE, tq), BF16), pltpu.VMEM((CH, tq), F32), pltpu.VMEM((8, tq), F32)],
        compiler_params=_cparams(("parallel", "arbitrary")),
        name="stick_breaking",
    )(oT, otok, oT, U)


def _band_kernel(qT_ref, k_ref, vT_ref, o_ref, lse_ref, *, tqb, max_dist):
    n = pl.program_id(2)
    n_prev = -(-max_dist // LANE)
    nk = n_prev + 1
    nsub = tqb // LANE
    row = lax.broadcasted_iota(jnp.int32, (nk * LANE, LANE), 0)
    col = lax.broadcasted_iota(jnp.int32, (nk * LANE, LANE), 1)
    for u in range(nsub):
        qt = n * nsub + u
        kt0 = jnp.maximum(qt - n_prev, 0)
        dist = (qt - kt0) * LANE + col - row
        kwin = k_ref[pl.ds(pl.multiple_of(kt0 * LANE, LANE), nk * LANE), :]
        o_parts, l_parts = [], []
        for h in range(4):
            p = h // 2
            qm = _pair_masked_q(qT_ref[u, p * LANE:(p + 1) * LANE, :], h)
            s = _dot(kwin[:, p * LANE:(p + 1) * LANE], qm)
            s = jnp.where(dist >= 0, jnp.where(dist <= max_dist, s, NEG), NEG)
            m = jnp.max(s, axis=0, keepdims=True)
            e = jnp.exp(s - m)
            den = jnp.sum(e, axis=0, keepdims=True)
            vb = _lane_tiles(vT_ref, kt0, nk, h * HEAD_DIM, HEAD_DIM)
            o_parts.append(_dot(vb, e.astype(BF16)) / den)
            l_parts.append(jnp.broadcast_to(m + jnp.log(den), (HEAD_DIM, LANE)))
        o_ref[u * LANE:(u + 1) * LANE, :] = jnp.concatenate(o_parts, axis=0).T.astype(o_ref.dtype)
        lse_ref[u * LANE:(u + 1) * LANE, :] = jnp.concatenate(l_parts, axis=0).T


def _band_call(oT, otok, *, q_chunk, k_chunk, v_chunk, max_dist):
    B, dil, nlt, _, _ = oT.shape
    L = nlt * LANE
    tqb = min(512, L)
    kern = functools.partial(_band_kernel, tqb=tqb, max_dist=max_dist)
    return pl.pallas_call(
        kern,
        grid=(B, dil, L // tqb),
        in_specs=[
            pl.BlockSpec((None, None, tqb // LANE, CH, LANE), lambda b, r, n: (b, r, n, q_chunk, 0)),
            pl.BlockSpec((None, None, L, CH), lambda b, r, n: (b, r, 0, k_chunk)),
            pl.BlockSpec((None, None, nlt, CH, LANE), lambda b, r, n: (b, r, 0, v_chunk, 0)),
        ],
        out_specs=[
            pl.BlockSpec((None, tqb, BRANCH_W), lambda b, r, n: (b, n, r)),
            pl.BlockSpec((None, tqb, BRANCH_W), lambda b, r, n: (b, n, r)),
        ],
        out_shape=[
            jax.ShapeDtypeStruct((B, L, dil * BRANCH_W), BF16),
            jax.ShapeDtypeStruct((B, L, dil * BRANCH_W), F32),
        ],
        compiler_params=_cparams(("parallel", "parallel", "parallel")),
        name="banded",
    )(oT, otok, oT)


def _moba_kernel(qT_ref, k_ref, vT_ref, A_ref, o_ref, kmean_ref, sel_ref, qm_ref, acc_ref, ml_ref, *, tq):
    i = pl.program_id(1)
    nt = tq // LANE
    nblk = A_ref.shape[0]

    @pl.when(i == 0)
    def _():
        kmean_ref[...] = _dot(A_ref[...], k_ref[...])

    blk = lax.broadcasted_iota(jnp.int32, (nblk, tq), 0)
    row = lax.broadcasted_iota(jnp.int32, (tq, tq), 0)
    col = lax.broadcasted_iota(jnp.int32, (tq, tq), 1)
    gates, diag = [], []
    for h in range(4):
        p = h // 2
        qm = _pair_masked_q(_lane_tiles(qT_ref, 0, nt, p * LANE, LANE), h)
        qm_ref[h] = qm
        k1, k2, k3 = _split3(kmean_ref[:, p * LANE:(p + 1) * LANE])
        gates.append(_dot(k1, qm) + _dot(k2, qm) + _dot(k3, qm))
        kb = k_ref[pl.ds(pl.multiple_of(i * tq, tq), tq), p * LANE:(p + 1) * LANE]
        diag.append(_dot(kb, qm))
    for h in range(4):
        g = jnp.where(blk < i, gates[h], NEG)
        sel = jnp.zeros((nblk, tq), F32)
        for _r in range(MOBA_TOPK):
            mx = jnp.max(g, axis=0, keepdims=True)
            idx = jnp.min(jnp.where(g == mx, blk, nblk), axis=0, keepdims=True)
            hit = blk == idx
            sel = jnp.where(hit, 1.0, sel)
            g = jnp.where(hit, -jnp.inf, g)
        sel_ref[h] = jnp.where(blk < i, sel, 0.0)

        s = jnp.where(row <= col, diag[h], NEG)
        m = jnp.max(s, axis=0, keepdims=True)
        pm = jnp.exp(s - m)
        ml_ref[h:h + 1, :] = m
        ml_ref[4 + h:5 + h, :] = jnp.sum(pm, axis=0, keepdims=True)
        acc_ref[h * HEAD_DIM:(h + 1) * HEAD_DIM, :] = _dot(
            _lane_tiles(vT_ref, i * nt, nt, h * HEAD_DIM, HEAD_DIM), pm.astype(BF16))

    def body(j, c):
        scores = []
        for h in range(4):
            p = h // 2
            kbj = k_ref[pl.ds(pl.multiple_of(j * tq, tq), tq), p * LANE:(p + 1) * LANE]
            scores.append(_dot(kbj, qm_ref[h]))
        for h in range(4):
            srow = sel_ref[h, pl.ds(j, 1), :]
            sj = jnp.where(srow > 0.0, scores[h], NEG)
            vb = _lane_tiles(vT_ref, j * nt, nt, h * HEAD_DIM, HEAD_DIM)
            rows = slice(h * HEAD_DIM, (h + 1) * HEAD_DIM)
            m, l, acc = _online(sj, ml_ref[h:h + 1, :], ml_ref[4 + h:5 + h, :], acc_ref[rows, :], vb)
            ml_ref[h:h + 1, :] = m
            ml_ref[4 + h:5 + h, :] = l
            acc_ref[rows, :] = acc
        return c

    lax.fori_loop(0, i, body, 0)
    inv = 1.0 / ml_ref[4:8, :]
    outs = [acc_ref[h * HEAD_DIM:(h + 1) * HEAD_DIM, :] * inv[h:h + 1, :] for h in range(4)]
    o_ref[...] = jnp.concatenate(outs, axis=0).T.astype(o_ref.dtype)


def _moba_call(oT, otok, A, *, q_chunk, k_chunk, v_chunk):
    B, _, nlt, _, _ = oT.shape
    S = nlt * LANE
    tq = MOBA_BLOCK
    nblk = S // MOBA_BLOCK
    kern = functools.partial(_moba_kernel, tq=tq)
    return pl.pallas_call(
        kern,
        grid=(B, S // tq),
        in_specs=[
            pl.BlockSpec((None, None, tq // LANE, CH, LANE), lambda b, i: (b, 0, i, q_chunk, 0)),
            pl.BlockSpec((None, None, S, CH), lambda b, i: (b, 0, 0, k_chunk)),
            pl.BlockSpec((None, None, nlt, CH, LANE), lambda b, i: (b, 0, 0, v_chunk, 0)),
            pl.BlockSpec((nblk, S), lambda b, i: (0, 0)),
        ],
        out_specs=pl.BlockSpec((None, tq, BRANCH_W), lambda b, i: (b, i, 0)),
        out_shape=jax.ShapeDtypeStruct((B, S, BRANCH_W), BF16),
        scratch_shapes=[pltpu.VMEM((nblk, CH), F32), pltpu.VMEM((4, nblk, tq), F32),
                        pltpu.VMEM((4, LANE, tq), BF16), pltpu.VMEM((CH, tq), F32), pltpu.VMEM((8, tq), F32)],
        compiler_params=_cparams(("parallel", "arbitrary")),
        name="moba",
    )(oT, otok, oT, A)


def _compress_kernel(x_ref, w1_ref, pe_ref, w1f_ref, w2_ref, gk_ref, G_ref, o_ref, oT_ref, acc_ref):
    l = pl.program_id(1)

    @pl.when(l == 0)
    def _():
        acc_ref[...] = jnp.zeros_like(acc_ref)

    x = x_ref[...]
    w_hi, w_lo = _split2(w1_ref[...])
    acc_ref[...] += _dot(x, w_hi) + _dot(x, w_lo)

    @pl.when(l == pl.num_programs(1) - 1)
    def _():
        nc = acc_ref.shape[0]
        r = acc_ref[...]
        bias = jnp.dot(pe_ref[...], w1f_ref[...], preferred_element_type=F32,
                       precision=lax.Precision.HIGHEST)
        p1 = jnp.concatenate([r[:, 0:64], r[:, 128:192]], axis=1)
        p2 = jnp.concatenate([r[:, 64:128], r[:, 192:256]], axis=1)
        hid = p1 + pltpu.roll(p2, nc - 1, 0) + bias[0:1, :]
        hid = hid * jax.nn.sigmoid(hid)
        comp = jnp.dot(hid, w2_ref[...], preferred_element_type=F32, precision=lax.Precision.HIGHEST)
        ss = _dot((comp * comp).astype(BF16), G_ref[...])
        lane = lax.broadcasted_iota(jnp.int32, comp.shape, 1)
        inv = jnp.where(lane < HEAD_DIM, lax.rsqrt(ss * (1.0 / HEAD_DIM) + EPS), 1.0)
        comp = comp * inv * gk_ref[...]
        rowi = lax.broadcasted_iota(jnp.int32, comp.shape, 0)
        comp = jnp.where(rowi < nc - 1, comp, 0.0)
        o_ref[...] = comp.astype(o_ref.dtype)
        oT_ref[...] = comp.T.astype(oT_ref.dtype)


def _compress_call(otok, w1blk, pe2, w1f, w2blk, gk, G128, *, kv_chunk):
    B, _, S, width = otok.shape
    nc = S // NSA_CMP_STRIDE
    xv = otok.reshape(B, nc, NSA_CMP_STRIDE * width)
    cpb = width // LANE
    kvb = kv_chunk * (CH // LANE)
    return pl.pallas_call(
        _compress_kernel,
        grid=(B, NSA_CMP_STRIDE),
        in_specs=[
            pl.BlockSpec((None, nc, LANE), lambda b, l: (b, 0, l * cpb + kvb)),
            pl.BlockSpec((None, LANE, CH), lambda b, l: (l, 0, 0)),
            pl.BlockSpec((8, 2 * NSA_CMP_LEN * HEAD_DIM), lambda b, l: (0, 0)),
            pl.BlockSpec((2 * NSA_CMP_LEN * HEAD_DIM, LANE), lambda b, l: (0, 0)),
            pl.BlockSpec((LANE, LANE), lambda b, l: (0, 0)),
            pl.BlockSpec((1, LANE), lambda b, l: (0, 0)),
            pl.BlockSpec((LANE, LANE), lambda b, l: (0, 0)),
        ],
        out_specs=[
            pl.BlockSpec((None, nc, LANE), lambda b, l: (b, 0, 0)),
            pl.BlockSpec((None, LANE, nc), lambda b, l: (b, 0, 0)),
        ],
        out_shape=[
            jax.ShapeDtypeStruct((B, nc, LANE), BF16),
            jax.ShapeDtypeStruct((B, LANE, nc), BF16),
        ],
        scratch_shapes=[pltpu.VMEM((nc, CH), F32)],
        compiler_params=_cparams(("parallel", "arbitrary")),
        name="nsa_compress",
    )(xv, w1blk, pe2, w1f, w2blk, gk, G128)


def _nsa_kernel(qT_ref, k_ref, vT_ref, kc_ref, kcT_ref, ov_ref, o_ref, sel_ref, qm_ref, acc_ref, ml_ref, *, tq):
    i = pl.program_id(1)
    nt = tq // LANE
    nq = 4 * tq
    nc = kc_ref.shape[0]
    nsel = ov_ref.shape[0]
    spb = tq // NSA_SEL_BLOCK

    zeros = jnp.zeros((HEAD_DIM, tq), BF16)
    q_heads = [_lane_tiles(qT_ref, 0, nt, h * HEAD_DIM, HEAD_DIM) for h in range(4)]
    qm_lo = jnp.concatenate([jnp.concatenate([q, zeros], axis=0) for q in q_heads], axis=1)
    qm_hi = jnp.concatenate([jnp.concatenate([zeros, q], axis=0) for q in q_heads], axis=1)
    qpos = i * tq + lax.broadcasted_iota(jnp.int32, (1, tq), 1)
    qpos4 = jnp.concatenate([qpos] * 4, axis=1)

    zc = _dot(kc_ref[...], qm_lo)
    c_end = lax.broadcasted_iota(jnp.int32, (nc, nq), 0) * NSA_CMP_STRIDE + (NSA_CMP_LEN - 1)
    cmask = c_end <= qpos4
    zc = jnp.where(cmask, zc, NEG)
    e = jnp.where(cmask, jnp.exp(zc - jnp.max(zc, axis=0, keepdims=True)), 0.0)
    pc = e / jnp.maximum(jnp.sum(e, axis=0, keepdims=True), 1.0)
    o_cmp = _dot(kcT_ref[HEAD_DIM:2 * HEAD_DIM, :], pc.astype(BF16))

    psum = pc[:, 0:tq] + pc[:, tq:2 * tq] + pc[:, 2 * tq:3 * tq] + pc[:, 3 * tq:4 * tq]
    p_hi, p_lo = _split2(psum)
    imp = _dot(ov_ref[...], p_hi) + _dot(ov_ref[...], p_lo)
    nid = lax.broadcasted_iota(jnp.int32, (nsel, tq), 0)
    cur = qpos // NSA_SEL_BLOCK
    imp = jnp.where(nid == 0, BIG, imp)
    imp = jnp.where(nid == cur, BIG, imp)
    imp = jnp.where(nid == cur - 1, BIG, imp)
    imp = jnp.where(nid > cur, NEG, imp)
    sel = jnp.zeros((nsel, tq), F32)
    for _r in range(min(NSA_SEL_TOPK, nsel)):
        mx = jnp.max(imp, axis=0, keepdims=True)
        idx = jnp.min(jnp.where(imp == mx, nid, nsel), axis=0, keepdims=True)
        hit = nid == idx
        sel = jnp.where(hit, 1.0, sel)
        imp = jnp.where(hit, -jnp.inf, imp)
    sel = jnp.where(nid <= cur, sel, 0.0)
    sel_ref[...] = sel
    for h in range(4):
        qm_ref[h] = qm_lo[:, h * tq:(h + 1) * tq]

    def sel_tile(j, causal):
        kb = k_ref[pl.ds(pl.multiple_of(j * tq, tq), tq), LANE:2 * LANE]
        scores = [_dot(kb, qm_ref[h]) for h in range(4)]
        srows = [sel_ref[pl.ds(j * spb + u, 1), :] for u in range(spb)]
        vb = _lane_tiles(vT_ref, j * nt, nt, 0, HEAD_DIM)
        for h in range(4):
            s = jnp.concatenate(
                [jnp.where(srows[u] > 0.0, scores[h][u * NSA_SEL_BLOCK:(u + 1) * NSA_SEL_BLOCK, :], NEG)
                 for u in range(spb)], axis=0)
            lanes = slice(h * tq, (h + 1) * tq)
            if causal:
                s = jnp.where(kpos_d <= qpos, s, NEG)
                m = jnp.max(s, axis=0, keepdims=True)
                pm = jnp.exp(s - m)
                l = jnp.sum(pm, axis=0, keepdims=True)
                acc = _dot(vb, pm.astype(BF16))
            else:
                m, l, acc = _online(s, ml_ref[0:1, lanes], ml_ref[1:2, lanes], acc_ref[:, lanes], vb)
            ml_ref[0:1, lanes] = m
            ml_ref[1:2, lanes] = l
            acc_ref[:, lanes] = acc

    kpos_d = i * tq + lax.broadcasted_iota(jnp.int32, (tq, tq), 0)
    sel_tile(i, True)

    def body(j, c):
        sel_tile(j, False)
        return c

    lax.fori_loop(0, i, body, 0)
    o_sel = acc_ref[...] / ml_ref[1:2, :]

    n_prev = -(-(NSA_WINDOW - 1) // tq)
    nk = n_prev + 1
    kt0 = jnp.maximum(i - n_prev, 0)
    kw = k_ref[pl.ds(pl.multiple_of(kt0 * tq, tq), nk * tq), LANE:2 * LANE]
    sw = _dot(kw, qm_hi)
    dist = qpos4 - (kt0 * tq + lax.broadcasted_iota(jnp.int32, (nk * tq, nq), 0))
    sw = jnp.where(dist >= 0, jnp.where(dist <= NSA_WINDOW - 1, sw, NEG), NEG)
    mw = jnp.max(sw, axis=0, keepdims=True)
    ew = jnp.exp(sw - mw)
    dw = jnp.sum(ew, axis=0, keepdims=True)
    vwin = _lane_tiles(vT_ref, kt0 * nt, nk * nt, HEAD_DIM, HEAD_DIM)
    o_win = _dot(vwin, ew.astype(BF16)) / dw

    gates = _lane_tiles(vT_ref, i * nt, nt, 2 * HEAD_DIM, 16).astype(F32)
    outs = []
    for h in range(4):
        sl = slice(h * tq, (h + 1) * tq)
        outs.append(gates[3 * h:3 * h + 1, :] * o_cmp[:, sl]
                    + gates[3 * h + 1:3 * h + 2, :] * o_sel[:, sl]
                    + gates[3 * h + 2:3 * h + 3, :] * o_win[:, sl])
    o_ref[...] = jnp.concatenate(outs, axis=0).T.astype(o_ref.dtype)


def _nsa_call(oT, otok, kc, kcT, ovT, *, q_chunk, kv_chunk, v_chunk):
    B, _, nlt, _, _ = oT.shape
    S = nlt * LANE
    tq = 256
    nc = S // NSA_CMP_STRIDE
    nsel = S // NSA_SEL_BLOCK
    kern = functools.partial(_nsa_kernel, tq=tq)
    return pl.pallas_call(
        kern,
        grid=(B, S // tq),
        in_specs=[
            pl.BlockSpec((None, None, tq // LANE, CH, LANE), lambda b, i: (b, 0, i, q_chunk, 0)),
            pl.BlockSpec((None, None, S, CH), lambda b, i: (b, 0, 0, kv_chunk)),
            pl.BlockSpec((None, None, nlt, CH, LANE), lambda b, i: (b, 0, 0, v_chunk, 0)),
            pl.BlockSpec((None, nc, LANE), lambda b, i: (b, 0, 0)),
            pl.BlockSpec((None, LANE, nc), lambda b, i: (b, 0, 0)),
            pl.BlockSpec((nsel, nc), lambda b, i: (0, 0)),
        ],
        out_specs=pl.BlockSpec((None, tq, BRANCH_W), lambda b, i: (b, i, 0)),
        out_shape=jax.ShapeDtypeStruct((B, S, BRANCH_W), BF16),
        scratch_shapes=[pltpu.VMEM((nsel, tq), F32), pltpu.VMEM((4, LANE, tq), BF16),
                        pltpu.VMEM((HEAD_DIM, 4 * tq), F32), pltpu.VMEM((8, 4 * tq), F32)],
        compiler_params=_cparams(("parallel", "arbitrary")),
        name="nsa",
    )(oT, otok, oT, kc, kcT, ovT)


def _epi_kernel(x_ref, ng_ref, ya_ref, yc_ref, yd_ref, ob0_ref, ob1_ref, ob2_ref, l0_ref, l1_ref, l2_ref,
                za_ref, zb_ref, zc_ref, zd_ref, wmg_ref, wbr_ref, wout_ref, o_ref):
    x = x_ref[...]
    ms = jnp.mean(x * x, axis=-1, keepdims=True)
    xn = (x * lax.rsqrt(ms + EPS) * ng_ref[...]).astype(BF16)

    l0, l1, l2 = l0_ref[...], l1_ref[...], l2_ref[...]
    mx = jnp.maximum(jnp.maximum(l0, l1), l2)
    e0, e1, e2 = jnp.exp(l0 - mx), jnp.exp(l1 - mx), jnp.exp(l2 - mx)
    yb = (e0 * ob0_ref[...].astype(F32) + e1 * ob1_ref[...].astype(F32)
          + e2 * ob2_ref[...].astype(F32)) / (e0 + e1 + e2)

    ys = (ya_ref[...].astype(F32), yb, yc_ref[...].astype(F32), yd_ref[...].astype(F32))
    zs = (za_ref, zb_ref, zc_ref, zd_ref)
    merged = None
    for i in range(4):
        z = zs[i][...].astype(F32)
        gated = (ys[i] * (z * jax.nn.sigmoid(z))).astype(BF16)
        br = _dot(gated, wbr_ref[i])
        mg = _dot(xn, wmg_ref[:, i * D_MODEL:(i + 1) * D_MODEL])
        term = jax.nn.sigmoid(mg) * br
        merged = term if merged is None else merged + term
    o_ref[...] = x + _dot(merged.astype(BF16), wout_ref[...])


def _epi_call(x2, ng, ya, yc, yd, obs, lses, otok2, wmg, wbr, wout, *, z_chunks, tm=512):
    T = x2.shape[0]
    row = lambda i: (i, 0)
    full2 = lambda i: (0, 0)
    yspec = pl.BlockSpec((tm, BRANCH_W), row)
    zspecs = [pl.BlockSpec((tm, CH), (lambda i, c=c: (i, c))) for c in z_chunks]
    return pl.pallas_call(
        _epi_kernel,
        grid=(T // tm,),
        in_specs=[pl.BlockSpec((tm, D_MODEL), row), pl.BlockSpec((1, D_MODEL), full2)]
        + [yspec] * 3 + [yspec] * 3 + [yspec] * 3 + zspecs
        + [pl.BlockSpec((D_MODEL, 4 * D_MODEL), full2),
           pl.BlockSpec((4, BRANCH_W, D_MODEL), lambda i: (0, 0, 0)),
           pl.BlockSpec((D_MODEL, D_MODEL), full2)],
        out_specs=pl.BlockSpec((tm, D_MODEL), row),
        out_shape=jax.ShapeDtypeStruct((T, D_MODEL), F32),
        compiler_params=_cparams(("parallel",)),
        name="epilogue",
    )(x2, ng, ya, yc, yd, *obs, *lses, otok2, otok2, otok2, otok2, wmg, wbr, wout)


TOK_AK, TOK_AZ, TOK_BK, TOK_BZ, TOK_CK, TOK_CZ, TOK_DKV, TOK_DZ = range(8)
T_AQ, T_AV, T_BQ, T_BV, T_CQ, T_CV, T_DQ, T_DX = range(8)
SCALE = 1.0 / math.sqrt(HEAD_DIM)

_PLAIN = (False, False, 1.0, False)
_MAIN_TOK_KINDS = ((False, False), (False, False), (True, True), (False, False),
                   (True, True), (False, False), (True, True), (False, False))
_MAIN_T_KINDS = (
    ((False, False, SCALE, False),) * 4,
    (_PLAIN,) * 4,
    ((True, True, SCALE, False),) * 4,
    (_PLAIN,) * 4,
    ((True, True, SCALE, False),) * 4,
    (_PLAIN,) * 4,
    ((True, True, SCALE, False),) * 4,
    (_PLAIN, _PLAIN, (False, False, 1.0, True), _PLAIN),
)
_DIL_TOK_KINDS = ((True, True),)
_DIL_T_KINDS = (((True, True, SCALE, False),) * 4, (_PLAIN,) * 4)


def _rope_tables(S, dil):
    L = S // dil
    pos = (jnp.arange(dil, dtype=jnp.int32)[:, None] + dil * jnp.arange(L, dtype=jnp.int32)[None, :]).astype(F32)
    inv = ROPE_THETA ** (-jnp.arange(0, ROT_DIM, 2, dtype=F32) / ROT_DIM)
    ang = pos[:, :, None] * inv[None, None, :]
    cos, sin = jnp.cos(ang), jnp.sin(ang)
    one = jnp.ones((dil, L, HEAD_DIM - ROT_DIM), F32)
    zero8 = jnp.zeros((dil, L, 8), F32)
    zero = jnp.zeros_like(one)
    c_head = jnp.concatenate([cos, cos, one], axis=-1)
    s1_head = jnp.concatenate([zero8, sin, zero], axis=-1)
    s2_head = jnp.concatenate([-sin, zero8, zero], axis=-1)
    rtok = jnp.stack([jnp.tile(t, (1, 1, LANE // HEAD_DIM)) for t in (c_head, s1_head, s2_head)], axis=1)
    rT = jnp.stack([cos.transpose(0, 2, 1), sin.transpose(0, 2, 1)], axis=1)
    return rtok, rT


def _tok_params(rows):
    out = []
    for nf, rf, gains in rows:
        nrow = jnp.concatenate([jnp.full((HEAD_DIM,), float(f), F32) for f in nf])
        rrow = jnp.concatenate([jnp.full((HEAD_DIM,), float(f), F32) for f in rf])
        grow = jnp.concatenate([g.astype(F32) for g in gains])
        out.append(jnp.concatenate([jnp.stack([nrow, rrow, grow]), jnp.zeros((5, CH), F32)], axis=0))
    return jnp.stack(out)


def kernel(x, norm_g, w_in, qk_g, cmp_pe, cmp_w1, cmp_w2, w_branch, w_out):
    B, S, _ = x.shape
    T = B * S
    o = COL_OFF
    ones = jnp.ones((HEAD_DIM,), F32)

    r = np.arange(CH)
    G = jnp.asarray((r[:, None] // HEAD_DIM == r[None, :] // HEAD_DIM).astype(np.float32), BF16)
    G128 = G[:LANE, :LANE]
    t = np.arange(256)
    U = jnp.asarray((t[None, :] > t[:, None]).astype(np.float32), BF16)
    nblk = S // MOBA_BLOCK
    A = jnp.asarray(np.repeat(np.eye(nblk, dtype=np.float32), MOBA_BLOCK, axis=1) / MOBA_BLOCK, BF16)
    nc, nsel = S // NSA_CMP_STRIDE, S // NSA_SEL_BLOCK
    cs = np.arange(nc) * NSA_CMP_STRIDE
    ss = np.arange(nsel) * NSA_SEL_BLOCK
    ov = ((cs[None, :] < ss[:, None] + NSA_SEL_BLOCK) & (cs[None, :] + NSA_CMP_LEN > ss[:, None]))
    ov[:, nc - 1] = False
    ovT = jnp.asarray(ov.astype(np.float32), BF16)
    ropes = {dil: _rope_tables(S, dil) for _, dil in DIL_PAIRS}

    for l in range(DEPTH):
        W = w_in[l]
        col = lambda a, b: W[:, a:b]
        dkv = o[13]
        wtok = jnp.concatenate([
            col(o[1], o[2]), col(o[3], o[4]), col(o[5], o[5] + CH), col(o[7], o[8]),
            col(o[9], o[10]), col(o[11], o[12]),
            col(dkv, dkv + 64), col(dkv + 64, dkv + 128), col(dkv + 128, dkv + 192), col(dkv + 256, dkv + 320),
            col(o[15], o[16])], axis=1).astype(BF16)
        wT = jnp.concatenate([
            col(o[0], o[1]), col(o[2], o[3]), col(o[4], o[4] + CH), col(o[6], o[6] + CH),
            col(o[8], o[9]), col(o[10], o[11]), col(o[12], o[13]),
            col(dkv + 192, dkv + 256), col(dkv + 320, dkv + 384), col(o[14], o[15]),
            jnp.zeros((D_MODEL, CH - 140), F32)], axis=1).T.astype(BF16)
        g = qk_g[l]
        tokp = _tok_params([
            ((0,) * 4, (0,) * 4, (ones,) * 4), ((0,) * 4, (0,) * 4, (ones,) * 4),
            ((1,) * 4, (1,) * 4, (g[1],) * 4), ((0,) * 4, (0,) * 4, (ones,) * 4),
            ((1,) * 4, (1,) * 4, (g[3],) * 4), ((0,) * 4, (0,) * 4, (ones,) * 4),
            ((0, 0, 1, 1), (1, 0, 1, 1), (ones, ones, g[6], g[7])), ((0,) * 4, (0,) * 4, (ones,) * 4)])
        gT = jnp.concatenate([jnp.tile(ones, 8), jnp.tile(g[0], 4), jnp.tile(ones, 4), jnp.tile(g[2], 4),
                              jnp.tile(ones, 4), jnp.tile(g[4], 4), jnp.tile(ones, 4)])[:, None]
        ng = norm_g[l][None, :]
        rtok, rT = ropes[1]
        otok, oT = _proj_call(x, ng, wtok, wT, tokp, gT, rtok, rT, G,
                              tok_kinds=_MAIN_TOK_KINDS, T_kinds=_MAIN_T_KINDS, tm=512)

        ya = _sb_call(oT, otok, U, q_chunk=T_AQ, k_chunk=TOK_AK, v_chunk=T_AV)

        obs, lses = [], []
        for gi, (window, dil) in enumerate(DIL_PAIRS):
            if dil == 1:
                ob, lse = _band_call(oT, otok, q_chunk=T_BQ, k_chunk=TOK_BK, v_chunk=T_BV, max_dist=window // dil)
            else:
                wtok_g = col(o[5] + gi * CH, o[5] + (gi + 1) * CH).astype(BF16)
                wT_g = jnp.concatenate([col(o[4] + gi * CH, o[4] + (gi + 1) * CH),
                                        col(o[6] + gi * CH, o[6] + (gi + 1) * CH)], axis=1).T.astype(BF16)
                tokp_g = _tok_params([((1,) * 4, (1,) * 4, (g[1],) * 4)])
                gT_g = jnp.concatenate([jnp.tile(g[0], 4), jnp.tile(ones, 4)])[:, None]
                rtok_g, rT_g = ropes[dil]
                L = S // dil
                otok_g, oT_g = _proj_call(x.reshape(B, L, dil * D_MODEL), ng, wtok_g, wT_g, tokp_g, gT_g,
                                          rtok_g, rT_g, G, tok_kinds=_DIL_TOK_KINDS, T_kinds=_DIL_T_KINDS,
                                          tm=min(512, L))
                ob, lse = _band_call(oT_g, otok_g, q_chunk=0, k_chunk=0, v_chunk=1, max_dist=window // dil)
            obs.append(ob.reshape(T, BRANCH_W))
            lses.append(lse.reshape(T, BRANCH_W))

        yc = _moba_call(oT, otok, A, q_chunk=T_CQ, k_chunk=TOK_CK, v_chunk=T_CV)

        w1 = cmp_w1[l].reshape(2, NSA_CMP_LEN, HEAD_DIM, HEAD_DIM)
        z64 = jnp.zeros((NSA_CMP_STRIDE, HEAD_DIM, HEAD_DIM), F32)
        w1blk = jnp.concatenate([
            jnp.concatenate([w1[0, :16], w1[0, 16:], z64, z64], axis=2),
            jnp.concatenate([z64, z64, w1[1, :16], w1[1, 16:]], axis=2)], axis=1)
        pe2 = jnp.concatenate([cmp_pe[l].reshape(1, -1), jnp.zeros((7, 2 * NSA_CMP_LEN * HEAD_DIM), F32)], axis=0)
        zf = jnp.zeros((NSA_CMP_LEN * HEAD_DIM, HEAD_DIM), F32)
        w1f = jnp.concatenate([jnp.concatenate([cmp_w1[l, 0], zf], axis=1),
                               jnp.concatenate([zf, cmp_w1[l, 1]], axis=1)], axis=0)
        z2 = jnp.zeros((HEAD_DIM, HEAD_DIM), F32)
        w2blk = jnp.concatenate([jnp.concatenate([cmp_w2[l, 0], z2], axis=1),
                                 jnp.concatenate([z2, cmp_w2[l, 1]], axis=1)], axis=0)
        gk = jnp.concatenate([g[5], ones])[None, :]
        kc, kcT = _compress_call(otok, w1blk, pe2, w1f, w2blk, gk, G128, kv_chunk=TOK_DKV)
        yd = _nsa_call(oT, otok, kc, kcT, ovT, q_chunk=T_DQ, kv_chunk=TOK_DKV, v_chunk=T_DX)

        wmg = col(o[16], o[17]).astype(BF16)
        x2 = _epi_call(x.reshape(T, D_MODEL), ng, ya.reshape(T, BRANCH_W), yc.reshape(T, BRANCH_W),
                       yd.reshape(T, BRANCH_W), obs, lses, otok.reshape(T, -1), wmg,
                       w_branch[l].astype(BF16), w_out[l].astype(BF16),
                       z_chunks=(TOK_AZ, TOK_BZ, TOK_CZ, TOK_DZ))
        x = x2.reshape(B, S, D_MODEL)
    return x
```

```python
import functools
import math

import numpy as np
import jax
import jax.numpy as jnp
from jax import lax
from jax.experimental import pallas as pl
from jax.experimental.pallas import tpu as pltpu

F32 = jnp.float32
BF16 = jnp.bfloat16

D_MODEL = 1024
DEPTH = 4
HEAD_DIM = 64
ROT_DIM = 16
ROPE_THETA = 500000.0
EPS = 1e-6
NEG = -1e30
BIG = 1e9
BRANCH_W = 256
DIL_PAIRS = ((128, 1), (512, 4), (2048, 16))
MOBA_BLOCK = 256
MOBA_TOPK = 3
NSA_CMP_LEN = 32
NSA_CMP_STRIDE = 16
NSA_SEL_BLOCK = 64
NSA_SEL_TOPK = 16
NSA_WINDOW = 512
COL_SIZES = (256, 256, 256, 256, 768, 768, 768, 256, 256, 256, 256, 256, 256, 384, 12, 256, 4096)
COL_OFF = tuple(int(v) for v in np.concatenate([[0], np.cumsum(COL_SIZES)]))

LANE = 128
CH = 256
VMEM_LIMIT = 56 * 1024 * 1024


def _cparams(sem):
    return pltpu.CompilerParams(dimension_semantics=sem, vmem_limit_bytes=VMEM_LIMIT)


def _dot(a, b):
    return jnp.dot(a, b, preferred_element_type=F32)


def _dot_nt(a, b):
    return lax.dot_general(a, b, (((1,), (1,)), ((), ())), preferred_element_type=F32)


def _split2(x):
    hi = x.astype(BF16)
    lo = (x - hi.astype(F32)).astype(BF16)
    return hi, lo


def _split3(x):
    hi = x.astype(BF16)
    r = x - hi.astype(F32)
    mid = r.astype(BF16)
    lo = (r - mid.astype(F32)).astype(BF16)
    return hi, mid, lo


def _proj_kernel(x_ref, ng_ref, wtok_ref, wT_ref, tokp_ref, gT_ref, rtok_ref, rT_ref, G_ref,
                 otok_ref, oT_ref, *, tok_kinds, T_kinds, tm):
    x = x_ref[...]
    ms = jnp.mean(x * x, axis=-1, keepdims=True)
    xn = (x * lax.rsqrt(ms + EPS) * ng_ref[...]).astype(BF16)

    for c, (has_norm, has_rope) in enumerate(tok_kinds):
        y = _dot(xn, wtok_ref[:, c * CH:(c + 1) * CH])
        prm = tokp_ref[c]
        if has_norm:
            ss = _dot((y * y).astype(BF16), G_ref[...])
            inv = lax.rsqrt(ss * (1.0 / HEAD_DIM) + EPS)
            y = y * jnp.where(prm[0:1, :] > 0.0, inv, 1.0)
        y = y * prm[2:3, :]
        if has_rope:
            halves = []
            for hf in range(CH // LANE):
                yh = y[:, hf * LANE:(hf + 1) * LANE]
                rf = prm[1:2, hf * LANE:(hf + 1) * LANE]
                cc = jnp.where(rf > 0.0, rtok_ref[0], 1.0)
                s1 = rtok_ref[1] * rf
                s2 = rtok_ref[2] * rf
                halves.append(yh * cc + pltpu.roll(yh, 8, 1) * s1 + pltpu.roll(yh, LANE - 8, 1) * s2)
            y = jnp.concatenate(halves, axis=1)
        otok_ref[:, c * CH:(c + 1) * CH] = y.astype(BF16)

    cosT = rT_ref[0]
    sinT = rT_ref[1]
    for c, heads in enumerate(T_kinds):
        y = _dot_nt(wT_ref[c * CH:(c + 1) * CH, :], xn)
        for h, (nrm, rope, scale, sigm) in enumerate(heads):
            r0 = c * CH + h * HEAD_DIM
            yh = y[h * HEAD_DIM:(h + 1) * HEAD_DIM, :]
            if nrm:
                msq = jnp.mean(yh * yh, axis=0, keepdims=True)
                yh = yh * lax.rsqrt(msq + EPS) * gT_ref[r0:r0 + HEAD_DIM, :]
            if rope:
                x1 = yh[0:8, :]
                x2 = yh[8:16, :]
                yh = jnp.concatenate([x1 * cosT - x2 * sinT, x2 * cosT + x1 * sinT, yh[16:, :]], axis=0)
            if scale != 1.0:
                yh = yh * scale
            if sigm:
                yh = jax.nn.sigmoid(yh)
            yb = yh.astype(BF16)
            for t in range(tm // LANE):
                oT_ref[t, r0:r0 + HEAD_DIM, :] = yb[:, t * LANE:(t + 1) * LANE]


def _proj_call(xv, ng, wtok, wT, tokp, gT, rtok, rT, G, *, tok_kinds, T_kinds, tm):
    B, L, wide = xv.shape
    dil = wide // D_MODEL
    ntok, nT = len(tok_kinds), len(T_kinds)
    kern = functools.partial(_proj_kernel, tok_kinds=tok_kinds, T_kinds=T_kinds, tm=tm)
    return pl.pallas_call(
        kern,
        grid=(B, dil, L // tm),
        in_specs=[
            pl.BlockSpec((None, tm, D_MODEL), lambda b, r, n: (b, n, r)),
            pl.BlockSpec((1, D_MODEL), lambda b, r, n: (0, 0)),
            pl.BlockSpec((D_MODEL, ntok * CH), lambda b, r, n: (0, 0)),
            pl.BlockSpec((nT * CH, D_MODEL), lambda b, r, n: (0, 0)),
            pl.BlockSpec((ntok, 8, CH), lambda b, r, n: (0, 0, 0)),
            pl.BlockSpec((nT * CH, 1), lambda b, r, n: (0, 0)),
            pl.BlockSpec((None, 3, tm, LANE), lambda b, r, n: (r, 0, n, 0)),
            pl.BlockSpec((None, 2, 8, tm), lambda b, r, n: (r, 0, 0, n)),
            pl.BlockSpec((CH, CH), lambda b, r, n: (0, 0)),
        ],
        out_specs=[
            pl.BlockSpec((None, None, tm, ntok * CH), lambda b, r, n: (b, r, n, 0)),
            pl.BlockSpec((None, None, tm // LANE, nT * CH, LANE), lambda b, r, n: (b, r, n, 0, 0)),
        ],
        out_shape=[
            jax.ShapeDtypeStruct((B, dil, L, ntok * CH), BF16),
            jax.ShapeDtypeStruct((B, dil, L // LANE, nT * CH, LANE), BF16),
        ],
        compiler_params=_cparams(("parallel", "parallel", "parallel")),
        name="proj",
    )(xv, ng, wtok, wT, tokp, gT, rtok, rT, G)


def _pair_masked_q(q_pair, h):
    rid = lax.broadcasted_iota(jnp.int32, q_pair.shape, 0)
    lo = (h % 2) * HEAD_DIM
    keep = jnp.where(rid >= lo, jnp.where(rid < lo + HEAD_DIM, 1.0, 0.0), 0.0).astype(BF16)
    return q_pair * keep


def _lane_tiles(ref, t0, nt, r0, nr):
    return jnp.concatenate([ref[t0 + t, r0:r0 + nr, :] for t in range(nt)], axis=1)


BIGPOS = 1e30
ONES_ROWS = 16
HA = HEAD_DIM + ONES_ROWS
LOG2E = 1.4426950408889634


def _v_aug(vb):
    return jnp.concatenate([vb, jnp.ones((ONES_ROWS, vb.shape[1]), BF16)], axis=0)


def _online_cols(s, colsel, m_old, acc_old, vb_aug):
    tmax = jnp.max(s, axis=0, keepdims=True)
    m_new = jnp.where(colsel > 0.0, jnp.maximum(m_old, tmax), m_old)
    m_use = jnp.where(colsel > 0.0, m_new, BIGPOS)
    p = jnp.exp2(s - m_use).astype(BF16)
    acc = jnp.exp2(m_old - m_new) * acc_old + _dot(vb_aug, p)
    return m_new, acc


SB_LOG_CUTOFF = -100.0


def _sb_kernel(qT_ref, k_ref, vT_ref, U_ref, o_ref, qm_ref, acc_ref, carry_ref, *, tq):
    i = pl.program_id(1)
    nt = tq // LANE
    row = lax.broadcasted_iota(jnp.int32, (tq, tq), 0)
    col = lax.broadcasted_iota(jnp.int32, (tq, tq), 1)
    past = row < col
    for h in range(4):
        p = h // 2
        qm_ref[h] = _pair_masked_q(_lane_tiles(qT_ref, 0, nt, p * LANE, LANE), h)
    acc_ref[...] = jnp.zeros_like(acc_ref)
    carry_ref[...] = jnp.zeros_like(carry_ref)

    def tile(j, masked):
        scores, logsig, laters = [], [], []
        for h in range(4):
            p = h // 2
            kb = k_ref[pl.ds(pl.multiple_of(j * tq, tq), tq), p * LANE:(p + 1) * LANE]
            scores.append(_dot(kb, qm_ref[h]))
        worst = None
        for h in range(4):
            s = scores[h]
            sp = jnp.maximum(s, 0.0) + jnp.log(1.0 + jnp.exp(-jnp.abs(s)))
            lg = -sp
            if masked:
                lg = jnp.where(past, lg, 0.0)
            hi, lo = _split2(lg)
            carry = carry_ref[h:h + 1, :]
            laters.append(_dot(U_ref[...], hi) + _dot(U_ref[...], lo) + carry)
            logsig.append(s - sp)
            carry = carry + jnp.sum(lg, axis=0, keepdims=True)
            carry_ref[h:h + 1, :] = carry
            worst = carry if worst is None else jnp.maximum(worst, carry)
        for h in range(4):
            w = jnp.exp(logsig[h] + laters[h])
            if masked:
                w = jnp.where(past, w, 0.0)
            vb = _lane_tiles(vT_ref, j * nt, nt, h * HEAD_DIM, HEAD_DIM)
            acc_ref[h * HEAD_DIM:(h + 1) * HEAD_DIM, :] += _dot(vb, w.astype(BF16))
        return jnp.max(worst)

    worst0 = tile(i, True)

    def cond(st):
        return jnp.logical_and(st[0] >= 0, st[1] > SB_LOG_CUTOFF)

    def body(st):
        return st[0] - 1, tile(st[0], False)

    lax.while_loop(cond, body, (i - 1, worst0))
    o_ref[...] = acc_ref[...].T.astype(o_ref.dtype)


def _sb_call(oT, otok, U, *, q_chunk, k_chunk, v_chunk, tq=256):
    B, _, nlt, _, _ = oT.shape
    S = nlt * LANE
    kern = functools.partial(_sb_kernel, tq=tq)
    return pl.pallas_call(
        kern,
        grid=(B, S // tq),
        in_specs=[
            pl.BlockSpec((None, None, tq // LANE, CH, LANE), lambda b, i: (b, 0, i, q_chunk, 0)),
            pl.BlockSpec((None, None, S, CH), lambda b, i: (b, 0, 0, k_chunk)),
            pl.BlockSpec((None, None, nlt, CH, LANE), lambda b, i: (b, 0, 0, v_chunk, 0)),
            pl.BlockSpec((tq, tq), lambda b, i: (0, 0)),
        ],
        out_specs=pl.BlockSpec((None, tq, BRANCH_W), lambda b, i: (b, i, 0)),
        out_shape=jax.ShapeDtypeStruct((B, S, BRANCH_W), BF16),
        scratch_shapes=[pltpu.VMEM((4, LANE, tq), BF16), pltpu.VMEM((CH, tq), F32), pltpu.VMEM((8, tq), F32)],
        compiler_params=_cparams(("parallel", "arbitrary")),
        name="stick_breaking",
    )(oT, otok, oT, U)


def _band_kernel(qT_ref, k_ref, vT_ref, o_ref, lse_ref, *, tqb, max_dist):
    n = pl.program_id(2)
    n_prev = -(-max_dist // LANE)
    nk = n_prev + 1
    nsub = tqb // LANE
    row = lax.broadcasted_iota(jnp.int32, (nk * LANE, LANE), 0)
    col = lax.broadcasted_iota(jnp.int32, (nk * LANE, LANE), 1)
    for u in range(nsub):
        qt = n * nsub + u
        kt0 = jnp.maximum(qt - n_prev, 0)
        dist = (qt - kt0) * LANE + col - row
        kwin = k_ref[pl.ds(pl.multiple_of(kt0 * LANE, LANE), nk * LANE), :]
        o_parts, l_parts = [], []
        for h in range(4):
            p = h // 2
            qm = _pair_masked_q(qT_ref[u, p * LANE:(p + 1) * LANE, :], h)
            s = _dot(kwin[:, p * LANE:(p + 1) * LANE], qm)
            s = jnp.where(dist >= 0, jnp.where(dist <= max_dist, s, NEG), NEG)
            m = jnp.max(s, axis=0, keepdims=True)
            e = jnp.exp(s - m)
            den = jnp.sum(e, axis=0, keepdims=True)
            vb = _lane_tiles(vT_ref, kt0, nk, h * HEAD_DIM, HEAD_DIM)
            o_parts.append(_dot(vb, e.astype(BF16)) / den)
            l_parts.append(jnp.broadcast_to(m + jnp.log(den), (HEAD_DIM, LANE)))
        o_ref[u * LANE:(u + 1) * LANE, :] = jnp.concatenate(o_parts, axis=0).T.astype(o_ref.dtype)
        lse_ref[u * LANE:(u + 1) * LANE, :] = jnp.concatenate(l_parts, axis=0).T


def _band_call(oT, otok, *, q_chunk, k_chunk, v_chunk, max_dist):
    B, dil, nlt, _, _ = oT.shape
    L = nlt * LANE
    tqb = min(512, L)
    kern = functools.partial(_band_kernel, tqb=tqb, max_dist=max_dist)
    return pl.pallas_call(
        kern,
        grid=(B, dil, L // tqb),
        in_specs=[
            pl.BlockSpec((None, None, tqb // LANE, CH, LANE), lambda b, r, n: (b, r, n, q_chunk, 0)),
            pl.BlockSpec((None, None, L, CH), lambda b, r, n: (b, r, 0, k_chunk)),
            pl.BlockSpec((None, None, nlt, CH, LANE), lambda b, r, n: (b, r, 0, v_chunk, 0)),
        ],
        out_specs=[
            pl.BlockSpec((None, tqb, BRANCH_W), lambda b, r, n: (b, n, r)),
            pl.BlockSpec((None, tqb, BRANCH_W), lambda b, r, n: (b, n, r)),
        ],
        out_shape=[
            jax.ShapeDtypeStruct((B, L, dil * BRANCH_W), BF16),
            jax.ShapeDtypeStruct((B, L, dil * BRANCH_W), F32),
        ],
        compiler_params=_cparams(("parallel", "parallel", "parallel")),
        name="banded",
    )(oT, otok, oT)


def _moba_kernel(qT_ref, k_ref, vT_ref, A_ref, o_ref, kmean_ref, sel_ref, qm_ref, acc_ref, ml_ref, s_ref, *, tq):
    i = pl.program_id(1)
    nt = tq // LANE
    nblk = A_ref.shape[0]

    @pl.when(i == 0)
    def _():
        kmean_ref[...] = _dot(A_ref[...], k_ref[...])

    blk = lax.broadcasted_iota(jnp.int32, (nblk, tq), 0)
    row = lax.broadcasted_iota(jnp.int32, (tq, tq), 0)
    col = lax.broadcasted_iota(jnp.int32, (tq, tq), 1)
    gates, diag = [], []
    for h in range(4):
        p = h // 2
        qm = _pair_masked_q(_lane_tiles(qT_ref, 0, nt, p * LANE, LANE), h)
        qm_ref[h] = qm
        k1, k2, k3 = _split3(kmean_ref[:, p * LANE:(p + 1) * LANE])
        gates.append(_dot(k1, qm) + _dot(k2, qm) + _dot(k3, qm))
        kb = k_ref[pl.ds(pl.multiple_of(i * tq, tq), tq), p * LANE:(p + 1) * LANE]
        diag.append(_dot(kb, qm))
    for h in range(4):
        g = jnp.where(blk < i, gates[h], NEG)
        sel = jnp.zeros((nblk, tq), F32)
        for _r in range(MOBA_TOPK):
            mx = jnp.max(g, axis=0, keepdims=True)
            idx = jnp.min(jnp.where(g == mx, blk, nblk), axis=0, keepdims=True)
            hit = blk == idx
            sel = jnp.where(hit, 1.0, sel)
            g = jnp.where(hit, -jnp.inf, g)
        sel_ref[h] = jnp.where(blk < i, sel, 0.0)

        s = jnp.where(row <= col, diag[h], NEG)
        m = jnp.max(s, axis=0, keepdims=True)
        ml_ref[h:h + 1, :] = m
        acc_ref[h * HA:(h + 1) * HA, :] = _dot(
            _v_aug(_lane_tiles(vT_ref, i * nt, nt, h * HEAD_DIM, HEAD_DIM)), jnp.exp2(s - m).astype(BF16))

    last = jnp.maximum(i - 1, 0)

    def qk(t, slot):
        for h in range(4):
            p = h // 2
            kbj = k_ref[pl.ds(pl.multiple_of(t * tq, tq), tq), p * LANE:(p + 1) * LANE]
            s_ref[slot, h] = _dot(kbj, qm_ref[h])

    def update(t, valid, slot):
        for h in range(4):
            srow = sel_ref[h, pl.ds(t, 1), :] * valid
            vb = _v_aug(_lane_tiles(vT_ref, t * nt, nt, h * HEAD_DIM, HEAD_DIM))
            rows = slice(h * HA, (h + 1) * HA)
            m, acc = _online_cols(s_ref[slot, h], srow, ml_ref[h:h + 1, :], acc_ref[rows, :], vb)
            ml_ref[h:h + 1, :] = m
            acc_ref[rows, :] = acc

    qk(0, 0)

    def body(jj, c):
        t0 = 2 * jj
        qk(jnp.minimum(t0 + 1, last), 1)
        update(t0, 1.0, 0)
        qk(jnp.minimum(t0 + 2, last), 0)
        update(jnp.minimum(t0 + 1, last), (t0 + 1 < i).astype(F32), 1)
        return c

    lax.fori_loop(0, (i + 1) // 2, body, 0)
    outs = [acc_ref[h * HA:h * HA + HEAD_DIM, :] / acc_ref[h * HA + HEAD_DIM:h * HA + HEAD_DIM + 1, :]
            for h in range(4)]
    o_ref[...] = jnp.concatenate(outs, axis=0).T.astype(o_ref.dtype)


def _moba_call(oT, otok, A, *, q_chunk, k_chunk, v_chunk):
    B, _, nlt, _, _ = oT.shape
    S = nlt * LANE
    tq = MOBA_BLOCK
    nblk = S // MOBA_BLOCK
    kern = functools.partial(_moba_kernel, tq=tq)
    return pl.pallas_call(
        kern,
        grid=(B, S // tq),
        in_specs=[
            pl.BlockSpec((None, None, tq // LANE, CH, LANE), lambda b, i: (b, 0, i, q_chunk, 0)),
            pl.BlockSpec((None, None, S, CH), lambda b, i: (b, 0, 0, k_chunk)),
            pl.BlockSpec((None, None, nlt, CH, LANE), lambda b, i: (b, 0, 0, v_chunk, 0)),
            pl.BlockSpec((nblk, S), lambda b, i: (0, 0)),
        ],
        out_specs=pl.BlockSpec((None, tq, BRANCH_W), lambda b, i: (b, i, 0)),
        out_shape=jax.ShapeDtypeStruct((B, S, BRANCH_W), BF16),
        scratch_shapes=[pltpu.VMEM((nblk, CH), F32), pltpu.VMEM((4, nblk, tq), F32),
                        pltpu.VMEM((4, LANE, tq), BF16), pltpu.VMEM((4 * HA, tq), F32), pltpu.VMEM((8, tq), F32),
                        pltpu.VMEM((2, 4, tq, tq), F32)],
        compiler_params=_cparams(("parallel", "arbitrary")),
        name="moba",
    )(oT, otok, oT, A)


def _compress_kernel(x_ref, w1_ref, pe_ref, w1f_ref, w2_ref, gk_ref, G_ref, o_ref, oT_ref, acc_ref):
    l = pl.program_id(1)

    @pl.when(l == 0)
    def _():
        acc_ref[...] = jnp.zeros_like(acc_ref)

    x = x_ref[...]
    w_hi, w_lo = _split2(w1_ref[...])
    acc_ref[...] += _dot(x, w_hi) + _dot(x, w_lo)

    @pl.when(l == pl.num_programs(1) - 1)
    def _():
        nc = acc_ref.shape[0]
        r = acc_ref[...]
        bias = jnp.dot(pe_ref[...], w1f_ref[...], preferred_element_type=F32,
                       precision=lax.Precision.HIGHEST)
        p1 = jnp.concatenate([r[:, 0:64], r[:, 128:192]], axis=1)
        p2 = jnp.concatenate([r[:, 64:128], r[:, 192:256]], axis=1)
        hid = p1 + pltpu.roll(p2, nc - 1, 0) + bias[0:1, :]
        hid = hid * jax.nn.sigmoid(hid)
        comp = jnp.dot(hid, w2_ref[...], preferred_element_type=F32, precision=lax.Precision.HIGHEST)
        ss = _dot((comp * comp).astype(BF16), G_ref[...])
        lane = lax.broadcasted_iota(jnp.int32, comp.shape, 1)
        inv = jnp.where(lane < HEAD_DIM, lax.rsqrt(ss * (1.0 / HEAD_DIM) + EPS), 1.0)
        comp = comp * inv * gk_ref[...]
        rowi = lax.broadcasted_iota(jnp.int32, comp.shape, 0)
        comp = jnp.where(rowi < nc - 1, comp, 0.0)
        o_ref[...] = comp.astype(o_ref.dtype)
        oT_ref[...] = comp.T.astype(oT_ref.dtype)


def _compress_call(otok, w1blk, pe2, w1f, w2blk, gk, G128, *, kv_chunk):
    B, _, S, width = otok.shape
    nc = S // NSA_CMP_STRIDE
    xv = otok.reshape(B, nc, NSA_CMP_STRIDE * width)
    cpb = width // LANE
    kvb = kv_chunk * (CH // LANE)
    return pl.pallas_call(
        _compress_kernel,
        grid=(B, NSA_CMP_STRIDE),
        in_specs=[
            pl.BlockSpec((None, nc, LANE), lambda b, l: (b, 0, l * cpb + kvb)),
            pl.BlockSpec((None, LANE, CH), lambda b, l: (l, 0, 0)),
            pl.BlockSpec((8, 2 * NSA_CMP_LEN * HEAD_DIM), lambda b, l: (0, 0)),
            pl.BlockSpec((2 * NSA_CMP_LEN * HEAD_DIM, LANE), lambda b, l: (0, 0)),
            pl.BlockSpec((LANE, LANE), lambda b, l: (0, 0)),
            pl.BlockSpec((1, LANE), lambda b, l: (0, 0)),
            pl.BlockSpec((LANE, LANE), lambda b, l: (0, 0)),
        ],
        out_specs=[
            pl.BlockSpec((None, nc, LANE), lambda b, l: (b, 0, 0)),
            pl.BlockSpec((None, LANE, nc), lambda b, l: (b, 0, 0)),
        ],
        out_shape=[
            jax.ShapeDtypeStruct((B, nc, LANE), BF16),
            jax.ShapeDtypeStruct((B, LANE, nc), BF16),
        ],
        scratch_shapes=[pltpu.VMEM((nc, CH), F32)],
        compiler_params=_cparams(("parallel", "arbitrary")),
        name="nsa_compress",
    )(xv, w1blk, pe2, w1f, w2blk, gk, G128)


def _nsa_kernel(qT_ref, k_ref, vT_ref, kc_ref, kcT_ref, ov_ref, o_ref, sel_ref, qm_ref, acc_ref, ml_ref, s_ref,
                *, tq):
    i = pl.program_id(1)
    nt = tq // LANE
    nq = 4 * tq
    nc = kc_ref.shape[0]
    nsel = ov_ref.shape[0]
    spb = tq // NSA_SEL_BLOCK

    zeros = jnp.zeros((HEAD_DIM, tq), BF16)
    q_heads = [_lane_tiles(qT_ref, 0, nt, h * HEAD_DIM, HEAD_DIM) for h in range(4)]
    qm_lo = jnp.concatenate([jnp.concatenate([q, zeros], axis=0) for q in q_heads], axis=1)
    qm_hi = jnp.concatenate([jnp.concatenate([zeros, q], axis=0) for q in q_heads], axis=1)
    qpos = i * tq + lax.broadcasted_iota(jnp.int32, (1, tq), 1)
    qpos4 = jnp.concatenate([qpos] * 4, axis=1)

    zc = _dot(kc_ref[...], qm_lo)
    c_end = lax.broadcasted_iota(jnp.int32, (nc, nq), 0) * NSA_CMP_STRIDE + (NSA_CMP_LEN - 1)
    cmask = c_end <= qpos4
    zc = jnp.where(cmask, zc, NEG)
    e = jnp.where(cmask, jnp.exp2(zc - jnp.max(zc, axis=0, keepdims=True)), 0.0)
    pc = e / jnp.maximum(jnp.sum(e, axis=0, keepdims=True), 1.0)
    o_cmp = _dot(kcT_ref[HEAD_DIM:2 * HEAD_DIM, :], pc.astype(BF16))

    psum = pc[:, 0:tq] + pc[:, tq:2 * tq] + pc[:, 2 * tq:3 * tq] + pc[:, 3 * tq:4 * tq]
    p_hi, p_lo = _split2(psum)
    imp = _dot(ov_ref[...], p_hi) + _dot(ov_ref[...], p_lo)
    nid = lax.broadcasted_iota(jnp.int32, (nsel, tq), 0)
    cur = qpos // NSA_SEL_BLOCK
    imp = jnp.where(nid == 0, BIG, imp)
    imp = jnp.where(nid == cur, BIG, imp)
    imp = jnp.where(nid == cur - 1, BIG, imp)
    imp = jnp.where(nid > cur, NEG, imp)
    sel = jnp.zeros((nsel, tq), F32)
    for _r in range(min(NSA_SEL_TOPK, nsel)):
        mx = jnp.max(imp, axis=0, keepdims=True)
        idx = jnp.min(jnp.where(imp == mx, nid, nsel), axis=0, keepdims=True)
        hit = nid == idx
        sel = jnp.where(hit, 1.0, sel)
        imp = jnp.where(hit, -jnp.inf, imp)
    sel = jnp.where(nid <= cur, sel, 0.0)
    sel_ref[...] = sel
    for h in range(4):
        qm_ref[h] = qm_lo[:, h * tq:(h + 1) * tq]

    def qk(t, slot):
        kb = k_ref[pl.ds(pl.multiple_of(t * tq, tq), tq), LANE:2 * LANE]
        for h in range(4):
            s_ref[slot, h] = _dot(kb, qm_ref[h])

    kpos_d = i * tq + lax.broadcasted_iota(jnp.int32, (tq, tq), 0)
    qk(i, 0)
    vb_d = _v_aug(_lane_tiles(vT_ref, i * nt, nt, 0, HEAD_DIM))
    srows_d = [sel_ref[pl.ds(i * spb + u, 1), :] for u in range(spb)]
    for h in range(4):
        sc = s_ref[0, h]
        s = jnp.concatenate(
            [jnp.where(srows_d[u] > 0.0, sc[u * NSA_SEL_BLOCK:(u + 1) * NSA_SEL_BLOCK, :], NEG)
             for u in range(spb)], axis=0)
        s = jnp.where(kpos_d <= qpos, s, NEG)
        m = jnp.max(s, axis=0, keepdims=True)
        lanes = slice(h * tq, (h + 1) * tq)
        ml_ref[0:1, lanes] = m
        acc_ref[:, lanes] = _dot(vb_d, jnp.exp2(s - m).astype(BF16))

    last = jnp.maximum(i - 1, 0)

    def update(t, valid, slot):
        vb = _v_aug(_lane_tiles(vT_ref, t * nt, nt, 0, HEAD_DIM))
        srows = [sel_ref[pl.ds(t * spb + u, 1), :] * valid for u in range(spb)]
        for h in range(4):
            lanes = slice(h * tq, (h + 1) * tq)
            sc = s_ref[slot, h]
            subs = [sc[u * NSA_SEL_BLOCK:(u + 1) * NSA_SEL_BLOCK, :] for u in range(spb)]
            tmax = None
            for u in range(spb):
                mu = jnp.where(srows[u] > 0.0, jnp.max(subs[u], axis=0, keepdims=True), NEG)
                tmax = mu if tmax is None else jnp.maximum(tmax, mu)
            m_old = ml_ref[0:1, lanes]
            m_new = jnp.maximum(m_old, tmax)
            p = jnp.concatenate(
                [jnp.exp2(subs[u] - jnp.where(srows[u] > 0.0, m_new, BIGPOS)).astype(BF16) for u in range(spb)],
                axis=0)
            ml_ref[0:1, lanes] = m_new
            acc_ref[:, lanes] = jnp.exp2(m_old - m_new) * acc_ref[:, lanes] + _dot(vb, p)

    qk(0, 0)

    def body(jj, c):
        t0 = 2 * jj
        qk(jnp.minimum(t0 + 1, last), 1)
        update(t0, 1.0, 0)
        qk(jnp.minimum(t0 + 2, last), 0)
        update(jnp.minimum(t0 + 1, last), (t0 + 1 < i).astype(F32), 1)
        return c

    lax.fori_loop(0, (i + 1) // 2, body, 0)
    o_sel = acc_ref[0:HEAD_DIM, :] / acc_ref[HEAD_DIM:HEAD_DIM + 1, :]

    n_prev = -(-(NSA_WINDOW - 1) // tq)
    nk = n_prev + 1
    kt0 = jnp.maximum(i - n_prev, 0)
    kw = k_ref[pl.ds(pl.multiple_of(kt0 * tq, tq), nk * tq), LANE:2 * LANE]
    sw = _dot(kw, qm_hi)
    dist = qpos4 - (kt0 * tq + lax.broadcasted_iota(jnp.int32, (nk * tq, nq), 0))
    sw = jnp.where(dist >= 0, jnp.where(dist <= NSA_WINDOW - 1, sw, NEG), NEG)
    mw = jnp.max(sw, axis=0, keepdims=True)
    ew = jnp.exp2(sw - mw)
    dw = jnp.sum(ew, axis=0, keepdims=True)
    vwin = _lane_tiles(vT_ref, kt0 * nt, nk * nt, HEAD_DIM, HEAD_DIM)
    o_win = _dot(vwin, ew.astype(BF16)) / dw

    gates = _lane_tiles(vT_ref, i * nt, nt, 2 * HEAD_DIM, 16).astype(F32)
    outs = []
    for h in range(4):
        sl = slice(h * tq, (h + 1) * tq)
        outs.append(gates[3 * h:3 * h + 1, :] * o_cmp[:, sl]
                    + gates[3 * h + 1:3 * h + 2, :] * o_sel[:, sl]
                    + gates[3 * h + 2:3 * h + 3, :] * o_win[:, sl])
    o_ref[...] = jnp.concatenate(outs, axis=0).T.astype(o_ref.dtype)


def _nsa_call(oT, otok, kc, kcT, ovT, *, q_chunk, kv_chunk, v_chunk):
    B, _, nlt, _, _ = oT.shape
    S = nlt * LANE
    tq = 256
    nc = S // NSA_CMP_STRIDE
    nsel = S // NSA_SEL_BLOCK
    kern = functools.partial(_nsa_kernel, tq=tq)
    return pl.pallas_call(
        kern,
        grid=(B, S // tq),
        in_specs=[
            pl.BlockSpec((None, None, tq // LANE, CH, LANE), lambda b, i: (b, 0, i, q_chunk, 0)),
            pl.BlockSpec((None, None, S, CH), lambda b, i: (b, 0, 0, kv_chunk)),
            pl.BlockSpec((None, None, nlt, CH, LANE), lambda b, i: (b, 0, 0, v_chunk, 0)),
            pl.BlockSpec((None, nc, LANE), lambda b, i: (b, 0, 0)),
            pl.BlockSpec((None, LANE, nc), lambda b, i: (b, 0, 0)),
            pl.BlockSpec((nsel, nc), lambda b, i: (0, 0)),
        ],
        out_specs=pl.BlockSpec((None, tq, BRANCH_W), lambda b, i: (b, i, 0)),
        out_shape=jax.ShapeDtypeStruct((B, S, BRANCH_W), BF16),
        scratch_shapes=[pltpu.VMEM((nsel, tq), F32), pltpu.VMEM((4, LANE, tq), BF16),
                        pltpu.VMEM((HA, 4 * tq), F32), pltpu.VMEM((8, 4 * tq), F32),
                        pltpu.VMEM((2, 4, tq, tq), F32)],
        compiler_params=_cparams(("parallel", "arbitrary")),
        name="nsa",
    )(oT, otok, oT, kc, kcT, ovT)


def _epi_kernel(x_ref, ng_ref, ya_ref, yc_ref, yd_ref, ob0_ref, ob1_ref, ob2_ref, l0_ref, l1_ref, l2_ref,
                za_ref, zb_ref, zc_ref, zd_ref, wmg_ref, wbr_ref, wout_ref, o_ref):
    x = x_ref[...]
    ms = jnp.mean(x * x, axis=-1, keepdims=True)
    xn = (x * lax.rsqrt(ms + EPS) * ng_ref[...]).astype(BF16)

    l0, l1, l2 = l0_ref[...], l1_ref[...], l2_ref[...]
    mx = jnp.maximum(jnp.maximum(l0, l1), l2)
    e0, e1, e2 = jnp.exp(l0 - mx), jnp.exp(l1 - mx), jnp.exp(l2 - mx)
    yb = (e0 * ob0_ref[...].astype(F32) + e1 * ob1_ref[...].astype(F32)
          + e2 * ob2_ref[...].astype(F32)) / (e0 + e1 + e2)

    ys = (ya_ref[...].astype(F32), yb, yc_ref[...].astype(F32), yd_ref[...].astype(F32))
    zs = (za_ref, zb_ref, zc_ref, zd_ref)
    merged = None
    for i in range(4):
        z = zs[i][...].astype(F32)
        gated = (ys[i] * (z * jax.nn.sigmoid(z))).astype(BF16)
        br = _dot(gated, wbr_ref[i])
        mg = _dot(xn, wmg_ref[:, i * D_MODEL:(i + 1) * D_MODEL])
        term = jax.nn.sigmoid(mg) * br
        merged = term if merged is None else merged + term
    o_ref[...] = x + _dot(merged.astype(BF16), wout_ref[...])


def _epi_call(x2, ng, ya, yc, yd, obs, lses, otok2, wmg, wbr, wout, *, z_chunks, tm=512):
    T = x2.shape[0]
    row = lambda i: (i, 0)
    full2 = lambda i: (0, 0)
    yspec = pl.BlockSpec((tm, BRANCH_W), row)
    zspecs = [pl.BlockSpec((tm, CH), (lambda i, c=c: (i, c))) for c in z_chunks]
    return pl.pallas_call(
        _epi_kernel,
        grid=(T // tm,),
        in_specs=[pl.BlockSpec((tm, D_MODEL), row), pl.BlockSpec((1, D_MODEL), full2)]
        + [yspec] * 3 + [yspec] * 3 + [yspec] * 3 + zspecs
        + [pl.BlockSpec((D_MODEL, 4 * D_MODEL), full2),
           pl.BlockSpec((4, BRANCH_W, D_MODEL), lambda i: (0, 0, 0)),
           pl.BlockSpec((D_MODEL, D_MODEL), full2)],
        out_specs=pl.BlockSpec((tm, D_MODEL), row),
        out_shape=jax.ShapeDtypeStruct((T, D_MODEL), F32),
        compiler_params=_cparams(("parallel",)),
        name="epilogue",
    )(x2, ng, ya, yc, yd, *obs, *lses, otok2, otok2, otok2, otok2, wmg, wbr, wout)


TOK_AK, TOK_AZ, TOK_BK, TOK_BZ, TOK_CK, TOK_CZ, TOK_DKV, TOK_DZ = range(8)
T_AQ, T_AV, T_BQ, T_BV, T_CQ, T_CV, T_DQ, T_DX = range(8)
SCALE = 1.0 / math.sqrt(HEAD_DIM)

_PLAIN = (False, False, 1.0, False)
_MAIN_TOK_KINDS = ((False, False), (False, False), (True, True), (False, False),
                   (True, True), (False, False), (True, True), (False, False))
_MAIN_T_KINDS = (
    ((False, False, SCALE, False),) * 4,
    (_PLAIN,) * 4,
    ((True, True, SCALE, False),) * 4,
    (_PLAIN,) * 4,
    ((True, True, SCALE * LOG2E, False),) * 4,
    (_PLAIN,) * 4,
    ((True, True, SCALE * LOG2E, False),) * 4,
    (_PLAIN, _PLAIN, (False, False, 1.0, True), _PLAIN),
)
_DIL_TOK_KINDS = ((True, True),)
_DIL_T_KINDS = (((True, True, SCALE, False),) * 4, (_PLAIN,) * 4)


def _rope_tables(S, dil):
    L = S // dil
    pos = (jnp.arange(dil, dtype=jnp.int32)[:, None] + dil * jnp.arange(L, dtype=jnp.int32)[None, :]).astype(F32)
    inv = ROPE_THETA ** (-jnp.arange(0, ROT_DIM, 2, dtype=F32) / ROT_DIM)
    ang = pos[:, :, None] * inv[None, None, :]
    cos, sin = jnp.cos(ang), jnp.sin(ang)
    one = jnp.ones((dil, L, HEAD_DIM - ROT_DIM), F32)
    zero8 = jnp.zeros((dil, L, 8), F32)
    zero = jnp.zeros_like(one)
    c_head = jnp.concatenate([cos, cos, one], axis=-1)
    s1_head = jnp.concatenate([zero8, sin, zero], axis=-1)
    s2_head = jnp.concatenate([-sin, zero8, zero], axis=-1)
    rtok = jnp.stack([jnp.tile(t, (1, 1, LANE // HEAD_DIM)) for t in (c_head, s1_head, s2_head)], axis=1)
    rT = jnp.stack([cos.transpose(0, 2, 1), sin.transpose(0, 2, 1)], axis=1)
    return rtok, rT


def _tok_params(rows):
    out = []
    for nf, rf, gains in rows:
        nrow = jnp.concatenate([jnp.full((HEAD_DIM,), float(f), F32) for f in nf])
        rrow = jnp.concatenate([jnp.full((HEAD_DIM,), float(f), F32) for f in rf])
        grow = jnp.concatenate([g.astype(F32) for g in gains])
        out.append(jnp.concatenate([jnp.stack([nrow, rrow, grow]), jnp.zeros((5, CH), F32)], axis=0))
    return jnp.stack(out)


def kernel(x, norm_g, w_in, qk_g, cmp_pe, cmp_w1, cmp_w2, w_branch, w_out):
    B, S, _ = x.shape
    T = B * S
    o = COL_OFF
    ones = jnp.ones((HEAD_DIM,), F32)

    r = np.arange(CH)
    G = jnp.asarray((r[:, None] // HEAD_DIM == r[None, :] // HEAD_DIM).astype(np.float32), BF16)
    G128 = G[:LANE, :LANE]
    t = np.arange(256)
    U = jnp.asarray((t[None, :] > t[:, None]).astype(np.float32), BF16)
    nblk = S // MOBA_BLOCK
    A = jnp.asarray(np.repeat(np.eye(nblk, dtype=np.float32), MOBA_BLOCK, axis=1) / MOBA_BLOCK, BF16)
    nc, nsel = S // NSA_CMP_STRIDE, S // NSA_SEL_BLOCK
    cs = np.arange(nc) * NSA_CMP_STRIDE
    ss = np.arange(nsel) * NSA_SEL_BLOCK
    ov = ((cs[None, :] < ss[:, None] + NSA_SEL_BLOCK) & (cs[None, :] + NSA_CMP_LEN > ss[:, None]))
    ov[:, nc - 1] = False
    ovT = jnp.asarray(ov.astype(np.float32), BF16)
    ropes = {dil: _rope_tables(S, dil) for _, dil in DIL_PAIRS}

    for l in range(DEPTH):
        W = w_in[l]
        col = lambda a, b: W[:, a:b]
        dkv = o[13]
        wtok = jnp.concatenate([
            col(o[1], o[2]), col(o[3], o[4]), col(o[5], o[5] + CH), col(o[7], o[8]),
            col(o[9], o[10]), col(o[11], o[12]),
            col(dkv, dkv + 64), col(dkv + 64, dkv + 128), col(dkv + 128, dkv + 192), col(dkv + 256, dkv + 320),
            col(o[15], o[16])], axis=1).astype(BF16)
        wT = jnp.concatenate([
            col(o[0], o[1]), col(o[2], o[3]), col(o[4], o[4] + CH), col(o[6], o[6] + CH),
            col(o[8], o[9]), col(o[10], o[11]), col(o[12], o[13]),
            col(dkv + 192, dkv + 256), col(dkv + 320, dkv + 384), col(o[14], o[15]),
            jnp.zeros((D_MODEL, CH - 140), F32)], axis=1).T.astype(BF16)
        g = qk_g[l]
        tokp = _tok_params([
            ((0,) * 4, (0,) * 4, (ones,) * 4), ((0,) * 4, (0,) * 4, (ones,) * 4),
            ((1,) * 4, (1,) * 4, (g[1],) * 4), ((0,) * 4, (0,) * 4, (ones,) * 4),
            ((1,) * 4, (1,) * 4, (g[3],) * 4), ((0,) * 4, (0,) * 4, (ones,) * 4),
            ((0, 0, 1, 1), (1, 0, 1, 1), (ones, ones, g[6], g[7])), ((0,) * 4, (0,) * 4, (ones,) * 4)])
        gT = jnp.concatenate([jnp.tile(ones, 8), jnp.tile(g[0], 4), jnp.tile(ones, 4), jnp.tile(g[2], 4),
                              jnp.tile(ones, 4), jnp.tile(g[4], 4), jnp.tile(ones, 4)])[:, None]
        ng = norm_g[l][None, :]
        rtok, rT = ropes[1]
        otok, oT = _proj_call(x, ng, wtok, wT, tokp, gT, rtok, rT, G,
                              tok_kinds=_MAIN_TOK_KINDS, T_kinds=_MAIN_T_KINDS, tm=512)

        ya = _sb_call(oT, otok, U, q_chunk=T_AQ, k_chunk=TOK_AK, v_chunk=T_AV)

        obs, lses = [], []
        for gi, (window, dil) in enumerate(DIL_PAIRS):
            if dil == 1:
                ob, lse = _band_call(oT, otok, q_chunk=T_BQ, k_chunk=TOK_BK, v_chunk=T_BV, max_dist=window // dil)
            else:
                wtok_g = col(o[5] + gi * CH, o[5] + (gi + 1) * CH).astype(BF16)
                wT_g = jnp.concatenate([col(o[4] + gi * CH, o[4] + (gi + 1) * CH),
                                        col(o[6] + gi * CH, o[6] + (gi + 1) * CH)], axis=1).T.astype(BF16)
                tokp_g = _tok_params([((1,) * 4, (1,) * 4, (g[1],) * 4)])
                gT_g = jnp.concatenate([jnp.tile(g[0], 4), jnp.tile(ones, 4)])[:, None]
                rtok_g, rT_g = ropes[dil]
                L = S // dil
                otok_g, oT_g = _proj_call(x.reshape(B, L, dil * D_MODEL), ng, wtok_g, wT_g, tokp_g, gT_g,
                                          rtok_g, rT_g, G, tok_kinds=_DIL_TOK_KINDS, T_kinds=_DIL_T_KINDS,
                                          tm=min(512, L))
                ob, lse = _band_call(oT_g, otok_g, q_chunk=0, k_chunk=0, v_chunk=1, max_dist=window // dil)
            obs.append(ob.reshape(T, BRANCH_W))
            lses.append(lse.reshape(T, BRANCH_W))

        yc = _moba_call(oT, otok, A, q_chunk=T_CQ, k_chunk=TOK_CK, v_chunk=T_CV)

        w1 = cmp_w1[l].reshape(2, NSA_CMP_LEN, HEAD_DIM, HEAD_DIM)
        z64 = jnp.zeros((NSA_CMP_STRIDE, HEAD_DIM, HEAD_DIM), F32)
        w1blk = jnp.concatenate([
            jnp.concatenate([w1[0, :16], w1[0, 16:], z64, z64], axis=2),
            jnp.concatenate([z64, z64, w1[1, :16], w1[1, 16:]], axis=2)], axis=1)
        pe2 = jnp.concatenate([cmp_pe[l].reshape(1, -1), jnp.zeros((7, 2 * NSA_CMP_LEN * HEAD_DIM), F32)], axis=0)
        zf = jnp.zeros((NSA_CMP_LEN * HEAD_DIM, HEAD_DIM), F32)
        w1f = jnp.concatenate([jnp.concatenate([cmp_w1[l, 0], zf], axis=1),
                               jnp.concatenate([zf, cmp_w1[l, 1]], axis=1)], axis=0)
        z2 = jnp.zeros((HEAD_DIM, HEAD_DIM), F32)
        w2blk = jnp.concatenate([jnp.concatenate([cmp_w2[l, 0], z2], axis=1),
                                 jnp.concatenate([z2, cmp_w2[l, 1]], axis=1)], axis=0)
        gk = jnp.concatenate([g[5], ones])[None, :]
        kc, kcT = _compress_call(otok, w1blk, pe2, w1f, w2blk, gk, G128, kv_chunk=TOK_DKV)
        yd = _nsa_call(oT, otok, kc, kcT, ovT, q_chunk=T_DQ, kv_chunk=TOK_DKV, v_chunk=T_DX)

        wmg = col(o[16], o[17]).astype(BF16)
        x2 = _epi_call(x.reshape(T, D_MODEL), ng, ya.reshape(T, BRANCH_W), yc.reshape(T, BRANCH_W),
                       yd.reshape(T, BRANCH_W), obs, lses, otok.reshape(T, -1), wmg,
                       w_branch[l].astype(BF16), w_out[l].astype(BF16),
                       z_chunks=(TOK_AZ, TOK_BZ, TOK_CZ, TOK_DZ))
        x = x2.reshape(B, S, D_MODEL)
    return x
```

```python
import functools
import math

import numpy as np
import jax
import jax.numpy as jnp
from jax import lax
from jax.experimental import pallas as pl
from jax.experimental.pallas import tpu as pltpu

F32 = jnp.float32
BF16 = jnp.bfloat16

D_MODEL = 1024
DEPTH = 4
HEAD_DIM = 64
ROT_DIM = 16
ROPE_THETA = 500000.0
EPS = 1e-6
NEG = -1e30
BIG = 1e9
BRANCH_W = 256
DIL_PAIRS = ((128, 1), (512, 4), (2048, 16))
MOBA_BLOCK = 256
MOBA_TOPK = 3
NSA_CMP_LEN = 32
NSA_CMP_STRIDE = 16
NSA_SEL_BLOCK = 64
NSA_SEL_TOPK = 16
NSA_WINDOW = 512
COL_SIZES = (256, 256, 256, 256, 768, 768, 768, 256, 256, 256, 256, 256, 256, 384, 12, 256, 4096)
COL_OFF = tuple(int(v) for v in np.concatenate([[0], np.cumsum(COL_SIZES)]))

LANE = 128
CH = 256
VMEM_LIMIT = 56 * 1024 * 1024


def _cparams(sem):
    return pltpu.CompilerParams(dimension_semantics=sem, vmem_limit_bytes=VMEM_LIMIT)


def _dot(a, b):
    return jnp.dot(a, b, preferred_element_type=F32)


def _dot_nt(a, b):
    return lax.dot_general(a, b, (((1,), (1,)), ((), ())), preferred_element_type=F32)


def _split2(x):
    hi = x.astype(BF16)
    lo = (x - hi.astype(F32)).astype(BF16)
    return hi, lo


def _split3(x):
    hi = x.astype(BF16)
    r = x - hi.astype(F32)
    mid = r.astype(BF16)
    lo = (r - mid.astype(F32)).astype(BF16)
    return hi, mid, lo


def _proj_kernel(x_ref, ng_ref, wtok_ref, wT_ref, tokp_ref, gT_ref, rtok_ref, rT_ref, G_ref,
                 omain_ref, og1_ref, og2_ref, okv_ref, oT_ref, stage_ref, *, tok_specs, T_kinds, tm):
    x = x_ref[...]
    ms = jnp.mean(x * x, axis=-1, keepdims=True)
    xn = (x * lax.rsqrt(ms + EPS) * ng_ref[...]).astype(BF16)

    dsts = (omain_ref, og1_ref, og2_ref, okv_ref)
    for c, (has_norm, has_rope, outs) in enumerate(tok_specs):
        y = _dot(xn, wtok_ref[:, c * CH:(c + 1) * CH])
        prm = tokp_ref[c]
        if has_norm:
            ss = _dot((y * y).astype(BF16), G_ref[...])
            inv = lax.rsqrt(ss * (1.0 / HEAD_DIM) + EPS)
            y = y * jnp.where(prm[0:1, :] > 0.0, inv, 1.0)
        y = y * prm[2:3, :]
        if has_rope:
            halves = []
            for hf in range(CH // LANE):
                yh = y[:, hf * LANE:(hf + 1) * LANE]
                rf = prm[1:2, hf * LANE:(hf + 1) * LANE]
                cc = jnp.where(rf > 0.0, rtok_ref[0], 1.0)
                s1 = rtok_ref[1] * rf
                s2 = rtok_ref[2] * rf
                halves.append(yh * cc + pltpu.roll(yh, 8, 1) * s1 + pltpu.roll(yh, LANE - 8, 1) * s2)
            y = jnp.concatenate(halves, axis=1)
        staged = False
        for dst, dil, col, lanes, rstride in outs:
            if dil == 1:
                dsts[dst][:, col:col + lanes] = y[:, :lanes].astype(BF16)
                continue
            if not staged:
                for hf in range(CH // LANE):
                    stage_ref[hf] = y[:, hf * LANE:(hf + 1) * LANE]
                staged = True
            for r in range(dil):
                for hf in range(lanes // LANE):
                    blk = stage_ref[hf, pl.ds(r, tm // dil, stride=dil), :]
                    c0 = r * rstride + col + hf * LANE
                    dsts[dst][:, c0:c0 + LANE] = blk.astype(BF16)

    cosT = rT_ref[0]
    sinT = rT_ref[1]
    for c, heads in enumerate(T_kinds):
        y = _dot_nt(wT_ref[c * CH:(c + 1) * CH, :], xn)
        for h, (nrm, rope, scale, sigm) in enumerate(heads):
            r0 = c * CH + h * HEAD_DIM
            yh = y[h * HEAD_DIM:(h + 1) * HEAD_DIM, :]
            if nrm:
                msq = jnp.mean(yh * yh, axis=0, keepdims=True)
                yh = yh * lax.rsqrt(msq + EPS) * gT_ref[r0:r0 + HEAD_DIM, :]
            if rope:
                x1 = yh[0:8, :]
                x2 = yh[8:16, :]
                yh = jnp.concatenate([x1 * cosT - x2 * sinT, x2 * cosT + x1 * sinT, yh[16:, :]], axis=0)
            if scale != 1.0:
                yh = yh * scale
            if sigm:
                yh = jax.nn.sigmoid(yh)
            yb = yh.astype(BF16)
            for t in range(tm // LANE):
                oT_ref[t, r0:r0 + HEAD_DIM, :] = yb[:, t * LANE:(t + 1) * LANE]


def _proj_call(x, ng, wtok, wT, tokp, gT, rtok, rT, G, *, tok_specs, T_kinds, n_main, tm=512):
    B, S, _ = x.shape
    ntok, nT = len(tok_specs), len(T_kinds)
    d1, d2 = DIL_PAIRS[1][1], DIL_PAIRS[2][1]
    kern = functools.partial(_proj_kernel, tok_specs=tok_specs, T_kinds=T_kinds, tm=tm)
    const2 = lambda b, n: (0, 0)
    return pl.pallas_call(
        kern,
        grid=(B, S // tm),
        in_specs=[
            pl.BlockSpec((None, tm, D_MODEL), lambda b, n: (b, n, 0)),
            pl.BlockSpec((1, D_MODEL), const2),
            pl.BlockSpec((D_MODEL, ntok * CH), const2),
            pl.BlockSpec((nT * CH, D_MODEL), const2),
            pl.BlockSpec((ntok, 8, CH), lambda b, n: (0, 0, 0)),
            pl.BlockSpec((nT * CH, 1), const2),
            pl.BlockSpec((3, tm, LANE), lambda b, n: (0, n, 0)),
            pl.BlockSpec((2, 8, tm), lambda b, n: (0, 0, n)),
            pl.BlockSpec((CH, CH), const2),
        ],
        out_specs=[
            pl.BlockSpec((None, tm, n_main * CH), lambda b, n: (b, n, 0)),
            pl.BlockSpec((None, tm // d1, d1 * 3 * CH), lambda b, n: (b, n, 0)),
            pl.BlockSpec((None, tm // d2, d2 * 3 * CH), lambda b, n: (b, n, 0)),
            pl.BlockSpec((None, tm // NSA_CMP_STRIDE, NSA_CMP_STRIDE * LANE), lambda b, n: (b, n, 0)),
            pl.BlockSpec((None, tm // LANE, nT * CH, LANE), lambda b, n: (b, n, 0, 0)),
        ],
        out_shape=[
            jax.ShapeDtypeStruct((B, S, n_main * CH), BF16),
            jax.ShapeDtypeStruct((B, S // d1, d1 * 3 * CH), BF16),
            jax.ShapeDtypeStruct((B, S // d2, d2 * 3 * CH), BF16),
            jax.ShapeDtypeStruct((B, S // NSA_CMP_STRIDE, NSA_CMP_STRIDE * LANE), BF16),
            jax.ShapeDtypeStruct((B, S // LANE, nT * CH, LANE), BF16),
        ],
        scratch_shapes=[pltpu.VMEM((CH // LANE, tm, LANE), F32)],
        compiler_params=_cparams(("parallel", "parallel")),
        name="proj",
    )(x, ng, wtok, wT, tokp, gT, rtok, rT, G)


def _pair_masked_q(q_pair, h):
    rid = lax.broadcasted_iota(jnp.int32, q_pair.shape, 0)
    lo = (h % 2) * HEAD_DIM
    keep = jnp.where(rid >= lo, jnp.where(rid < lo + HEAD_DIM, 1.0, 0.0), 0.0).astype(BF16)
    return q_pair * keep


def _lane_tiles(ref, t0, nt, r0, nr):
    return jnp.concatenate([ref[t0 + t, r0:r0 + nr, :] for t in range(nt)], axis=1)


BIGPOS = 1e30
ONES_ROWS = 16
HA = HEAD_DIM + ONES_ROWS
LOG2E = 1.4426950408889634


def _v_aug(vb):
    return jnp.concatenate([vb, jnp.ones((ONES_ROWS, vb.shape[1]), BF16)], axis=0)


def _online_cols(s, colsel, m_old, acc_old, vb_aug):
    tmax = jnp.max(s, axis=0, keepdims=True)
    m_new = jnp.where(colsel > 0.0, jnp.maximum(m_old, tmax), m_old)
    m_use = jnp.where(colsel > 0.0, m_new, BIGPOS)
    p = jnp.exp2(s - m_use).astype(BF16)
    acc = jnp.exp2(m_old - m_new) * acc_old + _dot(vb_aug, p)
    return m_new, acc


SB_LOG_CUTOFF = -100.0


def _sb_kernel(qT_ref, k_ref, vT_ref, U_ref, o_ref, qm_ref, acc_ref, carry_ref, *, tq):
    i = pl.program_id(1)
    nt = tq // LANE
    row = lax.broadcasted_iota(jnp.int32, (tq, tq), 0)
    col = lax.broadcasted_iota(jnp.int32, (tq, tq), 1)
    past = row < col
    for h in range(4):
        p = h // 2
        qm_ref[h] = _pair_masked_q(_lane_tiles(qT_ref, 0, nt, p * LANE, LANE), h)
    acc_ref[...] = jnp.zeros_like(acc_ref)
    carry_ref[...] = jnp.zeros_like(carry_ref)

    def tile(j, masked):
        scores, logsig, laters = [], [], []
        for h in range(4):
            p = h // 2
            kb = k_ref[pl.ds(pl.multiple_of(j * tq, tq), tq), p * LANE:(p + 1) * LANE]
            scores.append(_dot(kb, qm_ref[h]))
        worst = None
        for h in range(4):
            s = scores[h]
            sp = jnp.maximum(s, 0.0) + jnp.log(1.0 + jnp.exp(-jnp.abs(s)))
            lg = -sp
            if masked:
                lg = jnp.where(past, lg, 0.0)
            hi, lo = _split2(lg)
            carry = carry_ref[h:h + 1, :]
            laters.append(_dot(U_ref[...], hi) + _dot(U_ref[...], lo) + carry)
            logsig.append(s - sp)
            carry = carry + jnp.sum(lg, axis=0, keepdims=True)
            carry_ref[h:h + 1, :] = carry
            worst = carry if worst is None else jnp.maximum(worst, carry)
        for h in range(4):
            w = jnp.exp(logsig[h] + laters[h])
            if masked:
                w = jnp.where(past, w, 0.0)
            vb = _lane_tiles(vT_ref, j * nt, nt, h * HEAD_DIM, HEAD_DIM)
            acc_ref[h * HEAD_DIM:(h + 1) * HEAD_DIM, :] += _dot(vb, w.astype(BF16))
        return jnp.max(worst)

    worst0 = tile(i, True)

    def cond(st):
        return jnp.logical_and(st[0] >= 0, st[1] > SB_LOG_CUTOFF)

    def body(st):
        return st[0] - 1, tile(st[0], False)

    lax.while_loop(cond, body, (i - 1, worst0))
    o_ref[...] = acc_ref[...].T.astype(o_ref.dtype)


def _attn_specs(S, tq, q_chunk, k_chunk, v_chunk):
    return [
        pl.BlockSpec((None, tq // LANE, CH, LANE), lambda b, i: (b, i, q_chunk, 0)),
        pl.BlockSpec((None, S, CH), lambda b, i: (b, 0, k_chunk)),
        pl.BlockSpec((None, S // LANE, CH, LANE), lambda b, i: (b, 0, v_chunk, 0)),
    ]


def _sb_call(oT, otok, U, *, q_chunk, k_chunk, v_chunk, tq=256):
    B, nlt, _, _ = oT.shape
    S = nlt * LANE
    kern = functools.partial(_sb_kernel, tq=tq)
    return pl.pallas_call(
        kern,
        grid=(B, S // tq),
        in_specs=_attn_specs(S, tq, q_chunk, k_chunk, v_chunk) + [pl.BlockSpec((tq, tq), lambda b, i: (0, 0))],
        out_specs=pl.BlockSpec((None, tq, BRANCH_W), lambda b, i: (b, i, 0)),
        out_shape=jax.ShapeDtypeStruct((B, S, BRANCH_W), BF16),
        scratch_shapes=[pltpu.VMEM((4, LANE, tq), BF16), pltpu.VMEM((CH, tq), F32), pltpu.VMEM((8, tq), F32)],
        compiler_params=_cparams(("parallel", "arbitrary")),
        name="stick_breaking",
    )(oT, otok, oT, U)


def _head_lanes(x, lane, h):
    lo = h * HEAD_DIM
    return jnp.where(lane >= lo, jnp.where(lane < lo + HEAD_DIM, x, jnp.zeros_like(x)), jnp.zeros_like(x))


def _band_kernel(q_ref, k_ref, v_ref, ind_ref, o_ref, lse_ref, *, tqb, max_dist):
    n = pl.program_id(2)
    n_prev = -(-max_dist // LANE)
    nkr = (n_prev + 1) * LANE
    nsub = tqb // LANE
    row = lax.broadcasted_iota(jnp.int32, (nkr, LANE), 0)
    col = lax.broadcasted_iota(jnp.int32, (nkr, LANE), 1)
    lane_q = lax.broadcasted_iota(jnp.int32, (LANE, LANE), 1)
    lane_v = lax.broadcasted_iota(jnp.int32, (nkr, CH), 1)
    for u in range(nsub):
        qt = n * nsub + u
        kt0 = jnp.maximum(qt - n_prev, 0)
        dist = (qt - kt0) * LANE + col - row
        k0 = pl.multiple_of(kt0 * LANE, LANE)
        kwin = k_ref[pl.ds(k0, nkr), :]
        vwin = v_ref[pl.ds(k0, nkr), :]
        qu = q_ref[u * LANE:(u + 1) * LANE, :]
        o_acc = None
        lses = []
        for h in range(4):
            p = h // 2
            qm = _head_lanes(qu[:, p * LANE:(p + 1) * LANE], lane_q, h % 2)
            s = _dot_nt(kwin[:, p * LANE:(p + 1) * LANE], qm)
            s = jnp.where(dist >= 0, jnp.where(dist <= max_dist, s, NEG), NEG)
            m = jnp.max(s, axis=0, keepdims=True)
            e = jnp.exp2(s - m)
            den = jnp.sum(e, axis=0, keepdims=True)
            pn = e * (1.0 / den)
            contrib = _dot(pn.T.astype(BF16), _head_lanes(vwin, lane_v, h))
            o_acc = contrib if o_acc is None else o_acc + contrib
            lses.append(m + jnp.log(den) * LOG2E)
        o_ref[u * LANE:(u + 1) * LANE, :] = o_acc.astype(o_ref.dtype)
        ls = jnp.concatenate(lses + [jnp.zeros((LANE - 4, LANE), F32)], axis=0).T
        l1, l2, l3 = _split3(ls)
        lse_ref[u * LANE:(u + 1) * LANE, :] = (_dot(l1, ind_ref[...]) + _dot(l2, ind_ref[...])
                                               + _dot(l3, ind_ref[...]))


def _band_call(arr, ind, *, dil, per_res, qi, ki, vi, max_dist):
    B, L, _ = arr.shape
    tqb = min(512, L)
    kern = functools.partial(_band_kernel, tqb=tqb, max_dist=max_dist)
    return pl.pallas_call(
        kern,
        grid=(B, dil, L // tqb),
        in_specs=[
            pl.BlockSpec((None, tqb, CH), lambda b, r, n: (b, n, r * per_res + qi)),
            pl.BlockSpec((None, L, CH), lambda b, r, n: (b, 0, r * per_res + ki)),
            pl.BlockSpec((None, L, CH), lambda b, r, n: (b, 0, r * per_res + vi)),
            pl.BlockSpec((LANE, CH), lambda b, r, n: (0, 0)),
        ],
        out_specs=[
            pl.BlockSpec((None, tqb, BRANCH_W), lambda b, r, n: (b, n, r)),
            pl.BlockSpec((None, tqb, BRANCH_W), lambda b, r, n: (b, n, r)),
        ],
        out_shape=[
            jax.ShapeDtypeStruct((B, L, dil * BRANCH_W), BF16),
            jax.ShapeDtypeStruct((B, L, dil * BRANCH_W), F32),
        ],
        compiler_params=_cparams(("parallel", "parallel", "parallel")),
        name="banded",
    )(arr, arr, arr, ind)


def _moba_kernel(qT_ref, k_ref, vT_ref, A_ref, o_ref, kmean_ref, sel_ref, qm_ref, acc_ref, ml_ref, s_ref, *, tq):
    i = pl.program_id(1)
    nt = tq // LANE
    nblk = A_ref.shape[0]

    @pl.when(i == 0)
    def _():
        kmean_ref[...] = _dot(A_ref[...], k_ref[...])

    blk = lax.broadcasted_iota(jnp.int32, (nblk, tq), 0)
    row = lax.broadcasted_iota(jnp.int32, (tq, tq), 0)
    col = lax.broadcasted_iota(jnp.int32, (tq, tq), 1)
    gates, diag = [], []
    for h in range(4):
        p = h // 2
        qm = _pair_masked_q(_lane_tiles(qT_ref, 0, nt, p * LANE, LANE), h)
        qm_ref[h] = qm
        k1, k2, k3 = _split3(kmean_ref[:, p * LANE:(p + 1) * LANE])
        gates.append(_dot(k1, qm) + _dot(k2, qm) + _dot(k3, qm))
        kb = k_ref[pl.ds(pl.multiple_of(i * tq, tq), tq), p * LANE:(p + 1) * LANE]
        diag.append(_dot(kb, qm))
    for h in range(4):
        g = jnp.where(blk < i, gates[h], NEG)
        sel = jnp.zeros((nblk, tq), F32)
        for _r in range(MOBA_TOPK):
            mx = jnp.max(g, axis=0, keepdims=True)
            idx = jnp.min(jnp.where(g == mx, blk, nblk), axis=0, keepdims=True)
            hit = blk == idx
            sel = jnp.where(hit, 1.0, sel)
            g = jnp.where(hit, -jnp.inf, g)
        sel_ref[h] = jnp.where(blk < i, sel, 0.0)

        s = jnp.where(row <= col, diag[h], NEG)
        m = jnp.max(s, axis=0, keepdims=True)
        ml_ref[h:h + 1, :] = m
        acc_ref[h * HA:(h + 1) * HA, :] = _dot(
            _v_aug(_lane_tiles(vT_ref, i * nt, nt, h * HEAD_DIM, HEAD_DIM)), jnp.exp2(s - m).astype(BF16))

    last = jnp.maximum(i - 1, 0)

    def qk(t, slot):
        for h in range(4):
            p = h // 2
            kbj = k_ref[pl.ds(pl.multiple_of(t * tq, tq), tq), p * LANE:(p + 1) * LANE]
            s_ref[slot, h] = _dot(kbj, qm_ref[h])

    def update(t, valid, slot):
        for h in range(4):
            srow = sel_ref[h, pl.ds(t, 1), :] * valid
            vb = _v_aug(_lane_tiles(vT_ref, t * nt, nt, h * HEAD_DIM, HEAD_DIM))
            rows = slice(h * HA, (h + 1) * HA)
            m, acc = _online_cols(s_ref[slot, h], srow, ml_ref[h:h + 1, :], acc_ref[rows, :], vb)
            ml_ref[h:h + 1, :] = m
            acc_ref[rows, :] = acc

    qk(0, 0)

    def body(jj, c):
        t0 = 2 * jj
        qk(jnp.minimum(t0 + 1, last), 1)
        update(t0, 1.0, 0)
        qk(jnp.minimum(t0 + 2, last), 0)
        update(jnp.minimum(t0 + 1, last), (t0 + 1 < i).astype(F32), 1)
        return c

    lax.fori_loop(0, (i + 1) // 2, body, 0)
    outs = [acc_ref[h * HA:h * HA + HEAD_DIM, :] / acc_ref[h * HA + HEAD_DIM:h * HA + HEAD_DIM + 1, :]
            for h in range(4)]
    o_ref[...] = jnp.concatenate(outs, axis=0).T.astype(o_ref.dtype)


def _moba_call(oT, otok, A, *, q_chunk, k_chunk, v_chunk):
    B, nlt, _, _ = oT.shape
    S = nlt * LANE
    tq = MOBA_BLOCK
    nblk = S // MOBA_BLOCK
    kern = functools.partial(_moba_kernel, tq=tq)
    return pl.pallas_call(
        kern,
        grid=(B, S // tq),
        in_specs=_attn_specs(S, tq, q_chunk, k_chunk, v_chunk) + [pl.BlockSpec((nblk, S), lambda b, i: (0, 0))],
        out_specs=pl.BlockSpec((None, tq, BRANCH_W), lambda b, i: (b, i, 0)),
        out_shape=jax.ShapeDtypeStruct((B, S, BRANCH_W), BF16),
        scratch_shapes=[pltpu.VMEM((nblk, CH), F32), pltpu.VMEM((4, nblk, tq), F32),
                        pltpu.VMEM((4, LANE, tq), BF16), pltpu.VMEM((4 * HA, tq), F32), pltpu.VMEM((8, tq), F32),
                        pltpu.VMEM((2, 4, tq, tq), F32)],
        compiler_params=_cparams(("parallel", "arbitrary")),
        name="moba",
    )(oT, otok, oT, A)


def _compress_kernel(x_ref, w1_ref, pe_ref, w1f_ref, w2_ref, gk_ref, G_ref, o_ref, oT_ref, acc_ref):
    l = pl.program_id(1)

    @pl.when(l == 0)
    def _():
        acc_ref[...] = jnp.zeros_like(acc_ref)

    x = x_ref[...]
    w_hi, w_lo = _split2(w1_ref[...])
    acc_ref[...] += _dot(x, w_hi) + _dot(x, w_lo)

    @pl.when(l == pl.num_programs(1) - 1)
    def _():
        nc = acc_ref.shape[0]
        r = acc_ref[...]
        bias = jnp.dot(pe_ref[...], w1f_ref[...], preferred_element_type=F32,
                       precision=lax.Precision.HIGHEST)
        p1 = jnp.concatenate([r[:, 0:64], r[:, 128:192]], axis=1)
        p2 = jnp.concatenate([r[:, 64:128], r[:, 192:256]], axis=1)
        hid = p1 + pltpu.roll(p2, nc - 1, 0) + bias[0:1, :]
        hid = hid * jax.nn.sigmoid(hid)
        comp = jnp.dot(hid, w2_ref[...], preferred_element_type=F32, precision=lax.Precision.HIGHEST)
        ss = _dot((comp * comp).astype(BF16), G_ref[...])
        lane = lax.broadcasted_iota(jnp.int32, comp.shape, 1)
        inv = jnp.where(lane < HEAD_DIM, lax.rsqrt(ss * (1.0 / HEAD_DIM) + EPS), 1.0)
        comp = comp * inv * gk_ref[...]
        rowi = lax.broadcasted_iota(jnp.int32, comp.shape, 0)
        comp = jnp.where(rowi < nc - 1, comp, 0.0)
        o_ref[...] = comp.astype(o_ref.dtype)
        oT_ref[...] = comp.T.astype(oT_ref.dtype)


def _compress_call(xv, w1blk, pe2, w1f, w2blk, gk, G128):
    B, nc, _ = xv.shape
    return pl.pallas_call(
        _compress_kernel,
        grid=(B, NSA_CMP_STRIDE),
        in_specs=[
            pl.BlockSpec((None, nc, LANE), lambda b, l: (b, 0, l)),
            pl.BlockSpec((None, LANE, CH), lambda b, l: (l, 0, 0)),
            pl.BlockSpec((8, 2 * NSA_CMP_LEN * HEAD_DIM), lambda b, l: (0, 0)),
            pl.BlockSpec((2 * NSA_CMP_LEN * HEAD_DIM, LANE), lambda b, l: (0, 0)),
            pl.BlockSpec((LANE, LANE), lambda b, l: (0, 0)),
            pl.BlockSpec((1, LANE), lambda b, l: (0, 0)),
            pl.BlockSpec((LANE, LANE), lambda b, l: (0, 0)),
        ],
        out_specs=[
            pl.BlockSpec((None, nc, LANE), lambda b, l: (b, 0, 0)),
            pl.BlockSpec((None, LANE, nc), lambda b, l: (b, 0, 0)),
        ],
        out_shape=[
            jax.ShapeDtypeStruct((B, nc, LANE), BF16),
            jax.ShapeDtypeStruct((B, LANE, nc), BF16),
        ],
        scratch_shapes=[pltpu.VMEM((nc, CH), F32)],
        compiler_params=_cparams(("parallel", "arbitrary")),
        name="nsa_compress",
    )(xv, w1blk, pe2, w1f, w2blk, gk, G128)


def _nsa_kernel(qT_ref, k_ref, vT_ref, kc_ref, kcT_ref, ov_ref, o_ref, sel_ref, qm_ref, acc_ref, ml_ref, s_ref,
                *, tq):
    i = pl.program_id(1)
    nt = tq // LANE
    nq = 4 * tq
    nc = kc_ref.shape[0]
    nsel = ov_ref.shape[0]
    spb = tq // NSA_SEL_BLOCK

    zeros = jnp.zeros((HEAD_DIM, tq), BF16)
    q_heads = [_lane_tiles(qT_ref, 0, nt, h * HEAD_DIM, HEAD_DIM) for h in range(4)]
    qm_lo = jnp.concatenate([jnp.concatenate([q, zeros], axis=0) for q in q_heads], axis=1)
    qm_hi = jnp.concatenate([jnp.concatenate([zeros, q], axis=0) for q in q_heads], axis=1)
    qpos = i * tq + lax.broadcasted_iota(jnp.int32, (1, tq), 1)
    qpos4 = jnp.concatenate([qpos] * 4, axis=1)

    zc = _dot(kc_ref[...], qm_lo)
    c_end = lax.broadcasted_iota(jnp.int32, (nc, nq), 0) * NSA_CMP_STRIDE + (NSA_CMP_LEN - 1)
    cmask = c_end <= qpos4
    zc = jnp.where(cmask, zc, NEG)
    e = jnp.where(cmask, jnp.exp2(zc - jnp.max(zc, axis=0, keepdims=True)), 0.0)
    pc = e / jnp.maximum(jnp.sum(e, axis=0, keepdims=True), 1.0)
    o_cmp = _dot(kcT_ref[HEAD_DIM:2 * HEAD_DIM, :], pc.astype(BF16))

    psum = pc[:, 0:tq] + pc[:, tq:2 * tq] + pc[:, 2 * tq:3 * tq] + pc[:, 3 * tq:4 * tq]
    p_hi, p_lo = _split2(psum)
    imp = _dot(ov_ref[...], p_hi) + _dot(ov_ref[...], p_lo)
    nid = lax.broadcasted_iota(jnp.int32, (nsel, tq), 0)
    cur = qpos // NSA_SEL_BLOCK
    imp = jnp.where(nid == 0, BIG, imp)
    imp = jnp.where(nid == cur, BIG, imp)
    imp = jnp.where(nid == cur - 1, BIG, imp)
    imp = jnp.where(nid > cur, NEG, imp)
    sel = jnp.zeros((nsel, tq), F32)
    for _r in range(min(NSA_SEL_TOPK, nsel)):
        mx = jnp.max(imp, axis=0, keepdims=True)
        idx = jnp.min(jnp.where(imp == mx, nid, nsel), axis=0, keepdims=True)
        hit = nid == idx
        sel = jnp.where(hit, 1.0, sel)
        imp = jnp.where(hit, -jnp.inf, imp)
    sel = jnp.where(nid <= cur, sel, 0.0)
    sel_ref[...] = sel
    for h in range(4):
        qm_ref[h] = qm_lo[:, h * tq:(h + 1) * tq]

    def qk(t, slot):
        kb = k_ref[pl.ds(pl.multiple_of(t * tq, tq), tq), LANE:2 * LANE]
        for h in range(4):
            s_ref[slot, h] = _dot(kb, qm_ref[h])

    kpos_d = i * tq + lax.broadcasted_iota(jnp.int32, (tq, tq), 0)
    qk(i, 0)
    vb_d = _v_aug(_lane_tiles(vT_ref, i * nt, nt, 0, HEAD_DIM))
    srows_d = [sel_ref[pl.ds(i * spb + u, 1), :] for u in range(spb)]
    for h in range(4):
        sc = s_ref[0, h]
        s = jnp.concatenate(
            [jnp.where(srows_d[u] > 0.0, sc[u * NSA_SEL_BLOCK:(u + 1) * NSA_SEL_BLOCK, :], NEG)
             for u in range(spb)], axis=0)
        s = jnp.where(kpos_d <= qpos, s, NEG)
        m = jnp.max(s, axis=0, keepdims=True)
        lanes = slice(h * tq, (h + 1) * tq)
        ml_ref[0:1, lanes] = m
        acc_ref[:, lanes] = _dot(vb_d, jnp.exp2(s - m).astype(BF16))

    last = jnp.maximum(i - 1, 0)

    def update(t, valid, slot):
        vb = _v_aug(_lane_tiles(vT_ref, t * nt, nt, 0, HEAD_DIM))
        srows = [sel_ref[pl.ds(t * spb + u, 1), :] * valid for u in range(spb)]
        for h in range(4):
            lanes = slice(h * tq, (h + 1) * tq)
            sc = s_ref[slot, h]
            subs = [sc[u * NSA_SEL_BLOCK:(u + 1) * NSA_SEL_BLOCK, :] for u in range(spb)]
            tmax = None
            for u in range(spb):
                mu = jnp.where(srows[u] > 0.0, jnp.max(subs[u], axis=0, keepdims=True), NEG)
                tmax = mu if tmax is None else jnp.maximum(tmax, mu)
            m_old = ml_ref[0:1, lanes]
            m_new = jnp.maximum(m_old, tmax)
            p = jnp.concatenate(
                [jnp.exp2(subs[u] - jnp.where(srows[u] > 0.0, m_new, BIGPOS)).astype(BF16) for u in range(spb)],
                axis=0)
            ml_ref[0:1, lanes] = m_new
            acc_ref[:, lanes] = jnp.exp2(m_old - m_new) * acc_ref[:, lanes] + _dot(vb, p)

    qk(0, 0)

    def body(jj, c):
        t0 = 2 * jj
        qk(jnp.minimum(t0 + 1, last), 1)
        update(t0, 1.0, 0)
        qk(jnp.minimum(t0 + 2, last), 0)
        update(jnp.minimum(t0 + 1, last), (t0 + 1 < i).astype(F32), 1)
        return c

    lax.fori_loop(0, (i + 1) // 2, body, 0)
    o_sel = acc_ref[0:HEAD_DIM, :] / acc_ref[HEAD_DIM:HEAD_DIM + 1, :]

    n_prev = -(-(NSA_WINDOW - 1) // tq)
    nk = n_prev + 1
    kt0 = jnp.maximum(i - n_prev, 0)
    kw = k_ref[pl.ds(pl.multiple_of(kt0 * tq, tq), nk * tq), LANE:2 * LANE]
    sw = _dot(kw, qm_hi)
    dist = qpos4 - (kt0 * tq + lax.broadcasted_iota(jnp.int32, (nk * tq, nq), 0))
    sw = jnp.where(dist >= 0, jnp.where(dist <= NSA_WINDOW - 1, sw, NEG), NEG)
    mw = jnp.max(sw, axis=0, keepdims=True)
    ew = jnp.exp2(sw - mw)
    dw = jnp.sum(ew, axis=0, keepdims=True)
    vwin = _lane_tiles(vT_ref, kt0 * nt, nk * nt, HEAD_DIM, HEAD_DIM)
    o_win = _dot(vwin, ew.astype(BF16)) / dw

    gates = _lane_tiles(vT_ref, i * nt, nt, 2 * HEAD_DIM, 16).astype(F32)
    outs = []
    for h in range(4):
        sl = slice(h * tq, (h + 1) * tq)
        outs.append(gates[3 * h:3 * h + 1, :] * o_cmp[:, sl]
                    + gates[3 * h + 1:3 * h + 2, :] * o_sel[:, sl]
                    + gates[3 * h + 2:3 * h + 3, :] * o_win[:, sl])
    o_ref[...] = jnp.concatenate(outs, axis=0).T.astype(o_ref.dtype)


def _nsa_call(oT, otok, kc, kcT, ovT, *, q_chunk, kv_chunk, v_chunk):
    B, nlt, _, _ = oT.shape
    S = nlt * LANE
    tq = 256
    nc = S // NSA_CMP_STRIDE
    nsel = S // NSA_SEL_BLOCK
    kern = functools.partial(_nsa_kernel, tq=tq)
    return pl.pallas_call(
        kern,
        grid=(B, S // tq),
        in_specs=_attn_specs(S, tq, q_chunk, kv_chunk, v_chunk) + [
            pl.BlockSpec((None, nc, LANE), lambda b, i: (b, 0, 0)),
            pl.BlockSpec((None, LANE, nc), lambda b, i: (b, 0, 0)),
            pl.BlockSpec((nsel, nc), lambda b, i: (0, 0)),
        ],
        out_specs=pl.BlockSpec((None, tq, BRANCH_W), lambda b, i: (b, i, 0)),
        out_shape=jax.ShapeDtypeStruct((B, S, BRANCH_W), BF16),
        scratch_shapes=[pltpu.VMEM((nsel, tq), F32), pltpu.VMEM((4, LANE, tq), BF16),
                        pltpu.VMEM((HA, 4 * tq), F32), pltpu.VMEM((8, 4 * tq), F32),
                        pltpu.VMEM((2, 4, tq, tq), F32)],
        compiler_params=_cparams(("parallel", "arbitrary")),
        name="nsa",
    )(oT, otok, oT, kc, kcT, ovT)


def _epi_kernel(x_ref, ng_ref, ya_ref, yc_ref, yd_ref, ob0_ref, ob1_ref, ob2_ref, l0_ref, l1_ref, l2_ref,
                za_ref, zb_ref, zc_ref, zd_ref, wmg_ref, wbr_ref, wout_ref, o_ref, nat_ref, *, tm):
    x = x_ref[...]
    ms = jnp.mean(x * x, axis=-1, keepdims=True)
    xn = (x * lax.rsqrt(ms + EPS) * ng_ref[...]).astype(BF16)

    def natural(ref, slot, dil):
        nh = BRANCH_W // LANE
        for r in range(dil):
            for hf in range(nh):
                c0 = r * BRANCH_W + hf * LANE
                nat_ref[slot * nh + hf, pl.ds(r, tm // dil, stride=dil), :] = ref[:, c0:c0 + LANE].astype(F32)
        return jnp.concatenate([nat_ref[slot * nh + hf] for hf in range(nh)], axis=1)

    d1, d2 = DIL_PAIRS[1][1], DIL_PAIRS[2][1]
    l0, l1, l2 = l0_ref[...], natural(l1_ref, 0, d1), natural(l2_ref, 1, d2)
    mx = jnp.maximum(jnp.maximum(l0, l1), l2)
    e0, e1, e2 = jnp.exp2(l0 - mx), jnp.exp2(l1 - mx), jnp.exp2(l2 - mx)
    yb = (e0 * ob0_ref[...].astype(F32) + e1 * natural(ob1_ref, 2, d1)
          + e2 * natural(ob2_ref, 3, d2)) / (e0 + e1 + e2)

    ys = (ya_ref[...].astype(F32), yb, yc_ref[...].astype(F32), yd_ref[...].astype(F32))
    zs = (za_ref, zb_ref, zc_ref, zd_ref)
    merged = None
    for i in range(4):
        z = zs[i][...].astype(F32)
        gated = (ys[i] * (z * jax.nn.sigmoid(z))).astype(BF16)
        br = _dot(gated, wbr_ref[i])
        mg = _dot(xn, wmg_ref[:, i * D_MODEL:(i + 1) * D_MODEL])
        term = jax.nn.sigmoid(mg) * br
        merged = term if merged is None else merged + term
    o_ref[...] = x + _dot(merged.astype(BF16), wout_ref[...])


def _epi_call(x2, ng, ya, yc, yd, obs, lses, otok2, wmg, wbr, wout, *, z_chunks, tm=512):
    T = x2.shape[0]
    row = lambda i: (i, 0)
    full2 = lambda i: (0, 0)
    yspec = pl.BlockSpec((tm, BRANCH_W), row)
    zspecs = [pl.BlockSpec((tm, CH), (lambda i, c=c: (i, c))) for c in z_chunks]
    d1, d2 = DIL_PAIRS[1][1], DIL_PAIRS[2][1]
    gspecs = [yspec, pl.BlockSpec((tm // d1, d1 * BRANCH_W), row), pl.BlockSpec((tm // d2, d2 * BRANCH_W), row)]
    return pl.pallas_call(
        functools.partial(_epi_kernel, tm=tm),
        grid=(T // tm,),
        in_specs=[pl.BlockSpec((tm, D_MODEL), row), pl.BlockSpec((1, D_MODEL), full2)]
        + [yspec] * 3 + gspecs + gspecs + zspecs
        + [pl.BlockSpec((D_MODEL, 4 * D_MODEL), full2),
           pl.BlockSpec((4, BRANCH_W, D_MODEL), lambda i: (0, 0, 0)),
           pl.BlockSpec((D_MODEL, D_MODEL), full2)],
        out_specs=pl.BlockSpec((tm, D_MODEL), row),
        out_shape=jax.ShapeDtypeStruct((T, D_MODEL), F32),
        scratch_shapes=[pltpu.VMEM((4 * (BRANCH_W // LANE), tm, LANE), F32)],
        compiler_params=_cparams(("parallel",)),
        name="epilogue",
    )(x2, ng, ya, yc, yd, *obs, *lses, otok2, otok2, otok2, otok2, wmg, wbr, wout)


TOK_AK, TOK_AZ, TOK_BQ, TOK_BK, TOK_BV, TOK_BZ, TOK_CK, TOK_CZ, TOK_DKV, TOK_DZ = range(10)
N_MAIN = 10
T_AQ, T_AV, T_CQ, T_CV, T_DQ, T_DX = range(6)
SCALE = 1.0 / math.sqrt(HEAD_DIM)
QSCALE2 = SCALE * LOG2E

_PLAIN = (False, False, 1.0, False)
_T_KINDS = (
    ((False, False, SCALE, False),) * 4,
    (_PLAIN,) * 4,
    ((True, True, QSCALE2, False),) * 4,
    (_PLAIN,) * 4,
    ((True, True, QSCALE2, False),) * 4,
    (_PLAIN, _PLAIN, (False, False, 1.0, True), _PLAIN),
)


def _tok_specs():
    main = lambda j: (0, 1, j * CH, CH, 0)
    qkv = ((True, True), (True, True), (False, False))
    specs = []
    for j in range(N_MAIN):
        if j in (TOK_BQ, TOK_BK, TOK_CK):
            specs.append((True, True, (main(j),)))
        elif j == TOK_DKV:
            specs.append((True, True, (main(j), (3, NSA_CMP_STRIDE, 0, LANE, LANE))))
        else:
            specs.append((False, False, (main(j),)))
    for gi in (1, 2):
        dil = DIL_PAIRS[gi][1]
        for k, (nrm, rope) in enumerate(qkv):
            specs.append((nrm, rope, ((gi, dil, k * CH, CH, 3 * CH),)))
    return tuple(specs)


def _rope_tables(S, dil):
    L = S // dil
    pos = (jnp.arange(dil, dtype=jnp.int32)[:, None] + dil * jnp.arange(L, dtype=jnp.int32)[None, :]).astype(F32)
    inv = ROPE_THETA ** (-jnp.arange(0, ROT_DIM, 2, dtype=F32) / ROT_DIM)
    ang = pos[:, :, None] * inv[None, None, :]
    cos, sin = jnp.cos(ang), jnp.sin(ang)
    one = jnp.ones((dil, L, HEAD_DIM - ROT_DIM), F32)
    zero8 = jnp.zeros((dil, L, 8), F32)
    zero = jnp.zeros_like(one)
    c_head = jnp.concatenate([cos, cos, one], axis=-1)
    s1_head = jnp.concatenate([zero8, sin, zero], axis=-1)
    s2_head = jnp.concatenate([-sin, zero8, zero], axis=-1)
    rtok = jnp.stack([jnp.tile(t, (1, 1, LANE // HEAD_DIM)) for t in (c_head, s1_head, s2_head)], axis=1)
    rT = jnp.stack([cos.transpose(0, 2, 1), sin.transpose(0, 2, 1)], axis=1)
    return rtok, rT


def _tok_params(rows):
    out = []
    for nf, rf, gains in rows:
        nrow = jnp.concatenate([jnp.full((HEAD_DIM,), float(f), F32) for f in nf])
        rrow = jnp.concatenate([jnp.full((HEAD_DIM,), float(f), F32) for f in rf])
        grow = jnp.concatenate([g.astype(F32) for g in gains])
        out.append(jnp.concatenate([jnp.stack([nrow, rrow, grow]), jnp.zeros((5, CH), F32)], axis=0))
    return jnp.stack(out)


def kernel(x, norm_g, w_in, qk_g, cmp_pe, cmp_w1, cmp_w2, w_branch, w_out):
    B, S, _ = x.shape
    T = B * S
    o = COL_OFF
    ones = jnp.ones((HEAD_DIM,), F32)

    r = np.arange(CH)
    G = jnp.asarray((r[:, None] // HEAD_DIM == r[None, :] // HEAD_DIM).astype(np.float32), BF16)
    G128 = G[:LANE, :LANE]
    t = np.arange(256)
    U = jnp.asarray((t[None, :] > t[:, None]).astype(np.float32), BF16)
    nblk = S // MOBA_BLOCK
    A = jnp.asarray(np.repeat(np.eye(nblk, dtype=np.float32), MOBA_BLOCK, axis=1) / MOBA_BLOCK, BF16)
    nc, nsel = S // NSA_CMP_STRIDE, S // NSA_SEL_BLOCK
    cs = np.arange(nc) * NSA_CMP_STRIDE
    ss = np.arange(nsel) * NSA_SEL_BLOCK
    ov = ((cs[None, :] < ss[:, None] + NSA_SEL_BLOCK) & (cs[None, :] + NSA_CMP_LEN > ss[:, None]))
    ov[:, nc - 1] = False
    ovT = jnp.asarray(ov.astype(np.float32), BF16)
    hid = np.arange(LANE)[:, None]
    ind = jnp.asarray(((hid < 4) & (r[None, :] // HEAD_DIM == hid)).astype(np.float32), BF16)
    rtok4, rT4 = _rope_tables(S, 1)
    rtok, rT = rtok4[0], rT4[0]
    tok_specs = _tok_specs()

    for l in range(DEPTH):
        W = w_in[l]
        col = lambda a, b: W[:, a:b]
        dkv = o[13]
        bq = lambda gi: col(o[4] + gi * CH, o[4] + (gi + 1) * CH)
        bk = lambda gi: col(o[5] + gi * CH, o[5] + (gi + 1) * CH)
        bv = lambda gi: col(o[6] + gi * CH, o[6] + (gi + 1) * CH)
        wtok = jnp.concatenate([
            col(o[1], o[2]), col(o[3], o[4]), bq(0), bk(0), bv(0), col(o[7], o[8]),
            col(o[9], o[10]), col(o[11], o[12]),
            col(dkv, dkv + 64), col(dkv + 64, dkv + 128), col(dkv + 128, dkv + 192), col(dkv + 256, dkv + 320),
            col(o[15], o[16]),
            bq(1), bk(1), bv(1), bq(2), bk(2), bv(2)], axis=1).astype(BF16)
        wT = jnp.concatenate([
            col(o[0], o[1]), col(o[2], o[3]), col(o[8], o[9]), col(o[10], o[11]), col(o[12], o[13]),
            col(dkv + 192, dkv + 256), col(dkv + 320, dkv + 384), col(o[14], o[15]),
            jnp.zeros((D_MODEL, CH - 140), F32)], axis=1).T.astype(BF16)
        g = qk_g[l]
        plain = ((0,) * 4, (0,) * 4, (ones,) * 4)
        bq_p = ((1,) * 4, (1,) * 4, (g[0] * QSCALE2,) * 4)
        bk_p = ((1,) * 4, (1,) * 4, (g[1],) * 4)
        tokp = _tok_params([
            plain, plain, bq_p, bk_p, plain, plain,
            ((1,) * 4, (1,) * 4, (g[3],) * 4), plain,
            ((0, 0, 1, 1), (1, 0, 1, 1), (ones, ones, g[6], g[7])), plain,
            bq_p, bk_p, plain, bq_p, bk_p, plain])
        gT = jnp.concatenate([jnp.tile(ones, 8), jnp.tile(g[2], 4), jnp.tile(ones, 4), jnp.tile(g[4], 4),
                              jnp.tile(ones, 4)])[:, None]
        ng = norm_g[l][None, :]
        otok, og1, og2, okv, oT = _proj_call(x, ng, wtok, wT, tokp, gT, rtok, rT, G,
                                             tok_specs=tok_specs, T_kinds=_T_KINDS, n_main=N_MAIN)

        ya = _sb_call(oT, otok, U, q_chunk=T_AQ, k_chunk=TOK_AK, v_chunk=T_AV)

        obs, lses = [], []
        for gi, (window, dil) in enumerate(DIL_PAIRS):
            if dil == 1:
                ob, lse = _band_call(otok, ind, dil=1, per_res=N_MAIN, qi=TOK_BQ, ki=TOK_BK, vi=TOK_BV,
                                     max_dist=window)
            else:
                ob, lse = _band_call((og1, og2)[gi - 1], ind, dil=dil, per_res=3, qi=0, ki=1, vi=2,
                                     max_dist=window // dil)
            obs.append(ob.reshape(T // dil, dil * BRANCH_W))
            lses.append(lse.reshape(T // dil, dil * BRANCH_W))

        yc = _moba_call(oT, otok, A, q_chunk=T_CQ, k_chunk=TOK_CK, v_chunk=T_CV)

        w1 = cmp_w1[l].reshape(2, NSA_CMP_LEN, HEAD_DIM, HEAD_DIM)
        z64 = jnp.zeros((NSA_CMP_STRIDE, HEAD_DIM, HEAD_DIM), F32)
        w1blk = jnp.concatenate([
            jnp.concatenate([w1[0, :16], w1[0, 16:], z64, z64], axis=2),
            jnp.concatenate([z64, z64, w1[1, :16], w1[1, 16:]], axis=2)], axis=1)
        pe2 = jnp.concatenate([cmp_pe[l].reshape(1, -1), jnp.zeros((7, 2 * NSA_CMP_LEN * HEAD_DIM), F32)], axis=0)
        zf = jnp.zeros((NSA_CMP_LEN * HEAD_DIM, HEAD_DIM), F32)
        w1f = jnp.concatenate([jnp.concatenate([cmp_w1[l, 0], zf], axis=1),
                               jnp.concatenate([zf, cmp_w1[l, 1]], axis=1)], axis=0)
        z2 = jnp.zeros((HEAD_DIM, HEAD_DIM), F32)
        w2blk = jnp.concatenate([jnp.concatenate([cmp_w2[l, 0], z2], axis=1),
                                 jnp.concatenate([z2, cmp_w2[l, 1]], axis=1)], axis=0)
        gk = jnp.concatenate([g[5], ones])[None, :]
        kc, kcT = _compress_call(okv, w1blk, pe2, w1f, w2blk, gk, G128)
        yd = _nsa_call(oT, otok, kc, kcT, ovT, q_chunk=T_DQ, kv_chunk=TOK_DKV, v_chunk=T_DX)

        wmg = col(o[16], o[17]).astype(BF16)
        x2 = _epi_call(x.reshape(T, D_MODEL), ng, ya.reshape(T, BRANCH_W), yc.reshape(T, BRANCH_W),
                       yd.reshape(T, BRANCH_W), obs, lses, otok.reshape(T, N_MAIN * CH), wmg,
                       w_branch[l].astype(BF16), w_out[l].astype(BF16),
                       z_chunks=(TOK_AZ, TOK_BZ, TOK_CZ, TOK_DZ))
        x = x2.reshape(B, S, D_MODEL)
    return x
```

```python
import functools
import math

import numpy as np
import jax
import jax.numpy as jnp
from jax import lax
from jax.experimental import pallas as pl
from jax.experimental.pallas import tpu as pltpu

F32 = jnp.float32
BF16 = jnp.bfloat16

D_MODEL = 1024
DEPTH = 4
HEAD_DIM = 64
ROT_DIM = 16
ROPE_THETA = 500000.0
EPS = 1e-6
NEG = -1e30
BIG = 1e9
BRANCH_W = 256
DIL_PAIRS = ((128, 1), (512, 4), (2048, 16))
MOBA_BLOCK = 256
MOBA_TOPK = 3
NSA_CMP_LEN = 32
NSA_CMP_STRIDE = 16
NSA_SEL_BLOCK = 64
NSA_SEL_TOPK = 16
NSA_WINDOW = 512
COL_SIZES = (256, 256, 256, 256, 768, 768, 768, 256, 256, 256, 256, 256, 256, 384, 12, 256, 4096)
COL_OFF = tuple(int(v) for v in np.concatenate([[0], np.cumsum(COL_SIZES)]))

LANE = 128
CH = 256
VMEM_LIMIT = 56 * 1024 * 1024


def _cparams(sem):
    return pltpu.CompilerParams(dimension_semantics=sem, vmem_limit_bytes=VMEM_LIMIT)


def _dot(a, b):
    return jnp.dot(a, b, preferred_element_type=F32)


def _dot_nt(a, b):
    return lax.dot_general(a, b, (((1,), (1,)), ((), ())), preferred_element_type=F32)


def _split2(x):
    hi = x.astype(BF16)
    lo = (x - hi.astype(F32)).astype(BF16)
    return hi, lo


def _split3(x):
    hi = x.astype(BF16)
    r = x - hi.astype(F32)
    mid = r.astype(BF16)
    lo = (r - mid.astype(F32)).astype(BF16)
    return hi, mid, lo


def _repack_kernel(w_ref, *out_refs, plans):
    for o_ref, ranges in zip(out_refs, plans):
        width = o_ref.shape[-1]
        parts = [w_ref[:, a:b] for a, b in ranges]
        used = sum(b - a for a, b in ranges)
        if used < width:
            parts.append(jnp.zeros((w_ref.shape[0], width - used), F32))
        o_ref[...] = (parts[0] if len(parts) == 1 else jnp.concatenate(parts, axis=1)).astype(o_ref.dtype)


def _repack_call(w_in, plans, widths, rows=128):
    depth, d, n_in = w_in.shape
    kern = functools.partial(_repack_kernel, plans=plans)
    return pl.pallas_call(
        kern,
        grid=(depth, d // rows),
        in_specs=[pl.BlockSpec((None, rows, n_in), lambda l, i: (l, i, 0))],
        out_specs=[pl.BlockSpec((None, rows, w), lambda l, i: (l, i, 0)) for w in widths],
        out_shape=[jax.ShapeDtypeStruct((depth, d, w), BF16) for w in widths],
        compiler_params=_cparams(("parallel", "parallel")),
        name="repack",
    )(w_in)


def _proj_kernel(x_ref, ng_ref, wtok_ref, wT_ref, tokp_ref, gT_ref, rtok_ref, rT_ref, G_ref,
                 omain_ref, og1_ref, og2_ref, okv_ref, oT_ref, stage_ref, *, tok_specs, T_kinds, tm):
    x = x_ref[...]
    ms = jnp.mean(x * x, axis=-1, keepdims=True)
    xn = (x * lax.rsqrt(ms + EPS) * ng_ref[...]).astype(BF16)

    dsts = (omain_ref, og1_ref, og2_ref, okv_ref)
    for c, (has_norm, has_rope, outs) in enumerate(tok_specs):
        y = _dot(xn, wtok_ref[:, c * CH:(c + 1) * CH])
        prm = tokp_ref[c]
        if has_norm:
            ss = _dot((y * y).astype(BF16), G_ref[...])
            inv = lax.rsqrt(ss * (1.0 / HEAD_DIM) + EPS)
            y = y * jnp.where(prm[0:1, :] > 0.0, inv, 1.0)
        y = y * prm[2:3, :]
        if has_rope:
            halves = []
            for hf in range(CH // LANE):
                yh = y[:, hf * LANE:(hf + 1) * LANE]
                rf = prm[1:2, hf * LANE:(hf + 1) * LANE]
                cc = jnp.where(rf > 0.0, rtok_ref[0], 1.0)
                s1 = rtok_ref[1] * rf
                s2 = rtok_ref[2] * rf
                halves.append(yh * cc + pltpu.roll(yh, 8, 1) * s1 + pltpu.roll(yh, LANE - 8, 1) * s2)
            y = jnp.concatenate(halves, axis=1)
        staged = False
        for dst, dil, col, lanes, rstride in outs:
            if dil == 1:
                dsts[dst][:, col:col + lanes] = y[:, :lanes].astype(BF16)
                continue
            if not staged:
                for hf in range(CH // LANE):
                    stage_ref[hf] = y[:, hf * LANE:(hf + 1) * LANE]
                staged = True
            for r in range(dil):
                for hf in range(lanes // LANE):
                    blk = stage_ref[hf, pl.ds(r, tm // dil, stride=dil), :]
                    c0 = r * rstride + col + hf * LANE
                    dsts[dst][:, c0:c0 + LANE] = blk.astype(BF16)

    cosT = rT_ref[0]
    sinT = rT_ref[1]
    for c, heads in enumerate(T_kinds):
        y = _dot(xn, wT_ref[:, c * CH:(c + 1) * CH]).T
        for h, (nrm, rope, scale, sigm) in enumerate(heads):
            r0 = c * CH + h * HEAD_DIM
            yh = y[h * HEAD_DIM:(h + 1) * HEAD_DIM, :]
            if nrm:
                msq = jnp.mean(yh * yh, axis=0, keepdims=True)
                yh = yh * lax.rsqrt(msq + EPS) * gT_ref[r0:r0 + HEAD_DIM, :]
            if rope:
                x1 = yh[0:8, :]
                x2 = yh[8:16, :]
                yh = jnp.concatenate([x1 * cosT - x2 * sinT, x2 * cosT + x1 * sinT, yh[16:, :]], axis=0)
            if scale != 1.0:
                yh = yh * scale
            if sigm:
                yh = jax.nn.sigmoid(yh)
            yb = yh.astype(BF16)
            for t in range(tm // LANE):
                oT_ref[t, r0:r0 + HEAD_DIM, :] = yb[:, t * LANE:(t + 1) * LANE]


def _proj_call(x, ng, wtok, wT, tokp, gT, rtok, rT, G, *, tok_specs, T_kinds, n_main, tm=512):
    B, S, _ = x.shape
    ntok, nT = len(tok_specs), len(T_kinds)
    d1, d2 = DIL_PAIRS[1][1], DIL_PAIRS[2][1]
    kern = functools.partial(_proj_kernel, tok_specs=tok_specs, T_kinds=T_kinds, tm=tm)
    const2 = lambda b, n: (0, 0)
    return pl.pallas_call(
        kern,
        grid=(B, S // tm),
        in_specs=[
            pl.BlockSpec((None, tm, D_MODEL), lambda b, n: (b, n, 0)),
            pl.BlockSpec((1, D_MODEL), const2),
            pl.BlockSpec((D_MODEL, ntok * CH), const2),
            pl.BlockSpec((D_MODEL, nT * CH), const2),
            pl.BlockSpec((ntok, 8, CH), lambda b, n: (0, 0, 0)),
            pl.BlockSpec((nT * CH, 1), const2),
            pl.BlockSpec((3, tm, LANE), lambda b, n: (0, n, 0)),
            pl.BlockSpec((2, 8, tm), lambda b, n: (0, 0, n)),
            pl.BlockSpec((CH, CH), const2),
        ],
        out_specs=[
            pl.BlockSpec((None, tm, n_main * CH), lambda b, n: (b, n, 0)),
            pl.BlockSpec((None, tm // d1, d1 * 3 * CH), lambda b, n: (b, n, 0)),
            pl.BlockSpec((None, tm // d2, d2 * 3 * CH), lambda b, n: (b, n, 0)),
            pl.BlockSpec((None, tm // NSA_CMP_STRIDE, NSA_CMP_STRIDE * LANE), lambda b, n: (b, n, 0)),
            pl.BlockSpec((None, tm // LANE, nT * CH, LANE), lambda b, n: (b, n, 0, 0)),
        ],
        out_shape=[
            jax.ShapeDtypeStruct((B, S, n_main * CH), BF16),
            jax.ShapeDtypeStruct((B, S // d1, d1 * 3 * CH), BF16),
            jax.ShapeDtypeStruct((B, S // d2, d2 * 3 * CH), BF16),
            jax.ShapeDtypeStruct((B, S // NSA_CMP_STRIDE, NSA_CMP_STRIDE * LANE), BF16),
            jax.ShapeDtypeStruct((B, S // LANE, nT * CH, LANE), BF16),
        ],
        scratch_shapes=[pltpu.VMEM((CH // LANE, tm, LANE), F32)],
        compiler_params=_cparams(("parallel", "parallel")),
        name="proj",
    )(x, ng, wtok, wT, tokp, gT, rtok, rT, G)


def _pair_masked_q(q_pair, h):
    rid = lax.broadcasted_iota(jnp.int32, q_pair.shape, 0)
    lo = (h % 2) * HEAD_DIM
    keep = jnp.where(rid >= lo, jnp.where(rid < lo + HEAD_DIM, 1.0, 0.0), 0.0).astype(BF16)
    return q_pair * keep


def _lane_tiles(ref, t0, nt, r0, nr):
    return jnp.concatenate([ref[t0 + t, r0:r0 + nr, :] for t in range(nt)], axis=1)


BIGPOS = 1e30
ONES_ROWS = 16
HA = HEAD_DIM + ONES_ROWS
LOG2E = 1.4426950408889634


def _v_aug(vb):
    return jnp.concatenate([vb, jnp.ones((ONES_ROWS, vb.shape[1]), BF16)], axis=0)


def _online_cols(s, colsel, m_old, acc_old, vb_aug):
    tmax = jnp.max(s, axis=0, keepdims=True)
    m_new = jnp.where(colsel > 0.0, jnp.maximum(m_old, tmax), m_old)
    m_use = jnp.where(colsel > 0.0, m_new, BIGPOS)
    p = jnp.exp2(s - m_use).astype(BF16)
    acc = jnp.exp2(m_old - m_new) * acc_old + _dot(vb_aug, p)
    return m_new, acc


SB_LOG_CUTOFF = -100.0


def _sb_kernel(qT_ref, k_ref, vT_ref, U_ref, o_ref, qm_ref, acc_ref, carry_ref, *, tq):
    i = pl.program_id(1)
    nt = tq // LANE
    row = lax.broadcasted_iota(jnp.int32, (tq, tq), 0)
    col = lax.broadcasted_iota(jnp.int32, (tq, tq), 1)
    past = row < col
    for h in range(4):
        p = h // 2
        qm_ref[h] = _pair_masked_q(_lane_tiles(qT_ref, 0, nt, p * LANE, LANE), h)
    acc_ref[...] = jnp.zeros_like(acc_ref)
    carry_ref[...] = jnp.zeros_like(carry_ref)

    def tile(j, masked):
        scores, logsig, laters = [], [], []
        for h in range(4):
            p = h // 2
            kb = k_ref[pl.ds(pl.multiple_of(j * tq, tq), tq), p * LANE:(p + 1) * LANE]
            scores.append(_dot(kb, qm_ref[h]))
        worst = None
        for h in range(4):
            s = scores[h]
            sp = jnp.maximum(s, 0.0) + jnp.log(1.0 + jnp.exp(-jnp.abs(s)))
            lg = -sp
            if masked:
                lg = jnp.where(past, lg, 0.0)
            hi, lo = _split2(lg)
            carry = carry_ref[h:h + 1, :]
            laters.append(_dot(U_ref[...], hi) + _dot(U_ref[...], lo) + carry)
            logsig.append(s - sp)
            carry = carry + jnp.sum(lg, axis=0, keepdims=True)
            carry_ref[h:h + 1, :] = carry
            worst = carry if worst is None else jnp.maximum(worst, carry)
        for h in range(4):
            w = jnp.exp(logsig[h] + laters[h])
            if masked:
                w = jnp.where(past, w, 0.0)
            vb = _lane_tiles(vT_ref, j * nt, nt, h * HEAD_DIM, HEAD_DIM)
            acc_ref[h * HEAD_DIM:(h + 1) * HEAD_DIM, :] += _dot(vb, w.astype(BF16))
        return jnp.max(worst)

    worst0 = tile(i, True)

    def cond(st):
        return jnp.logical_and(st[0] >= 0, st[1] > SB_LOG_CUTOFF)

    def body(st):
        return st[0] - 1, tile(st[0], False)

    lax.while_loop(cond, body, (i - 1, worst0))
    o_ref[...] = acc_ref[...].T.astype(o_ref.dtype)


def _attn_specs(S, tq, q_chunk, k_chunk, v_chunk):
    return [
        pl.BlockSpec((None, tq // LANE, CH, LANE), lambda b, i: (b, i, q_chunk, 0)),
        pl.BlockSpec((None, S, CH), lambda b, i: (b, 0, k_chunk)),
        pl.BlockSpec((None, S // LANE, CH, LANE), lambda b, i: (b, 0, v_chunk, 0)),
    ]


def _sb_call(oT, otok, U, *, q_chunk, k_chunk, v_chunk, tq=256):
    B, nlt, _, _ = oT.shape
    S = nlt * LANE
    kern = functools.partial(_sb_kernel, tq=tq)
    return pl.pallas_call(
        kern,
        grid=(B, S // tq),
        in_specs=_attn_specs(S, tq, q_chunk, k_chunk, v_chunk) + [pl.BlockSpec((tq, tq), lambda b, i: (0, 0))],
        out_specs=pl.BlockSpec((None, tq, BRANCH_W), lambda b, i: (b, i, 0)),
        out_shape=jax.ShapeDtypeStruct((B, S, BRANCH_W), BF16),
        scratch_shapes=[pltpu.VMEM((4, LANE, tq), BF16), pltpu.VMEM((CH, tq), F32), pltpu.VMEM((8, tq), F32)],
        compiler_params=_cparams(("parallel", "arbitrary")),
        name="stick_breaking",
    )(oT, otok, oT, U)


def _head_lanes(x, lane, h):
    lo = h * HEAD_DIM
    return jnp.where(lane >= lo, jnp.where(lane < lo + HEAD_DIM, x, jnp.zeros_like(x)), jnp.zeros_like(x))


def _band_kernel(q_ref, k_ref, v_ref, ind_ref, o_ref, lse_ref, *, tqb, max_dist):
    n = pl.program_id(2)
    n_prev = -(-max_dist // LANE)
    nkr = (n_prev + 1) * LANE
    nsub = tqb // LANE
    row = lax.broadcasted_iota(jnp.int32, (nkr, LANE), 0)
    col = lax.broadcasted_iota(jnp.int32, (nkr, LANE), 1)
    lane_q = lax.broadcasted_iota(jnp.int32, (LANE, LANE), 1)
    lane_v = lax.broadcasted_iota(jnp.int32, (nkr, CH), 1)
    for u in range(nsub):
        qt = n * nsub + u
        kt0 = jnp.maximum(qt - n_prev, 0)
        dist = (qt - kt0) * LANE + col - row
        k0 = pl.multiple_of(kt0 * LANE, LANE)
        kwin = k_ref[pl.ds(k0, nkr), :]
        vwin = v_ref[pl.ds(k0, nkr), :]
        qu = q_ref[u * LANE:(u + 1) * LANE, :]
        o_acc = None
        lses = []
        for h in range(4):
            p = h // 2
            qm = _head_lanes(qu[:, p * LANE:(p + 1) * LANE], lane_q, h % 2)
            s = _dot_nt(kwin[:, p * LANE:(p + 1) * LANE], qm)
            s = jnp.where(dist >= 0, jnp.where(dist <= max_dist, s, NEG), NEG)
            m = jnp.max(s, axis=0, keepdims=True)
            e = jnp.exp2(s - m)
            den = jnp.sum(e, axis=0, keepdims=True)
            pn = e * (1.0 / den)
            contrib = _dot(pn.T.astype(BF16), _head_lanes(vwin, lane_v, h))
            o_acc = contrib if o_acc is None else o_acc + contrib
            lses.append(m + jnp.log(den) * LOG2E)
        o_ref[u * LANE:(u + 1) * LANE, :] = o_acc.astype(o_ref.dtype)
        ls = jnp.concatenate(lses + [jnp.zeros((LANE - 4, LANE), F32)], axis=0).T
        l1, l2, l3 = _split3(ls)
        lse_ref[u * LANE:(u + 1) * LANE, :] = (_dot(l1, ind_ref[...]) + _dot(l2, ind_ref[...])
                                               + _dot(l3, ind_ref[...]))


def _band_call(arr, ind, *, dil, per_res, qi, ki, vi, max_dist):
    B, L, _ = arr.shape
    tqb = min(512, L)
    kern = functools.partial(_band_kernel, tqb=tqb, max_dist=max_dist)
    return pl.pallas_call(
        kern,
        grid=(B, dil, L // tqb),
        in_specs=[
            pl.BlockSpec((None, tqb, CH), lambda b, r, n: (b, n, r * per_res + qi)),
            pl.BlockSpec((None, L, CH), lambda b, r, n: (b, 0, r * per_res + ki)),
            pl.BlockSpec((None, L, CH), lambda b, r, n: (b, 0, r * per_res + vi)),
            pl.BlockSpec((LANE, CH), lambda b, r, n: (0, 0)),
        ],
        out_specs=[
            pl.BlockSpec((None, tqb, BRANCH_W), lambda b, r, n: (b, n, r)),
            pl.BlockSpec((None, tqb, BRANCH_W), lambda b, r, n: (b, n, r)),
        ],
        out_shape=[
            jax.ShapeDtypeStruct((B, L, dil * BRANCH_W), BF16),
            jax.ShapeDtypeStruct((B, L, dil * BRANCH_W), F32),
        ],
        compiler_params=_cparams(("parallel", "parallel", "parallel")),
        name="banded",
    )(arr, arr, arr, ind)


def _moba_kernel(qT_ref, k_ref, vT_ref, A_ref, o_ref, kmean_ref, sel_ref, qm_ref, acc_ref, ml_ref, s_ref, *, tq):
    i = pl.program_id(1)
    nt = tq // LANE
    nblk = A_ref.shape[0]

    @pl.when(i == 0)
    def _():
        kmean_ref[...] = _dot(A_ref[...], k_ref[...])

    blk = lax.broadcasted_iota(jnp.int32, (nblk, tq), 0)
    row = lax.broadcasted_iota(jnp.int32, (tq, tq), 0)
    col = lax.broadcasted_iota(jnp.int32, (tq, tq), 1)
    gates, diag = [], []
    for h in range(4):
        p = h // 2
        qm = _pair_masked_q(_lane_tiles(qT_ref, 0, nt, p * LANE, LANE), h)
        qm_ref[h] = qm
        k1, k2, k3 = _split3(kmean_ref[:, p * LANE:(p + 1) * LANE])
        gates.append(_dot(k1, qm) + _dot(k2, qm) + _dot(k3, qm))
        kb = k_ref[pl.ds(pl.multiple_of(i * tq, tq), tq), p * LANE:(p + 1) * LANE]
        diag.append(_dot(kb, qm))
    for h in range(4):
        g = jnp.where(blk < i, gates[h], NEG)
        sel = jnp.zeros((nblk, tq), F32)
        for _r in range(MOBA_TOPK):
            mx = jnp.max(g, axis=0, keepdims=True)
            idx = jnp.min(jnp.where(g == mx, blk, nblk), axis=0, keepdims=True)
            hit = blk == idx
            sel = jnp.where(hit, 1.0, sel)
            g = jnp.where(hit, -jnp.inf, g)
        sel_ref[h] = jnp.where(blk < i, sel, 0.0)

        s = jnp.where(row <= col, diag[h], NEG)
        m = jnp.max(s, axis=0, keepdims=True)
        ml_ref[h:h + 1, :] = m
        acc_ref[h * HA:(h + 1) * HA, :] = _dot(
            _v_aug(_lane_tiles(vT_ref, i * nt, nt, h * HEAD_DIM, HEAD_DIM)), jnp.exp2(s - m).astype(BF16))

    last = jnp.maximum(i - 1, 0)

    def qk(t, slot):
        for h in range(4):
            p = h // 2
            kbj = k_ref[pl.ds(pl.multiple_of(t * tq, tq), tq), p * LANE:(p + 1) * LANE]
            s_ref[slot, h] = _dot(kbj, qm_ref[h])

    def update(t, valid, slot):
        for h in range(4):
            srow = sel_ref[h, pl.ds(t, 1), :] * valid
            vb = _v_aug(_lane_tiles(vT_ref, t * nt, nt, h * HEAD_DIM, HEAD_DIM))
            rows = slice(h * HA, (h + 1) * HA)
            m, acc = _online_cols(s_ref[slot, h], srow, ml_ref[h:h + 1, :], acc_ref[rows, :], vb)
            ml_ref[h:h + 1, :] = m
            acc_ref[rows, :] = acc

    qk(0, 0)

    def body(jj, c):
        t0 = 2 * jj
        qk(jnp.minimum(t0 + 1, last), 1)
        update(t0, 1.0, 0)
        qk(jnp.minimum(t0 + 2, last), 0)
        update(jnp.minimum(t0 + 1, last), (t0 + 1 < i).astype(F32), 1)
        return c

    lax.fori_loop(0, (i + 1) // 2, body, 0)
    outs = [acc_ref[h * HA:h * HA + HEAD_DIM, :] / acc_ref[h * HA + HEAD_DIM:h * HA + HEAD_DIM + 1, :]
            for h in range(4)]
    o_ref[...] = jnp.concatenate(outs, axis=0).T.astype(o_ref.dtype)


def _moba_call(oT, otok, A, *, q_chunk, k_chunk, v_chunk):
    B, nlt, _, _ = oT.shape
    S = nlt * LANE
    tq = MOBA_BLOCK
    nblk = S // MOBA_BLOCK
    kern = functools.partial(_moba_kernel, tq=tq)
    return pl.pallas_call(
        kern,
        grid=(B, S // tq),
        in_specs=_attn_specs(S, tq, q_chunk, k_chunk, v_chunk) + [pl.BlockSpec((nblk, S), lambda b, i: (0, 0))],
        out_specs=pl.BlockSpec((None, tq, BRANCH_W), lambda b, i: (b, i, 0)),
        out_shape=jax.ShapeDtypeStruct((B, S, BRANCH_W), BF16),
        scratch_shapes=[pltpu.VMEM((nblk, CH), F32), pltpu.VMEM((4, nblk, tq), F32),
                        pltpu.VMEM((4, LANE, tq), BF16), pltpu.VMEM((4 * HA, tq), F32), pltpu.VMEM((8, tq), F32),
                        pltpu.VMEM((2, 4, tq, tq), F32)],
        compiler_params=_cparams(("parallel", "arbitrary")),
        name="moba",
    )(oT, otok, oT, A)


def _compress_kernel(x_ref, w1_ref, pe_ref, w1f_ref, w2_ref, gk_ref, G_ref, o_ref, oT_ref, acc_ref):
    l = pl.program_id(1)

    @pl.when(l == 0)
    def _():
        acc_ref[...] = jnp.zeros_like(acc_ref)

    x = x_ref[...]
    w_hi, w_lo = _split2(w1_ref[...])
    acc_ref[...] += _dot(x, w_hi) + _dot(x, w_lo)

    @pl.when(l == pl.num_programs(1) - 1)
    def _():
        nc = acc_ref.shape[0]
        r = acc_ref[...]
        bias = jnp.dot(pe_ref[...], w1f_ref[...], preferred_element_type=F32,
                       precision=lax.Precision.HIGHEST)
        p1 = jnp.concatenate([r[:, 0:64], r[:, 128:192]], axis=1)
        p2 = jnp.concatenate([r[:, 64:128], r[:, 192:256]], axis=1)
        hid = p1 + pltpu.roll(p2, nc - 1, 0) + bias[0:1, :]
        hid = hid * jax.nn.sigmoid(hid)
        comp = jnp.dot(hid, w2_ref[...], preferred_element_type=F32, precision=lax.Precision.HIGHEST)
        ss = _dot((comp * comp).astype(BF16), G_ref[...])
        lane = lax.broadcasted_iota(jnp.int32, comp.shape, 1)
        inv = jnp.where(lane < HEAD_DIM, lax.rsqrt(ss * (1.0 / HEAD_DIM) + EPS), 1.0)
        comp = comp * inv * gk_ref[...]
        rowi = lax.broadcasted_iota(jnp.int32, comp.shape, 0)
        comp = jnp.where(rowi < nc - 1, comp, 0.0)
        o_ref[...] = comp.astype(o_ref.dtype)
        oT_ref[...] = comp.T.astype(oT_ref.dtype)


def _compress_call(xv, w1blk, pe2, w1f, w2blk, gk, G128):
    B, nc, _ = xv.shape
    return pl.pallas_call(
        _compress_kernel,
        grid=(B, NSA_CMP_STRIDE),
        in_specs=[
            pl.BlockSpec((None, nc, LANE), lambda b, l: (b, 0, l)),
            pl.BlockSpec((None, LANE, CH), lambda b, l: (l, 0, 0)),
            pl.BlockSpec((8, 2 * NSA_CMP_LEN * HEAD_DIM), lambda b, l: (0, 0)),
            pl.BlockSpec((2 * NSA_CMP_LEN * HEAD_DIM, LANE), lambda b, l: (0, 0)),
            pl.BlockSpec((LANE, LANE), lambda b, l: (0, 0)),
            pl.BlockSpec((1, LANE), lambda b, l: (0, 0)),
            pl.BlockSpec((LANE, LANE), lambda b, l: (0, 0)),
        ],
        out_specs=[
            pl.BlockSpec((None, nc, LANE), lambda b, l: (b, 0, 0)),
            pl.BlockSpec((None, LANE, nc), lambda b, l: (b, 0, 0)),
        ],
        out_shape=[
            jax.ShapeDtypeStruct((B, nc, LANE), BF16),
            jax.ShapeDtypeStruct((B, LANE, nc), BF16),
        ],
        scratch_shapes=[pltpu.VMEM((nc, CH), F32)],
        compiler_params=_cparams(("parallel", "arbitrary")),
        name="nsa_compress",
    )(xv, w1blk, pe2, w1f, w2blk, gk, G128)


def _nsa_kernel(qT_ref, k_ref, vT_ref, kc_ref, kcT_ref, ov_ref, o_ref, sel_ref, qm_ref, acc_ref, ml_ref, s_ref,
                *, tq):
    i = pl.program_id(1)
    nt = tq // LANE
    nq = 4 * tq
    nc = kc_ref.shape[0]
    nsel = ov_ref.shape[0]
    spb = tq // NSA_SEL_BLOCK

    zeros = jnp.zeros((HEAD_DIM, tq), BF16)
    q_heads = [_lane_tiles(qT_ref, 0, nt, h * HEAD_DIM, HEAD_DIM) for h in range(4)]
    qm_lo = jnp.concatenate([jnp.concatenate([q, zeros], axis=0) for q in q_heads], axis=1)
    qm_hi = jnp.concatenate([jnp.concatenate([zeros, q], axis=0) for q in q_heads], axis=1)
    qpos = i * tq + lax.broadcasted_iota(jnp.int32, (1, tq), 1)
    qpos4 = jnp.concatenate([qpos] * 4, axis=1)

    zc = _dot(kc_ref[...], qm_lo)
    c_end = lax.broadcasted_iota(jnp.int32, (nc, nq), 0) * NSA_CMP_STRIDE + (NSA_CMP_LEN - 1)
    cmask = c_end <= qpos4
    zc = jnp.where(cmask, zc, NEG)
    e = jnp.where(cmask, jnp.exp2(zc - jnp.max(zc, axis=0, keepdims=True)), 0.0)
    pc = e / jnp.maximum(jnp.sum(e, axis=0, keepdims=True), 1.0)
    o_cmp = _dot(kcT_ref[HEAD_DIM:2 * HEAD_DIM, :], pc.astype(BF16))

    psum = pc[:, 0:tq] + pc[:, tq:2 * tq] + pc[:, 2 * tq:3 * tq] + pc[:, 3 * tq:4 * tq]
    p_hi, p_lo = _split2(psum)
    imp = _dot(ov_ref[...], p_hi) + _dot(ov_ref[...], p_lo)
    nid = lax.broadcasted_iota(jnp.int32, (nsel, tq), 0)
    cur = qpos // NSA_SEL_BLOCK
    imp = jnp.where(nid == 0, BIG, imp)
    imp = jnp.where(nid == cur, BIG, imp)
    imp = jnp.where(nid == cur - 1, BIG, imp)
    imp = jnp.where(nid > cur, NEG, imp)
    sel = jnp.zeros((nsel, tq), F32)
    for _r in range(min(NSA_SEL_TOPK, nsel)):
        mx = jnp.max(imp, axis=0, keepdims=True)
        idx = jnp.min(jnp.where(imp == mx, nid, nsel), axis=0, keepdims=True)
        hit = nid == idx
        sel = jnp.where(hit, 1.0, sel)
        imp = jnp.where(hit, -jnp.inf, imp)
    sel = jnp.where(nid <= cur, sel, 0.0)
    sel_ref[...] = sel
    for h in range(4):
        qm_ref[h] = qm_lo[:, h * tq:(h + 1) * tq]

    def qk(t, slot):
        kb = k_ref[pl.ds(pl.multiple_of(t * tq, tq), tq), LANE:2 * LANE]
        for h in range(4):
            s_ref[slot, h] = _dot(kb, qm_ref[h])

    kpos_d = i * tq + lax.broadcasted_iota(jnp.int32, (tq, tq), 0)
    qk(i, 0)
    vb_d = _v_aug(_lane_tiles(vT_ref, i * nt, nt, 0, HEAD_DIM))
    srows_d = [sel_ref[pl.ds(i * spb + u, 1), :] for u in range(spb)]
    for h in range(4):
        sc = s_ref[0, h]
        s = jnp.concatenate(
            [jnp.where(srows_d[u] > 0.0, sc[u * NSA_SEL_BLOCK:(u + 1) * NSA_SEL_BLOCK, :], NEG)
             for u in range(spb)], axis=0)
        s = jnp.where(kpos_d <= qpos, s, NEG)
        m = jnp.max(s, axis=0, keepdims=True)
        lanes = slice(h * tq, (h + 1) * tq)
        ml_ref[0:1, lanes] = m
        acc_ref[:, lanes] = _dot(vb_d, jnp.exp2(s - m).astype(BF16))

    last = jnp.maximum(i - 1, 0)

    def update(t, valid, slot):
        vb = _v_aug(_lane_tiles(vT_ref, t * nt, nt, 0, HEAD_DIM))
        srows = [sel_ref[pl.ds(t * spb + u, 1), :] * valid for u in range(spb)]
        for h in range(4):
            lanes = slice(h * tq, (h + 1) * tq)
            sc = s_ref[slot, h]
            subs = [sc[u * NSA_SEL_BLOCK:(u + 1) * NSA_SEL_BLOCK, :] for u in range(spb)]
            tmax = None
            for u in range(spb):
                mu = jnp.where(srows[u] > 0.0, jnp.max(subs[u], axis=0, keepdims=True), NEG)
                tmax = mu if tmax is None else jnp.maximum(tmax, mu)
            m_old = ml_ref[0:1, lanes]
            m_new = jnp.maximum(m_old, tmax)
            p = jnp.concatenate(
                [jnp.exp2(subs[u] - jnp.where(srows[u] > 0.0, m_new, BIGPOS)).astype(BF16) for u in range(spb)],
                axis=0)
            ml_ref[0:1, lanes] = m_new
            acc_ref[:, lanes] = jnp.exp2(m_old - m_new) * acc_ref[:, lanes] + _dot(vb, p)

    qk(0, 0)

    def body(jj, c):
        t0 = 2 * jj
        qk(jnp.minimum(t0 + 1, last), 1)
        update(t0, 1.0, 0)
        qk(jnp.minimum(t0 + 2, last), 0)
        update(jnp.minimum(t0 + 1, last), (t0 + 1 < i).astype(F32), 1)
        return c

    lax.fori_loop(0, (i + 1) // 2, body, 0)
    o_sel = acc_ref[0:HEAD_DIM, :] / acc_ref[HEAD_DIM:HEAD_DIM + 1, :]

    n_prev = -(-(NSA_WINDOW - 1) // tq)
    nk = n_prev + 1
    kt0 = jnp.maximum(i - n_prev, 0)
    kw = k_ref[pl.ds(pl.multiple_of(kt0 * tq, tq), nk * tq), LANE:2 * LANE]
    sw = _dot(kw, qm_hi)
    dist = qpos4 - (kt0 * tq + lax.broadcasted_iota(jnp.int32, (nk * tq, nq), 0))
    sw = jnp.where(dist >= 0, jnp.where(dist <= NSA_WINDOW - 1, sw, NEG), NEG)
    mw = jnp.max(sw, axis=0, keepdims=True)
    ew = jnp.exp2(sw - mw)
    dw = jnp.sum(ew, axis=0, keepdims=True)
    vwin = _lane_tiles(vT_ref, kt0 * nt, nk * nt, HEAD_DIM, HEAD_DIM)
    o_win = _dot(vwin, ew.astype(BF16)) / dw

    gates = _lane_tiles(vT_ref, i * nt, nt, 2 * HEAD_DIM, 16).astype(F32)
    outs = []
    for h in range(4):
        sl = slice(h * tq, (h + 1) * tq)
        outs.append(gates[3 * h:3 * h + 1, :] * o_cmp[:, sl]
                    + gates[3 * h + 1:3 * h + 2, :] * o_sel[:, sl]
                    + gates[3 * h + 2:3 * h + 3, :] * o_win[:, sl])
    o_ref[...] = jnp.concatenate(outs, axis=0).T.astype(o_ref.dtype)


def _nsa_call(oT, otok, kc, kcT, ovT, *, q_chunk, kv_chunk, v_chunk):
    B, nlt, _, _ = oT.shape
    S = nlt * LANE
    tq = 256
    nc = S // NSA_CMP_STRIDE
    nsel = S // NSA_SEL_BLOCK
    kern = functools.partial(_nsa_kernel, tq=tq)
    return pl.pallas_call(
        kern,
        grid=(B, S // tq),
        in_specs=_attn_specs(S, tq, q_chunk, kv_chunk, v_chunk) + [
            pl.BlockSpec((None, nc, LANE), lambda b, i: (b, 0, 0)),
            pl.BlockSpec((None, LANE, nc), lambda b, i: (b, 0, 0)),
            pl.BlockSpec((nsel, nc), lambda b, i: (0, 0)),
        ],
        out_specs=pl.BlockSpec((None, tq, BRANCH_W), lambda b, i: (b, i, 0)),
        out_shape=jax.ShapeDtypeStruct((B, S, BRANCH_W), BF16),
        scratch_shapes=[pltpu.VMEM((nsel, tq), F32), pltpu.VMEM((4, LANE, tq), BF16),
                        pltpu.VMEM((HA, 4 * tq), F32), pltpu.VMEM((8, 4 * tq), F32),
                        pltpu.VMEM((2, 4, tq, tq), F32)],
        compiler_params=_cparams(("parallel", "arbitrary")),
        name="nsa",
    )(oT, otok, oT, kc, kcT, ovT)


def _epi_kernel(x_ref, ng_ref, ya_ref, yc_ref, yd_ref, ob0_ref, ob1_ref, ob2_ref, l0_ref, l1_ref, l2_ref,
                za_ref, zb_ref, zc_ref, zd_ref, wmg_ref, wbr_ref, wout_ref, o_ref, nat_ref, *, tm):
    x = x_ref[...]
    ms = jnp.mean(x * x, axis=-1, keepdims=True)
    xn = (x * lax.rsqrt(ms + EPS) * ng_ref[...]).astype(BF16)

    def natural(ref, slot, dil):
        nh = BRANCH_W // LANE
        for r in range(dil):
            for hf in range(nh):
                c0 = r * BRANCH_W + hf * LANE
                nat_ref[slot * nh + hf, pl.ds(r, tm // dil, stride=dil), :] = ref[:, c0:c0 + LANE].astype(F32)
        return jnp.concatenate([nat_ref[slot * nh + hf] for hf in range(nh)], axis=1)

    d1, d2 = DIL_PAIRS[1][1], DIL_PAIRS[2][1]
    l0, l1, l2 = l0_ref[...], natural(l1_ref, 0, d1), natural(l2_ref, 1, d2)
    mx = jnp.maximum(jnp.maximum(l0, l1), l2)
    e0, e1, e2 = jnp.exp2(l0 - mx), jnp.exp2(l1 - mx), jnp.exp2(l2 - mx)
    yb = (e0 * ob0_ref[...].astype(F32) + e1 * natural(ob1_ref, 2, d1)
          + e2 * natural(ob2_ref, 3, d2)) / (e0 + e1 + e2)

    ys = (ya_ref[...].astype(F32), yb, yc_ref[...].astype(F32), yd_ref[...].astype(F32))
    zs = (za_ref, zb_ref, zc_ref, zd_ref)
    merged = None
    for i in range(4):
        z = zs[i][...].astype(F32)
        gated = (ys[i] * (z * jax.nn.sigmoid(z))).astype(BF16)
        br = _dot(gated, wbr_ref[i])
        mg = _dot(xn, wmg_ref[:, i * D_MODEL:(i + 1) * D_MODEL])
        term = jax.nn.sigmoid(mg) * br
        merged = term if merged is None else merged + term
    o_ref[...] = x + _dot(merged.astype(BF16), wout_ref[...])


def _epi_call(x2, ng, ya, yc, yd, obs, lses, otok2, wmg, wbr, wout, *, z_chunks, tm=512):
    T = x2.shape[0]
    row = lambda i: (i, 0)
    full2 = lambda i: (0, 0)
    yspec = pl.BlockSpec((tm, BRANCH_W), row)
    zspecs = [pl.BlockSpec((tm, CH), (lambda i, c=c: (i, c))) for c in z_chunks]
    d1, d2 = DIL_PAIRS[1][1], DIL_PAIRS[2][1]
    gspecs = [yspec, pl.BlockSpec((tm // d1, d1 * BRANCH_W), row), pl.BlockSpec((tm // d2, d2 * BRANCH_W), row)]
    return pl.pallas_call(
        functools.partial(_epi_kernel, tm=tm),
        grid=(T // tm,),
        in_specs=[pl.BlockSpec((tm, D_MODEL), row), pl.BlockSpec((1, D_MODEL), full2)]
        + [yspec] * 3 + gspecs + gspecs + zspecs
        + [pl.BlockSpec((D_MODEL, 4 * D_MODEL), full2),
           pl.BlockSpec((4, BRANCH_W, D_MODEL), lambda i: (0, 0, 0)),
           pl.BlockSpec((D_MODEL, D_MODEL), full2)],
        out_specs=pl.BlockSpec((tm, D_MODEL), row),
        out_shape=jax.ShapeDtypeStruct((T, D_MODEL), F32),
        scratch_shapes=[pltpu.VMEM((4 * (BRANCH_W // LANE), tm, LANE), F32)],
        compiler_params=_cparams(("parallel",)),
        name="epilogue",
    )(x2, ng, ya, yc, yd, *obs, *lses, otok2, otok2, otok2, otok2, wmg, wbr, wout)


TOK_AK, TOK_AZ, TOK_BQ, TOK_BK, TOK_BV, TOK_BZ, TOK_CK, TOK_CZ, TOK_DKV, TOK_DZ = range(10)
N_MAIN = 10
T_AQ, T_AV, T_CQ, T_CV, T_DQ, T_DX = range(6)
SCALE = 1.0 / math.sqrt(HEAD_DIM)
QSCALE2 = SCALE * LOG2E

_PLAIN = (False, False, 1.0, False)
_T_KINDS = (
    ((False, False, SCALE, False),) * 4,
    (_PLAIN,) * 4,
    ((True, True, QSCALE2, False),) * 4,
    (_PLAIN,) * 4,
    ((True, True, QSCALE2, False),) * 4,
    (_PLAIN, _PLAIN, (False, False, 1.0, True), _PLAIN),
)


def _tok_specs():
    main = lambda j: (0, 1, j * CH, CH, 0)
    qkv = ((True, True), (True, True), (False, False))
    specs = []
    for j in range(N_MAIN):
        if j in (TOK_BQ, TOK_BK, TOK_CK):
            specs.append((True, True, (main(j),)))
        elif j == TOK_DKV:
            specs.append((True, True, (main(j), (3, NSA_CMP_STRIDE, 0, LANE, LANE))))
        else:
            specs.append((False, False, (main(j),)))
    for gi in (1, 2):
        dil = DIL_PAIRS[gi][1]
        for k, (nrm, rope) in enumerate(qkv):
            specs.append((nrm, rope, ((gi, dil, k * CH, CH, 3 * CH),)))
    return tuple(specs)


def _rope_tables(S, dil):
    L = S // dil
    pos = (jnp.arange(dil, dtype=jnp.int32)[:, None] + dil * jnp.arange(L, dtype=jnp.int32)[None, :]).astype(F32)
    inv = ROPE_THETA ** (-jnp.arange(0, ROT_DIM, 2, dtype=F32) / ROT_DIM)
    ang = pos[:, :, None] * inv[None, None, :]
    cos, sin = jnp.cos(ang), jnp.sin(ang)
    one = jnp.ones((dil, L, HEAD_DIM - ROT_DIM), F32)
    zero8 = jnp.zeros((dil, L, 8), F32)
    zero = jnp.zeros_like(one)
    c_head = jnp.concatenate([cos, cos, one], axis=-1)
    s1_head = jnp.concatenate([zero8, sin, zero], axis=-1)
    s2_head = jnp.concatenate([-sin, zero8, zero], axis=-1)
    rtok = jnp.stack([jnp.tile(t, (1, 1, LANE // HEAD_DIM)) for t in (c_head, s1_head, s2_head)], axis=1)
    rT = jnp.stack([cos.transpose(0, 2, 1), sin.transpose(0, 2, 1)], axis=1)
    return rtok, rT


def _tok_params(rows):
    out = []
    for nf, rf, gains in rows:
        nrow = jnp.concatenate([jnp.full((HEAD_DIM,), float(f), F32) for f in nf])
        rrow = jnp.concatenate([jnp.full((HEAD_DIM,), float(f), F32) for f in rf])
        grow = jnp.concatenate([g.astype(F32) for g in gains])
        out.append(jnp.concatenate([jnp.stack([nrow, rrow, grow]), jnp.zeros((5, CH), F32)], axis=0))
    return jnp.stack(out)


def kernel(x, norm_g, w_in, qk_g, cmp_pe, cmp_w1, cmp_w2, w_branch, w_out):
    B, S, _ = x.shape
    T = B * S
    o = COL_OFF
    ones = jnp.ones((HEAD_DIM,), F32)

    r = np.arange(CH)
    G = jnp.asarray((r[:, None] // HEAD_DIM == r[None, :] // HEAD_DIM).astype(np.float32), BF16)
    G128 = G[:LANE, :LANE]
    t = np.arange(256)
    U = jnp.asarray((t[None, :] > t[:, None]).astype(np.float32), BF16)
    nblk = S // MOBA_BLOCK
    A = jnp.asarray(np.repeat(np.eye(nblk, dtype=np.float32), MOBA_BLOCK, axis=1) / MOBA_BLOCK, BF16)
    nc, nsel = S // NSA_CMP_STRIDE, S // NSA_SEL_BLOCK
    cs = np.arange(nc) * NSA_CMP_STRIDE
    ss = np.arange(nsel) * NSA_SEL_BLOCK
    ov = ((cs[None, :] < ss[:, None] + NSA_SEL_BLOCK) & (cs[None, :] + NSA_CMP_LEN > ss[:, None]))
    ov[:, nc - 1] = False
    ovT = jnp.asarray(ov.astype(np.float32), BF16)
    hid = np.arange(LANE)[:, None]
    ind = jnp.asarray(((hid < 4) & (r[None, :] // HEAD_DIM == hid)).astype(np.float32), BF16)
    rtok4, rT4 = _rope_tables(S, 1)
    rtok, rT = rtok4[0], rT4[0]
    tok_specs = _tok_specs()

    dkv = o[13]
    bq = lambda gi: (o[4] + gi * CH, o[4] + (gi + 1) * CH)
    bk = lambda gi: (o[5] + gi * CH, o[5] + (gi + 1) * CH)
    bv = lambda gi: (o[6] + gi * CH, o[6] + (gi + 1) * CH)
    tok_plan = ((o[1], o[2]), (o[3], o[4]), bq(0), bk(0), bv(0), (o[7], o[8]), (o[9], o[10]), (o[11], o[12]),
                (dkv, dkv + 64), (dkv + 64, dkv + 128), (dkv + 128, dkv + 192), (dkv + 256, dkv + 320),
                (o[15], o[16]), bq(1), bk(1), bv(1), bq(2), bk(2), bv(2))
    T_plan = ((o[0], o[1]), (o[2], o[3]), (o[8], o[9]), (o[10], o[11]), (o[12], o[13]),
              (dkv + 192, dkv + 256), (dkv + 320, dkv + 384), (o[14], o[15]))
    wtok_all, wT_all, wmg_all = _repack_call(
        w_in, (tok_plan, T_plan, ((o[16], o[17]),)), (len(tok_specs) * CH, len(_T_KINDS) * CH, 4 * D_MODEL))

    for l in range(DEPTH):
        wtok, wT, wmg = wtok_all[l], wT_all[l], wmg_all[l]
        g = qk_g[l]
        plain = ((0,) * 4, (0,) * 4, (ones,) * 4)
        bq_p = ((1,) * 4, (1,) * 4, (g[0] * QSCALE2,) * 4)
        bk_p = ((1,) * 4, (1,) * 4, (g[1],) * 4)
        tokp = _tok_params([
            plain, plain, bq_p, bk_p, plain, plain,
            ((1,) * 4, (1,) * 4, (g[3],) * 4), plain,
            ((0, 0, 1, 1), (1, 0, 1, 1), (ones, ones, g[6], g[7])), plain,
            bq_p, bk_p, plain, bq_p, bk_p, plain])
        gT = jnp.concatenate([jnp.tile(ones, 8), jnp.tile(g[2], 4), jnp.tile(ones, 4), jnp.tile(g[4], 4),
                              jnp.tile(ones, 4)])[:, None]
        ng = norm_g[l][None, :]
        otok, og1, og2, okv, oT = _proj_call(x, ng, wtok, wT, tokp, gT, rtok, rT, G,
                                             tok_specs=tok_specs, T_kinds=_T_KINDS, n_main=N_MAIN)

        ya = _sb_call(oT, otok, U, q_chunk=T_AQ, k_chunk=TOK_AK, v_chunk=T_AV)

        obs, lses = [], []
        for gi, (window, dil) in enumerate(DIL_PAIRS):
            if dil == 1:
                ob, lse = _band_call(otok, ind, dil=1, per_res=N_MAIN, qi=TOK_BQ, ki=TOK_BK, vi=TOK_BV,
                                     max_dist=window)
            else:
                ob, lse = _band_call((og1, og2)[gi - 1], ind, dil=dil, per_res=3, qi=0, ki=1, vi=2,
                                     max_dist=window // dil)
            obs.append(ob.reshape(T // dil, dil * BRANCH_W))
            lses.append(lse.reshape(T // dil, dil * BRANCH_W))

        yc = _moba_call(oT, otok, A, q_chunk=T_CQ, k_chunk=TOK_CK, v_chunk=T_CV)

        w1 = cmp_w1[l].reshape(2, NSA_CMP_LEN, HEAD_DIM, HEAD_DIM)
        z64 = jnp.zeros((NSA_CMP_STRIDE, HEAD_DIM, HEAD_DIM), F32)
        w1blk = jnp.concatenate([
            jnp.concatenate([w1[0, :16], w1[0, 16:], z64, z64], axis=2),
            jnp.concatenate([z64, z64, w1[1, :16], w1[1, 16:]], axis=2)], axis=1)
        pe2 = jnp.concatenate([cmp_pe[l].reshape(1, -1), jnp.zeros((7, 2 * NSA_CMP_LEN * HEAD_DIM), F32)], axis=0)
        zf = jnp.zeros((NSA_CMP_LEN * HEAD_DIM, HEAD_DIM), F32)
        w1f = jnp.concatenate([jnp.concatenate([cmp_w1[l, 0], zf], axis=1),
                               jnp.concatenate([zf, cmp_w1[l, 1]], axis=1)], axis=0)
        z2 = jnp.zeros((HEAD_DIM, HEAD_DIM), F32)
        w2blk = jnp.concatenate([jnp.concatenate([cmp_w2[l, 0], z2], axis=1),
                                 jnp.concatenate([z2, cmp_w2[l, 1]], axis=1)], axis=0)
        gk = jnp.concatenate([g[5], ones])[None, :]
        kc, kcT = _compress_call(okv, w1blk, pe2, w1f, w2blk, gk, G128)
        yd = _nsa_call(oT, otok, kc, kcT, ovT, q_chunk=T_DQ, kv_chunk=TOK_DKV, v_chunk=T_DX)

        x2 = _epi_call(x.reshape(T, D_MODEL), ng, ya.reshape(T, BRANCH_W), yc.reshape(T, BRANCH_W),
                       yd.reshape(T, BRANCH_W), obs, lses, otok.reshape(T, N_MAIN * CH), wmg,
                       w_branch[l].astype(BF16), w_out[l].astype(BF16),
                       z_chunks=(TOK_AZ, TOK_BZ, TOK_CZ, TOK_DZ))
        x = x2.reshape(B, S, D_MODEL)
    return x
```

```python
import functools
import math

import numpy as np
import jax
import jax.numpy as jnp
from jax import lax
from jax.experimental import pallas as pl
from jax.experimental.pallas import tpu as pltpu

F32 = jnp.float32
BF16 = jnp.bfloat16

D_MODEL = 1024
DEPTH = 4
HEAD_DIM = 64
ROT_DIM = 16
ROPE_THETA = 500000.0
EPS = 1e-6
NEG = -1e30
BIG = 1e9
BRANCH_W = 256
DIL_PAIRS = ((128, 1), (512, 4), (2048, 16))
MOBA_BLOCK = 256
MOBA_TOPK = 3
NSA_CMP_LEN = 32
NSA_CMP_STRIDE = 16
NSA_SEL_BLOCK = 64
NSA_SEL_TOPK = 16
NSA_WINDOW = 512
COL_SIZES = (256, 256, 256, 256, 768, 768, 768, 256, 256, 256, 256, 256, 256, 384, 12, 256, 4096)
COL_OFF = tuple(int(v) for v in np.concatenate([[0], np.cumsum(COL_SIZES)]))

LANE = 128
CH = 256
VMEM_LIMIT = 56 * 1024 * 1024


def _cparams(sem):
    return pltpu.CompilerParams(dimension_semantics=sem, vmem_limit_bytes=VMEM_LIMIT)


def _dot(a, b):
    return jnp.dot(a, b, preferred_element_type=F32)


def _dot_nt(a, b):
    return lax.dot_general(a, b, (((1,), (1,)), ((), ())), preferred_element_type=F32)


def _split2(x):
    hi = x.astype(BF16)
    lo = (x - hi.astype(F32)).astype(BF16)
    return hi, lo


def _split3(x):
    hi = x.astype(BF16)
    r = x - hi.astype(F32)
    mid = r.astype(BF16)
    lo = (r - mid.astype(F32)).astype(BF16)
    return hi, mid, lo


def _repack_kernel(w_ref, *out_refs, plans):
    for o_ref, ranges in zip(out_refs, plans):
        width = o_ref.shape[-1]
        parts = [w_ref[:, a:b] for a, b in ranges]
        used = sum(b - a for a, b in ranges)
        if used < width:
            parts.append(jnp.zeros((w_ref.shape[0], width - used), F32))
        o_ref[...] = (parts[0] if len(parts) == 1 else jnp.concatenate(parts, axis=1)).astype(o_ref.dtype)


def _repack_call(w_in, plans, widths, rows=128):
    depth, d, n_in = w_in.shape
    kern = functools.partial(_repack_kernel, plans=plans)
    return pl.pallas_call(
        kern,
        grid=(depth, d // rows),
        in_specs=[pl.BlockSpec((None, rows, n_in), lambda l, i: (l, i, 0))],
        out_specs=[pl.BlockSpec((None, rows, w), lambda l, i: (l, i, 0)) for w in widths],
        out_shape=[jax.ShapeDtypeStruct((depth, d, w), BF16) for w in widths],
        compiler_params=_cparams(("parallel", "parallel")),
        name="repack",
    )(w_in)


def _proj_kernel(x_ref, ng_ref, wtok_ref, wT_ref, tokp_ref, gT_ref, rtok_ref, rT_ref, G_ref,
                 omain_ref, og1_ref, og2_ref, okv_ref, oT_ref, stage_ref, *, tok_specs, T_kinds, tm):
    x = x_ref[...]
    ms = jnp.mean(x * x, axis=-1, keepdims=True)
    xn = (x * lax.rsqrt(ms + EPS) * ng_ref[...]).astype(BF16)

    dsts = (omain_ref, og1_ref, og2_ref, okv_ref)
    for c, (has_norm, has_rope, outs) in enumerate(tok_specs):
        y = _dot(xn, wtok_ref[:, c * CH:(c + 1) * CH])
        prm = tokp_ref[c]
        if has_norm:
            ss = _dot((y * y).astype(BF16), G_ref[...])
            inv = lax.rsqrt(ss * (1.0 / HEAD_DIM) + EPS)
            y = y * (jnp.where(prm[0:1, :] > 0.0, inv, 1.0) if has_norm == "mixed" else inv)
        y = y * prm[2:3, :]
        if has_rope:
            halves = []
            for hf in range(CH // LANE):
                yh = y[:, hf * LANE:(hf + 1) * LANE]
                cc, s1, s2 = rtok_ref[0], rtok_ref[1], rtok_ref[2]
                if has_rope == "mixed":
                    rf = prm[1:2, hf * LANE:(hf + 1) * LANE]
                    cc, s1, s2 = jnp.where(rf > 0.0, cc, 1.0), s1 * rf, s2 * rf
                halves.append(yh * cc + pltpu.roll(yh, 8, 1) * s1 + pltpu.roll(yh, LANE - 8, 1) * s2)
            y = jnp.concatenate(halves, axis=1)
        staged = False
        for dst, dil, col, lanes, rstride in outs:
            if dil == 1:
                dsts[dst][:, col:col + lanes] = y[:, :lanes].astype(BF16)
                continue
            if not staged:
                for hf in range(CH // LANE):
                    stage_ref[hf] = y[:, hf * LANE:(hf + 1) * LANE]
                staged = True
            for r in range(dil):
                for hf in range(lanes // LANE):
                    blk = stage_ref[hf, pl.ds(r, tm // dil, stride=dil), :]
                    c0 = r * rstride + col + hf * LANE
                    dsts[dst][:, c0:c0 + LANE] = blk.astype(BF16)

    cosT = rT_ref[0]
    sinT = rT_ref[1]
    for c, heads in enumerate(T_kinds):
        y = _dot(xn, wT_ref[:, c * CH:(c + 1) * CH]).T
        for h, (nrm, rope, scale, sigm) in enumerate(heads):
            r0 = c * CH + h * HEAD_DIM
            yh = y[h * HEAD_DIM:(h + 1) * HEAD_DIM, :]
            if nrm:
                msq = jnp.mean(yh * yh, axis=0, keepdims=True)
                yh = yh * lax.rsqrt(msq + EPS) * gT_ref[r0:r0 + HEAD_DIM, :]
            if rope:
                x1 = yh[0:8, :]
                x2 = yh[8:16, :]
                yh = jnp.concatenate([x1 * cosT - x2 * sinT, x2 * cosT + x1 * sinT, yh[16:, :]], axis=0)
            if scale != 1.0:
                yh = yh * scale
            if sigm:
                yh = jax.nn.sigmoid(yh)
            yb = yh.astype(BF16)
            for t in range(tm // LANE):
                oT_ref[t, r0:r0 + HEAD_DIM, :] = yb[:, t * LANE:(t + 1) * LANE]


def _proj_call(x, ng, wtok, wT, tokp, gT, rtok, rT, G, *, layer, tok_specs, T_kinds, n_main, tm=512):
    B, S, _ = x.shape
    ntok, nT = len(tok_specs), len(T_kinds)
    d1, d2 = DIL_PAIRS[1][1], DIL_PAIRS[2][1]
    kern = functools.partial(_proj_kernel, tok_specs=tok_specs, T_kinds=T_kinds, tm=tm)
    const2 = lambda b, n: (0, 0)
    return pl.pallas_call(
        kern,
        grid=(B, S // tm),
        in_specs=[
            pl.BlockSpec((None, tm, D_MODEL), lambda b, n: (b, n, 0)),
            pl.BlockSpec((1, D_MODEL), const2),
            pl.BlockSpec((None, D_MODEL, ntok * CH), lambda b, n: (layer, 0, 0)),
            pl.BlockSpec((None, D_MODEL, nT * CH), lambda b, n: (layer, 0, 0)),
            pl.BlockSpec((ntok, 8, CH), lambda b, n: (0, 0, 0)),
            pl.BlockSpec((nT * CH, 1), const2),
            pl.BlockSpec((3, tm, LANE), lambda b, n: (0, n, 0)),
            pl.BlockSpec((2, 8, tm), lambda b, n: (0, 0, n)),
            pl.BlockSpec((CH, CH), const2),
        ],
        out_specs=[
            pl.BlockSpec((None, tm, n_main * CH), lambda b, n: (b, n, 0)),
            pl.BlockSpec((None, tm // d1, d1 * 3 * CH), lambda b, n: (b, n, 0)),
            pl.BlockSpec((None, tm // d2, d2 * 3 * CH), lambda b, n: (b, n, 0)),
            pl.BlockSpec((None, tm // NSA_CMP_STRIDE, NSA_CMP_STRIDE * LANE), lambda b, n: (b, n, 0)),
            pl.BlockSpec((None, tm // LANE, nT * CH, LANE), lambda b, n: (b, n, 0, 0)),
        ],
        out_shape=[
            jax.ShapeDtypeStruct((B, S, n_main * CH), BF16),
            jax.ShapeDtypeStruct((B, S // d1, d1 * 3 * CH), BF16),
            jax.ShapeDtypeStruct((B, S // d2, d2 * 3 * CH), BF16),
            jax.ShapeDtypeStruct((B, S // NSA_CMP_STRIDE, NSA_CMP_STRIDE * LANE), BF16),
            jax.ShapeDtypeStruct((B, S // LANE, nT * CH, LANE), BF16),
        ],
        scratch_shapes=[pltpu.VMEM((CH // LANE, tm, LANE), F32)],
        compiler_params=_cparams(("parallel", "parallel")),
        name="proj",
    )(x, ng, wtok, wT, tokp, gT, rtok, rT, G)


def _pair_masked_q(q_pair, h):
    rid = lax.broadcasted_iota(jnp.int32, q_pair.shape, 0)
    lo = (h % 2) * HEAD_DIM
    keep = jnp.where(rid >= lo, jnp.where(rid < lo + HEAD_DIM, 1.0, 0.0), 0.0).astype(BF16)
    return q_pair * keep


def _lane_tiles(ref, t0, nt, r0, nr):
    return jnp.concatenate([ref[t0 + t, r0:r0 + nr, :] for t in range(nt)], axis=1)


BIGPOS = 1e30
ONES_ROWS = 16
HA = HEAD_DIM + ONES_ROWS
LOG2E = 1.4426950408889634


def _v_aug(vb):
    return jnp.concatenate([vb, jnp.ones((ONES_ROWS, vb.shape[1]), BF16)], axis=0)


def _online_cols(s, colsel, m_old, acc_old, vb_aug):
    tmax = jnp.max(s, axis=0, keepdims=True)
    m_new = jnp.where(colsel > 0.0, jnp.maximum(m_old, tmax), m_old)
    m_use = jnp.where(colsel > 0.0, m_new, BIGPOS)
    p = jnp.exp2(s - m_use).astype(BF16)
    acc = jnp.exp2(m_old - m_new) * acc_old + _dot(vb_aug, p)
    return m_new, acc


SB_LOG_CUTOFF = -100.0


def _sb_kernel(qT_ref, k_ref, vT_ref, U_ref, o_ref, qm_ref, acc_ref, carry_ref, *, tq):
    i = pl.program_id(1)
    nt = tq // LANE
    row = lax.broadcasted_iota(jnp.int32, (tq, tq), 0)
    col = lax.broadcasted_iota(jnp.int32, (tq, tq), 1)
    past = row < col
    for h in range(4):
        p = h // 2
        qm_ref[h] = _pair_masked_q(_lane_tiles(qT_ref, 0, nt, p * LANE, LANE), h)
    acc_ref[...] = jnp.zeros_like(acc_ref)
    carry_ref[...] = jnp.zeros_like(carry_ref)

    def tile(j, masked):
        scores, logsig, laters = [], [], []
        for h in range(4):
            p = h // 2
            kb = k_ref[pl.ds(pl.multiple_of(j * tq, tq), tq), p * LANE:(p + 1) * LANE]
            scores.append(_dot(kb, qm_ref[h]))
        worst = None
        for h in range(4):
            s = scores[h]
            sp = jnp.maximum(s, 0.0) + jnp.log(1.0 + jnp.exp(-jnp.abs(s)))
            lg = -sp
            if masked:
                lg = jnp.where(past, lg, 0.0)
            hi, lo = _split2(lg)
            carry = carry_ref[h:h + 1, :]
            laters.append(_dot(U_ref[...], hi) + _dot(U_ref[...], lo) + carry)
            logsig.append(s - sp)
            carry = carry + jnp.sum(lg, axis=0, keepdims=True)
            carry_ref[h:h + 1, :] = carry
            worst = carry if worst is None else jnp.maximum(worst, carry)
        for h in range(4):
            w = jnp.exp(logsig[h] + laters[h])
            if masked:
                w = jnp.where(past, w, 0.0)
            vb = _lane_tiles(vT_ref, j * nt, nt, h * HEAD_DIM, HEAD_DIM)
            acc_ref[h * HEAD_DIM:(h + 1) * HEAD_DIM, :] += _dot(vb, w.astype(BF16))
        return jnp.max(worst)

    worst0 = tile(i, True)

    def cond(st):
        return jnp.logical_and(st[0] >= 0, st[1] > SB_LOG_CUTOFF)

    def body(st):
        return st[0] - 1, tile(st[0], False)

    lax.while_loop(cond, body, (i - 1, worst0))
    o_ref[...] = acc_ref[...].T.astype(o_ref.dtype)


def _attn_specs(S, tq, q_chunk, k_chunk, v_chunk):
    return [
        pl.BlockSpec((None, tq // LANE, CH, LANE), lambda b, i: (b, i, q_chunk, 0)),
        pl.BlockSpec((None, S, CH), lambda b, i: (b, 0, k_chunk)),
        pl.BlockSpec((None, S // LANE, CH, LANE), lambda b, i: (b, 0, v_chunk, 0)),
    ]


def _sb_call(oT, otok, U, *, q_chunk, k_chunk, v_chunk, tq=256):
    B, nlt, _, _ = oT.shape
    S = nlt * LANE
    kern = functools.partial(_sb_kernel, tq=tq)
    return pl.pallas_call(
        kern,
        grid=(B, S // tq),
        in_specs=_attn_specs(S, tq, q_chunk, k_chunk, v_chunk) + [pl.BlockSpec((tq, tq), lambda b, i: (0, 0))],
        out_specs=pl.BlockSpec((None, tq, BRANCH_W), lambda b, i: (b, i, 0)),
        out_shape=jax.ShapeDtypeStruct((B, S, BRANCH_W), BF16),
        scratch_shapes=[pltpu.VMEM((4, LANE, tq), BF16), pltpu.VMEM((CH, tq), F32), pltpu.VMEM((8, tq), F32)],
        compiler_params=_cparams(("parallel", "arbitrary")),
        name="stick_breaking",
    )(oT, otok, oT, U)


def _head_lanes(x, lane, h):
    lo = h * HEAD_DIM
    return jnp.where(lane >= lo, jnp.where(lane < lo + HEAD_DIM, x, jnp.zeros_like(x)), jnp.zeros_like(x))


def _band_kernel(q_ref, k_ref, v_ref, ind_ref, o_ref, lse_ref, *, tqb, max_dist):
    n = pl.program_id(2)
    n_prev = -(-max_dist // LANE)
    nkr = (n_prev + 1) * LANE
    nsub = tqb // LANE
    row = lax.broadcasted_iota(jnp.int32, (nkr, LANE), 0)
    col = lax.broadcasted_iota(jnp.int32, (nkr, LANE), 1)
    lane_q = lax.broadcasted_iota(jnp.int32, (LANE, LANE), 1)
    lane_v = lax.broadcasted_iota(jnp.int32, (nkr, CH), 1)
    for u in range(nsub):
        qt = n * nsub + u
        kt0 = jnp.maximum(qt - n_prev, 0)
        dist = (qt - kt0) * LANE + col - row
        bias = jnp.where(dist >= 0, jnp.where(dist <= max_dist, 0.0, NEG), NEG)
        k0 = pl.multiple_of(kt0 * LANE, LANE)
        kwin = k_ref[pl.ds(k0, nkr), :]
        vwin = v_ref[pl.ds(k0, nkr), :]
        qu = q_ref[u * LANE:(u + 1) * LANE, :]
        o_acc = None
        lses = []
        for h in range(4):
            p = h // 2
            qm = _head_lanes(qu[:, p * LANE:(p + 1) * LANE], lane_q, h % 2)
            s = _dot_nt(kwin[:, p * LANE:(p + 1) * LANE], qm) + bias
            m = jnp.max(s, axis=0, keepdims=True)
            e = jnp.exp2(s - m)
            den = jnp.sum(e, axis=0, keepdims=True)
            pn = e * (1.0 / den)
            contrib = _dot(pn.T.astype(BF16), _head_lanes(vwin, lane_v, h))
            o_acc = contrib if o_acc is None else o_acc + contrib
            lses.append(m + jnp.log(den) * LOG2E)
        o_ref[u * LANE:(u + 1) * LANE, :] = o_acc.astype(o_ref.dtype)
        ls = jnp.concatenate(lses + [jnp.zeros((LANE - 4, LANE), F32)], axis=0).T
        l1, l2, l3 = _split3(ls)
        lse_ref[u * LANE:(u + 1) * LANE, :] = (_dot(l1, ind_ref[...]) + _dot(l2, ind_ref[...])
                                               + _dot(l3, ind_ref[...]))


def _band_call(arr, ind, *, dil, per_res, qi, ki, vi, max_dist):
    B, L, _ = arr.shape
    tqb = min(512, L)
    kern = functools.partial(_band_kernel, tqb=tqb, max_dist=max_dist)
    return pl.pallas_call(
        kern,
        grid=(B, dil, L // tqb),
        in_specs=[
            pl.BlockSpec((None, tqb, CH), lambda b, r, n: (b, n, r * per_res + qi)),
            pl.BlockSpec((None, L, CH), lambda b, r, n: (b, 0, r * per_res + ki)),
            pl.BlockSpec((None, L, CH), lambda b, r, n: (b, 0, r * per_res + vi)),
            pl.BlockSpec((LANE, CH), lambda b, r, n: (0, 0)),
        ],
        out_specs=[
            pl.BlockSpec((None, tqb, BRANCH_W), lambda b, r, n: (b, n, r)),
            pl.BlockSpec((None, tqb, BRANCH_W), lambda b, r, n: (b, n, r)),
        ],
        out_shape=[
            jax.ShapeDtypeStruct((B, L, dil * BRANCH_W), BF16),
            jax.ShapeDtypeStruct((B, L, dil * BRANCH_W), F32),
        ],
        compiler_params=_cparams(("parallel", "parallel", "parallel")),
        name="banded",
    )(arr, arr, arr, ind)


def _moba_kernel(qT_ref, k_ref, vT_ref, A_ref, o_ref, kmean_ref, sel_ref, qm_ref, acc_ref, ml_ref, s_ref, *, tq):
    i = pl.program_id(1)
    nt = tq // LANE
    nblk = A_ref.shape[0]

    @pl.when(i == 0)
    def _():
        kmean_ref[...] = _dot(A_ref[...], k_ref[...])

    blk = lax.broadcasted_iota(jnp.int32, (nblk, tq), 0)
    row = lax.broadcasted_iota(jnp.int32, (tq, tq), 0)
    col = lax.broadcasted_iota(jnp.int32, (tq, tq), 1)
    gates, diag = [], []
    for h in range(4):
        p = h // 2
        qm = _pair_masked_q(_lane_tiles(qT_ref, 0, nt, p * LANE, LANE), h)
        qm_ref[h] = qm
        k1, k2, k3 = _split3(kmean_ref[:, p * LANE:(p + 1) * LANE])
        gates.append(_dot(k1, qm) + _dot(k2, qm) + _dot(k3, qm))
        kb = k_ref[pl.ds(pl.multiple_of(i * tq, tq), tq), p * LANE:(p + 1) * LANE]
        diag.append(_dot(kb, qm))
    for h in range(4):
        g = jnp.where(blk < i, gates[h], NEG)
        sel = jnp.zeros((nblk, tq), F32)
        for _r in range(MOBA_TOPK):
            mx = jnp.max(g, axis=0, keepdims=True)
            idx = jnp.min(jnp.where(g == mx, blk, nblk), axis=0, keepdims=True)
            hit = blk == idx
            sel = jnp.where(hit, 1.0, sel)
            g = jnp.where(hit, -jnp.inf, g)
        sel_ref[h] = jnp.where(blk < i, sel, 0.0)

        s = jnp.where(row <= col, diag[h], NEG)
        m = jnp.max(s, axis=0, keepdims=True)
        ml_ref[h:h + 1, :] = m
        acc_ref[h * HA:(h + 1) * HA, :] = _dot(
            _v_aug(_lane_tiles(vT_ref, i * nt, nt, h * HEAD_DIM, HEAD_DIM)), jnp.exp2(s - m).astype(BF16))

    last = jnp.maximum(i - 1, 0)

    def qk(t, slot):
        for h in range(4):
            p = h // 2
            kbj = k_ref[pl.ds(pl.multiple_of(t * tq, tq), tq), p * LANE:(p + 1) * LANE]
            s_ref[slot, h] = _dot(kbj, qm_ref[h])

    def update(t, valid, slot):
        for h in range(4):
            srow = sel_ref[h, pl.ds(t, 1), :] * valid
            vb = _v_aug(_lane_tiles(vT_ref, t * nt, nt, h * HEAD_DIM, HEAD_DIM))
            rows = slice(h * HA, (h + 1) * HA)
            m, acc = _online_cols(s_ref[slot, h], srow, ml_ref[h:h + 1, :], acc_ref[rows, :], vb)
            ml_ref[h:h + 1, :] = m
            acc_ref[rows, :] = acc

    qk(0, 0)

    def body(jj, c):
        t0 = 2 * jj
        qk(jnp.minimum(t0 + 1, last), 1)
        update(t0, 1.0, 0)
        qk(jnp.minimum(t0 + 2, last), 0)
        update(jnp.minimum(t0 + 1, last), (t0 + 1 < i).astype(F32), 1)
        return c

    lax.fori_loop(0, (i + 1) // 2, body, 0)
    outs = [acc_ref[h * HA:h * HA + HEAD_DIM, :] / acc_ref[h * HA + HEAD_DIM:h * HA + HEAD_DIM + 1, :]
            for h in range(4)]
    o_ref[...] = jnp.concatenate(outs, axis=0).T.astype(o_ref.dtype)


def _moba_call(oT, otok, A, *, q_chunk, k_chunk, v_chunk):
    B, nlt, _, _ = oT.shape
    S = nlt * LANE
    tq = MOBA_BLOCK
    nblk = S // MOBA_BLOCK
    kern = functools.partial(_moba_kernel, tq=tq)
    return pl.pallas_call(
        kern,
        grid=(B, S // tq),
        in_specs=_attn_specs(S, tq, q_chunk, k_chunk, v_chunk) + [pl.BlockSpec((nblk, S), lambda b, i: (0, 0))],
        out_specs=pl.BlockSpec((None, tq, BRANCH_W), lambda b, i: (b, i, 0)),
        out_shape=jax.ShapeDtypeStruct((B, S, BRANCH_W), BF16),
        scratch_shapes=[pltpu.VMEM((nblk, CH), F32), pltpu.VMEM((4, nblk, tq), F32),
                        pltpu.VMEM((4, LANE, tq), BF16), pltpu.VMEM((4 * HA, tq), F32), pltpu.VMEM((8, tq), F32),
                        pltpu.VMEM((2, 4, tq, tq), F32)],
        compiler_params=_cparams(("parallel", "arbitrary")),
        name="moba",
    )(oT, otok, oT, A)


def _compress_kernel(x_ref, w1_ref, pe_ref, w1f_ref, w2_ref, gk_ref, G_ref, o_ref, oT_ref, acc_ref):
    l = pl.program_id(1)

    @pl.when(l == 0)
    def _():
        acc_ref[...] = jnp.zeros_like(acc_ref)

    x = x_ref[...]
    w_hi, w_lo = _split2(w1_ref[...])
    acc_ref[...] += _dot(x, w_hi) + _dot(x, w_lo)

    @pl.when(l == pl.num_programs(1) - 1)
    def _():
        nc = acc_ref.shape[0]
        r = acc_ref[...]
        bias = jnp.dot(pe_ref[...], w1f_ref[...], preferred_element_type=F32,
                       precision=lax.Precision.HIGHEST)
        p1 = jnp.concatenate([r[:, 0:64], r[:, 128:192]], axis=1)
        p2 = jnp.concatenate([r[:, 64:128], r[:, 192:256]], axis=1)
        hid = p1 + pltpu.roll(p2, nc - 1, 0) + bias[0:1, :]
        hid = hid * jax.nn.sigmoid(hid)
        comp = jnp.dot(hid, w2_ref[...], preferred_element_type=F32, precision=lax.Precision.HIGHEST)
        ss = _dot((comp * comp).astype(BF16), G_ref[...])
        lane = lax.broadcasted_iota(jnp.int32, comp.shape, 1)
        inv = jnp.where(lane < HEAD_DIM, lax.rsqrt(ss * (1.0 / HEAD_DIM) + EPS), 1.0)
        comp = comp * inv * gk_ref[...]
        rowi = lax.broadcasted_iota(jnp.int32, comp.shape, 0)
        comp = jnp.where(rowi < nc - 1, comp, 0.0)
        o_ref[...] = comp.astype(o_ref.dtype)
        oT_ref[...] = comp.T.astype(oT_ref.dtype)


def _compress_call(xv, w1blk, pe2, w1f, w2blk, gk, G128):
    B, nc, _ = xv.shape
    return pl.pallas_call(
        _compress_kernel,
        grid=(B, NSA_CMP_STRIDE),
        in_specs=[
            pl.BlockSpec((None, nc, LANE), lambda b, l: (b, 0, l)),
            pl.BlockSpec((None, LANE, CH), lambda b, l: (l, 0, 0)),
            pl.BlockSpec((8, 2 * NSA_CMP_LEN * HEAD_DIM), lambda b, l: (0, 0)),
            pl.BlockSpec((2 * NSA_CMP_LEN * HEAD_DIM, LANE), lambda b, l: (0, 0)),
            pl.BlockSpec((LANE, LANE), lambda b, l: (0, 0)),
            pl.BlockSpec((1, LANE), lambda b, l: (0, 0)),
            pl.BlockSpec((LANE, LANE), lambda b, l: (0, 0)),
        ],
        out_specs=[
            pl.BlockSpec((None, nc, LANE), lambda b, l: (b, 0, 0)),
            pl.BlockSpec((None, LANE, nc), lambda b, l: (b, 0, 0)),
        ],
        out_shape=[
            jax.ShapeDtypeStruct((B, nc, LANE), BF16),
            jax.ShapeDtypeStruct((B, LANE, nc), BF16),
        ],
        scratch_shapes=[pltpu.VMEM((nc, CH), F32)],
        compiler_params=_cparams(("parallel", "arbitrary")),
        name="nsa_compress",
    )(xv, w1blk, pe2, w1f, w2blk, gk, G128)


def _nsa_kernel(qT_ref, k_ref, vT_ref, kc_ref, kcT_ref, ov_ref, o_ref, sel_ref, qm_ref, acc_ref, ml_ref, s_ref,
                *, tq):
    i = pl.program_id(1)
    nt = tq // LANE
    nq = 4 * tq
    nc = kc_ref.shape[0]
    nsel = ov_ref.shape[0]
    spb = tq // NSA_SEL_BLOCK

    zeros = jnp.zeros((HEAD_DIM, tq), BF16)
    q_heads = [_lane_tiles(qT_ref, 0, nt, h * HEAD_DIM, HEAD_DIM) for h in range(4)]
    qm_lo = jnp.concatenate([jnp.concatenate([q, zeros], axis=0) for q in q_heads], axis=1)
    qm_hi = jnp.concatenate([jnp.concatenate([zeros, q], axis=0) for q in q_heads], axis=1)
    qpos = i * tq + lax.broadcasted_iota(jnp.int32, (1, tq), 1)
    qpos4 = jnp.concatenate([qpos] * 4, axis=1)

    zc = _dot(kc_ref[...], qm_lo)
    c_end = lax.broadcasted_iota(jnp.int32, (nc, nq), 0) * NSA_CMP_STRIDE + (NSA_CMP_LEN - 1)
    cmask = c_end <= qpos4
    zc = jnp.where(cmask, zc, NEG)
    e = jnp.where(cmask, jnp.exp2(zc - jnp.max(zc, axis=0, keepdims=True)), 0.0)
    pc = e / jnp.maximum(jnp.sum(e, axis=0, keepdims=True), 1.0)
    o_cmp = _dot(kcT_ref[HEAD_DIM:2 * HEAD_DIM, :], pc.astype(BF16))

    psum = pc[:, 0:tq] + pc[:, tq:2 * tq] + pc[:, 2 * tq:3 * tq] + pc[:, 3 * tq:4 * tq]
    p_hi, p_lo = _split2(psum)
    imp = _dot(ov_ref[...], p_hi) + _dot(ov_ref[...], p_lo)
    nid = lax.broadcasted_iota(jnp.int32, (nsel, tq), 0)
    cur = qpos // NSA_SEL_BLOCK
    imp = jnp.where(nid == 0, BIG, imp)
    imp = jnp.where(nid == cur, BIG, imp)
    imp = jnp.where(nid == cur - 1, BIG, imp)
    imp = jnp.where(nid > cur, NEG, imp)
    sel = jnp.zeros((nsel, tq), F32)
    for _r in range(min(NSA_SEL_TOPK, nsel)):
        mx = jnp.max(imp, axis=0, keepdims=True)
        idx = jnp.min(jnp.where(imp == mx, nid, nsel), axis=0, keepdims=True)
        hit = nid == idx
        sel = jnp.where(hit, 1.0, sel)
        imp = jnp.where(hit, -jnp.inf, imp)
    sel = jnp.where(nid <= cur, sel, 0.0)
    sel_ref[...] = sel
    for h in range(4):
        qm_ref[h] = qm_lo[:, h * tq:(h + 1) * tq]

    def qk(t, slot):
        kb = k_ref[pl.ds(pl.multiple_of(t * tq, tq), tq), LANE:2 * LANE]
        for h in range(4):
            s_ref[slot, h] = _dot(kb, qm_ref[h])

    kpos_d = i * tq + lax.broadcasted_iota(jnp.int32, (tq, tq), 0)
    qk(i, 0)
    vb_d = _v_aug(_lane_tiles(vT_ref, i * nt, nt, 0, HEAD_DIM))
    srows_d = [sel_ref[pl.ds(i * spb + u, 1), :] for u in range(spb)]
    for h in range(4):
        sc = s_ref[0, h]
        s = jnp.concatenate(
            [jnp.where(srows_d[u] > 0.0, sc[u * NSA_SEL_BLOCK:(u + 1) * NSA_SEL_BLOCK, :], NEG)
             for u in range(spb)], axis=0)
        s = jnp.where(kpos_d <= qpos, s, NEG)
        m = jnp.max(s, axis=0, keepdims=True)
        lanes = slice(h * tq, (h + 1) * tq)
        ml_ref[0:1, lanes] = m
        acc_ref[:, lanes] = _dot(vb_d, jnp.exp2(s - m).astype(BF16))

    last = jnp.maximum(i - 1, 0)

    def update(t, valid, slot):
        vb = _v_aug(_lane_tiles(vT_ref, t * nt, nt, 0, HEAD_DIM))
        srows = [sel_ref[pl.ds(t * spb + u, 1), :] * valid for u in range(spb)]
        for h in range(4):
            lanes = slice(h * tq, (h + 1) * tq)
            sc = s_ref[slot, h]
            subs = [sc[u * NSA_SEL_BLOCK:(u + 1) * NSA_SEL_BLOCK, :] for u in range(spb)]
            tmax = None
            for u in range(spb):
                mu = jnp.where(srows[u] > 0.0, jnp.max(subs[u], axis=0, keepdims=True), NEG)
                tmax = mu if tmax is None else jnp.maximum(tmax, mu)
            m_old = ml_ref[0:1, lanes]
            m_new = jnp.maximum(m_old, tmax)
            p = jnp.concatenate(
                [jnp.exp2(subs[u] - jnp.where(srows[u] > 0.0, m_new, BIGPOS)).astype(BF16) for u in range(spb)],
                axis=0)
            ml_ref[0:1, lanes] = m_new
            acc_ref[:, lanes] = jnp.exp2(m_old - m_new) * acc_ref[:, lanes] + _dot(vb, p)

    qk(0, 0)

    def body(jj, c):
        t0 = 2 * jj
        qk(jnp.minimum(t0 + 1, last), 1)
        update(t0, 1.0, 0)
        qk(jnp.minimum(t0 + 2, last), 0)
        update(jnp.minimum(t0 + 1, last), (t0 + 1 < i).astype(F32), 1)
        return c

    lax.fori_loop(0, (i + 1) // 2, body, 0)
    o_sel = acc_ref[0:HEAD_DIM, :] / acc_ref[HEAD_DIM:HEAD_DIM + 1, :]

    n_prev = -(-(NSA_WINDOW - 1) // tq)
    nk = n_prev + 1
    kt0 = jnp.maximum(i - n_prev, 0)
    kw = k_ref[pl.ds(pl.multiple_of(kt0 * tq, tq), nk * tq), LANE:2 * LANE]
    sw = _dot(kw, qm_hi)
    dist = qpos4 - (kt0 * tq + lax.broadcasted_iota(jnp.int32, (nk * tq, nq), 0))
    sw = jnp.where(dist >= 0, jnp.where(dist <= NSA_WINDOW - 1, sw, NEG), NEG)
    mw = jnp.max(sw, axis=0, keepdims=True)
    ew = jnp.exp2(sw - mw)
    dw = jnp.sum(ew, axis=0, keepdims=True)
    vwin = _lane_tiles(vT_ref, kt0 * nt, nk * nt, HEAD_DIM, HEAD_DIM)
    o_win = _dot(vwin, ew.astype(BF16)) / dw

    gates = _lane_tiles(vT_ref, i * nt, nt, 2 * HEAD_DIM, 16).astype(F32)
    outs = []
    for h in range(4):
        sl = slice(h * tq, (h + 1) * tq)
        outs.append(gates[3 * h:3 * h + 1, :] * o_cmp[:, sl]
                    + gates[3 * h + 1:3 * h + 2, :] * o_sel[:, sl]
                    + gates[3 * h + 2:3 * h + 3, :] * o_win[:, sl])
    o_ref[...] = jnp.concatenate(outs, axis=0).T.astype(o_ref.dtype)


def _nsa_call(oT, otok, kc, kcT, ovT, *, q_chunk, kv_chunk, v_chunk):
    B, nlt, _, _ = oT.shape
    S = nlt * LANE
    tq = 256
    nc = S // NSA_CMP_STRIDE
    nsel = S // NSA_SEL_BLOCK
    kern = functools.partial(_nsa_kernel, tq=tq)
    return pl.pallas_call(
        kern,
        grid=(B, S // tq),
        in_specs=_attn_specs(S, tq, q_chunk, kv_chunk, v_chunk) + [
            pl.BlockSpec((None, nc, LANE), lambda b, i: (b, 0, 0)),
            pl.BlockSpec((None, LANE, nc), lambda b, i: (b, 0, 0)),
            pl.BlockSpec((nsel, nc), lambda b, i: (0, 0)),
        ],
        out_specs=pl.BlockSpec((None, tq, BRANCH_W), lambda b, i: (b, i, 0)),
        out_shape=jax.ShapeDtypeStruct((B, S, BRANCH_W), BF16),
        scratch_shapes=[pltpu.VMEM((nsel, tq), F32), pltpu.VMEM((4, LANE, tq), BF16),
                        pltpu.VMEM((HA, 4 * tq), F32), pltpu.VMEM((8, 4 * tq), F32),
                        pltpu.VMEM((2, 4, tq, tq), F32)],
        compiler_params=_cparams(("parallel", "arbitrary")),
        name="nsa",
    )(oT, otok, oT, kc, kcT, ovT)


def _epi_kernel(x_ref, ng_ref, ya_ref, yc_ref, yd_ref, ob0_ref, ob1_ref, ob2_ref, l0_ref, l1_ref, l2_ref,
                za_ref, zb_ref, zc_ref, zd_ref, wmg_ref, wbr_ref, wout_ref, o_ref, nat_ref, *, tm):
    x = x_ref[...]
    ms = jnp.mean(x * x, axis=-1, keepdims=True)
    xn = (x * lax.rsqrt(ms + EPS) * ng_ref[...]).astype(BF16)

    def natural(ref, slot, dil):
        nh = BRANCH_W // LANE
        for r in range(dil):
            for hf in range(nh):
                c0 = r * BRANCH_W + hf * LANE
                nat_ref[slot * nh + hf, pl.ds(r, tm // dil, stride=dil), :] = ref[:, c0:c0 + LANE].astype(F32)
        return jnp.concatenate([nat_ref[slot * nh + hf] for hf in range(nh)], axis=1)

    d1, d2 = DIL_PAIRS[1][1], DIL_PAIRS[2][1]
    l0, l1, l2 = l0_ref[...], natural(l1_ref, 0, d1), natural(l2_ref, 1, d2)
    mx = jnp.maximum(jnp.maximum(l0, l1), l2)
    e0, e1, e2 = jnp.exp2(l0 - mx), jnp.exp2(l1 - mx), jnp.exp2(l2 - mx)
    yb = (e0 * ob0_ref[...].astype(F32) + e1 * natural(ob1_ref, 2, d1)
          + e2 * natural(ob2_ref, 3, d2)) / (e0 + e1 + e2)

    ys = (ya_ref[...].astype(F32), yb, yc_ref[...].astype(F32), yd_ref[...].astype(F32))
    zs = (za_ref, zb_ref, zc_ref, zd_ref)
    merged = None
    for i in range(4):
        z = zs[i][...].astype(F32)
        gated = (ys[i] * (z * jax.nn.sigmoid(z))).astype(BF16)
        br = _dot(gated, wbr_ref[i])
        mg = _dot(xn, wmg_ref[:, i * D_MODEL:(i + 1) * D_MODEL])
        term = jax.nn.sigmoid(mg) * br
        merged = term if merged is None else merged + term
    o_ref[...] = x + _dot(merged.astype(BF16), wout_ref[...])


def _epi_call(x2, ng, ya, yc, yd, obs, lses, otok2, wmg, wbr, wout, *, layer, z_chunks, tm=512):
    T = x2.shape[0]
    row = lambda i: (i, 0)
    full2 = lambda i: (0, 0)
    yspec = pl.BlockSpec((tm, BRANCH_W), row)
    zspecs = [pl.BlockSpec((tm, CH), (lambda i, c=c: (i, c))) for c in z_chunks]
    d1, d2 = DIL_PAIRS[1][1], DIL_PAIRS[2][1]
    gspecs = [yspec, pl.BlockSpec((tm // d1, d1 * BRANCH_W), row), pl.BlockSpec((tm // d2, d2 * BRANCH_W), row)]
    return pl.pallas_call(
        functools.partial(_epi_kernel, tm=tm),
        grid=(T // tm,),
        in_specs=[pl.BlockSpec((tm, D_MODEL), row), pl.BlockSpec((1, D_MODEL), full2)]
        + [yspec] * 3 + gspecs + gspecs + zspecs
        + [pl.BlockSpec((None, D_MODEL, 4 * D_MODEL), lambda i: (layer, 0, 0)),
           pl.BlockSpec((None, 4, BRANCH_W, D_MODEL), lambda i: (layer, 0, 0, 0)),
           pl.BlockSpec((None, D_MODEL, D_MODEL), lambda i: (layer, 0, 0))],
        out_specs=pl.BlockSpec((tm, D_MODEL), row),
        out_shape=jax.ShapeDtypeStruct((T, D_MODEL), F32),
        scratch_shapes=[pltpu.VMEM((4 * (BRANCH_W // LANE), tm, LANE), F32)],
        compiler_params=_cparams(("parallel",)),
        name="epilogue",
    )(x2, ng, ya, yc, yd, *obs, *lses, otok2, otok2, otok2, otok2, wmg, wbr, wout)


TOK_AK, TOK_AZ, TOK_BQ, TOK_BK, TOK_BV, TOK_BZ, TOK_CK, TOK_CZ, TOK_DKV, TOK_DZ = range(10)
N_MAIN = 10
T_AQ, T_AV, T_CQ, T_CV, T_DQ, T_DX = range(6)
SCALE = 1.0 / math.sqrt(HEAD_DIM)
QSCALE2 = SCALE * LOG2E

_PLAIN = (False, False, 1.0, False)
_T_KINDS = (
    ((False, False, SCALE, False),) * 4,
    (_PLAIN,) * 4,
    ((True, True, QSCALE2, False),) * 4,
    (_PLAIN,) * 4,
    ((True, True, QSCALE2, False),) * 4,
    (_PLAIN, _PLAIN, (False, False, 1.0, True), _PLAIN),
)


def _tok_specs():
    main = lambda j: (0, 1, j * CH, CH, 0)
    qkv = ((True, True), (True, True), (False, False))
    specs = []
    for j in range(N_MAIN):
        if j in (TOK_BQ, TOK_BK, TOK_CK):
            specs.append((True, True, (main(j),)))
        elif j == TOK_DKV:
            specs.append(("mixed", "mixed", (main(j), (3, NSA_CMP_STRIDE, 0, LANE, LANE))))
        else:
            specs.append((False, False, (main(j),)))
    for gi in (1, 2):
        dil = DIL_PAIRS[gi][1]
        for k, (nrm, rope) in enumerate(qkv):
            specs.append((nrm, rope, ((gi, dil, k * CH, CH, 3 * CH),)))
    return tuple(specs)


def _rope_tables(S, dil):
    L = S // dil
    pos = (jnp.arange(dil, dtype=jnp.int32)[:, None] + dil * jnp.arange(L, dtype=jnp.int32)[None, :]).astype(F32)
    inv = ROPE_THETA ** (-jnp.arange(0, ROT_DIM, 2, dtype=F32) / ROT_DIM)
    ang = pos[:, :, None] * inv[None, None, :]
    cos, sin = jnp.cos(ang), jnp.sin(ang)
    one = jnp.ones((dil, L, HEAD_DIM - ROT_DIM), F32)
    zero8 = jnp.zeros((dil, L, 8), F32)
    zero = jnp.zeros_like(one)
    c_head = jnp.concatenate([cos, cos, one], axis=-1)
    s1_head = jnp.concatenate([zero8, sin, zero], axis=-1)
    s2_head = jnp.concatenate([-sin, zero8, zero], axis=-1)
    rtok = jnp.stack([jnp.tile(t, (1, 1, LANE // HEAD_DIM)) for t in (c_head, s1_head, s2_head)], axis=1)
    rT = jnp.stack([cos.transpose(0, 2, 1), sin.transpose(0, 2, 1)], axis=1)
    return rtok, rT


def _tok_params(rows):
    out = []
    for nf, rf, gains in rows:
        nrow = jnp.concatenate([jnp.full((HEAD_DIM,), float(f), F32) for f in nf])
        rrow = jnp.concatenate([jnp.full((HEAD_DIM,), float(f), F32) for f in rf])
        grow = jnp.concatenate([g.astype(F32) for g in gains])
        out.append(jnp.concatenate([jnp.stack([nrow, rrow, grow]), jnp.zeros((5, CH), F32)], axis=0))
    return jnp.stack(out)


def kernel(x, norm_g, w_in, qk_g, cmp_pe, cmp_w1, cmp_w2, w_branch, w_out):
    B, S, _ = x.shape
    T = B * S
    o = COL_OFF
    ones = jnp.ones((HEAD_DIM,), F32)

    r = np.arange(CH)
    G = jnp.asarray((r[:, None] // HEAD_DIM == r[None, :] // HEAD_DIM).astype(np.float32), BF16)
    G128 = G[:LANE, :LANE]
    t = np.arange(256)
    U = jnp.asarray((t[None, :] > t[:, None]).astype(np.float32), BF16)
    nblk = S // MOBA_BLOCK
    A = jnp.asarray(np.repeat(np.eye(nblk, dtype=np.float32), MOBA_BLOCK, axis=1) / MOBA_BLOCK, BF16)
    nc, nsel = S // NSA_CMP_STRIDE, S // NSA_SEL_BLOCK
    cs = np.arange(nc) * NSA_CMP_STRIDE
    ss = np.arange(nsel) * NSA_SEL_BLOCK
    ov = ((cs[None, :] < ss[:, None] + NSA_SEL_BLOCK) & (cs[None, :] + NSA_CMP_LEN > ss[:, None]))
    ov[:, nc - 1] = False
    ovT = jnp.asarray(ov.astype(np.float32), BF16)
    hid = np.arange(LANE)[:, None]
    ind = jnp.asarray(((hid < 4) & (r[None, :] // HEAD_DIM == hid)).astype(np.float32), BF16)
    rtok4, rT4 = _rope_tables(S, 1)
    rtok, rT = rtok4[0], rT4[0]
    tok_specs = _tok_specs()

    dkv = o[13]
    bq = lambda gi: (o[4] + gi * CH, o[4] + (gi + 1) * CH)
    bk = lambda gi: (o[5] + gi * CH, o[5] + (gi + 1) * CH)
    bv = lambda gi: (o[6] + gi * CH, o[6] + (gi + 1) * CH)
    tok_plan = ((o[1], o[2]), (o[3], o[4]), bq(0), bk(0), bv(0), (o[7], o[8]), (o[9], o[10]), (o[11], o[12]),
                (dkv, dkv + 64), (dkv + 64, dkv + 128), (dkv + 128, dkv + 192), (dkv + 256, dkv + 320),
                (o[15], o[16]), bq(1), bk(1), bv(1), bq(2), bk(2), bv(2))
    T_plan = ((o[0], o[1]), (o[2], o[3]), (o[8], o[9]), (o[10], o[11]), (o[12], o[13]),
              (dkv + 192, dkv + 256), (dkv + 320, dkv + 384), (o[14], o[15]))
    wtok_all, wT_all, wmg_all = _repack_call(
        w_in, (tok_plan, T_plan, ((o[16], o[17]),)), (len(tok_specs) * CH, len(_T_KINDS) * CH, 4 * D_MODEL))

    wbr_all = w_branch.astype(BF16)
    wout_all = w_out.astype(BF16)

    for l in range(DEPTH):
        g = qk_g[l]
        plain = ((0,) * 4, (0,) * 4, (ones,) * 4)
        bq_p = ((1,) * 4, (1,) * 4, (g[0] * QSCALE2,) * 4)
        bk_p = ((1,) * 4, (1,) * 4, (g[1],) * 4)
        tokp = _tok_params([
            plain, plain, bq_p, bk_p, plain, plain,
            ((1,) * 4, (1,) * 4, (g[3],) * 4), plain,
            ((0, 0, 1, 1), (1, 0, 1, 1), (ones, ones, g[6], g[7])), plain,
            bq_p, bk_p, plain, bq_p, bk_p, plain])
        gT = jnp.concatenate([jnp.tile(ones, 8), jnp.tile(g[2], 4), jnp.tile(ones, 4), jnp.tile(g[4], 4),
                              jnp.tile(ones, 4)])[:, None]
        ng = norm_g[l][None, :]
        otok, og1, og2, okv, oT = _proj_call(x, ng, wtok_all, wT_all, tokp, gT, rtok, rT, G, layer=l,
                                             tok_specs=tok_specs, T_kinds=_T_KINDS, n_main=N_MAIN)

        ya = _sb_call(oT, otok, U, q_chunk=T_AQ, k_chunk=TOK_AK, v_chunk=T_AV)

        obs, lses = [], []
        for gi, (window, dil) in enumerate(DIL_PAIRS):
            if dil == 1:
                ob, lse = _band_call(otok, ind, dil=1, per_res=N_MAIN, qi=TOK_BQ, ki=TOK_BK, vi=TOK_BV,
                                     max_dist=window)
            else:
                ob, lse = _band_call((og1, og2)[gi - 1], ind, dil=dil, per_res=3, qi=0, ki=1, vi=2,
                                     max_dist=window // dil)
            obs.append(ob.reshape(T // dil, dil * BRANCH_W))
            lses.append(lse.reshape(T // dil, dil * BRANCH_W))

        yc = _moba_call(oT, otok, A, q_chunk=T_CQ, k_chunk=TOK_CK, v_chunk=T_CV)

        w1 = cmp_w1[l].reshape(2, NSA_CMP_LEN, HEAD_DIM, HEAD_DIM)
        z64 = jnp.zeros((NSA_CMP_STRIDE, HEAD_DIM, HEAD_DIM), F32)
        w1blk = jnp.concatenate([
            jnp.concatenate([w1[0, :16], w1[0, 16:], z64, z64], axis=2),
            jnp.concatenate([z64, z64, w1[1, :16], w1[1, 16:]], axis=2)], axis=1)
        pe2 = jnp.concatenate([cmp_pe[l].reshape(1, -1), jnp.zeros((7, 2 * NSA_CMP_LEN * HEAD_DIM), F32)], axis=0)
        zf = jnp.zeros((NSA_CMP_LEN * HEAD_DIM, HEAD_DIM), F32)
        w1f = jnp.concatenate([jnp.concatenate([cmp_w1[l, 0], zf], axis=1),
                               jnp.concatenate([zf, cmp_w1[l, 1]], axis=1)], axis=0)
        z2 = jnp.zeros((HEAD_DIM, HEAD_DIM), F32)
        w2blk = jnp.concatenate([jnp.concatenate([cmp_w2[l, 0], z2], axis=1),
                                 jnp.concatenate([z2, cmp_w2[l, 1]], axis=1)], axis=0)
        gk = jnp.concatenate([g[5], ones])[None, :]
        kc, kcT = _compress_call(okv, w1blk, pe2, w1f, w2blk, gk, G128)
        yd = _nsa_call(oT, otok, kc, kcT, ovT, q_chunk=T_DQ, kv_chunk=TOK_DKV, v_chunk=T_DX)

        x2 = _epi_call(x.reshape(T, D_MODEL), ng, ya.reshape(T, BRANCH_W), yc.reshape(T, BRANCH_W),
                       yd.reshape(T, BRANCH_W), obs, lses, otok.reshape(T, N_MAIN * CH), wmg_all,
                       wbr_all, wout_all, layer=l, z_chunks=(TOK_AZ, TOK_BZ, TOK_CZ, TOK_DZ))
        x = x2.reshape(B, S, D_MODEL)
    return x
```

```python
import functools
import math

import numpy as np
import jax
import jax.numpy as jnp
from jax import lax
from jax.experimental import pallas as pl
from jax.experimental.pallas import tpu as pltpu

F32 = jnp.float32
BF16 = jnp.bfloat16

D_MODEL = 1024
DEPTH = 4
HEAD_DIM = 64
ROT_DIM = 16
ROPE_THETA = 500000.0
EPS = 1e-6
NEG = -1e30
BIG = 1e9
BRANCH_W = 256
DIL_PAIRS = ((128, 1), (512, 4), (2048, 16))
MOBA_BLOCK = 256
MOBA_TOPK = 3
NSA_CMP_LEN = 32
NSA_CMP_STRIDE = 16
NSA_SEL_BLOCK = 64
NSA_SEL_TOPK = 16
NSA_WINDOW = 512
COL_SIZES = (256, 256, 256, 256, 768, 768, 768, 256, 256, 256, 256, 256, 256, 384, 12, 256, 4096)
COL_OFF = tuple(int(v) for v in np.concatenate([[0], np.cumsum(COL_SIZES)]))

LANE = 128
CH = 256
VMEM_LIMIT = 56 * 1024 * 1024


def _cparams(sem):
    return pltpu.CompilerParams(dimension_semantics=sem, vmem_limit_bytes=VMEM_LIMIT)


def _dot(a, b):
    return jnp.dot(a, b, preferred_element_type=F32)


def _dot_nt(a, b):
    return lax.dot_general(a, b, (((1,), (1,)), ((), ())), preferred_element_type=F32)


def _split2(x):
    hi = x.astype(BF16)
    lo = (x - hi.astype(F32)).astype(BF16)
    return hi, lo


def _split3(x):
    hi = x.astype(BF16)
    r = x - hi.astype(F32)
    mid = r.astype(BF16)
    lo = (r - mid.astype(F32)).astype(BF16)
    return hi, mid, lo


def _repack_kernel(w_ref, *out_refs, plans):
    for o_ref, ranges in zip(out_refs, plans):
        width = o_ref.shape[-1]
        parts = [w_ref[:, a:b] for a, b in ranges]
        used = sum(b - a for a, b in ranges)
        if used < width:
            parts.append(jnp.zeros((w_ref.shape[0], width - used), F32))
        o_ref[...] = (parts[0] if len(parts) == 1 else jnp.concatenate(parts, axis=1)).astype(o_ref.dtype)


def _repack_call(w_in, plans, widths, rows=128):
    depth, d, n_in = w_in.shape
    kern = functools.partial(_repack_kernel, plans=plans)
    return pl.pallas_call(
        kern,
        grid=(depth, d // rows),
        in_specs=[pl.BlockSpec((None, rows, n_in), lambda l, i: (l, i, 0))],
        out_specs=[pl.BlockSpec((None, rows, w), lambda l, i: (l, i, 0)) for w in widths],
        out_shape=[jax.ShapeDtypeStruct((depth, d, w), BF16) for w in widths],
        compiler_params=_cparams(("parallel", "parallel")),
        name="repack",
    )(w_in)


def _proj_kernel(x_ref, ng_ref, wtok_ref, wT_ref, tokp_ref, gT_ref, rtok_ref, rT_ref, G_ref,
                 omain_ref, og1_ref, og2_ref, okv_ref, oT_ref, stage_ref, *, tok_specs, T_kinds, tm):
    x = x_ref[...]
    ms = jnp.mean(x * x, axis=-1, keepdims=True)
    xn = (x * lax.rsqrt(ms + EPS) * ng_ref[...]).astype(BF16)

    dsts = (omain_ref, og1_ref, og2_ref, okv_ref)
    for c, (has_norm, has_rope, outs) in enumerate(tok_specs):
        y = _dot(xn, wtok_ref[:, c * CH:(c + 1) * CH])
        prm = tokp_ref[c]
        if has_norm:
            ss = _dot((y * y).astype(BF16), G_ref[...])
            inv = lax.rsqrt(ss * (1.0 / HEAD_DIM) + EPS)
            y = y * (jnp.where(prm[0:1, :] > 0.0, inv, 1.0) if has_norm == "mixed" else inv)
        y = y * prm[2:3, :]
        if has_rope:
            halves = []
            for hf in range(CH // LANE):
                yh = y[:, hf * LANE:(hf + 1) * LANE]
                cc, s1, s2 = rtok_ref[0], rtok_ref[1], rtok_ref[2]
                if has_rope == "mixed":
                    rf = prm[1:2, hf * LANE:(hf + 1) * LANE]
                    cc, s1, s2 = jnp.where(rf > 0.0, cc, 1.0), s1 * rf, s2 * rf
                halves.append(yh * cc + pltpu.roll(yh, 8, 1) * s1 + pltpu.roll(yh, LANE - 8, 1) * s2)
            y = jnp.concatenate(halves, axis=1)
        staged = False
        for dst, dil, col, lanes, rstride in outs:
            if dil == 1:
                dsts[dst][:, col:col + lanes] = y[:, :lanes].astype(BF16)
                continue
            if not staged:
                for hf in range(CH // LANE):
                    stage_ref[hf] = y[:, hf * LANE:(hf + 1) * LANE]
                staged = True
            for r in range(dil):
                for hf in range(lanes // LANE):
                    blk = stage_ref[hf, pl.ds(r, tm // dil, stride=dil), :]
                    c0 = r * rstride + col + hf * LANE
                    dsts[dst][:, c0:c0 + LANE] = blk.astype(BF16)

    cosT = rT_ref[0]
    sinT = rT_ref[1]
    for c, heads in enumerate(T_kinds):
        y = _dot(xn, wT_ref[:, c * CH:(c + 1) * CH]).T
        for h, (nrm, rope, scale, sigm) in enumerate(heads):
            r0 = c * CH + h * HEAD_DIM
            yh = y[h * HEAD_DIM:(h + 1) * HEAD_DIM, :]
            if nrm:
                msq = jnp.mean(yh * yh, axis=0, keepdims=True)
                yh = yh * lax.rsqrt(msq + EPS) * gT_ref[r0:r0 + HEAD_DIM, :]
            if rope:
                x1 = yh[0:8, :]
                x2 = yh[8:16, :]
                yh = jnp.concatenate([x1 * cosT - x2 * sinT, x2 * cosT + x1 * sinT, yh[16:, :]], axis=0)
            if scale != 1.0:
                yh = yh * scale
            if sigm:
                yh = jax.nn.sigmoid(yh)
            yb = yh.astype(BF16)
            for t in range(tm // LANE):
                oT_ref[t, r0:r0 + HEAD_DIM, :] = yb[:, t * LANE:(t + 1) * LANE]


def _proj_call(x, ng, wtok, wT, tokp, gT, rtok, rT, G, *, layer, tok_specs, T_kinds, n_main, tm=512):
    B, S, _ = x.shape
    ntok, nT = len(tok_specs), len(T_kinds)
    d1, d2 = DIL_PAIRS[1][1], DIL_PAIRS[2][1]
    kern = functools.partial(_proj_kernel, tok_specs=tok_specs, T_kinds=T_kinds, tm=tm)
    const2 = lambda b, n: (0, 0)
    return pl.pallas_call(
        kern,
        grid=(B, S // tm),
        in_specs=[
            pl.BlockSpec((None, tm, D_MODEL), lambda b, n: (b, n, 0)),
            pl.BlockSpec((1, D_MODEL), const2),
            pl.BlockSpec((None, D_MODEL, ntok * CH), lambda b, n: (layer, 0, 0)),
            pl.BlockSpec((None, D_MODEL, nT * CH), lambda b, n: (layer, 0, 0)),
            pl.BlockSpec((ntok, 8, CH), lambda b, n: (0, 0, 0)),
            pl.BlockSpec((nT * CH, 1), const2),
            pl.BlockSpec((3, tm, LANE), lambda b, n: (0, n, 0)),
            pl.BlockSpec((2, 8, tm), lambda b, n: (0, 0, n)),
            pl.BlockSpec((CH, CH), const2),
        ],
        out_specs=[
            pl.BlockSpec((None, tm, n_main * CH), lambda b, n: (b, n, 0)),
            pl.BlockSpec((None, tm // d1, d1 * 3 * CH), lambda b, n: (b, n, 0)),
            pl.BlockSpec((None, tm // d2, d2 * 3 * CH), lambda b, n: (b, n, 0)),
            pl.BlockSpec((None, tm // NSA_CMP_STRIDE, NSA_CMP_STRIDE * LANE), lambda b, n: (b, n, 0)),
            pl.BlockSpec((None, tm // LANE, nT * CH, LANE), lambda b, n: (b, n, 0, 0)),
        ],
        out_shape=[
            jax.ShapeDtypeStruct((B, S, n_main * CH), BF16),
            jax.ShapeDtypeStruct((B, S // d1, d1 * 3 * CH), BF16),
            jax.ShapeDtypeStruct((B, S // d2, d2 * 3 * CH), BF16),
            jax.ShapeDtypeStruct((B, S // NSA_CMP_STRIDE, NSA_CMP_STRIDE * LANE), BF16),
            jax.ShapeDtypeStruct((B, S // LANE, nT * CH, LANE), BF16),
        ],
        scratch_shapes=[pltpu.VMEM((CH // LANE, tm, LANE), F32)],
        compiler_params=_cparams(("parallel", "parallel")),
        name="proj",
    )(x, ng, wtok, wT, tokp, gT, rtok, rT, G)


def _pair_masked_q(q_pair, h):
    rid = lax.broadcasted_iota(jnp.int32, q_pair.shape, 0)
    lo = (h % 2) * HEAD_DIM
    keep = jnp.where(rid >= lo, jnp.where(rid < lo + HEAD_DIM, 1.0, 0.0), 0.0).astype(BF16)
    return q_pair * keep


def _lane_tiles(ref, t0, nt, r0, nr):
    return jnp.concatenate([ref[t0 + t, r0:r0 + nr, :] for t in range(nt)], axis=1)


BIGPOS = 1e30
ONES_ROWS = 16
HA = HEAD_DIM + ONES_ROWS
LOG2E = 1.4426950408889634


def _v_aug(vb):
    return jnp.concatenate([vb, jnp.ones((ONES_ROWS, vb.shape[1]), BF16)], axis=0)


def _online_cols(s, colsel, m_old, acc_old, vb_aug):
    tmax = jnp.max(s, axis=0, keepdims=True)
    m_new = jnp.where(colsel > 0.0, jnp.maximum(m_old, tmax), m_old)
    m_use = jnp.where(colsel > 0.0, m_new, BIGPOS)
    p = jnp.exp2(s - m_use).astype(BF16)
    acc = jnp.exp2(m_old - m_new) * acc_old + _dot(vb_aug, p)
    return m_new, acc


SB_LOG_CUTOFF = -100.0


def _sb_kernel(qT_ref, k_ref, vT_ref, U_ref, o_ref, qm_ref, acc_ref, carry_ref, *, tq):
    i = pl.program_id(1)
    nt = tq // LANE
    row = lax.broadcasted_iota(jnp.int32, (tq, tq), 0)
    col = lax.broadcasted_iota(jnp.int32, (tq, tq), 1)
    past = row < col
    for h in range(4):
        p = h // 2
        qm_ref[h] = _pair_masked_q(_lane_tiles(qT_ref, 0, nt, p * LANE, LANE), h)
    acc_ref[...] = jnp.zeros_like(acc_ref)
    carry_ref[...] = jnp.zeros_like(carry_ref)

    def tile(j, masked):
        scores, logsig, laters = [], [], []
        for h in range(4):
            p = h // 2
            kb = k_ref[pl.ds(pl.multiple_of(j * tq, tq), tq), p * LANE:(p + 1) * LANE]
            scores.append(_dot(kb, qm_ref[h]))
        worst = None
        for h in range(4):
            s = scores[h]
            sp = jnp.maximum(s, 0.0) + jnp.log(1.0 + jnp.exp(-jnp.abs(s)))
            lg = -sp
            if masked:
                lg = jnp.where(past, lg, 0.0)
            hi, lo = _split2(lg)
            carry = carry_ref[h:h + 1, :]
            laters.append(_dot(U_ref[...], hi) + _dot(U_ref[...], lo) + carry)
            logsig.append(s - sp)
            carry = carry + jnp.sum(lg, axis=0, keepdims=True)
            carry_ref[h:h + 1, :] = carry
            worst = carry if worst is None else jnp.maximum(worst, carry)
        for h in range(4):
            w = jnp.exp(logsig[h] + laters[h])
            if masked:
                w = jnp.where(past, w, 0.0)
            vb = _lane_tiles(vT_ref, j * nt, nt, h * HEAD_DIM, HEAD_DIM)
            acc_ref[h * HEAD_DIM:(h + 1) * HEAD_DIM, :] += _dot(vb, w.astype(BF16))
        return jnp.max(worst)

    worst0 = tile(i, True)

    def cond(st):
        return jnp.logical_and(st[0] >= 0, st[1] > SB_LOG_CUTOFF)

    def body(st):
        return st[0] - 1, tile(st[0], False)

    lax.while_loop(cond, body, (i - 1, worst0))
    o_ref[...] = acc_ref[...].T.astype(o_ref.dtype)


def _attn_specs(S, tq, q_chunk, k_chunk, v_chunk):
    return [
        pl.BlockSpec((None, tq // LANE, CH, LANE), lambda b, i: (b, i, q_chunk, 0)),
        pl.BlockSpec((None, S, CH), lambda b, i: (b, 0, k_chunk)),
        pl.BlockSpec((None, S // LANE, CH, LANE), lambda b, i: (b, 0, v_chunk, 0)),
    ]


def _sb_call(oT, otok, U, *, q_chunk, k_chunk, v_chunk, tq=256):
    B, nlt, _, _ = oT.shape
    S = nlt * LANE
    kern = functools.partial(_sb_kernel, tq=tq)
    return pl.pallas_call(
        kern,
        grid=(B, S // tq),
        in_specs=_attn_specs(S, tq, q_chunk, k_chunk, v_chunk) + [pl.BlockSpec((tq, tq), lambda b, i: (0, 0))],
        out_specs=pl.BlockSpec((None, tq, BRANCH_W), lambda b, i: (b, i, 0)),
        out_shape=jax.ShapeDtypeStruct((B, S, BRANCH_W), BF16),
        scratch_shapes=[pltpu.VMEM((4, LANE, tq), BF16), pltpu.VMEM((CH, tq), F32), pltpu.VMEM((8, tq), F32)],
        compiler_params=_cparams(("parallel", "arbitrary")),
        name="stick_breaking",
    )(oT, otok, oT, U)


def _head_lanes(x, lane, h):
    lo = h * HEAD_DIM
    return jnp.where(lane >= lo, jnp.where(lane < lo + HEAD_DIM, x, jnp.zeros_like(x)), jnp.zeros_like(x))


def _band_kernel(q_ref, k_ref, v_ref, ind_ref, o_ref, lse_ref, *, tqb, max_dist):
    n = pl.program_id(2)
    n_prev = -(-max_dist // LANE)
    nkr = (n_prev + 1) * LANE
    nsub = tqb // LANE
    row = lax.broadcasted_iota(jnp.int32, (nkr, LANE), 0)
    col = lax.broadcasted_iota(jnp.int32, (nkr, LANE), 1)
    lane_q = lax.broadcasted_iota(jnp.int32, (LANE, LANE), 1)
    lane_v = lax.broadcasted_iota(jnp.int32, (nkr, CH), 1)
    for u in range(nsub):
        qt = n * nsub + u
        kt0 = jnp.maximum(qt - n_prev, 0)
        dist = (qt - kt0) * LANE + col - row
        bias = jnp.where(dist >= 0, jnp.where(dist <= max_dist, 0.0, NEG), NEG)
        k0 = pl.multiple_of(kt0 * LANE, LANE)
        kwin = k_ref[pl.ds(k0, nkr), :]
        vwin = v_ref[pl.ds(k0, nkr), :]
        qu = q_ref[u * LANE:(u + 1) * LANE, :]
        o_acc = None
        lses = []
        for h in range(4):
            p = h // 2
            qm = _head_lanes(qu[:, p * LANE:(p + 1) * LANE], lane_q, h % 2)
            s = _dot_nt(kwin[:, p * LANE:(p + 1) * LANE], qm) + bias
            m = jnp.max(s, axis=0, keepdims=True)
            e = jnp.exp2(s - m)
            den = jnp.sum(e, axis=0, keepdims=True)
            pn = e * (1.0 / den)
            contrib = _dot(pn.T.astype(BF16), _head_lanes(vwin, lane_v, h))
            o_acc = contrib if o_acc is None else o_acc + contrib
            lses.append(m + jnp.log(den) * LOG2E)
        o_ref[u * LANE:(u + 1) * LANE, :] = o_acc.astype(o_ref.dtype)
        ls = jnp.concatenate(lses + [jnp.zeros((LANE - 4, LANE), F32)], axis=0).T
        l1, l2, l3 = _split3(ls)
        lse_ref[u * LANE:(u + 1) * LANE, :] = (_dot(l1, ind_ref[...]) + _dot(l2, ind_ref[...])
                                               + _dot(l3, ind_ref[...]))


def _band_call(arr, ind, *, dil, per_res, qi, ki, vi, max_dist):
    B, L, _ = arr.shape
    tqb = min(512, L)
    kern = functools.partial(_band_kernel, tqb=tqb, max_dist=max_dist)
    return pl.pallas_call(
        kern,
        grid=(B, dil, L // tqb),
        in_specs=[
            pl.BlockSpec((None, tqb, CH), lambda b, r, n: (b, n, r * per_res + qi)),
            pl.BlockSpec((None, L, CH), lambda b, r, n: (b, 0, r * per_res + ki)),
            pl.BlockSpec((None, L, CH), lambda b, r, n: (b, 0, r * per_res + vi)),
            pl.BlockSpec((LANE, CH), lambda b, r, n: (0, 0)),
        ],
        out_specs=[
            pl.BlockSpec((None, tqb, BRANCH_W), lambda b, r, n: (b, n, r)),
            pl.BlockSpec((None, tqb, BRANCH_W), lambda b, r, n: (b, n, r)),
        ],
        out_shape=[
            jax.ShapeDtypeStruct((B, L, dil * BRANCH_W), BF16),
            jax.ShapeDtypeStruct((B, L, dil * BRANCH_W), F32),
        ],
        compiler_params=_cparams(("parallel", "parallel", "parallel")),
        name="banded",
    )(arr, arr, arr, ind)


def _moba_kernel(qT_ref, k_ref, vT_ref, A_ref, o_ref, kmean_ref, sel_ref, qm_ref, acc_ref, ml_ref, s_ref, *, tq):
    i = pl.program_id(1)
    nt = tq // LANE
    nblk = A_ref.shape[0]

    @pl.when(i == 0)
    def _():
        kmean_ref[...] = _dot(A_ref[...], k_ref[...])

    blk = lax.broadcasted_iota(jnp.int32, (nblk, tq), 0)
    row = lax.broadcasted_iota(jnp.int32, (tq, tq), 0)
    col = lax.broadcasted_iota(jnp.int32, (tq, tq), 1)
    gates, diag = [], []
    for h in range(4):
        p = h // 2
        qm = _pair_masked_q(_lane_tiles(qT_ref, 0, nt, p * LANE, LANE), h)
        qm_ref[h] = qm
        k1, k2, k3 = _split3(kmean_ref[:, p * LANE:(p + 1) * LANE])
        gates.append(_dot(k1, qm) + _dot(k2, qm) + _dot(k3, qm))
        kb = k_ref[pl.ds(pl.multiple_of(i * tq, tq), tq), p * LANE:(p + 1) * LANE]
        diag.append(_dot(kb, qm))
    for h in range(4):
        g = jnp.where(blk < i, gates[h], NEG)
        sel = jnp.zeros((nblk, tq), F32)
        for _r in range(MOBA_TOPK):
            mx = jnp.max(g, axis=0, keepdims=True)
            idx = jnp.min(jnp.where(g == mx, blk, nblk), axis=0, keepdims=True)
            hit = blk == idx
            sel = jnp.where(hit, 1.0, sel)
            g = jnp.where(hit, -jnp.inf, g)
        sel_ref[h] = jnp.where(blk < i, sel, 0.0)

        s = jnp.where(row <= col, diag[h], NEG)
        m = jnp.max(s, axis=0, keepdims=True)
        ml_ref[h:h + 1, :] = m
        acc_ref[h * HA:(h + 1) * HA, :] = _dot(
            _v_aug(_lane_tiles(vT_ref, i * nt, nt, h * HEAD_DIM, HEAD_DIM)), jnp.exp2(s - m).astype(BF16))

    last = jnp.maximum(i - 1, 0)

    def qk(t, slot):
        for h in range(4):
            p = h // 2
            kbj = k_ref[pl.ds(pl.multiple_of(t * tq, tq), tq), p * LANE:(p + 1) * LANE]
            s_ref[slot, h] = _dot(kbj, qm_ref[h])

    def update(t, valid, slot):
        for h in range(4):
            srow = sel_ref[h, pl.ds(t, 1), :] * valid
            vb = _v_aug(_lane_tiles(vT_ref, t * nt, nt, h * HEAD_DIM, HEAD_DIM))
            rows = slice(h * HA, (h + 1) * HA)
            m, acc = _online_cols(s_ref[slot, h], srow, ml_ref[h:h + 1, :], acc_ref[rows, :], vb)
            ml_ref[h:h + 1, :] = m
            acc_ref[rows, :] = acc

    qk(0, 0)

    def body(jj, c):
        t0 = 2 * jj
        qk(jnp.minimum(t0 + 1, last), 1)
        update(t0, 1.0, 0)
        qk(jnp.minimum(t0 + 2, last), 0)
        update(jnp.minimum(t0 + 1, last), (t0 + 1 < i).astype(F32), 1)
        return c

    lax.fori_loop(0, (i + 1) // 2, body, 0)
    outs = [acc_ref[h * HA:h * HA + HEAD_DIM, :] / acc_ref[h * HA + HEAD_DIM:h * HA + HEAD_DIM + 1, :]
            for h in range(4)]
    o_ref[...] = jnp.concatenate(outs, axis=0).T.astype(o_ref.dtype)


def _moba_call(oT, otok, A, *, q_chunk, k_chunk, v_chunk):
    B, nlt, _, _ = oT.shape
    S = nlt * LANE
    tq = MOBA_BLOCK
    nblk = S // MOBA_BLOCK
    kern = functools.partial(_moba_kernel, tq=tq)
    return pl.pallas_call(
        kern,
        grid=(B, S // tq),
        in_specs=_attn_specs(S, tq, q_chunk, k_chunk, v_chunk) + [pl.BlockSpec((nblk, S), lambda b, i: (0, 0))],
        out_specs=pl.BlockSpec((None, tq, BRANCH_W), lambda b, i: (b, i, 0)),
        out_shape=jax.ShapeDtypeStruct((B, S, BRANCH_W), BF16),
        scratch_shapes=[pltpu.VMEM((nblk, CH), F32), pltpu.VMEM((4, nblk, tq), F32),
                        pltpu.VMEM((4, LANE, tq), BF16), pltpu.VMEM((4 * HA, tq), F32), pltpu.VMEM((8, tq), F32),
                        pltpu.VMEM((2, 4, tq, tq), F32)],
        compiler_params=_cparams(("parallel", "arbitrary")),
        name="moba",
    )(oT, otok, oT, A)


def _compress_kernel(x_ref, w1_ref, pe_ref, w1f_ref, w2_ref, gk_ref, G_ref, o_ref, oT_ref, acc_ref):
    l = pl.program_id(1)

    @pl.when(l == 0)
    def _():
        acc_ref[...] = jnp.zeros_like(acc_ref)

    x = x_ref[...]
    w_hi, w_lo = _split2(w1_ref[...])
    acc_ref[...] += _dot(x, w_hi) + _dot(x, w_lo)

    @pl.when(l == pl.num_programs(1) - 1)
    def _():
        nc = acc_ref.shape[0]
        r = acc_ref[...]
        bias = jnp.dot(pe_ref[...], w1f_ref[...], preferred_element_type=F32,
                       precision=lax.Precision.HIGHEST)
        p1 = jnp.concatenate([r[:, 0:64], r[:, 128:192]], axis=1)
        p2 = jnp.concatenate([r[:, 64:128], r[:, 192:256]], axis=1)
        hid = p1 + pltpu.roll(p2, nc - 1, 0) + bias[0:1, :]
        hid = hid * jax.nn.sigmoid(hid)
        comp = jnp.dot(hid, w2_ref[...], preferred_element_type=F32, precision=lax.Precision.HIGHEST)
        ss = _dot((comp * comp).astype(BF16), G_ref[...])
        lane = lax.broadcasted_iota(jnp.int32, comp.shape, 1)
        inv = jnp.where(lane < HEAD_DIM, lax.rsqrt(ss * (1.0 / HEAD_DIM) + EPS), 1.0)
        comp = comp * inv * gk_ref[...]
        rowi = lax.broadcasted_iota(jnp.int32, comp.shape, 0)
        comp = jnp.where(rowi < nc - 1, comp, 0.0)
        o_ref[...] = comp.astype(o_ref.dtype)
        oT_ref[...] = comp.T.astype(oT_ref.dtype)


def _compress_call(xv, w1blk, pe2, w1f, w2blk, gk, G128):
    B, nc, _ = xv.shape
    return pl.pallas_call(
        _compress_kernel,
        grid=(B, NSA_CMP_STRIDE),
        in_specs=[
            pl.BlockSpec((None, nc, LANE), lambda b, l: (b, 0, l)),
            pl.BlockSpec((None, LANE, CH), lambda b, l: (l, 0, 0)),
            pl.BlockSpec((8, 2 * NSA_CMP_LEN * HEAD_DIM), lambda b, l: (0, 0)),
            pl.BlockSpec((2 * NSA_CMP_LEN * HEAD_DIM, LANE), lambda b, l: (0, 0)),
            pl.BlockSpec((LANE, LANE), lambda b, l: (0, 0)),
            pl.BlockSpec((1, LANE), lambda b, l: (0, 0)),
            pl.BlockSpec((LANE, LANE), lambda b, l: (0, 0)),
        ],
        out_specs=[
            pl.BlockSpec((None, nc, LANE), lambda b, l: (b, 0, 0)),
            pl.BlockSpec((None, LANE, nc), lambda b, l: (b, 0, 0)),
        ],
        out_shape=[
            jax.ShapeDtypeStruct((B, nc, LANE), BF16),
            jax.ShapeDtypeStruct((B, LANE, nc), BF16),
        ],
        scratch_shapes=[pltpu.VMEM((nc, CH), F32)],
        compiler_params=_cparams(("parallel", "arbitrary")),
        name="nsa_compress",
    )(xv, w1blk, pe2, w1f, w2blk, gk, G128)


def _nsa_kernel(qT_ref, k_ref, vT_ref, kc_ref, kcT_ref, ov_ref, o_ref, sel_ref, qm_ref, acc_ref, ml_ref, s_ref,
                *, tq):
    i = pl.program_id(1)
    nt = tq // LANE
    nq = 4 * tq
    nc = kc_ref.shape[0]
    nsel = ov_ref.shape[0]
    spb = tq // NSA_SEL_BLOCK

    zeros = jnp.zeros((HEAD_DIM, tq), BF16)
    q_heads = [_lane_tiles(qT_ref, 0, nt, h * HEAD_DIM, HEAD_DIM) for h in range(4)]
    qm_lo = jnp.concatenate([jnp.concatenate([q, zeros], axis=0) for q in q_heads], axis=1)
    qm_hi = jnp.concatenate([jnp.concatenate([zeros, q], axis=0) for q in q_heads], axis=1)
    qpos = i * tq + lax.broadcasted_iota(jnp.int32, (1, tq), 1)
    qpos4 = jnp.concatenate([qpos] * 4, axis=1)

    zc = _dot(kc_ref[...], qm_lo)
    c_end = lax.broadcasted_iota(jnp.int32, (nc, nq), 0) * NSA_CMP_STRIDE + (NSA_CMP_LEN - 1)
    cmask = c_end <= qpos4
    zc = jnp.where(cmask, zc, NEG)
    e = jnp.where(cmask, jnp.exp2(zc - jnp.max(zc, axis=0, keepdims=True)), 0.0)
    pc = e / jnp.maximum(jnp.sum(e, axis=0, keepdims=True), 1.0)
    o_cmp = _dot(kcT_ref[HEAD_DIM:2 * HEAD_DIM, :], pc.astype(BF16))

    psum = pc[:, 0:tq] + pc[:, tq:2 * tq] + pc[:, 2 * tq:3 * tq] + pc[:, 3 * tq:4 * tq]
    p_hi, p_lo = _split2(psum)
    imp = _dot(ov_ref[...], p_hi) + _dot(ov_ref[...], p_lo)
    nid = lax.broadcasted_iota(jnp.int32, (nsel, tq), 0)
    cur = qpos // NSA_SEL_BLOCK
    imp = jnp.where(nid == 0, BIG, imp)
    imp = jnp.where(nid == cur, BIG, imp)
    imp = jnp.where(nid == cur - 1, BIG, imp)
    imp = jnp.where(nid > cur, NEG, imp)
    sel = jnp.zeros((nsel, tq), F32)
    for _r in range(min(NSA_SEL_TOPK, nsel)):
        mx = jnp.max(imp, axis=0, keepdims=True)
        idx = jnp.min(jnp.where(imp == mx, nid, nsel), axis=0, keepdims=True)
        hit = nid == idx
        sel = jnp.where(hit, 1.0, sel)
        imp = jnp.where(hit, -jnp.inf, imp)
    sel = jnp.where(nid <= cur, sel, 0.0)
    sel_ref[...] = sel
    for h in range(4):
        qm_ref[h] = qm_lo[:, h * tq:(h + 1) * tq]

    def qk(t, slot):
        kb = k_ref[pl.ds(pl.multiple_of(t * tq, tq), tq), LANE:2 * LANE]
        for h in range(4):
            s_ref[slot, h] = _dot(kb, qm_ref[h])

    kpos_d = i * tq + lax.broadcasted_iota(jnp.int32, (tq, tq), 0)
    qk(i, 0)
    vb_d = _v_aug(_lane_tiles(vT_ref, i * nt, nt, 0, HEAD_DIM))
    srows_d = [sel_ref[pl.ds(i * spb + u, 1), :] for u in range(spb)]
    for h in range(4):
        sc = s_ref[0, h]
        s = jnp.concatenate(
            [jnp.where(srows_d[u] > 0.0, sc[u * NSA_SEL_BLOCK:(u + 1) * NSA_SEL_BLOCK, :], NEG)
             for u in range(spb)], axis=0)
        s = jnp.where(kpos_d <= qpos, s, NEG)
        m = jnp.max(s, axis=0, keepdims=True)
        lanes = slice(h * tq, (h + 1) * tq)
        ml_ref[0:1, lanes] = m
        acc_ref[:, lanes] = _dot(vb_d, jnp.exp2(s - m).astype(BF16))

    last = jnp.maximum(i - 1, 0)

    def update(t, valid, slot):
        vb = _v_aug(_lane_tiles(vT_ref, t * nt, nt, 0, HEAD_DIM))
        srows = [sel_ref[pl.ds(t * spb + u, 1), :] * valid for u in range(spb)]
        for h in range(4):
            lanes = slice(h * tq, (h + 1) * tq)
            sc = s_ref[slot, h]
            subs = [sc[u * NSA_SEL_BLOCK:(u + 1) * NSA_SEL_BLOCK, :] for u in range(spb)]
            tmax = None
            for u in range(spb):
                mu = jnp.where(srows[u] > 0.0, jnp.max(subs[u], axis=0, keepdims=True), NEG)
                tmax = mu if tmax is None else jnp.maximum(tmax, mu)
            m_old = ml_ref[0:1, lanes]
            m_new = jnp.maximum(m_old, tmax)
            p = jnp.concatenate(
                [jnp.exp2(subs[u] - jnp.where(srows[u] > 0.0, m_new, BIGPOS)).astype(BF16) for u in range(spb)],
                axis=0)
            ml_ref[0:1, lanes] = m_new
            acc_ref[:, lanes] = jnp.exp2(m_old - m_new) * acc_ref[:, lanes] + _dot(vb, p)

    qk(0, 0)

    def body(jj, c):
        t0 = 2 * jj
        qk(jnp.minimum(t0 + 1, last), 1)
        update(t0, 1.0, 0)
        qk(jnp.minimum(t0 + 2, last), 0)
        update(jnp.minimum(t0 + 1, last), (t0 + 1 < i).astype(F32), 1)
        return c

    lax.fori_loop(0, (i + 1) // 2, body, 0)
    o_sel = acc_ref[0:HEAD_DIM, :] / acc_ref[HEAD_DIM:HEAD_DIM + 1, :]

    n_prev = -(-(NSA_WINDOW - 1) // tq)
    nk = n_prev + 1
    kt0 = jnp.maximum(i - n_prev, 0)
    kw = k_ref[pl.ds(pl.multiple_of(kt0 * tq, tq), nk * tq), LANE:2 * LANE]
    sw = _dot(kw, qm_hi)
    dist = qpos4 - (kt0 * tq + lax.broadcasted_iota(jnp.int32, (nk * tq, nq), 0))
    sw = jnp.where(dist >= 0, jnp.where(dist <= NSA_WINDOW - 1, sw, NEG), NEG)
    mw = jnp.max(sw, axis=0, keepdims=True)
    ew = jnp.exp2(sw - mw)
    dw = jnp.sum(ew, axis=0, keepdims=True)
    vwin = _lane_tiles(vT_ref, kt0 * nt, nk * nt, HEAD_DIM, HEAD_DIM)
    o_win = _dot(vwin, ew.astype(BF16)) / dw

    gates = _lane_tiles(vT_ref, i * nt, nt, 2 * HEAD_DIM, 16).astype(F32)
    outs = []
    for h in range(4):
        sl = slice(h * tq, (h + 1) * tq)
        outs.append(gates[3 * h:3 * h + 1, :] * o_cmp[:, sl]
                    + gates[3 * h + 1:3 * h + 2, :] * o_sel[:, sl]
                    + gates[3 * h + 2:3 * h + 3, :] * o_win[:, sl])
    o_ref[...] = jnp.concatenate(outs, axis=0).T.astype(o_ref.dtype)


def _nsa_call(oT, otok, kc, kcT, ovT, *, q_chunk, kv_chunk, v_chunk):
    B, nlt, _, _ = oT.shape
    S = nlt * LANE
    tq = 256
    nc = S // NSA_CMP_STRIDE
    nsel = S // NSA_SEL_BLOCK
    kern = functools.partial(_nsa_kernel, tq=tq)
    return pl.pallas_call(
        kern,
        grid=(B, S // tq),
        in_specs=_attn_specs(S, tq, q_chunk, kv_chunk, v_chunk) + [
            pl.BlockSpec((None, nc, LANE), lambda b, i: (b, 0, 0)),
            pl.BlockSpec((None, LANE, nc), lambda b, i: (b, 0, 0)),
            pl.BlockSpec((nsel, nc), lambda b, i: (0, 0)),
        ],
        out_specs=pl.BlockSpec((None, tq, BRANCH_W), lambda b, i: (b, i, 0)),
        out_shape=jax.ShapeDtypeStruct((B, S, BRANCH_W), BF16),
        scratch_shapes=[pltpu.VMEM((nsel, tq), F32), pltpu.VMEM((4, LANE, tq), BF16),
                        pltpu.VMEM((HA, 4 * tq), F32), pltpu.VMEM((8, 4 * tq), F32),
                        pltpu.VMEM((2, 4, tq, tq), F32)],
        compiler_params=_cparams(("parallel", "arbitrary")),
        name="nsa",
    )(oT, otok, oT, kc, kcT, ovT)


def _epi_kernel(x_ref, ng_ref, ya_ref, yc_ref, yd_ref, ob0_ref, ob1_ref, ob2_ref, l0_ref, l1_ref, l2_ref,
                za_ref, zb_ref, zc_ref, zd_ref, wmg_ref, wbr_ref, wout_ref, o_ref, nat_ref, *, tm):
    x = x_ref[...]
    ms = jnp.mean(x * x, axis=-1, keepdims=True)
    xn = (x * lax.rsqrt(ms + EPS) * ng_ref[...]).astype(BF16)

    def natural(ref, slot, dil):
        nh = BRANCH_W // LANE
        for r in range(dil):
            for hf in range(nh):
                c0 = r * BRANCH_W + hf * LANE
                nat_ref[slot * nh + hf, pl.ds(r, tm // dil, stride=dil), :] = ref[:, c0:c0 + LANE].astype(F32)
        return jnp.concatenate([nat_ref[slot * nh + hf] for hf in range(nh)], axis=1)

    d1, d2 = DIL_PAIRS[1][1], DIL_PAIRS[2][1]
    l0, l1, l2 = l0_ref[...], natural(l1_ref, 0, d1), natural(l2_ref, 1, d2)
    mx = jnp.maximum(jnp.maximum(l0, l1), l2)
    e0, e1, e2 = jnp.exp2(l0 - mx), jnp.exp2(l1 - mx), jnp.exp2(l2 - mx)
    yb = (e0 * ob0_ref[...].astype(F32) + e1 * natural(ob1_ref, 2, d1)
          + e2 * natural(ob2_ref, 3, d2)) / (e0 + e1 + e2)

    ys = (ya_ref[...].astype(F32), yb, yc_ref[...].astype(F32), yd_ref[...].astype(F32))
    zs = (za_ref, zb_ref, zc_ref, zd_ref)
    merged = None
    for i in range(4):
        z = zs[i][...].astype(F32)
        gated = (ys[i] * (z * jax.nn.sigmoid(z))).astype(BF16)
        br = _dot(gated, wbr_ref[i])
        mg = _dot(xn, wmg_ref[:, i * D_MODEL:(i + 1) * D_MODEL])
        term = jax.nn.sigmoid(mg) * br
        merged = term if merged is None else merged + term
    o_ref[...] = x + _dot(merged.astype(BF16), wout_ref[...])


def _epi_call(x2, ng, ya, yc, yd, obs, lses, otok2, wmg, wbr, wout, *, layer, z_chunks, tm=256):
    T = x2.shape[0]
    row = lambda i: (i, 0)
    full2 = lambda i: (0, 0)
    yspec = pl.BlockSpec((tm, BRANCH_W), row)
    zspecs = [pl.BlockSpec((tm, CH), (lambda i, c=c: (i, c))) for c in z_chunks]
    d1, d2 = DIL_PAIRS[1][1], DIL_PAIRS[2][1]
    gspecs = [yspec, pl.BlockSpec((tm // d1, d1 * BRANCH_W), row), pl.BlockSpec((tm // d2, d2 * BRANCH_W), row)]
    return pl.pallas_call(
        functools.partial(_epi_kernel, tm=tm),
        grid=(T // tm,),
        in_specs=[pl.BlockSpec((tm, D_MODEL), row), pl.BlockSpec((1, D_MODEL), full2)]
        + [yspec] * 3 + gspecs + gspecs + zspecs
        + [pl.BlockSpec((None, D_MODEL, 4 * D_MODEL), lambda i: (layer, 0, 0)),
           pl.BlockSpec((None, 4, BRANCH_W, D_MODEL), lambda i: (layer, 0, 0, 0)),
           pl.BlockSpec((None, D_MODEL, D_MODEL), lambda i: (layer, 0, 0))],
        out_specs=pl.BlockSpec((tm, D_MODEL), row),
        out_shape=jax.ShapeDtypeStruct((T, D_MODEL), F32),
        scratch_shapes=[pltpu.VMEM((4 * (BRANCH_W // LANE), tm, LANE), F32)],
        compiler_params=_cparams(("parallel",)),
        name="epilogue",
    )(x2, ng, ya, yc, yd, *obs, *lses, otok2, otok2, otok2, otok2, wmg, wbr, wout)


TOK_AK, TOK_AZ, TOK_BQ, TOK_BK, TOK_BV, TOK_BZ, TOK_CK, TOK_CZ, TOK_DKV, TOK_DZ = range(10)
N_MAIN = 10
T_AQ, T_AV, T_CQ, T_CV, T_DQ, T_DX = range(6)
SCALE = 1.0 / math.sqrt(HEAD_DIM)
QSCALE2 = SCALE * LOG2E

_PLAIN = (False, False, 1.0, False)
_T_KINDS = (
    ((False, False, SCALE, False),) * 4,
    (_PLAIN,) * 4,
    ((True, True, QSCALE2, False),) * 4,
    (_PLAIN,) * 4,
    ((True, True, QSCALE2, False),) * 4,
    (_PLAIN, _PLAIN, (False, False, 1.0, True), _PLAIN),
)


def _tok_specs():
    main = lambda j: (0, 1, j * CH, CH, 0)
    qkv = ((True, True), (True, True), (False, False))
    specs = []
    for j in range(N_MAIN):
        if j in (TOK_BQ, TOK_BK, TOK_CK):
            specs.append((True, True, (main(j),)))
        elif j == TOK_DKV:
            specs.append(("mixed", "mixed", (main(j), (3, NSA_CMP_STRIDE, 0, LANE, LANE))))
        else:
            specs.append((False, False, (main(j),)))
    for gi in (1, 2):
        dil = DIL_PAIRS[gi][1]
        for k, (nrm, rope) in enumerate(qkv):
            specs.append((nrm, rope, ((gi, dil, k * CH, CH, 3 * CH),)))
    return tuple(specs)


def _rope_tables(S, dil):
    L = S // dil
    pos = (jnp.arange(dil, dtype=jnp.int32)[:, None] + dil * jnp.arange(L, dtype=jnp.int32)[None, :]).astype(F32)
    inv = ROPE_THETA ** (-jnp.arange(0, ROT_DIM, 2, dtype=F32) / ROT_DIM)
    ang = pos[:, :, None] * inv[None, None, :]
    cos, sin = jnp.cos(ang), jnp.sin(ang)
    one = jnp.ones((dil, L, HEAD_DIM - ROT_DIM), F32)
    zero8 = jnp.zeros((dil, L, 8), F32)
    zero = jnp.zeros_like(one)
    c_head = jnp.concatenate([cos, cos, one], axis=-1)
    s1_head = jnp.concatenate([zero8, sin, zero], axis=-1)
    s2_head = jnp.concatenate([-sin, zero8, zero], axis=-1)
    rtok = jnp.stack([jnp.tile(t, (1, 1, LANE // HEAD_DIM)) for t in (c_head, s1_head, s2_head)], axis=1)
    rT = jnp.stack([cos.transpose(0, 2, 1), sin.transpose(0, 2, 1)], axis=1)
    return rtok, rT


def _tok_params(rows):
    out = []
    for nf, rf, gains in rows:
        nrow = jnp.concatenate([jnp.full((HEAD_DIM,), float(f), F32) for f in nf])
        rrow = jnp.concatenate([jnp.full((HEAD_DIM,), float(f), F32) for f in rf])
        grow = jnp.concatenate([g.astype(F32) for g in gains])
        out.append(jnp.concatenate([jnp.stack([nrow, rrow, grow]), jnp.zeros((5, CH), F32)], axis=0))
    return jnp.stack(out)


def kernel(x, norm_g, w_in, qk_g, cmp_pe, cmp_w1, cmp_w2, w_branch, w_out):
    B, S, _ = x.shape
    T = B * S
    o = COL_OFF
    ones = jnp.ones((HEAD_DIM,), F32)

    r = np.arange(CH)
    G = jnp.asarray((r[:, None] // HEAD_DIM == r[None, :] // HEAD_DIM).astype(np.float32), BF16)
    G128 = G[:LANE, :LANE]
    t = np.arange(256)
    U = jnp.asarray((t[None, :] > t[:, None]).astype(np.float32), BF16)
    nblk = S // MOBA_BLOCK
    A = jnp.asarray(np.repeat(np.eye(nblk, dtype=np.float32), MOBA_BLOCK, axis=1) / MOBA_BLOCK, BF16)
    nc, nsel = S // NSA_CMP_STRIDE, S // NSA_SEL_BLOCK
    cs = np.arange(nc) * NSA_CMP_STRIDE
    ss = np.arange(nsel) * NSA_SEL_BLOCK
    ov = ((cs[None, :] < ss[:, None] + NSA_SEL_BLOCK) & (cs[None, :] + NSA_CMP_LEN > ss[:, None]))
    ov[:, nc - 1] = False
    ovT = jnp.asarray(ov.astype(np.float32), BF16)
    hid = np.arange(LANE)[:, None]
    ind = jnp.asarray(((hid < 4) & (r[None, :] // HEAD_DIM == hid)).astype(np.float32), BF16)
    rtok4, rT4 = _rope_tables(S, 1)
    rtok, rT = rtok4[0], rT4[0]
    tok_specs = _tok_specs()

    dkv = o[13]
    bq = lambda gi: (o[4] + gi * CH, o[4] + (gi + 1) * CH)
    bk = lambda gi: (o[5] + gi * CH, o[5] + (gi + 1) * CH)
    bv = lambda gi: (o[6] + gi * CH, o[6] + (gi + 1) * CH)
    tok_plan = ((o[1], o[2]), (o[3], o[4]), bq(0), bk(0), bv(0), (o[7], o[8]), (o[9], o[10]), (o[11], o[12]),
                (dkv, dkv + 64), (dkv + 64, dkv + 128), (dkv + 128, dkv + 192), (dkv + 256, dkv + 320),
                (o[15], o[16]), bq(1), bk(1), bv(1), bq(2), bk(2), bv(2))
    T_plan = ((o[0], o[1]), (o[2], o[3]), (o[8], o[9]), (o[10], o[11]), (o[12], o[13]),
              (dkv + 192, dkv + 256), (dkv + 320, dkv + 384), (o[14], o[15]))
    wtok_all, wT_all, wmg_all = _repack_call(
        w_in, (tok_plan, T_plan, ((o[16], o[17]),)), (len(tok_specs) * CH, len(_T_KINDS) * CH, 4 * D_MODEL))

    wbr_all = w_branch.astype(BF16)
    wout_all = w_out.astype(BF16)

    for l in range(DEPTH):
        g = qk_g[l]
        plain = ((0,) * 4, (0,) * 4, (ones,) * 4)
        bq_p = ((1,) * 4, (1,) * 4, (g[0] * QSCALE2,) * 4)
        bk_p = ((1,) * 4, (1,) * 4, (g[1],) * 4)
        tokp = _tok_params([
            plain, plain, bq_p, bk_p, plain, plain,
            ((1,) * 4, (1,) * 4, (g[3],) * 4), plain,
            ((0, 0, 1, 1), (1, 0, 1, 1), (ones, ones, g[6], g[7])), plain,
            bq_p, bk_p, plain, bq_p, bk_p, plain])
        gT = jnp.concatenate([jnp.tile(ones, 8), jnp.tile(g[2], 4), jnp.tile(ones, 4), jnp.tile(g[4], 4),
                              jnp.tile(ones, 4)])[:, None]
        ng = norm_g[l][None, :]
        otok, og1, og2, okv, oT = _proj_call(x, ng, wtok_all, wT_all, tokp, gT, rtok, rT, G, layer=l,
                                             tok_specs=tok_specs, T_kinds=_T_KINDS, n_main=N_MAIN)

        ya = _sb_call(oT, otok, U, q_chunk=T_AQ, k_chunk=TOK_AK, v_chunk=T_AV)

        obs, lses = [], []
        for gi, (window, dil) in enumerate(DIL_PAIRS):
            if dil == 1:
                ob, lse = _band_call(otok, ind, dil=1, per_res=N_MAIN, qi=TOK_BQ, ki=TOK_BK, vi=TOK_BV,
                                     max_dist=window)
            else:
                ob, lse = _band_call((og1, og2)[gi - 1], ind, dil=dil, per_res=3, qi=0, ki=1, vi=2,
                                     max_dist=window // dil)
            obs.append(ob.reshape(T // dil, dil * BRANCH_W))
            lses.append(lse.reshape(T // dil, dil * BRANCH_W))

        yc = _moba_call(oT, otok, A, q_chunk=T_CQ, k_chunk=TOK_CK, v_chunk=T_CV)

        w1 = cmp_w1[l].reshape(2, NSA_CMP_LEN, HEAD_DIM, HEAD_DIM)
        z64 = jnp.zeros((NSA_CMP_STRIDE, HEAD_DIM, HEAD_DIM), F32)
        w1blk = jnp.concatenate([
            jnp.concatenate([w1[0, :16], w1[0, 16:], z64, z64], axis=2),
            jnp.concatenate([z64, z64, w1[1, :16], w1[1, 16:]], axis=2)], axis=1)
        pe2 = jnp.concatenate([cmp_pe[l].reshape(1, -1), jnp.zeros((7, 2 * NSA_CMP_LEN * HEAD_DIM), F32)], axis=0)
        zf = jnp.zeros((NSA_CMP_LEN * HEAD_DIM, HEAD_DIM), F32)
        w1f = jnp.concatenate([jnp.concatenate([cmp_w1[l, 0], zf], axis=1),
                               jnp.concatenate([zf, cmp_w1[l, 1]], axis=1)], axis=0)
        z2 = jnp.zeros((HEAD_DIM, HEAD_DIM), F32)
        w2blk = jnp.concatenate([jnp.concatenate([cmp_w2[l, 0], z2], axis=1),
                                 jnp.concatenate([z2, cmp_w2[l, 1]], axis=1)], axis=0)
        gk = jnp.concatenate([g[5], ones])[None, :]
        kc, kcT = _compress_call(okv, w1blk, pe2, w1f, w2blk, gk, G128)
        yd = _nsa_call(oT, otok, kc, kcT, ovT, q_chunk=T_DQ, kv_chunk=TOK_DKV, v_chunk=T_DX)

        x2 = _epi_call(x.reshape(T, D_MODEL), ng, ya.reshape(T, BRANCH_W), yc.reshape(T, BRANCH_W),
                       yd.reshape(T, BRANCH_W), obs, lses, otok.reshape(T, N_MAIN * CH), wmg_all,
                       wbr_all, wout_all, layer=l, z_chunks=(TOK_AZ, TOK_BZ, TOK_CZ, TOK_DZ))
        x = x2.reshape(B, S, D_MODEL)
    return x
```

```python
import functools
import math

import numpy as np
import jax
import jax.numpy as jnp
from jax import lax
from jax.experimental import pallas as pl
from jax.experimental.pallas import tpu as pltpu

F32 = jnp.float32
BF16 = jnp.bfloat16

D_MODEL = 1024
DEPTH = 4
HEAD_DIM = 64
ROT_DIM = 16
ROPE_THETA = 500000.0
EPS = 1e-6
NEG = -1e30
BIG = 1e9
BRANCH_W = 256
DIL_PAIRS = ((128, 1), (512, 4), (2048, 16))
MOBA_BLOCK = 256
MOBA_TOPK = 3
NSA_CMP_LEN = 32
NSA_CMP_STRIDE = 16
NSA_SEL_BLOCK = 64
NSA_SEL_TOPK = 16
NSA_WINDOW = 512
COL_SIZES = (256, 256, 256, 256, 768, 768, 768, 256, 256, 256, 256, 256, 256, 384, 12, 256, 4096)
COL_OFF = tuple(int(v) for v in np.concatenate([[0], np.cumsum(COL_SIZES)]))

LANE = 128
CH = 256
VMEM_LIMIT = 56 * 1024 * 1024


def _cparams(sem):
    return pltpu.CompilerParams(dimension_semantics=sem, vmem_limit_bytes=VMEM_LIMIT)


def _dot(a, b):
    return jnp.dot(a, b, preferred_element_type=F32)


def _dot_nt(a, b):
    return lax.dot_general(a, b, (((1,), (1,)), ((), ())), preferred_element_type=F32)


def _split2(x):
    hi = x.astype(BF16)
    lo = (x - hi.astype(F32)).astype(BF16)
    return hi, lo


def _split3(x):
    hi = x.astype(BF16)
    r = x - hi.astype(F32)
    mid = r.astype(BF16)
    lo = (r - mid.astype(F32)).astype(BF16)
    return hi, mid, lo


def _repack_kernel(w_ref, *out_refs, plans):
    for o_ref, ranges in zip(out_refs, plans):
        width = o_ref.shape[-1]
        parts = [w_ref[:, a:b] for a, b in ranges]
        used = sum(b - a for a, b in ranges)
        if used < width:
            parts.append(jnp.zeros((w_ref.shape[0], width - used), F32))
        o_ref[...] = (parts[0] if len(parts) == 1 else jnp.concatenate(parts, axis=1)).astype(o_ref.dtype)


def _repack_call(w_in, plans, widths, rows=128):
    depth, d, n_in = w_in.shape
    kern = functools.partial(_repack_kernel, plans=plans)
    return pl.pallas_call(
        kern,
        grid=(depth, d // rows),
        in_specs=[pl.BlockSpec((None, rows, n_in), lambda l, i: (l, i, 0))],
        out_specs=[pl.BlockSpec((None, rows, w), lambda l, i: (l, i, 0)) for w in widths],
        out_shape=[jax.ShapeDtypeStruct((depth, d, w), BF16) for w in widths],
        compiler_params=_cparams(("parallel", "parallel")),
        name="repack",
    )(w_in)


def _proj_kernel(x_ref, ng_ref, wtok_ref, wT_ref, tokp_ref, gT_ref, rtok_ref, rT_ref, G_ref,
                 omain_ref, og1_ref, og2_ref, okv_ref, oT_ref, stage_ref, *, tok_specs, T_kinds, tm):
    x = x_ref[...]
    ms = jnp.mean(x * x, axis=-1, keepdims=True)
    xn = (x * lax.rsqrt(ms + EPS) * ng_ref[...]).astype(BF16)

    dsts = (omain_ref, og1_ref, og2_ref, okv_ref)
    for c, (has_norm, has_rope, outs) in enumerate(tok_specs):
        y = _dot(xn, wtok_ref[:, c * CH:(c + 1) * CH])
        prm = tokp_ref[c]
        if has_norm:
            ss = _dot((y * y).astype(BF16), G_ref[...])
            inv = lax.rsqrt(ss * (1.0 / HEAD_DIM) + EPS)
            y = y * (jnp.where(prm[0:1, :] > 0.0, inv, 1.0) if has_norm == "mixed" else inv)
        y = y * prm[2:3, :]
        if has_rope:
            halves = []
            for hf in range(CH // LANE):
                yh = y[:, hf * LANE:(hf + 1) * LANE]
                cc, s1, s2 = rtok_ref[0], rtok_ref[1], rtok_ref[2]
                if has_rope == "mixed":
                    rf = prm[1:2, hf * LANE:(hf + 1) * LANE]
                    cc, s1, s2 = jnp.where(rf > 0.0, cc, 1.0), s1 * rf, s2 * rf
                halves.append(yh * cc + pltpu.roll(yh, 8, 1) * s1 + pltpu.roll(yh, LANE - 8, 1) * s2)
            y = jnp.concatenate(halves, axis=1)
        staged = False
        for dst, dil, col, lanes, rstride in outs:
            if dil == 1:
                dsts[dst][:, col:col + lanes] = y[:, :lanes].astype(BF16)
                continue
            if not staged:
                for hf in range(CH // LANE):
                    stage_ref[hf] = y[:, hf * LANE:(hf + 1) * LANE]
                staged = True
            for r in range(dil):
                for hf in range(lanes // LANE):
                    blk = stage_ref[hf, pl.ds(r, tm // dil, stride=dil), :]
                    c0 = r * rstride + col + hf * LANE
                    dsts[dst][:, c0:c0 + LANE] = blk.astype(BF16)

    cosT = rT_ref[0]
    sinT = rT_ref[1]
    for c, heads in enumerate(T_kinds):
        y = _dot(xn, wT_ref[:, c * CH:(c + 1) * CH]).T
        for h, (nrm, rope, scale, sigm) in enumerate(heads):
            r0 = c * CH + h * HEAD_DIM
            yh = y[h * HEAD_DIM:(h + 1) * HEAD_DIM, :]
            if nrm:
                msq = jnp.mean(yh * yh, axis=0, keepdims=True)
                yh = yh * lax.rsqrt(msq + EPS) * gT_ref[r0:r0 + HEAD_DIM, :]
            if rope:
                x1 = yh[0:8, :]
                x2 = yh[8:16, :]
                yh = jnp.concatenate([x1 * cosT - x2 * sinT, x2 * cosT + x1 * sinT, yh[16:, :]], axis=0)
            if scale != 1.0:
                yh = yh * scale
            if sigm:
                yh = jax.nn.sigmoid(yh)
            yb = yh.astype(BF16)
            for t in range(tm // LANE):
                oT_ref[t, r0:r0 + HEAD_DIM, :] = yb[:, t * LANE:(t + 1) * LANE]


def _proj_call(x, ng, wtok, wT, tokp, gT, rtok, rT, G, *, layer, tok_specs, T_kinds, n_main, tm=512):
    B, S, _ = x.shape
    ntok, nT = len(tok_specs), len(T_kinds)
    d1, d2 = DIL_PAIRS[1][1], DIL_PAIRS[2][1]
    kern = functools.partial(_proj_kernel, tok_specs=tok_specs, T_kinds=T_kinds, tm=tm)
    const2 = lambda b, n: (0, 0)
    return pl.pallas_call(
        kern,
        grid=(B, S // tm),
        in_specs=[
            pl.BlockSpec((None, tm, D_MODEL), lambda b, n: (b, n, 0)),
            pl.BlockSpec((1, D_MODEL), const2),
            pl.BlockSpec((None, D_MODEL, ntok * CH), lambda b, n: (layer, 0, 0)),
            pl.BlockSpec((None, D_MODEL, nT * CH), lambda b, n: (layer, 0, 0)),
            pl.BlockSpec((ntok, 8, CH), lambda b, n: (0, 0, 0)),
            pl.BlockSpec((nT * CH, 1), const2),
            pl.BlockSpec((3, tm, LANE), lambda b, n: (0, n, 0)),
            pl.BlockSpec((2, 8, tm), lambda b, n: (0, 0, n)),
            pl.BlockSpec((CH, CH), const2),
        ],
        out_specs=[
            pl.BlockSpec((None, tm, n_main * CH), lambda b, n: (b, n, 0)),
            pl.BlockSpec((None, tm // d1, d1 * 3 * CH), lambda b, n: (b, n, 0)),
            pl.BlockSpec((None, tm // d2, d2 * 3 * CH), lambda b, n: (b, n, 0)),
            pl.BlockSpec((None, tm // NSA_CMP_STRIDE, NSA_CMP_STRIDE * LANE), lambda b, n: (b, n, 0)),
            pl.BlockSpec((None, tm // LANE, nT * CH, LANE), lambda b, n: (b, n, 0, 0)),
        ],
        out_shape=[
            jax.ShapeDtypeStruct((B, S, n_main * CH), BF16),
            jax.ShapeDtypeStruct((B, S // d1, d1 * 3 * CH), BF16),
            jax.ShapeDtypeStruct((B, S // d2, d2 * 3 * CH), BF16),
            jax.ShapeDtypeStruct((B, S // NSA_CMP_STRIDE, NSA_CMP_STRIDE * LANE), BF16),
            jax.ShapeDtypeStruct((B, S // LANE, nT * CH, LANE), BF16),
        ],
        scratch_shapes=[pltpu.VMEM((CH // LANE, tm, LANE), F32)],
        compiler_params=_cparams(("parallel", "parallel")),
        name="proj",
    )(x, ng, wtok, wT, tokp, gT, rtok, rT, G)


def _pair_masked_q(q_pair, h):
    rid = lax.broadcasted_iota(jnp.int32, q_pair.shape, 0)
    lo = (h % 2) * HEAD_DIM
    keep = jnp.where(rid >= lo, jnp.where(rid < lo + HEAD_DIM, 1.0, 0.0), 0.0).astype(BF16)
    return q_pair * keep


def _lane_tiles(ref, t0, nt, r0, nr):
    return jnp.concatenate([ref[t0 + t, r0:r0 + nr, :] for t in range(nt)], axis=1)


BIGPOS = 1e30
ONES_ROWS = 16
HA = HEAD_DIM + ONES_ROWS
LOG2E = 1.4426950408889634


def _v_aug(vb):
    return jnp.concatenate([vb, jnp.ones((ONES_ROWS, vb.shape[1]), BF16)], axis=0)


def _online_cols(s, colsel, m_old, acc_old, vb_aug):
    tmax = jnp.max(s, axis=0, keepdims=True)
    m_new = jnp.where(colsel > 0.0, jnp.maximum(m_old, tmax), m_old)
    m_use = jnp.where(colsel > 0.0, m_new, BIGPOS)
    p = jnp.exp2(s - m_use).astype(BF16)
    acc = jnp.exp2(m_old - m_new) * acc_old + _dot(vb_aug, p)
    return m_new, acc


SB_LOG_CUTOFF = -144.0


def _sb_kernel(qT_ref, k_ref, vT_ref, U_ref, o_ref, qm_ref, acc_ref, carry_ref, *, tq):
    i = pl.program_id(1)
    nt = tq // LANE
    row = lax.broadcasted_iota(jnp.int32, (tq, tq), 0)
    col = lax.broadcasted_iota(jnp.int32, (tq, tq), 1)
    past = row < col
    for h in range(4):
        p = h // 2
        qm_ref[h] = _pair_masked_q(_lane_tiles(qT_ref, 0, nt, p * LANE, LANE), h)
    acc_ref[...] = jnp.zeros_like(acc_ref)
    carry_ref[...] = jnp.zeros_like(carry_ref)

    def tile(j, masked):
        scores, logsig, laters = [], [], []
        for h in range(4):
            p = h // 2
            kb = k_ref[pl.ds(pl.multiple_of(j * tq, tq), tq), p * LANE:(p + 1) * LANE]
            scores.append(_dot(kb, qm_ref[h]))
        worst = None
        for h in range(4):
            s = scores[h]
            sp = jnp.maximum(s, 0.0) + jnp.log2(1.0 + jnp.exp2(-jnp.abs(s)))
            lg = -sp
            if masked:
                lg = jnp.where(past, lg, 0.0)
            hi, lo = _split2(lg)
            carry = carry_ref[h:h + 1, :]
            laters.append(_dot(U_ref[...], hi) + _dot(U_ref[...], lo) + carry)
            logsig.append(s - sp)
            carry = carry + jnp.sum(lg, axis=0, keepdims=True)
            carry_ref[h:h + 1, :] = carry
            worst = carry if worst is None else jnp.maximum(worst, carry)
        for h in range(4):
            w = jnp.exp2(logsig[h] + laters[h])
            if masked:
                w = jnp.where(past, w, 0.0)
            vb = _lane_tiles(vT_ref, j * nt, nt, h * HEAD_DIM, HEAD_DIM)
            acc_ref[h * HEAD_DIM:(h + 1) * HEAD_DIM, :] += _dot(vb, w.astype(BF16))
        return jnp.max(worst)

    worst0 = tile(i, True)

    def cond(st):
        return jnp.logical_and(st[0] >= 0, st[1] > SB_LOG_CUTOFF)

    def body(st):
        return st[0] - 1, tile(st[0], False)

    lax.while_loop(cond, body, (i - 1, worst0))
    o_ref[...] = acc_ref[...].T.astype(o_ref.dtype)


def _attn_specs(S, tq, q_chunk, k_chunk, v_chunk):
    return [
        pl.BlockSpec((None, tq // LANE, CH, LANE), lambda b, i: (b, i, q_chunk, 0)),
        pl.BlockSpec((None, S, CH), lambda b, i: (b, 0, k_chunk)),
        pl.BlockSpec((None, S // LANE, CH, LANE), lambda b, i: (b, 0, v_chunk, 0)),
    ]


def _sb_call(oT, otok, U, *, q_chunk, k_chunk, v_chunk, tq=256):
    B, nlt, _, _ = oT.shape
    S = nlt * LANE
    kern = functools.partial(_sb_kernel, tq=tq)
    return pl.pallas_call(
        kern,
        grid=(B, S // tq),
        in_specs=_attn_specs(S, tq, q_chunk, k_chunk, v_chunk) + [pl.BlockSpec((tq, tq), lambda b, i: (0, 0))],
        out_specs=pl.BlockSpec((None, tq, BRANCH_W), lambda b, i: (b, i, 0)),
        out_shape=jax.ShapeDtypeStruct((B, S, BRANCH_W), BF16),
        scratch_shapes=[pltpu.VMEM((4, LANE, tq), BF16), pltpu.VMEM((CH, tq), F32), pltpu.VMEM((8, tq), F32)],
        compiler_params=_cparams(("parallel", "arbitrary")),
        name="stick_breaking",
    )(oT, otok, oT, U)


def _head_lanes(x, lane, h):
    lo = h * HEAD_DIM
    return jnp.where(lane >= lo, jnp.where(lane < lo + HEAD_DIM, x, jnp.zeros_like(x)), jnp.zeros_like(x))


def _band_kernel(q_ref, k_ref, v_ref, ind_ref, o_ref, lse_ref, *, tqb, max_dist):
    n = pl.program_id(2)
    n_prev = -(-max_dist // LANE)
    nkr = (n_prev + 1) * LANE
    nsub = tqb // LANE
    row = lax.broadcasted_iota(jnp.int32, (nkr, LANE), 0)
    col = lax.broadcasted_iota(jnp.int32, (nkr, LANE), 1)
    lane_q = lax.broadcasted_iota(jnp.int32, (LANE, LANE), 1)
    lane_v = lax.broadcasted_iota(jnp.int32, (nkr, CH), 1)
    for u in range(nsub):
        qt = n * nsub + u
        kt0 = jnp.maximum(qt - n_prev, 0)
        dist = (qt - kt0) * LANE + col - row
        bias = jnp.where(dist >= 0, jnp.where(dist <= max_dist, 0.0, NEG), NEG)
        k0 = pl.multiple_of(kt0 * LANE, LANE)
        kwin = k_ref[pl.ds(k0, nkr), :]
        vwin = v_ref[pl.ds(k0, nkr), :]
        qu = q_ref[u * LANE:(u + 1) * LANE, :]
        o_acc = None
        lses = []
        for h in range(4):
            p = h // 2
            qm = _head_lanes(qu[:, p * LANE:(p + 1) * LANE], lane_q, h % 2)
            s = _dot_nt(kwin[:, p * LANE:(p + 1) * LANE], qm) + bias
            m = jnp.max(s, axis=0, keepdims=True)
            e = jnp.exp2(s - m)
            den = jnp.sum(e, axis=0, keepdims=True)
            pn = e * (1.0 / den)
            contrib = _dot(pn.T.astype(BF16), _head_lanes(vwin, lane_v, h))
            o_acc = contrib if o_acc is None else o_acc + contrib
            lses.append(m + jnp.log(den) * LOG2E)
        o_ref[u * LANE:(u + 1) * LANE, :] = o_acc.astype(o_ref.dtype)
        ls = jnp.concatenate(lses + [jnp.zeros((LANE - 4, LANE), F32)], axis=0).T
        l1, l2, l3 = _split3(ls)
        lse_ref[u * LANE:(u + 1) * LANE, :] = (_dot(l1, ind_ref[...]) + _dot(l2, ind_ref[...])
                                               + _dot(l3, ind_ref[...]))


def _band_call(arr, ind, *, dil, per_res, qi, ki, vi, max_dist):
    B, L, _ = arr.shape
    tqb = min(512, L)
    kern = functools.partial(_band_kernel, tqb=tqb, max_dist=max_dist)
    return pl.pallas_call(
        kern,
        grid=(B, dil, L // tqb),
        in_specs=[
            pl.BlockSpec((None, tqb, CH), lambda b, r, n: (b, n, r * per_res + qi)),
            pl.BlockSpec((None, L, CH), lambda b, r, n: (b, 0, r * per_res + ki)),
            pl.BlockSpec((None, L, CH), lambda b, r, n: (b, 0, r * per_res + vi)),
            pl.BlockSpec((LANE, CH), lambda b, r, n: (0, 0)),
        ],
        out_specs=[
            pl.BlockSpec((None, tqb, BRANCH_W), lambda b, r, n: (b, n, r)),
            pl.BlockSpec((None, tqb, BRANCH_W), lambda b, r, n: (b, n, r)),
        ],
        out_shape=[
            jax.ShapeDtypeStruct((B, L, dil * BRANCH_W), BF16),
            jax.ShapeDtypeStruct((B, L, dil * BRANCH_W), F32),
        ],
        compiler_params=_cparams(("parallel", "parallel", "parallel")),
        name="banded",
    )(arr, arr, arr, ind)


def _moba_kernel(qT_ref, k_ref, vT_ref, A_ref, o_ref, kmean_ref, sel_ref, qm_ref, acc_ref, ml_ref, s_ref, *, tq):
    i = pl.program_id(1)
    nt = tq // LANE
    nblk = A_ref.shape[0]

    @pl.when(i == 0)
    def _():
        kmean_ref[...] = _dot(A_ref[...], k_ref[...])

    blk = lax.broadcasted_iota(jnp.int32, (nblk, tq), 0)
    row = lax.broadcasted_iota(jnp.int32, (tq, tq), 0)
    col = lax.broadcasted_iota(jnp.int32, (tq, tq), 1)
    gates, diag = [], []
    for h in range(4):
        p = h // 2
        qm = _pair_masked_q(_lane_tiles(qT_ref, 0, nt, p * LANE, LANE), h)
        qm_ref[h] = qm
        k1, k2, k3 = _split3(kmean_ref[:, p * LANE:(p + 1) * LANE])
        gates.append(_dot(k1, qm) + _dot(k2, qm) + _dot(k3, qm))
        kb = k_ref[pl.ds(pl.multiple_of(i * tq, tq), tq), p * LANE:(p + 1) * LANE]
        diag.append(_dot(kb, qm))
    for h in range(4):
        g = jnp.where(blk < i, gates[h], NEG)
        sel = jnp.zeros((nblk, tq), F32)
        for _r in range(MOBA_TOPK):
            mx = jnp.max(g, axis=0, keepdims=True)
            idx = jnp.min(jnp.where(g == mx, blk, nblk), axis=0, keepdims=True)
            hit = blk == idx
            sel = jnp.where(hit, 1.0, sel)
            g = jnp.where(hit, -jnp.inf, g)
        sel_ref[h] = jnp.where(blk < i, sel, 0.0)

        s = jnp.where(row <= col, diag[h], NEG)
        m = jnp.max(s, axis=0, keepdims=True)
        ml_ref[h:h + 1, :] = m
        acc_ref[h * HA:(h + 1) * HA, :] = _dot(
            _v_aug(_lane_tiles(vT_ref, i * nt, nt, h * HEAD_DIM, HEAD_DIM)), jnp.exp2(s - m).astype(BF16))

    last = jnp.maximum(i - 1, 0)

    def qk(t, slot):
        for h in range(4):
            p = h // 2
            kbj = k_ref[pl.ds(pl.multiple_of(t * tq, tq), tq), p * LANE:(p + 1) * LANE]
            s_ref[slot, h] = _dot(kbj, qm_ref[h])

    def update(t, valid, slot):
        for h in range(4):
            srow = sel_ref[h, pl.ds(t, 1), :] * valid
            vb = _v_aug(_lane_tiles(vT_ref, t * nt, nt, h * HEAD_DIM, HEAD_DIM))
            rows = slice(h * HA, (h + 1) * HA)
            m, acc = _online_cols(s_ref[slot, h], srow, ml_ref[h:h + 1, :], acc_ref[rows, :], vb)
            ml_ref[h:h + 1, :] = m
            acc_ref[rows, :] = acc

    qk(0, 0)

    def body(jj, c):
        t0 = 2 * jj
        qk(jnp.minimum(t0 + 1, last), 1)
        update(t0, 1.0, 0)
        qk(jnp.minimum(t0 + 2, last), 0)
        update(jnp.minimum(t0 + 1, last), (t0 + 1 < i).astype(F32), 1)
        return c

    lax.fori_loop(0, (i + 1) // 2, body, 0)
    outs = [acc_ref[h * HA:h * HA + HEAD_DIM, :] / acc_ref[h * HA + HEAD_DIM:h * HA + HEAD_DIM + 1, :]
            for h in range(4)]
    o_ref[...] = jnp.concatenate(outs, axis=0).T.astype(o_ref.dtype)


def _moba_call(oT, otok, A, *, q_chunk, k_chunk, v_chunk):
    B, nlt, _, _ = oT.shape
    S = nlt * LANE
    tq = MOBA_BLOCK
    nblk = S // MOBA_BLOCK
    kern = functools.partial(_moba_kernel, tq=tq)
    return pl.pallas_call(
        kern,
        grid=(B, S // tq),
        in_specs=_attn_specs(S, tq, q_chunk, k_chunk, v_chunk) + [pl.BlockSpec((nblk, S), lambda b, i: (0, 0))],
        out_specs=pl.BlockSpec((None, tq, BRANCH_W), lambda b, i: (b, i, 0)),
        out_shape=jax.ShapeDtypeStruct((B, S, BRANCH_W), BF16),
        scratch_shapes=[pltpu.VMEM((nblk, CH), F32), pltpu.VMEM((4, nblk, tq), F32),
                        pltpu.VMEM((4, LANE, tq), BF16), pltpu.VMEM((4 * HA, tq), F32), pltpu.VMEM((8, tq), F32),
                        pltpu.VMEM((2, 4, tq, tq), F32)],
        compiler_params=_cparams(("parallel", "arbitrary")),
        name="moba",
    )(oT, otok, oT, A)


def _compress_kernel(x_ref, w1_ref, pe_ref, w1f_ref, w2_ref, gk_ref, G_ref, o_ref, oT_ref, acc_ref):
    l = pl.program_id(1)

    @pl.when(l == 0)
    def _():
        acc_ref[...] = jnp.zeros_like(acc_ref)

    x = x_ref[...]
    w_hi, w_lo = _split2(w1_ref[...])
    acc_ref[...] += _dot(x, w_hi) + _dot(x, w_lo)

    @pl.when(l == pl.num_programs(1) - 1)
    def _():
        nc = acc_ref.shape[0]
        r = acc_ref[...]
        bias = jnp.dot(pe_ref[...], w1f_ref[...], preferred_element_type=F32,
                       precision=lax.Precision.HIGHEST)
        p1 = jnp.concatenate([r[:, 0:64], r[:, 128:192]], axis=1)
        p2 = jnp.concatenate([r[:, 64:128], r[:, 192:256]], axis=1)
        hid = p1 + pltpu.roll(p2, nc - 1, 0) + bias[0:1, :]
        hid = hid * jax.nn.sigmoid(hid)
        comp = jnp.dot(hid, w2_ref[...], preferred_element_type=F32, precision=lax.Precision.HIGHEST)
        ss = _dot((comp * comp).astype(BF16), G_ref[...])
        lane = lax.broadcasted_iota(jnp.int32, comp.shape, 1)
        inv = jnp.where(lane < HEAD_DIM, lax.rsqrt(ss * (1.0 / HEAD_DIM) + EPS), 1.0)
        comp = comp * inv * gk_ref[...]
        rowi = lax.broadcasted_iota(jnp.int32, comp.shape, 0)
        comp = jnp.where(rowi < nc - 1, comp, 0.0)
        o_ref[...] = comp.astype(o_ref.dtype)
        oT_ref[...] = comp.T.astype(oT_ref.dtype)


def _compress_call(xv, w1blk, pe2, w1f, w2blk, gk, G128):
    B, nc, _ = xv.shape
    return pl.pallas_call(
        _compress_kernel,
        grid=(B, NSA_CMP_STRIDE),
        in_specs=[
            pl.BlockSpec((None, nc, LANE), lambda b, l: (b, 0, l)),
            pl.BlockSpec((None, LANE, CH), lambda b, l: (l, 0, 0)),
            pl.BlockSpec((8, 2 * NSA_CMP_LEN * HEAD_DIM), lambda b, l: (0, 0)),
            pl.BlockSpec((2 * NSA_CMP_LEN * HEAD_DIM, LANE), lambda b, l: (0, 0)),
            pl.BlockSpec((LANE, LANE), lambda b, l: (0, 0)),
            pl.BlockSpec((1, LANE), lambda b, l: (0, 0)),
            pl.BlockSpec((LANE, LANE), lambda b, l: (0, 0)),
        ],
        out_specs=[
            pl.BlockSpec((None, nc, LANE), lambda b, l: (b, 0, 0)),
            pl.BlockSpec((None, LANE, nc), lambda b, l: (b, 0, 0)),
        ],
        out_shape=[
            jax.ShapeDtypeStruct((B, nc, LANE), BF16),
            jax.ShapeDtypeStruct((B, LANE, nc), BF16),
        ],
        scratch_shapes=[pltpu.VMEM((nc, CH), F32)],
        compiler_params=_cparams(("parallel", "arbitrary")),
        name="nsa_compress",
    )(xv, w1blk, pe2, w1f, w2blk, gk, G128)


def _nsa_kernel(qT_ref, k_ref, vT_ref, kc_ref, kcT_ref, ov_ref, o_ref, sel_ref, qm_ref, acc_ref, ml_ref, s_ref,
                *, tq):
    i = pl.program_id(1)
    nt = tq // LANE
    nq = 4 * tq
    nc = kc_ref.shape[0]
    nsel = ov_ref.shape[0]
    spb = tq // NSA_SEL_BLOCK

    zeros = jnp.zeros((HEAD_DIM, tq), BF16)
    q_heads = [_lane_tiles(qT_ref, 0, nt, h * HEAD_DIM, HEAD_DIM) for h in range(4)]
    qm_lo = jnp.concatenate([jnp.concatenate([q, zeros], axis=0) for q in q_heads], axis=1)
    qm_hi = jnp.concatenate([jnp.concatenate([zeros, q], axis=0) for q in q_heads], axis=1)
    qpos = i * tq + lax.broadcasted_iota(jnp.int32, (1, tq), 1)
    qpos4 = jnp.concatenate([qpos] * 4, axis=1)

    zc = _dot(kc_ref[...], qm_lo)
    c_end = lax.broadcasted_iota(jnp.int32, (nc, tq), 0) * NSA_CMP_STRIDE + (NSA_CMP_LEN - 1)
    cbias = jnp.where(c_end <= qpos, 0.0, NEG)
    zc = zc + jnp.concatenate([cbias] * 4, axis=1)
    e = jnp.exp2(zc - jnp.max(zc, axis=0, keepdims=True))
    seen = jnp.where(qpos4 >= NSA_CMP_LEN - 1, 1.0, 0.0)
    pc = e * (seen / jnp.maximum(jnp.sum(e, axis=0, keepdims=True), 1.0))
    o_cmp = _dot(kcT_ref[HEAD_DIM:2 * HEAD_DIM, :], pc.astype(BF16))

    psum = pc[:, 0:tq] + pc[:, tq:2 * tq] + pc[:, 2 * tq:3 * tq] + pc[:, 3 * tq:4 * tq]
    p_hi, p_lo = _split2(psum)
    imp = _dot(ov_ref[...], p_hi) + _dot(ov_ref[...], p_lo)
    nid = lax.broadcasted_iota(jnp.int32, (nsel, tq), 0)
    cur = qpos // NSA_SEL_BLOCK
    imp = jnp.where(nid == 0, BIG, imp)
    imp = jnp.where(nid == cur, BIG, imp)
    imp = jnp.where(nid == cur - 1, BIG, imp)
    imp = jnp.where(nid > cur, NEG, imp)
    for _r in range(min(NSA_SEL_TOPK, nsel)):
        mx = jnp.max(imp, axis=0, keepdims=True)
        idx = jnp.min(jnp.where(imp == mx, nid, nsel), axis=0, keepdims=True)
        imp = jnp.where(nid == idx, -jnp.inf, imp)
    sel = jnp.where(nid <= cur, jnp.where(imp == -jnp.inf, 1.0, 0.0), 0.0)
    sel_ref[...] = sel
    for h in range(4):
        qm_ref[h] = qm_lo[:, h * tq:(h + 1) * tq]

    def qk(t, slot):
        kb = k_ref[pl.ds(pl.multiple_of(t * tq, tq), tq), LANE:2 * LANE]
        for h in range(4):
            s_ref[slot, h] = _dot(kb, qm_ref[h])

    kpos_d = i * tq + lax.broadcasted_iota(jnp.int32, (tq, tq), 0)
    qk(i, 0)
    vb_d = _v_aug(_lane_tiles(vT_ref, i * nt, nt, 0, HEAD_DIM))
    srows_d = [sel_ref[pl.ds(i * spb + u, 1), :] for u in range(spb)]
    for h in range(4):
        sc = s_ref[0, h]
        s = jnp.concatenate(
            [jnp.where(srows_d[u] > 0.0, sc[u * NSA_SEL_BLOCK:(u + 1) * NSA_SEL_BLOCK, :], NEG)
             for u in range(spb)], axis=0)
        s = jnp.where(kpos_d <= qpos, s, NEG)
        m = jnp.max(s, axis=0, keepdims=True)
        lanes = slice(h * tq, (h + 1) * tq)
        ml_ref[0:1, lanes] = m
        acc_ref[:, lanes] = _dot(vb_d, jnp.exp2(s - m).astype(BF16))

    last = jnp.maximum(i - 1, 0)

    def update(t, valid, slot):
        vb = _v_aug(_lane_tiles(vT_ref, t * nt, nt, 0, HEAD_DIM))
        srows = [sel_ref[pl.ds(t * spb + u, 1), :] * valid for u in range(spb)]
        for h in range(4):
            lanes = slice(h * tq, (h + 1) * tq)
            sc = s_ref[slot, h]
            subs = [sc[u * NSA_SEL_BLOCK:(u + 1) * NSA_SEL_BLOCK, :] for u in range(spb)]
            tmax = None
            for u in range(spb):
                mu = jnp.where(srows[u] > 0.0, jnp.max(subs[u], axis=0, keepdims=True), NEG)
                tmax = mu if tmax is None else jnp.maximum(tmax, mu)
            m_old = ml_ref[0:1, lanes]
            m_new = jnp.maximum(m_old, tmax)
            p = jnp.concatenate(
                [jnp.exp2(subs[u] - jnp.where(srows[u] > 0.0, m_new, BIGPOS)).astype(BF16) for u in range(spb)],
                axis=0)
            ml_ref[0:1, lanes] = m_new
            acc_ref[:, lanes] = jnp.exp2(m_old - m_new) * acc_ref[:, lanes] + _dot(vb, p)

    qk(0, 0)

    def body(jj, c):
        t0 = 2 * jj
        qk(jnp.minimum(t0 + 1, last), 1)
        update(t0, 1.0, 0)
        qk(jnp.minimum(t0 + 2, last), 0)
        update(jnp.minimum(t0 + 1, last), (t0 + 1 < i).astype(F32), 1)
        return c

    lax.fori_loop(0, (i + 1) // 2, body, 0)
    o_sel = acc_ref[0:HEAD_DIM, :] / acc_ref[HEAD_DIM:HEAD_DIM + 1, :]

    n_prev = -(-(NSA_WINDOW - 1) // tq)
    nk = n_prev + 1
    kt0 = jnp.maximum(i - n_prev, 0)
    kw = k_ref[pl.ds(pl.multiple_of(kt0 * tq, tq), nk * tq), LANE:2 * LANE]
    sw = _dot(kw, qm_hi)
    dist = qpos - (kt0 * tq + lax.broadcasted_iota(jnp.int32, (nk * tq, tq), 0))
    wbias = jnp.where(dist >= 0, jnp.where(dist <= NSA_WINDOW - 1, 0.0, NEG), NEG)
    sw = sw + jnp.concatenate([wbias] * 4, axis=1)
    mw = jnp.max(sw, axis=0, keepdims=True)
    ew = jnp.exp2(sw - mw)
    dw = jnp.sum(ew, axis=0, keepdims=True)
    vwin = _lane_tiles(vT_ref, kt0 * nt, nk * nt, HEAD_DIM, HEAD_DIM)
    o_win = _dot(vwin, ew.astype(BF16)) / dw

    gates = _lane_tiles(vT_ref, i * nt, nt, 2 * HEAD_DIM, 16).astype(F32)
    outs = []
    for h in range(4):
        sl = slice(h * tq, (h + 1) * tq)
        outs.append(gates[3 * h:3 * h + 1, :] * o_cmp[:, sl]
                    + gates[3 * h + 1:3 * h + 2, :] * o_sel[:, sl]
                    + gates[3 * h + 2:3 * h + 3, :] * o_win[:, sl])
    o_ref[...] = jnp.concatenate(outs, axis=0).T.astype(o_ref.dtype)


def _nsa_call(oT, otok, kc, kcT, ovT, *, q_chunk, kv_chunk, v_chunk):
    B, nlt, _, _ = oT.shape
    S = nlt * LANE
    tq = 256
    nc = S // NSA_CMP_STRIDE
    nsel = S // NSA_SEL_BLOCK
    kern = functools.partial(_nsa_kernel, tq=tq)
    return pl.pallas_call(
        kern,
        grid=(B, S // tq),
        in_specs=_attn_specs(S, tq, q_chunk, kv_chunk, v_chunk) + [
            pl.BlockSpec((None, nc, LANE), lambda b, i: (b, 0, 0)),
            pl.BlockSpec((None, LANE, nc), lambda b, i: (b, 0, 0)),
            pl.BlockSpec((nsel, nc), lambda b, i: (0, 0)),
        ],
        out_specs=pl.BlockSpec((None, tq, BRANCH_W), lambda b, i: (b, i, 0)),
        out_shape=jax.ShapeDtypeStruct((B, S, BRANCH_W), BF16),
        scratch_shapes=[pltpu.VMEM((nsel, tq), F32), pltpu.VMEM((4, LANE, tq), BF16),
                        pltpu.VMEM((HA, 4 * tq), F32), pltpu.VMEM((8, 4 * tq), F32),
                        pltpu.VMEM((2, 4, tq, tq), F32)],
        compiler_params=_cparams(("parallel", "arbitrary")),
        name="nsa",
    )(oT, otok, oT, kc, kcT, ovT)


def _epi_kernel(x_ref, ng_ref, ya_ref, yc_ref, yd_ref, ob0_ref, ob1_ref, ob2_ref, l0_ref, l1_ref, l2_ref,
                za_ref, zb_ref, zc_ref, zd_ref, wmg_ref, wbr_ref, wout_ref, o_ref, nat_ref, *, tm):
    x = x_ref[...]
    ms = jnp.mean(x * x, axis=-1, keepdims=True)
    xn = (x * lax.rsqrt(ms + EPS) * ng_ref[...]).astype(BF16)

    def natural(ref, slot, dil):
        nh = BRANCH_W // LANE
        for r in range(dil):
            for hf in range(nh):
                c0 = r * BRANCH_W + hf * LANE
                nat_ref[slot * nh + hf, pl.ds(r, tm // dil, stride=dil), :] = ref[:, c0:c0 + LANE].astype(F32)
        return jnp.concatenate([nat_ref[slot * nh + hf] for hf in range(nh)], axis=1)

    d1, d2 = DIL_PAIRS[1][1], DIL_PAIRS[2][1]
    l0, l1, l2 = l0_ref[...], natural(l1_ref, 0, d1), natural(l2_ref, 1, d2)
    mx = jnp.maximum(jnp.maximum(l0, l1), l2)
    e0, e1, e2 = jnp.exp2(l0 - mx), jnp.exp2(l1 - mx), jnp.exp2(l2 - mx)
    yb = (e0 * ob0_ref[...].astype(F32) + e1 * natural(ob1_ref, 2, d1)
          + e2 * natural(ob2_ref, 3, d2)) / (e0 + e1 + e2)

    ys = (ya_ref[...].astype(F32), yb, yc_ref[...].astype(F32), yd_ref[...].astype(F32))
    zs = (za_ref, zb_ref, zc_ref, zd_ref)
    merged = None
    for i in range(4):
        z = zs[i][...].astype(F32)
        gated = (ys[i] * (z * jax.nn.sigmoid(z))).astype(BF16)
        br = _dot(gated, wbr_ref[i])
        mg = _dot(xn, wmg_ref[:, i * D_MODEL:(i + 1) * D_MODEL])
        term = jax.nn.sigmoid(mg) * br
        merged = term if merged is None else merged + term
    o_ref[...] = x + _dot(merged.astype(BF16), wout_ref[...])


def _epi_call(x2, ng, ya, yc, yd, obs, lses, otok2, wmg, wbr, wout, *, layer, z_chunks, tm=256):
    T = x2.shape[0]
    row = lambda i: (i, 0)
    full2 = lambda i: (0, 0)
    yspec = pl.BlockSpec((tm, BRANCH_W), row)
    zspecs = [pl.BlockSpec((tm, CH), (lambda i, c=c: (i, c))) for c in z_chunks]
    d1, d2 = DIL_PAIRS[1][1], DIL_PAIRS[2][1]
    gspecs = [yspec, pl.BlockSpec((tm // d1, d1 * BRANCH_W), row), pl.BlockSpec((tm // d2, d2 * BRANCH_W), row)]
    return pl.pallas_call(
        functools.partial(_epi_kernel, tm=tm),
        grid=(T // tm,),
        in_specs=[pl.BlockSpec((tm, D_MODEL), row), pl.BlockSpec((1, D_MODEL), full2)]
        + [yspec] * 3 + gspecs + gspecs + zspecs
        + [pl.BlockSpec((None, D_MODEL, 4 * D_MODEL), lambda i: (layer, 0, 0)),
           pl.BlockSpec((None, 4, BRANCH_W, D_MODEL), lambda i: (layer, 0, 0, 0)),
           pl.BlockSpec((None, D_MODEL, D_MODEL), lambda i: (layer, 0, 0))],
        out_specs=pl.BlockSpec((tm, D_MODEL), row),
        out_shape=jax.ShapeDtypeStruct((T, D_MODEL), F32),
        scratch_shapes=[pltpu.VMEM((4 * (BRANCH_W // LANE), tm, LANE), F32)],
        compiler_params=_cparams(("parallel",)),
        name="epilogue",
    )(x2, ng, ya, yc, yd, *obs, *lses, otok2, otok2, otok2, otok2, wmg, wbr, wout)


TOK_AK, TOK_AZ, TOK_BQ, TOK_BK, TOK_BV, TOK_BZ, TOK_CK, TOK_CZ, TOK_DKV, TOK_DZ = range(10)
N_MAIN = 10
T_AQ, T_AV, T_CQ, T_CV, T_DQ, T_DX = range(6)
SCALE = 1.0 / math.sqrt(HEAD_DIM)
QSCALE2 = SCALE * LOG2E

_PLAIN = (False, False, 1.0, False)
_T_KINDS = (
    ((False, False, QSCALE2, False),) * 4,
    (_PLAIN,) * 4,
    ((True, True, QSCALE2, False),) * 4,
    (_PLAIN,) * 4,
    ((True, True, QSCALE2, False),) * 4,
    (_PLAIN, _PLAIN, (False, False, 1.0, True), _PLAIN),
)


def _tok_specs():
    main = lambda j: (0, 1, j * CH, CH, 0)
    qkv = ((True, True), (True, True), (False, False))
    specs = []
    for j in range(N_MAIN):
        if j in (TOK_BQ, TOK_BK, TOK_CK):
            specs.append((True, True, (main(j),)))
        elif j == TOK_DKV:
            specs.append(("mixed", "mixed", (main(j), (3, NSA_CMP_STRIDE, 0, LANE, LANE))))
        else:
            specs.append((False, False, (main(j),)))
    for gi in (1, 2):
        dil = DIL_PAIRS[gi][1]
        for k, (nrm, rope) in enumerate(qkv):
            specs.append((nrm, rope, ((gi, dil, k * CH, CH, 3 * CH),)))
    return tuple(specs)


def _rope_tables(S, dil):
    L = S // dil
    pos = (jnp.arange(dil, dtype=jnp.int32)[:, None] + dil * jnp.arange(L, dtype=jnp.int32)[None, :]).astype(F32)
    inv = ROPE_THETA ** (-jnp.arange(0, ROT_DIM, 2, dtype=F32) / ROT_DIM)
    ang = pos[:, :, None] * inv[None, None, :]
    cos, sin = jnp.cos(ang), jnp.sin(ang)
    one = jnp.ones((dil, L, HEAD_DIM - ROT_DIM), F32)
    zero8 = jnp.zeros((dil, L, 8), F32)
    zero = jnp.zeros_like(one)
    c_head = jnp.concatenate([cos, cos, one], axis=-1)
    s1_head = jnp.concatenate([zero8, sin, zero], axis=-1)
    s2_head = jnp.concatenate([-sin, zero8, zero], axis=-1)
    rtok = jnp.stack([jnp.tile(t, (1, 1, LANE // HEAD_DIM)) for t in (c_head, s1_head, s2_head)], axis=1)
    rT = jnp.stack([cos.transpose(0, 2, 1), sin.transpose(0, 2, 1)], axis=1)
    return rtok, rT


def _tok_params(rows):
    out = []
    for nf, rf, gains in rows:
        nrow = jnp.concatenate([jnp.full((HEAD_DIM,), float(f), F32) for f in nf])
        rrow = jnp.concatenate([jnp.full((HEAD_DIM,), float(f), F32) for f in rf])
        grow = jnp.concatenate([g.astype(F32) for g in gains])
        out.append(jnp.concatenate([jnp.stack([nrow, rrow, grow]), jnp.zeros((5, CH), F32)], axis=0))
    return jnp.stack(out)


def kernel(x, norm_g, w_in, qk_g, cmp_pe, cmp_w1, cmp_w2, w_branch, w_out):
    B, S, _ = x.shape
    T = B * S
    o = COL_OFF
    ones = jnp.ones((HEAD_DIM,), F32)

    r = np.arange(CH)
    G = jnp.asarray((r[:, None] // HEAD_DIM == r[None, :] // HEAD_DIM).astype(np.float32), BF16)
    G128 = G[:LANE, :LANE]
    t = np.arange(256)
    U = jnp.asarray((t[None, :] > t[:, None]).astype(np.float32), BF16)
    nblk = S // MOBA_BLOCK
    A = jnp.asarray(np.repeat(np.eye(nblk, dtype=np.float32), MOBA_BLOCK, axis=1) / MOBA_BLOCK, BF16)
    nc, nsel = S // NSA_CMP_STRIDE, S // NSA_SEL_BLOCK
    cs = np.arange(nc) * NSA_CMP_STRIDE
    ss = np.arange(nsel) * NSA_SEL_BLOCK
    ov = ((cs[None, :] < ss[:, None] + NSA_SEL_BLOCK) & (cs[None, :] + NSA_CMP_LEN > ss[:, None]))
    ov[:, nc - 1] = False
    ovT = jnp.asarray(ov.astype(np.float32), BF16)
    hid = np.arange(LANE)[:, None]
    ind = jnp.asarray(((hid < 4) & (r[None, :] // HEAD_DIM == hid)).astype(np.float32), BF16)
    rtok4, rT4 = _rope_tables(S, 1)
    rtok, rT = rtok4[0], rT4[0]
    tok_specs = _tok_specs()

    dkv = o[13]
    bq = lambda gi: (o[4] + gi * CH, o[4] + (gi + 1) * CH)
    bk = lambda gi: (o[5] + gi * CH, o[5] + (gi + 1) * CH)
    bv = lambda gi: (o[6] + gi * CH, o[6] + (gi + 1) * CH)
    tok_plan = ((o[1], o[2]), (o[3], o[4]), bq(0), bk(0), bv(0), (o[7], o[8]), (o[9], o[10]), (o[11], o[12]),
                (dkv, dkv + 64), (dkv + 64, dkv + 128), (dkv + 128, dkv + 192), (dkv + 256, dkv + 320),
                (o[15], o[16]), bq(1), bk(1), bv(1), bq(2), bk(2), bv(2))
    T_plan = ((o[0], o[1]), (o[2], o[3]), (o[8], o[9]), (o[10], o[11]), (o[12], o[13]),
              (dkv + 192, dkv + 256), (dkv + 320, dkv + 384), (o[14], o[15]))
    wtok_all, wT_all, wmg_all = _repack_call(
        w_in, (tok_plan, T_plan, ((o[16], o[17]),)), (len(tok_specs) * CH, len(_T_KINDS) * CH, 4 * D_MODEL))

    wbr_all = w_branch.astype(BF16)
    wout_all = w_out.astype(BF16)

    for l in range(DEPTH):
        g = qk_g[l]
        plain = ((0,) * 4, (0,) * 4, (ones,) * 4)
        bq_p = ((1,) * 4, (1,) * 4, (g[0] * QSCALE2,) * 4)
        bk_p = ((1,) * 4, (1,) * 4, (g[1],) * 4)
        tokp = _tok_params([
            plain, plain, bq_p, bk_p, plain, plain,
            ((1,) * 4, (1,) * 4, (g[3],) * 4), plain,
            ((0, 0, 1, 1), (1, 0, 1, 1), (ones, ones, g[6], g[7])), plain,
            bq_p, bk_p, plain, bq_p, bk_p, plain])
        gT = jnp.concatenate([jnp.tile(ones, 8), jnp.tile(g[2], 4), jnp.tile(ones, 4), jnp.tile(g[4], 4),
                              jnp.tile(ones, 4)])[:, None]
        ng = norm_g[l][None, :]
        otok, og1, og2, okv, oT = _proj_call(x, ng, wtok_all, wT_all, tokp, gT, rtok, rT, G, layer=l,
                                             tok_specs=tok_specs, T_kinds=_T_KINDS, n_main=N_MAIN)

        ya = _sb_call(oT, otok, U, q_chunk=T_AQ, k_chunk=TOK_AK, v_chunk=T_AV)

        obs, lses = [], []
        for gi, (window, dil) in enumerate(DIL_PAIRS):
            if dil == 1:
                ob, lse = _band_call(otok, ind, dil=1, per_res=N_MAIN, qi=TOK_BQ, ki=TOK_BK, vi=TOK_BV,
                                     max_dist=window)
            else:
                ob, lse = _band_call((og1, og2)[gi - 1], ind, dil=dil, per_res=3, qi=0, ki=1, vi=2,
                                     max_dist=window // dil)
            obs.append(ob.reshape(T // dil, dil * BRANCH_W))
            lses.append(lse.reshape(T // dil, dil * BRANCH_W))

        yc = _moba_call(oT, otok, A, q_chunk=T_CQ, k_chunk=TOK_CK, v_chunk=T_CV)

        w1 = cmp_w1[l].reshape(2, NSA_CMP_LEN, HEAD_DIM, HEAD_DIM)
        z64 = jnp.zeros((NSA_CMP_STRIDE, HEAD_DIM, HEAD_DIM), F32)
        w1blk = jnp.concatenate([
            jnp.concatenate([w1[0, :16], w1[0, 16:], z64, z64], axis=2),
            jnp.concatenate([z64, z64, w1[1, :16], w1[1, 16:]], axis=2)], axis=1)
        pe2 = jnp.concatenate([cmp_pe[l].reshape(1, -1), jnp.zeros((7, 2 * NSA_CMP_LEN * HEAD_DIM), F32)], axis=0)
        zf = jnp.zeros((NSA_CMP_LEN * HEAD_DIM, HEAD_DIM), F32)
        w1f = jnp.concatenate([jnp.concatenate([cmp_w1[l, 0], zf], axis=1),
                               jnp.concatenate([zf, cmp_w1[l, 1]], axis=1)], axis=0)
        z2 = jnp.zeros((HEAD_DIM, HEAD_DIM), F32)
        w2blk = jnp.concatenate([jnp.concatenate([cmp_w2[l, 0], z2], axis=1),
                                 jnp.concatenate([z2, cmp_w2[l, 1]], axis=1)], axis=0)
        gk = jnp.concatenate([g[5], ones])[None, :]
        kc, kcT = _compress_call(okv, w1blk, pe2, w1f, w2blk, gk, G128)
        yd = _nsa_call(oT, otok, kc, kcT, ovT, q_chunk=T_DQ, kv_chunk=TOK_DKV, v_chunk=T_DX)

        x2 = _epi_call(x.reshape(T, D_MODEL), ng, ya.reshape(T, BRANCH_W), yc.reshape(T, BRANCH_W),
                       yd.reshape(T, BRANCH_W), obs, lses, otok.reshape(T, N_MAIN * CH), wmg_all,
                       wbr_all, wout_all, layer=l, z_chunks=(TOK_AZ, TOK_BZ, TOK_CZ, TOK_DZ))
        x = x2.reshape(B, S, D_MODEL)
    return x
```

```python
import functools
import math

import numpy as np
import jax
import jax.numpy as jnp
from jax import lax
from jax.experimental import pallas as pl
from jax.experimental.pallas import tpu as pltpu

F32 = jnp.float32
BF16 = jnp.bfloat16

D_MODEL = 1024
DEPTH = 4
HEAD_DIM = 64
ROT_DIM = 16
ROPE_THETA = 500000.0
EPS = 1e-6
NEG = -1e30
BIG = 1e9
BRANCH_W = 256
DIL_PAIRS = ((128, 1), (512, 4), (2048, 16))
MOBA_BLOCK = 256
MOBA_TOPK = 3
NSA_CMP_LEN = 32
NSA_CMP_STRIDE = 16
NSA_SEL_BLOCK = 64
NSA_SEL_TOPK = 16
NSA_WINDOW = 512
COL_SIZES = (256, 256, 256, 256, 768, 768, 768, 256, 256, 256, 256, 256, 256, 384, 12, 256, 4096)
COL_OFF = tuple(int(v) for v in np.concatenate([[0], np.cumsum(COL_SIZES)]))

LANE = 128
CH = 256
VMEM_LIMIT = 56 * 1024 * 1024


def _cparams(sem):
    return pltpu.CompilerParams(dimension_semantics=sem, vmem_limit_bytes=VMEM_LIMIT)


def _dot(a, b):
    return jnp.dot(a, b, preferred_element_type=F32)


def _dot_nt(a, b):
    return lax.dot_general(a, b, (((1,), (1,)), ((), ())), preferred_element_type=F32)


def _split2(x):
    hi = x.astype(BF16)
    lo = (x - hi.astype(F32)).astype(BF16)
    return hi, lo


def _split3(x):
    hi = x.astype(BF16)
    r = x - hi.astype(F32)
    mid = r.astype(BF16)
    lo = (r - mid.astype(F32)).astype(BF16)
    return hi, mid, lo


def _repack_kernel(w_ref, *out_refs, plans):
    for o_ref, ranges in zip(out_refs, plans):
        width = o_ref.shape[-1]
        parts = [w_ref[:, a:b] for a, b in ranges]
        used = sum(b - a for a, b in ranges)
        if used < width:
            parts.append(jnp.zeros((w_ref.shape[0], width - used), F32))
        o_ref[...] = (parts[0] if len(parts) == 1 else jnp.concatenate(parts, axis=1)).astype(o_ref.dtype)


def _repack_call(w_in, plans, widths, rows=128):
    depth, d, n_in = w_in.shape
    kern = functools.partial(_repack_kernel, plans=plans)
    return pl.pallas_call(
        kern,
        grid=(depth, d // rows),
        in_specs=[pl.BlockSpec((None, rows, n_in), lambda l, i: (l, i, 0))],
        out_specs=[pl.BlockSpec((None, rows, w), lambda l, i: (l, i, 0)) for w in widths],
        out_shape=[jax.ShapeDtypeStruct((depth, d, w), BF16) for w in widths],
        compiler_params=_cparams(("parallel", "parallel")),
        name="repack",
    )(w_in)


def _proj_kernel(x_ref, ng_ref, wtok_ref, wT_ref, tokp_ref, gT_ref, rtok_ref, rT_ref, G_ref,
                 omain_ref, og1_ref, og2_ref, okv_ref, oT_ref, stage_ref, *, tok_specs, T_kinds, tm):
    x = x_ref[...]
    ms = jnp.mean(x * x, axis=-1, keepdims=True)
    xn = (x * lax.rsqrt(ms + EPS) * ng_ref[...]).astype(BF16)

    dsts = (omain_ref, og1_ref, og2_ref, okv_ref)
    for c, (has_norm, has_rope, outs) in enumerate(tok_specs):
        y = _dot(xn, wtok_ref[:, c * CH:(c + 1) * CH])
        prm = tokp_ref[c]
        if has_norm:
            ss = _dot((y * y).astype(BF16), G_ref[...])
            inv = lax.rsqrt(ss * (1.0 / HEAD_DIM) + EPS)
            y = y * (jnp.where(prm[0:1, :] > 0.0, inv, 1.0) if has_norm == "mixed" else inv)
        y = y * prm[2:3, :]
        if has_rope:
            halves = []
            for hf in range(CH // LANE):
                yh = y[:, hf * LANE:(hf + 1) * LANE]
                cc, s1, s2 = rtok_ref[0], rtok_ref[1], rtok_ref[2]
                if has_rope == "mixed":
                    rf = prm[1:2, hf * LANE:(hf + 1) * LANE]
                    cc, s1, s2 = jnp.where(rf > 0.0, cc, 1.0), s1 * rf, s2 * rf
                halves.append(yh * cc + pltpu.roll(yh, 8, 1) * s1 + pltpu.roll(yh, LANE - 8, 1) * s2)
            y = jnp.concatenate(halves, axis=1)
        staged = False
        for dst, dil, col, lanes, rstride in outs:
            if dil == 1:
                dsts[dst][:, col:col + lanes] = y[:, :lanes].astype(BF16)
                continue
            if not staged:
                for hf in range(CH // LANE):
                    stage_ref[hf] = y[:, hf * LANE:(hf + 1) * LANE]
                staged = True
            for r in range(dil):
                for hf in range(lanes // LANE):
                    blk = stage_ref[hf, pl.ds(r, tm // dil, stride=dil), :]
                    c0 = r * rstride + col + hf * LANE
                    dsts[dst][:, c0:c0 + LANE] = blk.astype(BF16)

    cosT = rT_ref[0]
    sinT = rT_ref[1]
    for c, heads in enumerate(T_kinds):
        y = _dot(xn, wT_ref[:, c * CH:(c + 1) * CH]).T
        for h, (nrm, rope, scale, sigm) in enumerate(heads):
            r0 = c * CH + h * HEAD_DIM
            yh = y[h * HEAD_DIM:(h + 1) * HEAD_DIM, :]
            if nrm:
                msq = jnp.mean(yh * yh, axis=0, keepdims=True)
                yh = yh * lax.rsqrt(msq + EPS) * gT_ref[r0:r0 + HEAD_DIM, :]
            if rope:
                x1 = yh[0:8, :]
                x2 = yh[8:16, :]
                yh = jnp.concatenate([x1 * cosT - x2 * sinT, x2 * cosT + x1 * sinT, yh[16:, :]], axis=0)
            if scale != 1.0:
                yh = yh * scale
            if sigm:
                yh = jax.nn.sigmoid(yh)
            yb = yh.astype(BF16)
            for t in range(tm // LANE):
                oT_ref[t, r0:r0 + HEAD_DIM, :] = yb[:, t * LANE:(t + 1) * LANE]


def _proj_call(x, ng, wtok, wT, tokp, gT, rtok, rT, G, *, layer, tok_specs, T_kinds, n_main, tm=512):
    B, S, _ = x.shape
    ntok, nT = len(tok_specs), len(T_kinds)
    d1, d2 = DIL_PAIRS[1][1], DIL_PAIRS[2][1]
    kern = functools.partial(_proj_kernel, tok_specs=tok_specs, T_kinds=T_kinds, tm=tm)
    const2 = lambda b, n: (0, 0)
    return pl.pallas_call(
        kern,
        grid=(B, S // tm),
        in_specs=[
            pl.BlockSpec((None, tm, D_MODEL), lambda b, n: (b, n, 0)),
            pl.BlockSpec((1, D_MODEL), const2),
            pl.BlockSpec((None, D_MODEL, ntok * CH), lambda b, n: (layer, 0, 0)),
            pl.BlockSpec((None, D_MODEL, nT * CH), lambda b, n: (layer, 0, 0)),
            pl.BlockSpec((ntok, 8, CH), lambda b, n: (0, 0, 0)),
            pl.BlockSpec((nT * CH, 1), const2),
            pl.BlockSpec((3, tm, LANE), lambda b, n: (0, n, 0)),
            pl.BlockSpec((2, 8, tm), lambda b, n: (0, 0, n)),
            pl.BlockSpec((CH, CH), const2),
        ],
        out_specs=[
            pl.BlockSpec((None, tm, n_main * CH), lambda b, n: (b, n, 0)),
            pl.BlockSpec((None, tm // d1, d1 * 3 * CH), lambda b, n: (b, n, 0)),
            pl.BlockSpec((None, tm // d2, d2 * 3 * CH), lambda b, n: (b, n, 0)),
            pl.BlockSpec((None, tm // NSA_CMP_STRIDE, NSA_CMP_STRIDE * LANE), lambda b, n: (b, n, 0)),
            pl.BlockSpec((None, tm // LANE, nT * CH, LANE), lambda b, n: (b, n, 0, 0)),
        ],
        out_shape=[
            jax.ShapeDtypeStruct((B, S, n_main * CH), BF16),
            jax.ShapeDtypeStruct((B, S // d1, d1 * 3 * CH), BF16),
            jax.ShapeDtypeStruct((B, S // d2, d2 * 3 * CH), BF16),
            jax.ShapeDtypeStruct((B, S // NSA_CMP_STRIDE, NSA_CMP_STRIDE * LANE), BF16),
            jax.ShapeDtypeStruct((B, S // LANE, nT * CH, LANE), BF16),
        ],
        scratch_shapes=[pltpu.VMEM((CH // LANE, tm, LANE), F32)],
        compiler_params=_cparams(("parallel", "parallel")),
        name="proj",
    )(x, ng, wtok, wT, tokp, gT, rtok, rT, G)


def _pair_masked_q(q_pair, h):
    rid = lax.broadcasted_iota(jnp.int32, q_pair.shape, 0)
    lo = (h % 2) * HEAD_DIM
    keep = jnp.where(rid >= lo, jnp.where(rid < lo + HEAD_DIM, 1.0, 0.0), 0.0).astype(BF16)
    return q_pair * keep


def _lane_tiles(ref, t0, nt, r0, nr):
    return jnp.concatenate([ref[t0 + t, r0:r0 + nr, :] for t in range(nt)], axis=1)


BIGPOS = 1e30
ONES_ROWS = 16
HA = HEAD_DIM + ONES_ROWS
LOG2E = 1.4426950408889634


def _v_aug(vb):
    return jnp.concatenate([vb, jnp.ones((ONES_ROWS, vb.shape[1]), BF16)], axis=0)


def _online_cols(s, colsel, m_old, acc_old, vb_aug):
    tmax = jnp.max(s, axis=0, keepdims=True)
    m_new = jnp.where(colsel > 0.0, jnp.maximum(m_old, tmax), m_old)
    m_use = jnp.where(colsel > 0.0, m_new, BIGPOS)
    p = jnp.exp2(s - m_use).astype(BF16)
    acc = jnp.exp2(m_old - m_new) * acc_old + _dot(vb_aug, p)
    return m_new, acc


SB_LOG_CUTOFF = -144.0


def _sb_kernel(qT_ref, k_ref, vT_ref, U_ref, o_ref, qm_ref, acc_ref, carry_ref, *, tq):
    i = pl.program_id(1)
    nt = tq // LANE
    row = lax.broadcasted_iota(jnp.int32, (tq, tq), 0)
    col = lax.broadcasted_iota(jnp.int32, (tq, tq), 1)
    past = row < col
    for h in range(4):
        p = h // 2
        qm_ref[h] = _pair_masked_q(_lane_tiles(qT_ref, 0, nt, p * LANE, LANE), h)
    acc_ref[...] = jnp.zeros_like(acc_ref)
    carry_ref[...] = jnp.zeros_like(carry_ref)

    def tile(j, masked):
        scores, logsig, laters = [], [], []
        for h in range(4):
            p = h // 2
            kb = k_ref[pl.ds(pl.multiple_of(j * tq, tq), tq), p * LANE:(p + 1) * LANE]
            scores.append(_dot(kb, qm_ref[h]))
        worst = None
        for h in range(4):
            s = scores[h]
            sp = jnp.maximum(s, 0.0) + jnp.log2(1.0 + jnp.exp2(-jnp.abs(s)))
            lg = -sp
            if masked:
                lg = jnp.where(past, lg, 0.0)
            hi, lo = _split2(lg)
            carry = carry_ref[h:h + 1, :]
            laters.append(_dot(U_ref[...], hi) + _dot(U_ref[...], lo) + carry)
            logsig.append(s - sp)
            carry = carry + jnp.sum(lg, axis=0, keepdims=True)
            carry_ref[h:h + 1, :] = carry
            worst = carry if worst is None else jnp.maximum(worst, carry)
        for h in range(4):
            w = jnp.exp2(logsig[h] + laters[h])
            if masked:
                w = jnp.where(past, w, 0.0)
            vb = _lane_tiles(vT_ref, j * nt, nt, h * HEAD_DIM, HEAD_DIM)
            acc_ref[h * HEAD_DIM:(h + 1) * HEAD_DIM, :] += _dot(vb, w.astype(BF16))
        return jnp.max(worst)

    worst0 = tile(i, True)

    def cond(st):
        return jnp.logical_and(st[0] >= 0, st[1] > SB_LOG_CUTOFF)

    def body(st):
        return st[0] - 1, tile(st[0], False)

    lax.while_loop(cond, body, (i - 1, worst0))
    o_ref[...] = acc_ref[...].T.astype(o_ref.dtype)


def _attn_specs(S, tq, q_chunk, k_chunk, v_chunk):
    return [
        pl.BlockSpec((None, tq // LANE, CH, LANE), lambda b, i: (b, i, q_chunk, 0)),
        pl.BlockSpec((None, S, CH), lambda b, i: (b, 0, k_chunk)),
        pl.BlockSpec((None, S // LANE, CH, LANE), lambda b, i: (b, 0, v_chunk, 0)),
    ]


def _sb_call(oT, otok, U, *, q_chunk, k_chunk, v_chunk, tq=256):
    B, nlt, _, _ = oT.shape
    S = nlt * LANE
    kern = functools.partial(_sb_kernel, tq=tq)
    return pl.pallas_call(
        kern,
        grid=(B, S // tq),
        in_specs=_attn_specs(S, tq, q_chunk, k_chunk, v_chunk) + [pl.BlockSpec((tq, tq), lambda b, i: (0, 0))],
        out_specs=pl.BlockSpec((None, tq, BRANCH_W), lambda b, i: (b, i, 0)),
        out_shape=jax.ShapeDtypeStruct((B, S, BRANCH_W), BF16),
        scratch_shapes=[pltpu.VMEM((4, LANE, tq), BF16), pltpu.VMEM((CH, tq), F32), pltpu.VMEM((8, tq), F32)],
        compiler_params=_cparams(("parallel", "arbitrary")),
        name="stick_breaking",
    )(oT, otok, oT, U)


def _head_lanes(x, lane, h):
    lo = h * HEAD_DIM
    return jnp.where(lane >= lo, jnp.where(lane < lo + HEAD_DIM, x, jnp.zeros_like(x)), jnp.zeros_like(x))


def _band_kernel(q_ref, k_ref, v_ref, ind_ref, o_ref, lse_ref, *, tqb, max_dist):
    n = pl.program_id(2)
    n_prev = -(-max_dist // LANE)
    nkr = (n_prev + 1) * LANE
    nsub = tqb // LANE
    row = lax.broadcasted_iota(jnp.int32, (nkr, LANE), 0)
    col = lax.broadcasted_iota(jnp.int32, (nkr, LANE), 1)
    lane_q = lax.broadcasted_iota(jnp.int32, (LANE, LANE), 1)
    lane_v = lax.broadcasted_iota(jnp.int32, (nkr, CH), 1)
    for u in range(nsub):
        qt = n * nsub + u
        kt0 = jnp.maximum(qt - n_prev, 0)
        dist = (qt - kt0) * LANE + col - row
        bias = jnp.where(dist >= 0, jnp.where(dist <= max_dist, 0.0, NEG), NEG)
        k0 = pl.multiple_of(kt0 * LANE, LANE)
        kwin = k_ref[pl.ds(k0, nkr), :]
        vwin = v_ref[pl.ds(k0, nkr), :]
        qu = q_ref[u * LANE:(u + 1) * LANE, :]
        o_acc = None
        lses = []
        for h in range(4):
            p = h // 2
            qm = _head_lanes(qu[:, p * LANE:(p + 1) * LANE], lane_q, h % 2)
            s = _dot_nt(kwin[:, p * LANE:(p + 1) * LANE], qm) + bias
            m = jnp.max(s, axis=0, keepdims=True)
            e = jnp.exp2(s - m)
            den = jnp.sum(e, axis=0, keepdims=True)
            pn = e * (1.0 / den)
            contrib = _dot(pn.T.astype(BF16), _head_lanes(vwin, lane_v, h))
            o_acc = contrib if o_acc is None else o_acc + contrib
            lses.append(m + jnp.log(den) * LOG2E)
        o_ref[u * LANE:(u + 1) * LANE, :] = o_acc.astype(o_ref.dtype)
        ls = jnp.concatenate(lses + [jnp.zeros((LANE - 4, LANE), F32)], axis=0).T
        l1, l2, l3 = _split3(ls)
        lse_ref[u * LANE:(u + 1) * LANE, :] = (_dot(l1, ind_ref[...]) + _dot(l2, ind_ref[...])
                                               + _dot(l3, ind_ref[...]))


def _band_call(arr, ind, *, dil, per_res, qi, ki, vi, max_dist):
    B, L, _ = arr.shape
    tqb = min(512, L)
    kern = functools.partial(_band_kernel, tqb=tqb, max_dist=max_dist)
    return pl.pallas_call(
        kern,
        grid=(B, dil, L // tqb),
        in_specs=[
            pl.BlockSpec((None, tqb, CH), lambda b, r, n: (b, n, r * per_res + qi)),
            pl.BlockSpec((None, L, CH), lambda b, r, n: (b, 0, r * per_res + ki)),
            pl.BlockSpec((None, L, CH), lambda b, r, n: (b, 0, r * per_res + vi)),
            pl.BlockSpec((LANE, CH), lambda b, r, n: (0, 0)),
        ],
        out_specs=[
            pl.BlockSpec((None, tqb, BRANCH_W), lambda b, r, n: (b, n, r)),
            pl.BlockSpec((None, tqb, BRANCH_W), lambda b, r, n: (b, n, r)),
        ],
        out_shape=[
            jax.ShapeDtypeStruct((B, L, dil * BRANCH_W), BF16),
            jax.ShapeDtypeStruct((B, L, dil * BRANCH_W), F32),
        ],
        compiler_params=_cparams(("parallel", "parallel", "parallel")),
        name="banded",
    )(arr, arr, arr, ind)


def _moba_kernel(qT_ref, k_ref, vT_ref, A_ref, o_ref, kmean_ref, sel_ref, qm_ref, acc_ref, ml_ref, s_ref, *, tq):
    i = pl.program_id(1)
    nt = tq // LANE
    nblk = A_ref.shape[0]

    @pl.when(i == 0)
    def _():
        kmean_ref[...] = _dot(A_ref[...], k_ref[...])

    blk = lax.broadcasted_iota(jnp.int32, (nblk, tq), 0)
    row = lax.broadcasted_iota(jnp.int32, (tq, tq), 0)
    col = lax.broadcasted_iota(jnp.int32, (tq, tq), 1)
    gates, diag = [], []
    for h in range(4):
        p = h // 2
        qm = _pair_masked_q(_lane_tiles(qT_ref, 0, nt, p * LANE, LANE), h)
        qm_ref[h] = qm
        k1, k2, k3 = _split3(kmean_ref[:, p * LANE:(p + 1) * LANE])
        gates.append(_dot(k1, qm) + _dot(k2, qm) + _dot(k3, qm))
        kb = k_ref[pl.ds(pl.multiple_of(i * tq, tq), tq), p * LANE:(p + 1) * LANE]
        diag.append(_dot(kb, qm))
    for h in range(4):
        g = jnp.where(blk < i, gates[h], NEG)
        sel = jnp.zeros((nblk, tq), F32)
        for _r in range(MOBA_TOPK):
            mx = jnp.max(g, axis=0, keepdims=True)
            idx = jnp.min(jnp.where(g == mx, blk, nblk), axis=0, keepdims=True)
            hit = blk == idx
            sel = jnp.where(hit, 1.0, sel)
            g = jnp.where(hit, -jnp.inf, g)
        sel_ref[h] = jnp.where(blk < i, sel, 0.0)

        s = jnp.where(row <= col, diag[h], NEG)
        m = jnp.max(s, axis=0, keepdims=True)
        ml_ref[h:h + 1, :] = m
        acc_ref[h * HA:(h + 1) * HA, :] = _dot(
            _v_aug(_lane_tiles(vT_ref, i * nt, nt, h * HEAD_DIM, HEAD_DIM)), jnp.exp2(s - m).astype(BF16))

    last = jnp.maximum(i - 1, 0)

    def qk(t, slot):
        for h in range(4):
            p = h // 2
            kbj = k_ref[pl.ds(pl.multiple_of(t * tq, tq), tq), p * LANE:(p + 1) * LANE]
            s_ref[slot, h] = _dot(kbj, qm_ref[h])

    def update(t, valid, slot):
        for h in range(4):
            srow = sel_ref[h, pl.ds(t, 1), :] * valid
            vb = _v_aug(_lane_tiles(vT_ref, t * nt, nt, h * HEAD_DIM, HEAD_DIM))
            rows = slice(h * HA, (h + 1) * HA)
            m, acc = _online_cols(s_ref[slot, h], srow, ml_ref[h:h + 1, :], acc_ref[rows, :], vb)
            ml_ref[h:h + 1, :] = m
            acc_ref[rows, :] = acc

    qk(0, 0)

    def body(jj, c):
        t0 = 2 * jj
        qk(jnp.minimum(t0 + 1, last), 1)
        update(t0, 1.0, 0)
        qk(jnp.minimum(t0 + 2, last), 0)
        update(jnp.minimum(t0 + 1, last), (t0 + 1 < i).astype(F32), 1)
        return c

    lax.fori_loop(0, (i + 1) // 2, body, 0)
    outs = [acc_ref[h * HA:h * HA + HEAD_DIM, :] / acc_ref[h * HA + HEAD_DIM:h * HA + HEAD_DIM + 1, :]
            for h in range(4)]
    o_ref[...] = jnp.concatenate(outs, axis=0).T.astype(o_ref.dtype)


def _moba_call(oT, otok, A, *, q_chunk, k_chunk, v_chunk):
    B, nlt, _, _ = oT.shape
    S = nlt * LANE
    tq = MOBA_BLOCK
    nblk = S // MOBA_BLOCK
    kern = functools.partial(_moba_kernel, tq=tq)
    return pl.pallas_call(
        kern,
        grid=(B, S // tq),
        in_specs=_attn_specs(S, tq, q_chunk, k_chunk, v_chunk) + [pl.BlockSpec((nblk, S), lambda b, i: (0, 0))],
        out_specs=pl.BlockSpec((None, tq, BRANCH_W), lambda b, i: (b, i, 0)),
        out_shape=jax.ShapeDtypeStruct((B, S, BRANCH_W), BF16),
        scratch_shapes=[pltpu.VMEM((nblk, CH), F32), pltpu.VMEM((4, nblk, tq), F32),
                        pltpu.VMEM((4, LANE, tq), BF16), pltpu.VMEM((4 * HA, tq), F32), pltpu.VMEM((8, tq), F32),
                        pltpu.VMEM((2, 4, tq, tq), F32)],
        compiler_params=_cparams(("parallel", "arbitrary")),
        name="moba",
    )(oT, otok, oT, A)


def _compress_kernel(x_ref, w1_ref, pe_ref, w1f_ref, w2_ref, gk_ref, G_ref, o_ref, oT_ref, acc_ref):
    l = pl.program_id(1)

    @pl.when(l == 0)
    def _():
        acc_ref[...] = jnp.zeros_like(acc_ref)

    x = x_ref[...]
    w_hi, w_lo = _split2(w1_ref[...])
    acc_ref[...] += _dot(x, w_hi) + _dot(x, w_lo)

    @pl.when(l == pl.num_programs(1) - 1)
    def _():
        nc = acc_ref.shape[0]
        r = acc_ref[...]
        bias = jnp.dot(pe_ref[...], w1f_ref[...], preferred_element_type=F32,
                       precision=lax.Precision.HIGHEST)
        p1 = jnp.concatenate([r[:, 0:64], r[:, 128:192]], axis=1)
        p2 = jnp.concatenate([r[:, 64:128], r[:, 192:256]], axis=1)
        hid = p1 + pltpu.roll(p2, nc - 1, 0) + bias[0:1, :]
        hid = hid * jax.nn.sigmoid(hid)
        comp = jnp.dot(hid, w2_ref[...], preferred_element_type=F32, precision=lax.Precision.HIGHEST)
        ss = _dot((comp * comp).astype(BF16), G_ref[...])
        lane = lax.broadcasted_iota(jnp.int32, comp.shape, 1)
        inv = jnp.where(lane < HEAD_DIM, lax.rsqrt(ss * (1.0 / HEAD_DIM) + EPS), 1.0)
        comp = comp * inv * gk_ref[...]
        rowi = lax.broadcasted_iota(jnp.int32, comp.shape, 0)
        comp = jnp.where(rowi < nc - 1, comp, 0.0)
        o_ref[...] = comp.astype(o_ref.dtype)
        oT_ref[...] = comp.T.astype(oT_ref.dtype)


def _compress_call(xv, w1blk, pe2, w1f, w2blk, gk, G128):
    B, nc, _ = xv.shape
    return pl.pallas_call(
        _compress_kernel,
        grid=(B, NSA_CMP_STRIDE),
        in_specs=[
            pl.BlockSpec((None, nc, LANE), lambda b, l: (b, 0, l)),
            pl.BlockSpec((None, LANE, CH), lambda b, l: (l, 0, 0)),
            pl.BlockSpec((8, 2 * NSA_CMP_LEN * HEAD_DIM), lambda b, l: (0, 0)),
            pl.BlockSpec((2 * NSA_CMP_LEN * HEAD_DIM, LANE), lambda b, l: (0, 0)),
            pl.BlockSpec((LANE, LANE), lambda b, l: (0, 0)),
            pl.BlockSpec((1, LANE), lambda b, l: (0, 0)),
            pl.BlockSpec((LANE, LANE), lambda b, l: (0, 0)),
        ],
        out_specs=[
            pl.BlockSpec((None, nc, LANE), lambda b, l: (b, 0, 0)),
            pl.BlockSpec((None, LANE, nc), lambda b, l: (b, 0, 0)),
        ],
        out_shape=[
            jax.ShapeDtypeStruct((B, nc, LANE), BF16),
            jax.ShapeDtypeStruct((B, LANE, nc), BF16),
        ],
        scratch_shapes=[pltpu.VMEM((nc, CH), F32)],
        compiler_params=_cparams(("parallel", "arbitrary")),
        name="nsa_compress",
    )(xv, w1blk, pe2, w1f, w2blk, gk, G128)


def _nsa_kernel(qT_ref, k_ref, vT_ref, kc_ref, kcT_ref, ov_ref, o_ref, sel_ref, qm_ref, acc_ref, ml_ref, s_ref,
                win_ref, *, tq):
    i = pl.program_id(1)
    nt = tq // LANE
    nq = 4 * tq
    nc = kc_ref.shape[0]
    nsel = ov_ref.shape[0]
    spb = tq // NSA_SEL_BLOCK

    zeros = jnp.zeros((HEAD_DIM, tq), BF16)
    q_heads = [_lane_tiles(qT_ref, 0, nt, h * HEAD_DIM, HEAD_DIM) for h in range(4)]
    qm_lo = jnp.concatenate([jnp.concatenate([q, zeros], axis=0) for q in q_heads], axis=1)
    qm_hi = jnp.concatenate([jnp.concatenate([zeros, q], axis=0) for q in q_heads], axis=1)
    qpos = i * tq + lax.broadcasted_iota(jnp.int32, (1, tq), 1)
    qpos4 = jnp.concatenate([qpos] * 4, axis=1)

    n_prev = -(-(NSA_WINDOW - 1) // tq)
    nk = n_prev + 1
    kt0 = jnp.maximum(i - n_prev, 0)
    kw = k_ref[pl.ds(pl.multiple_of(kt0 * tq, tq), nk * tq), LANE:2 * LANE]
    sw = _dot(kw, qm_hi)
    dist = qpos - (kt0 * tq + lax.broadcasted_iota(jnp.int32, (nk * tq, tq), 0))
    wbias = jnp.where(dist >= 0, jnp.where(dist <= NSA_WINDOW - 1, 0.0, NEG), NEG)
    sw = sw + jnp.concatenate([wbias] * 4, axis=1)
    mw = jnp.max(sw, axis=0, keepdims=True)
    ew = jnp.exp2(sw - mw)
    dw = jnp.sum(ew, axis=0, keepdims=True)
    vwin = _lane_tiles(vT_ref, kt0 * nt, nk * nt, HEAD_DIM, HEAD_DIM)
    win_ref[...] = _dot(vwin, ew.astype(BF16)) / dw

    zc = _dot(kc_ref[...], qm_lo)
    c_end = lax.broadcasted_iota(jnp.int32, (nc, tq), 0) * NSA_CMP_STRIDE + (NSA_CMP_LEN - 1)
    cbias = jnp.where(c_end <= qpos, 0.0, NEG)
    zc = zc + jnp.concatenate([cbias] * 4, axis=1)
    e = jnp.exp2(zc - jnp.max(zc, axis=0, keepdims=True))
    seen = jnp.where(qpos4 >= NSA_CMP_LEN - 1, 1.0, 0.0)
    pc = e * (seen / jnp.maximum(jnp.sum(e, axis=0, keepdims=True), 1.0))
    o_cmp = _dot(kcT_ref[HEAD_DIM:2 * HEAD_DIM, :], pc.astype(BF16))

    psum = pc[:, 0:tq] + pc[:, tq:2 * tq] + pc[:, 2 * tq:3 * tq] + pc[:, 3 * tq:4 * tq]
    p_hi, p_lo = _split2(psum)
    imp = _dot(ov_ref[...], p_hi) + _dot(ov_ref[...], p_lo)
    nid = lax.broadcasted_iota(jnp.int32, (nsel, tq), 0)
    cur = qpos // NSA_SEL_BLOCK
    imp = jnp.where(nid == 0, BIG, imp)
    imp = jnp.where(nid == cur, BIG, imp)
    imp = jnp.where(nid == cur - 1, BIG, imp)
    imp = jnp.where(nid > cur, NEG, imp)
    for _r in range(min(NSA_SEL_TOPK, nsel)):
        mx = jnp.max(imp, axis=0, keepdims=True)
        idx = jnp.min(jnp.where(imp == mx, nid, nsel), axis=0, keepdims=True)
        imp = jnp.where(nid == idx, -jnp.inf, imp)
    sel = jnp.where(nid <= cur, jnp.where(imp == -jnp.inf, 1.0, 0.0), 0.0)
    sel_ref[...] = sel
    for h in range(4):
        qm_ref[h] = qm_lo[:, h * tq:(h + 1) * tq]

    def qk(t, slot):
        kb = k_ref[pl.ds(pl.multiple_of(t * tq, tq), tq), LANE:2 * LANE]
        for h in range(4):
            s_ref[slot, h] = _dot(kb, qm_ref[h])

    kpos_d = i * tq + lax.broadcasted_iota(jnp.int32, (tq, tq), 0)
    qk(i, 0)
    vb_d = _v_aug(_lane_tiles(vT_ref, i * nt, nt, 0, HEAD_DIM))
    srows_d = [sel_ref[pl.ds(i * spb + u, 1), :] for u in range(spb)]
    for h in range(4):
        sc = s_ref[0, h]
        s = jnp.concatenate(
            [jnp.where(srows_d[u] > 0.0, sc[u * NSA_SEL_BLOCK:(u + 1) * NSA_SEL_BLOCK, :], NEG)
             for u in range(spb)], axis=0)
        s = jnp.where(kpos_d <= qpos, s, NEG)
        m = jnp.max(s, axis=0, keepdims=True)
        lanes = slice(h * tq, (h + 1) * tq)
        ml_ref[0:1, lanes] = m
        acc_ref[:, lanes] = _dot(vb_d, jnp.exp2(s - m).astype(BF16))

    last = jnp.maximum(i - 1, 0)

    def update(t, valid, slot):
        vb = _v_aug(_lane_tiles(vT_ref, t * nt, nt, 0, HEAD_DIM))
        srows = [sel_ref[pl.ds(t * spb + u, 1), :] * valid for u in range(spb)]
        for h in range(4):
            lanes = slice(h * tq, (h + 1) * tq)
            sc = s_ref[slot, h]
            subs = [sc[u * NSA_SEL_BLOCK:(u + 1) * NSA_SEL_BLOCK, :] for u in range(spb)]
            tmax = None
            for u in range(spb):
                mu = jnp.where(srows[u] > 0.0, jnp.max(subs[u], axis=0, keepdims=True), NEG)
                tmax = mu if tmax is None else jnp.maximum(tmax, mu)
            m_old = ml_ref[0:1, lanes]
            m_new = jnp.maximum(m_old, tmax)
            p = jnp.concatenate(
                [jnp.exp2(subs[u] - jnp.where(srows[u] > 0.0, m_new, BIGPOS)).astype(BF16) for u in range(spb)],
                axis=0)
            ml_ref[0:1, lanes] = m_new
            acc_ref[:, lanes] = jnp.exp2(m_old - m_new) * acc_ref[:, lanes] + _dot(vb, p)

    qk(0, 0)

    def body(jj, c):
        t0 = 2 * jj
        qk(jnp.minimum(t0 + 1, last), 1)
        update(t0, 1.0, 0)
        qk(jnp.minimum(t0 + 2, last), 0)
        update(jnp.minimum(t0 + 1, last), (t0 + 1 < i).astype(F32), 1)
        return c

    lax.fori_loop(0, (i + 1) // 2, body, 0)
    o_sel = acc_ref[0:HEAD_DIM, :] / acc_ref[HEAD_DIM:HEAD_DIM + 1, :]

    o_win = win_ref[...]

    gates = _lane_tiles(vT_ref, i * nt, nt, 2 * HEAD_DIM, 16).astype(F32)
    outs = []
    for h in range(4):
        sl = slice(h * tq, (h + 1) * tq)
        outs.append(gates[3 * h:3 * h + 1, :] * o_cmp[:, sl]
                    + gates[3 * h + 1:3 * h + 2, :] * o_sel[:, sl]
                    + gates[3 * h + 2:3 * h + 3, :] * o_win[:, sl])
    o_ref[...] = jnp.concatenate(outs, axis=0).T.astype(o_ref.dtype)


def _nsa_call(oT, otok, kc, kcT, ovT, *, q_chunk, kv_chunk, v_chunk):
    B, nlt, _, _ = oT.shape
    S = nlt * LANE
    tq = 256
    nc = S // NSA_CMP_STRIDE
    nsel = S // NSA_SEL_BLOCK
    kern = functools.partial(_nsa_kernel, tq=tq)
    return pl.pallas_call(
        kern,
        grid=(B, S // tq),
        in_specs=_attn_specs(S, tq, q_chunk, kv_chunk, v_chunk) + [
            pl.BlockSpec((None, nc, LANE), lambda b, i: (b, 0, 0)),
            pl.BlockSpec((None, LANE, nc), lambda b, i: (b, 0, 0)),
            pl.BlockSpec((nsel, nc), lambda b, i: (0, 0)),
        ],
        out_specs=pl.BlockSpec((None, tq, BRANCH_W), lambda b, i: (b, i, 0)),
        out_shape=jax.ShapeDtypeStruct((B, S, BRANCH_W), BF16),
        scratch_shapes=[pltpu.VMEM((nsel, tq), F32), pltpu.VMEM((4, LANE, tq), BF16),
                        pltpu.VMEM((HA, 4 * tq), F32), pltpu.VMEM((8, 4 * tq), F32),
                        pltpu.VMEM((2, 4, tq, tq), F32), pltpu.VMEM((HEAD_DIM, 4 * tq), F32)],
        compiler_params=_cparams(("parallel", "arbitrary")),
        name="nsa",
    )(oT, otok, oT, kc, kcT, ovT)


def _epi_kernel(x_ref, ng_ref, ya_ref, yc_ref, yd_ref, ob0_ref, ob1_ref, ob2_ref, l0_ref, l1_ref, l2_ref,
                za_ref, zb_ref, zc_ref, zd_ref, wmg_ref, wbr_ref, wout_ref, o_ref, nat_ref, *, tm):
    x = x_ref[...]
    ms = jnp.mean(x * x, axis=-1, keepdims=True)
    xn = (x * lax.rsqrt(ms + EPS) * ng_ref[...]).astype(BF16)

    def natural(ref, slot, dil):
        nh = BRANCH_W // LANE
        for r in range(dil):
            for hf in range(nh):
                c0 = r * BRANCH_W + hf * LANE
                nat_ref[slot * nh + hf, pl.ds(r, tm // dil, stride=dil), :] = ref[:, c0:c0 + LANE].astype(F32)
        return jnp.concatenate([nat_ref[slot * nh + hf] for hf in range(nh)], axis=1)

    d1, d2 = DIL_PAIRS[1][1], DIL_PAIRS[2][1]
    l0, l1, l2 = l0_ref[...], natural(l1_ref, 0, d1), natural(l2_ref, 1, d2)
    mx = jnp.maximum(jnp.maximum(l0, l1), l2)
    e0, e1, e2 = jnp.exp2(l0 - mx), jnp.exp2(l1 - mx), jnp.exp2(l2 - mx)
    yb = (e0 * ob0_ref[...].astype(F32) + e1 * natural(ob1_ref, 2, d1)
          + e2 * natural(ob2_ref, 3, d2)) / (e0 + e1 + e2)

    ys = (ya_ref[...].astype(F32), yb, yc_ref[...].astype(F32), yd_ref[...].astype(F32))
    zs = (za_ref, zb_ref, zc_ref, zd_ref)
    merged = None
    for i in range(4):
        z = zs[i][...].astype(F32)
        gated = (ys[i] * (z * jax.nn.sigmoid(z))).astype(BF16)
        br = _dot(gated, wbr_ref[i])
        mg = _dot(xn, wmg_ref[:, i * D_MODEL:(i + 1) * D_MODEL])
        term = jax.nn.sigmoid(mg) * br
        merged = term if merged is None else merged + term
    o_ref[...] = x + _dot(merged.astype(BF16), wout_ref[...])


def _epi_call(x2, ng, ya, yc, yd, obs, lses, otok2, wmg, wbr, wout, *, layer, z_chunks, tm=256):
    T = x2.shape[0]
    row = lambda i: (i, 0)
    full2 = lambda i: (0, 0)
    yspec = pl.BlockSpec((tm, BRANCH_W), row)
    zspecs = [pl.BlockSpec((tm, CH), (lambda i, c=c: (i, c))) for c in z_chunks]
    d1, d2 = DIL_PAIRS[1][1], DIL_PAIRS[2][1]
    gspecs = [yspec, pl.BlockSpec((tm // d1, d1 * BRANCH_W), row), pl.BlockSpec((tm // d2, d2 * BRANCH_W), row)]
    return pl.pallas_call(
        functools.partial(_epi_kernel, tm=tm),
        grid=(T // tm,),
        in_specs=[pl.BlockSpec((tm, D_MODEL), row), pl.BlockSpec((1, D_MODEL), full2)]
        + [yspec] * 3 + gspecs + gspecs + zspecs
        + [pl.BlockSpec((None, D_MODEL, 4 * D_MODEL), lambda i: (layer, 0, 0)),
           pl.BlockSpec((None, 4, BRANCH_W, D_MODEL), lambda i: (layer, 0, 0, 0)),
           pl.BlockSpec((None, D_MODEL, D_MODEL), lambda i: (layer, 0, 0))],
        out_specs=pl.BlockSpec((tm, D_MODEL), row),
        out_shape=jax.ShapeDtypeStruct((T, D_MODEL), F32),
        scratch_shapes=[pltpu.VMEM((4 * (BRANCH_W // LANE), tm, LANE), F32)],
        compiler_params=_cparams(("parallel",)),
        name="epilogue",
    )(x2, ng, ya, yc, yd, *obs, *lses, otok2, otok2, otok2, otok2, wmg, wbr, wout)


TOK_AK, TOK_AZ, TOK_BQ, TOK_BK, TOK_BV, TOK_BZ, TOK_CK, TOK_CZ, TOK_DKV, TOK_DZ = range(10)
N_MAIN = 10
T_AQ, T_AV, T_CQ, T_CV, T_DQ, T_DX = range(6)
SCALE = 1.0 / math.sqrt(HEAD_DIM)
QSCALE2 = SCALE * LOG2E

_PLAIN = (False, False, 1.0, False)
_T_KINDS = (
    ((False, False, QSCALE2, False),) * 4,
    (_PLAIN,) * 4,
    ((True, True, QSCALE2, False),) * 4,
    (_PLAIN,) * 4,
    ((True, True, QSCALE2, False),) * 4,
    (_PLAIN, _PLAIN, (False, False, 1.0, True), _PLAIN),
)


def _tok_specs():
    main = lambda j: (0, 1, j * CH, CH, 0)
    qkv = ((True, True), (True, True), (False, False))
    specs = []
    for j in range(N_MAIN):
        if j in (TOK_BQ, TOK_BK, TOK_CK):
            specs.append((True, True, (main(j),)))
        elif j == TOK_DKV:
            specs.append(("mixed", "mixed", (main(j), (3, NSA_CMP_STRIDE, 0, LANE, LANE))))
        else:
            specs.append((False, False, (main(j),)))
    for gi in (1, 2):
        dil = DIL_PAIRS[gi][1]
        for k, (nrm, rope) in enumerate(qkv):
            specs.append((nrm, rope, ((gi, dil, k * CH, CH, 3 * CH),)))
    return tuple(specs)


def _rope_tables(S, dil):
    L = S // dil
    pos = (jnp.arange(dil, dtype=jnp.int32)[:, None] + dil * jnp.arange(L, dtype=jnp.int32)[None, :]).astype(F32)
    inv = ROPE_THETA ** (-jnp.arange(0, ROT_DIM, 2, dtype=F32) / ROT_DIM)
    ang = pos[:, :, None] * inv[None, None, :]
    cos, sin = jnp.cos(ang), jnp.sin(ang)
    one = jnp.ones((dil, L, HEAD_DIM - ROT_DIM), F32)
    zero8 = jnp.zeros((dil, L, 8), F32)
    zero = jnp.zeros_like(one)
    c_head = jnp.concatenate([cos, cos, one], axis=-1)
    s1_head = jnp.concatenate([zero8, sin, zero], axis=-1)
    s2_head = jnp.concatenate([-sin, zero8, zero], axis=-1)
    rtok = jnp.stack([jnp.tile(t, (1, 1, LANE // HEAD_DIM)) for t in (c_head, s1_head, s2_head)], axis=1)
    rT = jnp.stack([cos.transpose(0, 2, 1), sin.transpose(0, 2, 1)], axis=1)
    return rtok, rT


def _tok_params(rows):
    out = []
    for nf, rf, gains in rows:
        nrow = jnp.concatenate([jnp.full((HEAD_DIM,), float(f), F32) for f in nf])
        rrow = jnp.concatenate([jnp.full((HEAD_DIM,), float(f), F32) for f in rf])
        grow = jnp.concatenate([g.astype(F32) for g in gains])
        out.append(jnp.concatenate([jnp.stack([nrow, rrow, grow]), jnp.zeros((5, CH), F32)], axis=0))
    return jnp.stack(out)


def kernel(x, norm_g, w_in, qk_g, cmp_pe, cmp_w1, cmp_w2, w_branch, w_out):
    B, S, _ = x.shape
    T = B * S
    o = COL_OFF
    ones = jnp.ones((HEAD_DIM,), F32)

    r = np.arange(CH)
    G = jnp.asarray((r[:, None] // HEAD_DIM == r[None, :] // HEAD_DIM).astype(np.float32), BF16)
    G128 = G[:LANE, :LANE]
    t = np.arange(256)
    U = jnp.asarray((t[None, :] > t[:, None]).astype(np.float32), BF16)
    nblk = S // MOBA_BLOCK
    A = jnp.asarray(np.repeat(np.eye(nblk, dtype=np.float32), MOBA_BLOCK, axis=1) / MOBA_BLOCK, BF16)
    nc, nsel = S // NSA_CMP_STRIDE, S // NSA_SEL_BLOCK
    cs = np.arange(nc) * NSA_CMP_STRIDE
    ss = np.arange(nsel) * NSA_SEL_BLOCK
    ov = ((cs[None, :] < ss[:, None] + NSA_SEL_BLOCK) & (cs[None, :] + NSA_CMP_LEN > ss[:, None]))
    ov[:, nc - 1] = False
    ovT = jnp.asarray(ov.astype(np.float32), BF16)
    hid = np.arange(LANE)[:, None]
    ind = jnp.asarray(((hid < 4) & (r[None, :] // HEAD_DIM == hid)).astype(np.float32), BF16)
    rtok4, rT4 = _rope_tables(S, 1)
    rtok, rT = rtok4[0], rT4[0]
    tok_specs = _tok_specs()

    dkv = o[13]
    bq = lambda gi: (o[4] + gi * CH, o[4] + (gi + 1) * CH)
    bk = lambda gi: (o[5] + gi * CH, o[5] + (gi + 1) * CH)
    bv = lambda gi: (o[6] + gi * CH, o[6] + (gi + 1) * CH)
    tok_plan = ((o[1], o[2]), (o[3], o[4]), bq(0), bk(0), bv(0), (o[7], o[8]), (o[9], o[10]), (o[11], o[12]),
                (dkv, dkv + 64), (dkv + 64, dkv + 128), (dkv + 128, dkv + 192), (dkv + 256, dkv + 320),
                (o[15], o[16]), bq(1), bk(1), bv(1), bq(2), bk(2), bv(2))
    T_plan = ((o[0], o[1]), (o[2], o[3]), (o[8], o[9]), (o[10], o[11]), (o[12], o[13]),
              (dkv + 192, dkv + 256), (dkv + 320, dkv + 384), (o[14], o[15]))
    wtok_all, wT_all, wmg_all = _repack_call(
        w_in, (tok_plan, T_plan, ((o[16], o[17]),)), (len(tok_specs) * CH, len(_T_KINDS) * CH, 4 * D_MODEL))

    wbr_all = w_branch.astype(BF16)
    wout_all = w_out.astype(BF16)

    for l in range(DEPTH):
        g = qk_g[l]
        plain = ((0,) * 4, (0,) * 4, (ones,) * 4)
        bq_p = ((1,) * 4, (1,) * 4, (g[0] * QSCALE2,) * 4)
        bk_p = ((1,) * 4, (1,) * 4, (g[1],) * 4)
        tokp = _tok_params([
            plain, plain, bq_p, bk_p, plain, plain,
            ((1,) * 4, (1,) * 4, (g[3],) * 4), plain,
            ((0, 0, 1, 1), (1, 0, 1, 1), (ones, ones, g[6], g[7])), plain,
            bq_p, bk_p, plain, bq_p, bk_p, plain])
        gT = jnp.concatenate([jnp.tile(ones, 8), jnp.tile(g[2], 4), jnp.tile(ones, 4), jnp.tile(g[4], 4),
                              jnp.tile(ones, 4)])[:, None]
        ng = norm_g[l][None, :]
        otok, og1, og2, okv, oT = _proj_call(x, ng, wtok_all, wT_all, tokp, gT, rtok, rT, G, layer=l,
                                             tok_specs=tok_specs, T_kinds=_T_KINDS, n_main=N_MAIN)

        ya = _sb_call(oT, otok, U, q_chunk=T_AQ, k_chunk=TOK_AK, v_chunk=T_AV)

        obs, lses = [], []
        for gi, (window, dil) in enumerate(DIL_PAIRS):
            if dil == 1:
                ob, lse = _band_call(otok, ind, dil=1, per_res=N_MAIN, qi=TOK_BQ, ki=TOK_BK, vi=TOK_BV,
                                     max_dist=window)
            else:
                ob, lse = _band_call((og1, og2)[gi - 1], ind, dil=dil, per_res=3, qi=0, ki=1, vi=2,
                                     max_dist=window // dil)
            obs.append(ob.reshape(T // dil, dil * BRANCH_W))
            lses.append(lse.reshape(T // dil, dil * BRANCH_W))

        yc = _moba_call(oT, otok, A, q_chunk=T_CQ, k_chunk=TOK_CK, v_chunk=T_CV)

        w1 = cmp_w1[l].reshape(2, NSA_CMP_LEN, HEAD_DIM, HEAD_DIM)
        z64 = jnp.zeros((NSA_CMP_STRIDE, HEAD_DIM, HEAD_DIM), F32)
        w1blk = jnp.concatenate([
            jnp.concatenate([w1[0, :16], w1[0, 16:], z64, z64], axis=2),
            jnp.concatenate([z64, z64, w1[1, :16], w1[1, 16:]], axis=2)], axis=1)
        pe2 = jnp.concatenate([cmp_pe[l].reshape(1, -1), jnp.zeros((7, 2 * NSA_CMP_LEN * HEAD_DIM), F32)], axis=0)
        zf = jnp.zeros((NSA_CMP_LEN * HEAD_DIM, HEAD_DIM), F32)
        w1f = jnp.concatenate([jnp.concatenate([cmp_w1[l, 0], zf], axis=1),
                               jnp.concatenate([zf, cmp_w1[l, 1]], axis=1)], axis=0)
        z2 = jnp.zeros((HEAD_DIM, HEAD_DIM), F32)
        w2blk = jnp.concatenate([jnp.concatenate([cmp_w2[l, 0], z2], axis=1),
                                 jnp.concatenate([z2, cmp_w2[l, 1]], axis=1)], axis=0)
        gk = jnp.concatenate([g[5], ones])[None, :]
        kc, kcT = _compress_call(okv, w1blk, pe2, w1f, w2blk, gk, G128)
        yd = _nsa_call(oT, otok, kc, kcT, ovT, q_chunk=T_DQ, kv_chunk=TOK_DKV, v_chunk=T_DX)

        x2 = _epi_call(x.reshape(T, D_MODEL), ng, ya.reshape(T, BRANCH_W), yc.reshape(T, BRANCH_W),
                       yd.reshape(T, BRANCH_W), obs, lses, otok.reshape(T, N_MAIN * CH), wmg_all,
                       wbr_all, wout_all, layer=l, z_chunks=(TOK_AZ, TOK_BZ, TOK_CZ, TOK_DZ))
        x = x2.reshape(B, S, D_MODEL)
    return x
```

```python
import functools
import math

import numpy as np
import jax
import jax.numpy as jnp
from jax import lax
from jax.experimental import pallas as pl
from jax.experimental.pallas import tpu as pltpu

F32 = jnp.float32
BF16 = jnp.bfloat16

D_MODEL = 1024
DEPTH = 4
HEAD_DIM = 64
ROT_DIM = 16
ROPE_THETA = 500000.0
EPS = 1e-6
NEG = -1e30
BIG = 1e9
BRANCH_W = 256
DIL_PAIRS = ((128, 1), (512, 4), (2048, 16))
MOBA_BLOCK = 256
MOBA_TOPK = 3
NSA_CMP_LEN = 32
NSA_CMP_STRIDE = 16
NSA_SEL_BLOCK = 64
NSA_SEL_TOPK = 16
NSA_WINDOW = 512
COL_SIZES = (256, 256, 256, 256, 768, 768, 768, 256, 256, 256, 256, 256, 256, 384, 12, 256, 4096)
COL_OFF = tuple(int(v) for v in np.concatenate([[0], np.cumsum(COL_SIZES)]))

LANE = 128
CH = 256
VMEM_LIMIT = 56 * 1024 * 1024


def _cparams(sem):
    return pltpu.CompilerParams(dimension_semantics=sem, vmem_limit_bytes=VMEM_LIMIT)


def _dot(a, b):
    return jnp.dot(a, b, preferred_element_type=F32)


def _dot_nt(a, b):
    return lax.dot_general(a, b, (((1,), (1,)), ((), ())), preferred_element_type=F32)


def _split2(x):
    hi = x.astype(BF16)
    lo = (x - hi.astype(F32)).astype(BF16)
    return hi, lo


def _split3(x):
    hi = x.astype(BF16)
    r = x - hi.astype(F32)
    mid = r.astype(BF16)
    lo = (r - mid.astype(F32)).astype(BF16)
    return hi, mid, lo


def _repack_kernel(w_ref, *out_refs, plans):
    for o_ref, ranges in zip(out_refs, plans):
        width = o_ref.shape[-1]
        parts = [w_ref[:, a:b] for a, b in ranges]
        used = sum(b - a for a, b in ranges)
        if used < width:
            parts.append(jnp.zeros((w_ref.shape[0], width - used), F32))
        o_ref[...] = (parts[0] if len(parts) == 1 else jnp.concatenate(parts, axis=1)).astype(o_ref.dtype)


def _repack_call(w_in, plans, widths, rows=128):
    depth, d, n_in = w_in.shape
    kern = functools.partial(_repack_kernel, plans=plans)
    return pl.pallas_call(
        kern,
        grid=(depth, d // rows),
        in_specs=[pl.BlockSpec((None, rows, n_in), lambda l, i: (l, i, 0))],
        out_specs=[pl.BlockSpec((None, rows, w), lambda l, i: (l, i, 0)) for w in widths],
        out_shape=[jax.ShapeDtypeStruct((depth, d, w), BF16) for w in widths],
        compiler_params=_cparams(("parallel", "parallel")),
        name="repack",
    )(w_in)


def _proj_kernel(x_ref, ng_ref, wtok_ref, wT_ref, tokp_ref, gT_ref, rtok_ref, rT_ref, G_ref,
                 omain_ref, og1_ref, og2_ref, okv_ref, oT_ref, stage_ref, *, tok_specs, T_kinds, tm):
    x = x_ref[...]
    ms = jnp.mean(x * x, axis=-1, keepdims=True)
    xn = (x * lax.rsqrt(ms + EPS) * ng_ref[...]).astype(BF16)

    dsts = (omain_ref, og1_ref, og2_ref, okv_ref)
    for c, (has_norm, has_rope, outs) in enumerate(tok_specs):
        y = _dot(xn, wtok_ref[:, c * CH:(c + 1) * CH])
        prm = tokp_ref[c]
        if has_norm:
            ss = _dot((y * y).astype(BF16), G_ref[...])
            inv = lax.rsqrt(ss * (1.0 / HEAD_DIM) + EPS)
            y = y * (jnp.where(prm[0:1, :] > 0.0, inv, 1.0) if has_norm == "mixed" else inv)
        y = y * prm[2:3, :]
        if has_rope:
            halves = []
            for hf in range(CH // LANE):
                yh = y[:, hf * LANE:(hf + 1) * LANE]
                cc, s1, s2 = rtok_ref[0], rtok_ref[1], rtok_ref[2]
                if has_rope == "mixed":
                    rf = prm[1:2, hf * LANE:(hf + 1) * LANE]
                    cc, s1, s2 = jnp.where(rf > 0.0, cc, 1.0), s1 * rf, s2 * rf
                halves.append(yh * cc + pltpu.roll(yh, 8, 1) * s1 + pltpu.roll(yh, LANE - 8, 1) * s2)
            y = jnp.concatenate(halves, axis=1)
        staged = False
        for dst, dil, col, lanes, rstride in outs:
            if dil == 1:
                dsts[dst][:, col:col + lanes] = y[:, :lanes].astype(BF16)
                continue
            if not staged:
                for hf in range(CH // LANE):
                    stage_ref[hf] = y[:, hf * LANE:(hf + 1) * LANE]
                staged = True
            for r in range(dil):
                for hf in range(lanes // LANE):
                    blk = stage_ref[hf, pl.ds(r, tm // dil, stride=dil), :]
                    c0 = r * rstride + col + hf * LANE
                    dsts[dst][:, c0:c0 + LANE] = blk.astype(BF16)

    cosT = rT_ref[0]
    sinT = rT_ref[1]
    for c, heads in enumerate(T_kinds):
        y = _dot(xn, wT_ref[:, c * CH:(c + 1) * CH]).T
        for h, (nrm, rope, scale, sigm) in enumerate(heads):
            r0 = c * CH + h * HEAD_DIM
            yh = y[h * HEAD_DIM:(h + 1) * HEAD_DIM, :]
            if nrm:
                msq = jnp.mean(yh * yh, axis=0, keepdims=True)
                yh = yh * lax.rsqrt(msq + EPS) * gT_ref[r0:r0 + HEAD_DIM, :]
            if rope:
                x1 = yh[0:8, :]
                x2 = yh[8:16, :]
                yh = jnp.concatenate([x1 * cosT - x2 * sinT, x2 * cosT + x1 * sinT, yh[16:, :]], axis=0)
            if scale != 1.0:
                yh = yh * scale
            if sigm:
                yh = jax.nn.sigmoid(yh)
            yb = yh.astype(BF16)
            for t in range(tm // LANE):
                oT_ref[t, r0:r0 + HEAD_DIM, :] = yb[:, t * LANE:(t + 1) * LANE]


def _proj_call(x, ng, wtok, wT, tokp, gT, rtok, rT, G, *, layer, tok_specs, T_kinds, n_main, tm=512):
    B, S, _ = x.shape
    ntok, nT = len(tok_specs), len(T_kinds)
    d1, d2 = DIL_PAIRS[1][1], DIL_PAIRS[2][1]
    kern = functools.partial(_proj_kernel, tok_specs=tok_specs, T_kinds=T_kinds, tm=tm)
    const2 = lambda b, n: (0, 0)
    return pl.pallas_call(
        kern,
        grid=(B, S // tm),
        in_specs=[
            pl.BlockSpec((None, tm, D_MODEL), lambda b, n: (b, n, 0)),
            pl.BlockSpec((1, D_MODEL), const2),
            pl.BlockSpec((None, D_MODEL, ntok * CH), lambda b, n: (layer, 0, 0)),
            pl.BlockSpec((None, D_MODEL, nT * CH), lambda b, n: (layer, 0, 0)),
            pl.BlockSpec((ntok, 8, CH), lambda b, n: (0, 0, 0)),
            pl.BlockSpec((nT * CH, 1), const2),
            pl.BlockSpec((3, tm, LANE), lambda b, n: (0, n, 0)),
            pl.BlockSpec((2, 8, tm), lambda b, n: (0, 0, n)),
            pl.BlockSpec((CH, CH), const2),
        ],
        out_specs=[
            pl.BlockSpec((None, tm, n_main * CH), lambda b, n: (b, n, 0)),
            pl.BlockSpec((None, tm // d1, d1 * 3 * CH), lambda b, n: (b, n, 0)),
            pl.BlockSpec((None, tm // d2, d2 * 3 * CH), lambda b, n: (b, n, 0)),
            pl.BlockSpec((None, tm // NSA_CMP_STRIDE, NSA_CMP_STRIDE * LANE), lambda b, n: (b, n, 0)),
            pl.BlockSpec((None, tm // LANE, nT * CH, LANE), lambda b, n: (b, n, 0, 0)),
        ],
        out_shape=[
            jax.ShapeDtypeStruct((B, S, n_main * CH), BF16),
            jax.ShapeDtypeStruct((B, S // d1, d1 * 3 * CH), BF16),
            jax.ShapeDtypeStruct((B, S // d2, d2 * 3 * CH), BF16),
            jax.ShapeDtypeStruct((B, S // NSA_CMP_STRIDE, NSA_CMP_STRIDE * LANE), BF16),
            jax.ShapeDtypeStruct((B, S // LANE, nT * CH, LANE), BF16),
        ],
        scratch_shapes=[pltpu.VMEM((CH // LANE, tm, LANE), F32)],
        compiler_params=_cparams(("parallel", "parallel")),
        name="proj",
    )(x, ng, wtok, wT, tokp, gT, rtok, rT, G)


def _pair_masked_q(q_pair, h):
    rid = lax.broadcasted_iota(jnp.int32, q_pair.shape, 0)
    lo = (h % 2) * HEAD_DIM
    keep = jnp.where(rid >= lo, jnp.where(rid < lo + HEAD_DIM, 1.0, 0.0), 0.0).astype(BF16)
    return q_pair * keep


def _lane_tiles(ref, t0, nt, r0, nr):
    return jnp.concatenate([ref[t0 + t, r0:r0 + nr, :] for t in range(nt)], axis=1)


BIGPOS = 1e30
ONES_ROWS = 16
HA = HEAD_DIM + ONES_ROWS
LOG2E = 1.4426950408889634


def _v_aug(vb):
    return jnp.concatenate([vb, jnp.ones((ONES_ROWS, vb.shape[1]), BF16)], axis=0)


def _online_cols(s, colsel, m_old, acc_old, vb_aug):
    tmax = jnp.max(s, axis=0, keepdims=True)
    m_new = jnp.where(colsel > 0.0, jnp.maximum(m_old, tmax), m_old)
    m_use = jnp.where(colsel > 0.0, m_new, BIGPOS)
    p = jnp.exp2(s - m_use).astype(BF16)
    acc = jnp.exp2(m_old - m_new) * acc_old + _dot(vb_aug, p)
    return m_new, acc


SB_LOG_CUTOFF = -144.0


def _sb_kernel(qT_ref, k_ref, vT_ref, U_ref, o_ref, qm_ref, acc_ref, carry_ref, *, tq):
    i = pl.program_id(1)
    nt = tq // LANE
    row = lax.broadcasted_iota(jnp.int32, (tq, tq), 0)
    col = lax.broadcasted_iota(jnp.int32, (tq, tq), 1)
    past = row < col
    for h in range(4):
        p = h // 2
        qm_ref[h] = _pair_masked_q(_lane_tiles(qT_ref, 0, nt, p * LANE, LANE), h)
    acc_ref[...] = jnp.zeros_like(acc_ref)
    carry_ref[...] = jnp.zeros_like(carry_ref)

    def tile(j, masked):
        scores, logsig, laters = [], [], []
        for h in range(4):
            p = h // 2
            kb = k_ref[pl.ds(pl.multiple_of(j * tq, tq), tq), p * LANE:(p + 1) * LANE]
            scores.append(_dot(kb, qm_ref[h]))
        worst = None
        for h in range(4):
            s = scores[h]
            sp = jnp.maximum(s, 0.0) + jnp.log2(1.0 + jnp.exp2(-jnp.abs(s)))
            lg = -sp
            if masked:
                lg = jnp.where(past, lg, 0.0)
            hi, lo = _split2(lg)
            carry = carry_ref[h:h + 1, :]
            laters.append(_dot(U_ref[...], hi) + _dot(U_ref[...], lo) + carry)
            logsig.append(s - sp)
            carry = carry + jnp.sum(lg, axis=0, keepdims=True)
            carry_ref[h:h + 1, :] = carry
            worst = carry if worst is None else jnp.maximum(worst, carry)
        for h in range(4):
            w = jnp.exp2(logsig[h] + laters[h])
            if masked:
                w = jnp.where(past, w, 0.0)
            vb = _lane_tiles(vT_ref, j * nt, nt, h * HEAD_DIM, HEAD_DIM)
            acc_ref[h * HEAD_DIM:(h + 1) * HEAD_DIM, :] += _dot(vb, w.astype(BF16))
        return jnp.max(worst)

    worst0 = tile(i, True)

    def cond(st):
        return jnp.logical_and(st[0] >= 0, st[1] > SB_LOG_CUTOFF)

    def body(st):
        return st[0] - 1, tile(st[0], False)

    lax.while_loop(cond, body, (i - 1, worst0))
    o_ref[...] = acc_ref[...].T.astype(o_ref.dtype)


def _attn_specs(S, tq, q_chunk, k_chunk, v_chunk):
    return [
        pl.BlockSpec((None, tq // LANE, CH, LANE), lambda b, i: (b, i, q_chunk, 0)),
        pl.BlockSpec((None, S, CH), lambda b, i: (b, 0, k_chunk)),
        pl.BlockSpec((None, S // LANE, CH, LANE), lambda b, i: (b, 0, v_chunk, 0)),
    ]


def _sb_call(oT, otok, U, *, q_chunk, k_chunk, v_chunk, tq=256):
    B, nlt, _, _ = oT.shape
    S = nlt * LANE
    kern = functools.partial(_sb_kernel, tq=tq)
    return pl.pallas_call(
        kern,
        grid=(B, S // tq),
        in_specs=_attn_specs(S, tq, q_chunk, k_chunk, v_chunk) + [pl.BlockSpec((tq, tq), lambda b, i: (0, 0))],
        out_specs=pl.BlockSpec((None, tq, BRANCH_W), lambda b, i: (b, i, 0)),
        out_shape=jax.ShapeDtypeStruct((B, S, BRANCH_W), BF16),
        scratch_shapes=[pltpu.VMEM((4, LANE, tq), BF16), pltpu.VMEM((CH, tq), F32), pltpu.VMEM((8, tq), F32)],
        compiler_params=_cparams(("parallel", "arbitrary")),
        name="stick_breaking",
    )(oT, otok, oT, U)


def _head_lanes(x, lane, h):
    lo = h * HEAD_DIM
    return jnp.where(lane >= lo, jnp.where(lane < lo + HEAD_DIM, x, jnp.zeros_like(x)), jnp.zeros_like(x))


def _band_kernel(q_ref, k_ref, v_ref, ind_ref, o_ref, lse_ref, *, tqb, max_dist):
    n = pl.program_id(2)
    n_prev = -(-max_dist // LANE)
    nkr = (n_prev + 1) * LANE
    nsub = tqb // LANE
    row = lax.broadcasted_iota(jnp.int32, (nkr, LANE), 0)
    col = lax.broadcasted_iota(jnp.int32, (nkr, LANE), 1)
    lane_q = lax.broadcasted_iota(jnp.int32, (LANE, LANE), 1)
    lane_v = lax.broadcasted_iota(jnp.int32, (nkr, CH), 1)
    for u in range(nsub):
        qt = n * nsub + u
        kt0 = jnp.maximum(qt - n_prev, 0)
        dist = (qt - kt0) * LANE + col - row
        bias = jnp.where(dist >= 0, jnp.where(dist <= max_dist, 0.0, NEG), NEG)
        k0 = pl.multiple_of(kt0 * LANE, LANE)
        kwin = k_ref[pl.ds(k0, nkr), :]
        vwin = v_ref[pl.ds(k0, nkr), :]
        qu = q_ref[u * LANE:(u + 1) * LANE, :]
        o_acc = None
        lses = []
        for h in range(4):
            p = h // 2
            qm = _head_lanes(qu[:, p * LANE:(p + 1) * LANE], lane_q, h % 2)
            s = _dot_nt(kwin[:, p * LANE:(p + 1) * LANE], qm) + bias
            m = jnp.max(s, axis=0, keepdims=True)
            e = jnp.exp2(s - m)
            den = jnp.sum(e, axis=0, keepdims=True)
            pn = e * (1.0 / den)
            contrib = _dot(pn.T.astype(BF16), _head_lanes(vwin, lane_v, h))
            o_acc = contrib if o_acc is None else o_acc + contrib
            lses.append(m + jnp.log(den) * LOG2E)
        o_ref[u * LANE:(u + 1) * LANE, :] = o_acc.astype(o_ref.dtype)
        ls = jnp.concatenate(lses + [jnp.zeros((LANE - 4, LANE), F32)], axis=0).T
        l1, l2, l3 = _split3(ls)
        lse_ref[u * LANE:(u + 1) * LANE, :] = (_dot(l1, ind_ref[...]) + _dot(l2, ind_ref[...])
                                               + _dot(l3, ind_ref[...]))


def _band_call(arr, ind, *, dil, per_res, qi, ki, vi, max_dist):
    B, L, _ = arr.shape
    tqb = min(512, L)
    kern = functools.partial(_band_kernel, tqb=tqb, max_dist=max_dist)
    return pl.pallas_call(
        kern,
        grid=(B, dil, L // tqb),
        in_specs=[
            pl.BlockSpec((None, tqb, CH), lambda b, r, n: (b, n, r * per_res + qi)),
            pl.BlockSpec((None, L, CH), lambda b, r, n: (b, 0, r * per_res + ki)),
            pl.BlockSpec((None, L, CH), lambda b, r, n: (b, 0, r * per_res + vi)),
            pl.BlockSpec((LANE, CH), lambda b, r, n: (0, 0)),
        ],
        out_specs=[
            pl.BlockSpec((None, tqb, BRANCH_W), lambda b, r, n: (b, n, r)),
            pl.BlockSpec((None, tqb, BRANCH_W), lambda b, r, n: (b, n, r)),
        ],
        out_shape=[
            jax.ShapeDtypeStruct((B, L, dil * BRANCH_W), BF16),
            jax.ShapeDtypeStruct((B, L, dil * BRANCH_W), F32),
        ],
        compiler_params=_cparams(("parallel", "parallel", "parallel")),
        name="banded",
    )(arr, arr, arr, ind)


def _moba_kernel(qT_ref, k_ref, vT_ref, A_ref, o_ref, kmean_ref, sel_ref, qm_ref, acc_ref, ml_ref, s_ref, *, tq):
    i = pl.program_id(1)
    nt = tq // LANE
    nblk = A_ref.shape[0]

    @pl.when(i == 0)
    def _():
        kmean_ref[...] = _dot(A_ref[...], k_ref[...])

    blk = lax.broadcasted_iota(jnp.int32, (nblk, tq), 0)
    row = lax.broadcasted_iota(jnp.int32, (tq, tq), 0)
    col = lax.broadcasted_iota(jnp.int32, (tq, tq), 1)
    gates, diag = [], []
    for h in range(4):
        p = h // 2
        qm = _pair_masked_q(_lane_tiles(qT_ref, 0, nt, p * LANE, LANE), h)
        qm_ref[h] = qm
        k1, k2, k3 = _split3(kmean_ref[:, p * LANE:(p + 1) * LANE])
        gates.append(_dot(k1, qm) + _dot(k2, qm) + _dot(k3, qm))
        kb = k_ref[pl.ds(pl.multiple_of(i * tq, tq), tq), p * LANE:(p + 1) * LANE]
        diag.append(_dot(kb, qm))
    for h in range(4):
        g = jnp.where(blk < i, gates[h], NEG)
        sel = jnp.zeros((nblk, tq), F32)
        for _r in range(MOBA_TOPK):
            mx = jnp.max(g, axis=0, keepdims=True)
            idx = jnp.min(jnp.where(g == mx, blk, nblk), axis=0, keepdims=True)
            hit = blk == idx
            sel = jnp.where(hit, 1.0, sel)
            g = jnp.where(hit, -jnp.inf, g)
        sel_ref[h] = jnp.where(blk < i, sel, 0.0)

        s = jnp.where(row <= col, diag[h], NEG)
        m = jnp.max(s, axis=0, keepdims=True)
        ml_ref[h:h + 1, :] = m
        acc_ref[h * HA:(h + 1) * HA, :] = _dot(
            _v_aug(_lane_tiles(vT_ref, i * nt, nt, h * HEAD_DIM, HEAD_DIM)), jnp.exp2(s - m).astype(BF16))

    last = jnp.maximum(i - 1, 0)

    def qk(t, slot):
        for h in range(4):
            p = h // 2
            kbj = k_ref[pl.ds(pl.multiple_of(t * tq, tq), tq), p * LANE:(p + 1) * LANE]
            s_ref[slot, h] = _dot(kbj, qm_ref[h])

    def update(t, valid, slot):
        for h in range(4):
            srow = sel_ref[h, pl.ds(t, 1), :] * valid
            vb = _v_aug(_lane_tiles(vT_ref, t * nt, nt, h * HEAD_DIM, HEAD_DIM))
            rows = slice(h * HA, (h + 1) * HA)
            m, acc = _online_cols(s_ref[slot, h], srow, ml_ref[h:h + 1, :], acc_ref[rows, :], vb)
            ml_ref[h:h + 1, :] = m
            acc_ref[rows, :] = acc

    qk(0, 0)

    def body(jj, c):
        t0 = 2 * jj
        qk(jnp.minimum(t0 + 1, last), 1)
        update(t0, 1.0, 0)
        qk(jnp.minimum(t0 + 2, last), 0)
        update(jnp.minimum(t0 + 1, last), (t0 + 1 < i).astype(F32), 1)
        return c

    lax.fori_loop(0, (i + 1) // 2, body, 0)
    outs = [acc_ref[h * HA:h * HA + HEAD_DIM, :] / acc_ref[h * HA + HEAD_DIM:h * HA + HEAD_DIM + 1, :]
            for h in range(4)]
    o_ref[...] = jnp.concatenate(outs, axis=0).T.astype(o_ref.dtype)


def _moba_call(oT, otok, A, *, q_chunk, k_chunk, v_chunk):
    B, nlt, _, _ = oT.shape
    S = nlt * LANE
    tq = MOBA_BLOCK
    nblk = S // MOBA_BLOCK
    kern = functools.partial(_moba_kernel, tq=tq)
    return pl.pallas_call(
        kern,
        grid=(B, S // tq),
        in_specs=_attn_specs(S, tq, q_chunk, k_chunk, v_chunk) + [pl.BlockSpec((nblk, S), lambda b, i: (0, 0))],
        out_specs=pl.BlockSpec((None, tq, BRANCH_W), lambda b, i: (b, i, 0)),
        out_shape=jax.ShapeDtypeStruct((B, S, BRANCH_W), BF16),
        scratch_shapes=[pltpu.VMEM((nblk, CH), F32), pltpu.VMEM((4, nblk, tq), F32),
                        pltpu.VMEM((4, LANE, tq), BF16), pltpu.VMEM((4 * HA, tq), F32), pltpu.VMEM((8, tq), F32),
                        pltpu.VMEM((2, 4, tq, tq), F32)],
        compiler_params=_cparams(("parallel", "arbitrary")),
        name="moba",
    )(oT, otok, oT, A)


def _compress_kernel(x_ref, w1_ref, pe_ref, w1f_ref, w2_ref, gk_ref, G_ref, o_ref, oT_ref, acc_ref):
    l = pl.program_id(1)

    @pl.when(l == 0)
    def _():
        acc_ref[...] = jnp.zeros_like(acc_ref)

    x = x_ref[...]
    w_hi, w_lo = _split2(w1_ref[...])
    acc_ref[...] += _dot(x, w_hi) + _dot(x, w_lo)

    @pl.when(l == pl.num_programs(1) - 1)
    def _():
        nc = acc_ref.shape[0]
        r = acc_ref[...]
        bias = jnp.dot(pe_ref[...], w1f_ref[...], preferred_element_type=F32,
                       precision=lax.Precision.HIGHEST)
        p1 = jnp.concatenate([r[:, 0:64], r[:, 128:192]], axis=1)
        p2 = jnp.concatenate([r[:, 64:128], r[:, 192:256]], axis=1)
        hid = p1 + pltpu.roll(p2, nc - 1, 0) + bias[0:1, :]
        hid = hid * jax.nn.sigmoid(hid)
        comp = jnp.dot(hid, w2_ref[...], preferred_element_type=F32, precision=lax.Precision.HIGHEST)
        ss = _dot((comp * comp).astype(BF16), G_ref[...])
        lane = lax.broadcasted_iota(jnp.int32, comp.shape, 1)
        inv = jnp.where(lane < HEAD_DIM, lax.rsqrt(ss * (1.0 / HEAD_DIM) + EPS), 1.0)
        comp = comp * inv * gk_ref[...]
        rowi = lax.broadcasted_iota(jnp.int32, comp.shape, 0)
        comp = jnp.where(rowi < nc - 1, comp, 0.0)
        o_ref[...] = comp.astype(o_ref.dtype)
        oT_ref[...] = comp.T.astype(oT_ref.dtype)


def _compress_call(xv, w1blk, pe2, w1f, w2blk, gk, G128):
    B, nc, _ = xv.shape
    return pl.pallas_call(
        _compress_kernel,
        grid=(B, NSA_CMP_STRIDE),
        in_specs=[
            pl.BlockSpec((None, nc, LANE), lambda b, l: (b, 0, l)),
            pl.BlockSpec((None, LANE, CH), lambda b, l: (l, 0, 0)),
            pl.BlockSpec((8, 2 * NSA_CMP_LEN * HEAD_DIM), lambda b, l: (0, 0)),
            pl.BlockSpec((2 * NSA_CMP_LEN * HEAD_DIM, LANE), lambda b, l: (0, 0)),
            pl.BlockSpec((LANE, LANE), lambda b, l: (0, 0)),
            pl.BlockSpec((1, LANE), lambda b, l: (0, 0)),
            pl.BlockSpec((LANE, LANE), lambda b, l: (0, 0)),
        ],
        out_specs=[
            pl.BlockSpec((None, nc, LANE), lambda b, l: (b, 0, 0)),
            pl.BlockSpec((None, LANE, nc), lambda b, l: (b, 0, 0)),
        ],
        out_shape=[
            jax.ShapeDtypeStruct((B, nc, LANE), BF16),
            jax.ShapeDtypeStruct((B, LANE, nc), BF16),
        ],
        scratch_shapes=[pltpu.VMEM((nc, CH), F32)],
        compiler_params=_cparams(("parallel", "arbitrary")),
        name="nsa_compress",
    )(xv, w1blk, pe2, w1f, w2blk, gk, G128)


def _nsa_kernel(qT_ref, k_ref, vT_ref, kc_ref, kcT_ref, ov_ref, o_ref, sel_ref, qm_ref, acc_ref, ml_ref, s_ref,
                win_ref, *, tq):
    i = pl.program_id(1)
    nt = tq // LANE
    nq = 4 * tq
    nc = kc_ref.shape[0]
    nsel = ov_ref.shape[0]
    spb = tq // NSA_SEL_BLOCK

    zeros = jnp.zeros((HEAD_DIM, tq), BF16)
    q_heads = [_lane_tiles(qT_ref, 0, nt, h * HEAD_DIM, HEAD_DIM) for h in range(4)]
    qm_lo = jnp.concatenate([jnp.concatenate([q, zeros], axis=0) for q in q_heads], axis=1)
    qm_hi = jnp.concatenate([jnp.concatenate([zeros, q], axis=0) for q in q_heads], axis=1)
    qpos = i * tq + lax.broadcasted_iota(jnp.int32, (1, tq), 1)
    qpos4 = jnp.concatenate([qpos] * 4, axis=1)

    n_prev = -(-(NSA_WINDOW - 1) // tq)
    nk = n_prev + 1
    kt0 = jnp.maximum(i - n_prev, 0)
    kw = k_ref[pl.ds(pl.multiple_of(kt0 * tq, tq), nk * tq), LANE:2 * LANE]
    sw = _dot(kw, qm_hi)
    dist = qpos - (kt0 * tq + lax.broadcasted_iota(jnp.int32, (nk * tq, tq), 0))
    wbias = jnp.where(dist >= 0, jnp.where(dist <= NSA_WINDOW - 1, 0.0, NEG), NEG)
    sw = sw + jnp.concatenate([wbias] * 4, axis=1)
    mw = jnp.max(sw, axis=0, keepdims=True)
    ew = jnp.exp2(sw - mw)
    dw = jnp.sum(ew, axis=0, keepdims=True)
    vwin = _lane_tiles(vT_ref, kt0 * nt, nk * nt, HEAD_DIM, HEAD_DIM)
    win_ref[...] = _dot(vwin, ew.astype(BF16)) / dw

    zc = _dot(kc_ref[...], qm_lo)
    c_end = lax.broadcasted_iota(jnp.int32, (nc, tq), 0) * NSA_CMP_STRIDE + (NSA_CMP_LEN - 1)
    cbias = jnp.where(c_end <= qpos, 0.0, NEG)
    zc = zc + jnp.concatenate([cbias] * 4, axis=1)
    e = jnp.exp2(zc - jnp.max(zc, axis=0, keepdims=True))
    seen = jnp.where(qpos4 >= NSA_CMP_LEN - 1, 1.0, 0.0)
    pc = e * (seen / jnp.maximum(jnp.sum(e, axis=0, keepdims=True), 1.0))
    o_cmp = _dot(kcT_ref[HEAD_DIM:2 * HEAD_DIM, :], pc.astype(BF16))

    psum = pc[:, 0:tq] + pc[:, tq:2 * tq] + pc[:, 2 * tq:3 * tq] + pc[:, 3 * tq:4 * tq]
    p_hi, p_lo = _split2(psum)
    imp = _dot(ov_ref[...], p_hi) + _dot(ov_ref[...], p_lo)
    nid = lax.broadcasted_iota(jnp.int32, (nsel, tq), 0)
    cur = qpos // NSA_SEL_BLOCK
    imp = jnp.where(nid == 0, BIG, imp)
    imp = jnp.where(nid == cur, BIG, imp)
    imp = jnp.where(nid == cur - 1, BIG, imp)
    imp = jnp.where(nid > cur, NEG, imp)
    for _r in range(min(NSA_SEL_TOPK, nsel)):
        mx = jnp.max(imp, axis=0, keepdims=True)
        idx = jnp.min(jnp.where(imp == mx, nid, nsel), axis=0, keepdims=True)
        imp = jnp.where(nid == idx, -jnp.inf, imp)
    sel = jnp.where(nid <= cur, jnp.where(imp == -jnp.inf, 1.0, 0.0), 0.0)
    sel_ref[...] = sel
    for h in range(4):
        qm_ref[h] = qm_lo[:, h * tq:(h + 1) * tq]

    def qk(t, slot):
        kb = k_ref[pl.ds(pl.multiple_of(t * tq, tq), tq), LANE:2 * LANE]
        for h in range(4):
            s_ref[slot, h] = _dot(kb, qm_ref[h])

    kpos_d = i * tq + lax.broadcasted_iota(jnp.int32, (tq, tq), 0)
    qk(i, 0)
    vb_d = _v_aug(_lane_tiles(vT_ref, i * nt, nt, 0, HEAD_DIM))
    srows_d = [sel_ref[pl.ds(i * spb + u, 1), :] for u in range(spb)]
    for h in range(4):
        sc = s_ref[0, h]
        s = jnp.concatenate(
            [jnp.where(srows_d[u] > 0.0, sc[u * NSA_SEL_BLOCK:(u + 1) * NSA_SEL_BLOCK, :], NEG)
             for u in range(spb)], axis=0)
        s = jnp.where(kpos_d <= qpos, s, NEG)
        m = jnp.max(s, axis=0, keepdims=True)
        lanes = slice(h * tq, (h + 1) * tq)
        ml_ref[0:1, lanes] = m
        acc_ref[:, lanes] = _dot(vb_d, jnp.exp2(s - m).astype(BF16))

    last = jnp.maximum(i - 1, 0)

    def update(t, valid, slot):
        vb = _v_aug(_lane_tiles(vT_ref, t * nt, nt, 0, HEAD_DIM))
        srows = [sel_ref[pl.ds(t * spb + u, 1), :] * valid for u in range(spb)]
        for h in range(4):
            lanes = slice(h * tq, (h + 1) * tq)
            sc = s_ref[slot, h]
            subs = [sc[u * NSA_SEL_BLOCK:(u + 1) * NSA_SEL_BLOCK, :] for u in range(spb)]
            tmax = None
            for u in range(spb):
                mu = jnp.where(srows[u] > 0.0, jnp.max(subs[u], axis=0, keepdims=True), NEG)
                tmax = mu if tmax is None else jnp.maximum(tmax, mu)
            m_old = ml_ref[0:1, lanes]
            m_new = jnp.maximum(m_old, tmax)
            p = jnp.concatenate(
                [jnp.exp2(subs[u] - jnp.where(srows[u] > 0.0, m_new, BIGPOS)).astype(BF16) for u in range(spb)],
                axis=0)
            ml_ref[0:1, lanes] = m_new
            acc_ref[:, lanes] = jnp.exp2(m_old - m_new) * acc_ref[:, lanes] + _dot(vb, p)

    qk(0, 0)

    def body(jj, c):
        t0 = 2 * jj
        qk(jnp.minimum(t0 + 1, last), 1)
        update(t0, 1.0, 0)
        qk(jnp.minimum(t0 + 2, last), 0)
        update(jnp.minimum(t0 + 1, last), (t0 + 1 < i).astype(F32), 1)
        return c

    lax.fori_loop(0, (i + 1) // 2, body, 0)
    o_sel = acc_ref[0:HEAD_DIM, :] / acc_ref[HEAD_DIM:HEAD_DIM + 1, :]

    o_win = win_ref[...]

    gates = _lane_tiles(vT_ref, i * nt, nt, 2 * HEAD_DIM, 16).astype(F32)
    outs = []
    for h in range(4):
        sl = slice(h * tq, (h + 1) * tq)
        outs.append(gates[3 * h:3 * h + 1, :] * o_cmp[:, sl]
                    + gates[3 * h + 1:3 * h + 2, :] * o_sel[:, sl]
                    + gates[3 * h + 2:3 * h + 3, :] * o_win[:, sl])
    o_ref[...] = jnp.concatenate(outs, axis=0).T.astype(o_ref.dtype)


def _nsa_call(oT, otok, kc, kcT, ovT, *, q_chunk, kv_chunk, v_chunk):
    B, nlt, _, _ = oT.shape
    S = nlt * LANE
    tq = 256
    nc = S // NSA_CMP_STRIDE
    nsel = S // NSA_SEL_BLOCK
    kern = functools.partial(_nsa_kernel, tq=tq)
    return pl.pallas_call(
        kern,
        grid=(B, S // tq),
        in_specs=_attn_specs(S, tq, q_chunk, kv_chunk, v_chunk) + [
            pl.BlockSpec((None, nc, LANE), lambda b, i: (b, 0, 0)),
            pl.BlockSpec((None, LANE, nc), lambda b, i: (b, 0, 0)),
            pl.BlockSpec((nsel, nc), lambda b, i: (0, 0)),
        ],
        out_specs=pl.BlockSpec((None, tq, BRANCH_W), lambda b, i: (b, i, 0)),
        out_shape=jax.ShapeDtypeStruct((B, S, BRANCH_W), BF16),
        scratch_shapes=[pltpu.VMEM((nsel, tq), F32), pltpu.VMEM((4, LANE, tq), BF16),
                        pltpu.VMEM((HA, 4 * tq), F32), pltpu.VMEM((8, 4 * tq), F32),
                        pltpu.VMEM((2, 4, tq, tq), F32), pltpu.VMEM((HEAD_DIM, 4 * tq), F32)],
        compiler_params=_cparams(("parallel", "arbitrary")),
        name="nsa",
    )(oT, otok, oT, kc, kcT, ovT)


def _epi_kernel(x_ref, ng_ref, ya_ref, yc_ref, yd_ref, ob0_ref, ob1_ref, ob2_ref, l0_ref, l1_ref, l2_ref,
                wz_ref, wmg_ref, wbr_ref, wout_ref, o_ref, nat_ref, *, tm):
    x = x_ref[...]
    ms = jnp.mean(x * x, axis=-1, keepdims=True)
    xn = (x * lax.rsqrt(ms + EPS) * ng_ref[...]).astype(BF16)

    def natural(ref, slot, dil):
        nh = BRANCH_W // LANE
        for r in range(dil):
            for hf in range(nh):
                c0 = r * BRANCH_W + hf * LANE
                nat_ref[slot * nh + hf, pl.ds(r, tm // dil, stride=dil), :] = ref[:, c0:c0 + LANE].astype(F32)
        return jnp.concatenate([nat_ref[slot * nh + hf] for hf in range(nh)], axis=1)

    d1, d2 = DIL_PAIRS[1][1], DIL_PAIRS[2][1]
    l0, l1, l2 = l0_ref[...], natural(l1_ref, 0, d1), natural(l2_ref, 1, d2)
    mx = jnp.maximum(jnp.maximum(l0, l1), l2)
    e0, e1, e2 = jnp.exp2(l0 - mx), jnp.exp2(l1 - mx), jnp.exp2(l2 - mx)
    yb = (e0 * ob0_ref[...].astype(F32) + e1 * natural(ob1_ref, 2, d1)
          + e2 * natural(ob2_ref, 3, d2)) / (e0 + e1 + e2)

    ys = (ya_ref[...].astype(F32), yb, yc_ref[...].astype(F32), yd_ref[...].astype(F32))
    merged = None
    for i in range(4):
        z = _dot(xn, wz_ref[:, i * BRANCH_W:(i + 1) * BRANCH_W])
        gated = (ys[i] * (z * jax.nn.sigmoid(z))).astype(BF16)
        br = _dot(gated, wbr_ref[i])
        mg = _dot(xn, wmg_ref[:, i * D_MODEL:(i + 1) * D_MODEL])
        term = jax.nn.sigmoid(mg) * br
        merged = term if merged is None else merged + term
    o_ref[...] = x + _dot(merged.astype(BF16), wout_ref[...])


def _epi_call(x2, ng, ya, yc, yd, obs, lses, wz, wmg, wbr, wout, *, layer, tm=256):
    T = x2.shape[0]
    row = lambda i: (i, 0)
    full2 = lambda i: (0, 0)
    yspec = pl.BlockSpec((tm, BRANCH_W), row)
    d1, d2 = DIL_PAIRS[1][1], DIL_PAIRS[2][1]
    gspecs = [yspec, pl.BlockSpec((tm // d1, d1 * BRANCH_W), row), pl.BlockSpec((tm // d2, d2 * BRANCH_W), row)]
    return pl.pallas_call(
        functools.partial(_epi_kernel, tm=tm),
        grid=(T // tm,),
        in_specs=[pl.BlockSpec((tm, D_MODEL), row), pl.BlockSpec((1, D_MODEL), full2)]
        + [yspec] * 3 + gspecs + gspecs
        + [pl.BlockSpec((None, D_MODEL, 4 * BRANCH_W), lambda i: (layer, 0, 0)),
           pl.BlockSpec((None, D_MODEL, 4 * D_MODEL), lambda i: (layer, 0, 0)),
           pl.BlockSpec((None, 4, BRANCH_W, D_MODEL), lambda i: (layer, 0, 0, 0)),
           pl.BlockSpec((None, D_MODEL, D_MODEL), lambda i: (layer, 0, 0))],
        out_specs=pl.BlockSpec((tm, D_MODEL), row),
        out_shape=jax.ShapeDtypeStruct((T, D_MODEL), F32),
        scratch_shapes=[pltpu.VMEM((4 * (BRANCH_W // LANE), tm, LANE), F32)],
        compiler_params=_cparams(("parallel",)),
        name="epilogue",
    )(x2, ng, ya, yc, yd, *obs, *lses, wz, wmg, wbr, wout)


TOK_AK, TOK_BQ, TOK_BK, TOK_BV, TOK_CK, TOK_DKV = range(6)
N_MAIN = 6
T_AQ, T_AV, T_CQ, T_CV, T_DQ, T_DX = range(6)
SCALE = 1.0 / math.sqrt(HEAD_DIM)
QSCALE2 = SCALE * LOG2E

_PLAIN = (False, False, 1.0, False)
_T_KINDS = (
    ((False, False, QSCALE2, False),) * 4,
    (_PLAIN,) * 4,
    ((True, True, QSCALE2, False),) * 4,
    (_PLAIN,) * 4,
    ((True, True, QSCALE2, False),) * 4,
    (_PLAIN, _PLAIN, (False, False, 1.0, True), _PLAIN),
)


def _tok_specs():
    main = lambda j: (0, 1, j * CH, CH, 0)
    qkv = ((True, True), (True, True), (False, False))
    specs = []
    for j in range(N_MAIN):
        if j in (TOK_BQ, TOK_BK, TOK_CK):
            specs.append((True, True, (main(j),)))
        elif j == TOK_DKV:
            specs.append(("mixed", "mixed", (main(j), (3, NSA_CMP_STRIDE, 0, LANE, LANE))))
        else:
            specs.append((False, False, (main(j),)))
    for gi in (1, 2):
        dil = DIL_PAIRS[gi][1]
        for k, (nrm, rope) in enumerate(qkv):
            specs.append((nrm, rope, ((gi, dil, k * CH, CH, 3 * CH),)))
    return tuple(specs)


def _rope_tables(S, dil):
    L = S // dil
    pos = (jnp.arange(dil, dtype=jnp.int32)[:, None] + dil * jnp.arange(L, dtype=jnp.int32)[None, :]).astype(F32)
    inv = ROPE_THETA ** (-jnp.arange(0, ROT_DIM, 2, dtype=F32) / ROT_DIM)
    ang = pos[:, :, None] * inv[None, None, :]
    cos, sin = jnp.cos(ang), jnp.sin(ang)
    one = jnp.ones((dil, L, HEAD_DIM - ROT_DIM), F32)
    zero8 = jnp.zeros((dil, L, 8), F32)
    zero = jnp.zeros_like(one)
    c_head = jnp.concatenate([cos, cos, one], axis=-1)
    s1_head = jnp.concatenate([zero8, sin, zero], axis=-1)
    s2_head = jnp.concatenate([-sin, zero8, zero], axis=-1)
    rtok = jnp.stack([jnp.tile(t, (1, 1, LANE // HEAD_DIM)) for t in (c_head, s1_head, s2_head)], axis=1)
    rT = jnp.stack([cos.transpose(0, 2, 1), sin.transpose(0, 2, 1)], axis=1)
    return rtok, rT


def _tok_params(rows):
    out = []
    for nf, rf, gains in rows:
        nrow = jnp.concatenate([jnp.full((HEAD_DIM,), float(f), F32) for f in nf])
        rrow = jnp.concatenate([jnp.full((HEAD_DIM,), float(f), F32) for f in rf])
        grow = jnp.concatenate([g.astype(F32) for g in gains])
        out.append(jnp.concatenate([jnp.stack([nrow, rrow, grow]), jnp.zeros((5, CH), F32)], axis=0))
    return jnp.stack(out)


def kernel(x, norm_g, w_in, qk_g, cmp_pe, cmp_w1, cmp_w2, w_branch, w_out):
    B, S, _ = x.shape
    T = B * S
    o = COL_OFF
    ones = jnp.ones((HEAD_DIM,), F32)

    r = np.arange(CH)
    G = jnp.asarray((r[:, None] // HEAD_DIM == r[None, :] // HEAD_DIM).astype(np.float32), BF16)
    G128 = G[:LANE, :LANE]
    t = np.arange(256)
    U = jnp.asarray((t[None, :] > t[:, None]).astype(np.float32), BF16)
    nblk = S // MOBA_BLOCK
    A = jnp.asarray(np.repeat(np.eye(nblk, dtype=np.float32), MOBA_BLOCK, axis=1) / MOBA_BLOCK, BF16)
    nc, nsel = S // NSA_CMP_STRIDE, S // NSA_SEL_BLOCK
    cs = np.arange(nc) * NSA_CMP_STRIDE
    ss = np.arange(nsel) * NSA_SEL_BLOCK
    ov = ((cs[None, :] < ss[:, None] + NSA_SEL_BLOCK) & (cs[None, :] + NSA_CMP_LEN > ss[:, None]))
    ov[:, nc - 1] = False
    ovT = jnp.asarray(ov.astype(np.float32), BF16)
    hid = np.arange(LANE)[:, None]
    ind = jnp.asarray(((hid < 4) & (r[None, :] // HEAD_DIM == hid)).astype(np.float32), BF16)
    rtok4, rT4 = _rope_tables(S, 1)
    rtok, rT = rtok4[0], rT4[0]
    tok_specs = _tok_specs()

    dkv = o[13]
    bq = lambda gi: (o[4] + gi * CH, o[4] + (gi + 1) * CH)
    bk = lambda gi: (o[5] + gi * CH, o[5] + (gi + 1) * CH)
    bv = lambda gi: (o[6] + gi * CH, o[6] + (gi + 1) * CH)
    tok_plan = ((o[1], o[2]), bq(0), bk(0), bv(0), (o[9], o[10]),
                (dkv, dkv + 64), (dkv + 64, dkv + 128), (dkv + 128, dkv + 192), (dkv + 256, dkv + 320),
                bq(1), bk(1), bv(1), bq(2), bk(2), bv(2))
    T_plan = ((o[0], o[1]), (o[2], o[3]), (o[8], o[9]), (o[10], o[11]), (o[12], o[13]),
              (dkv + 192, dkv + 256), (dkv + 320, dkv + 384), (o[14], o[15]))
    z_plan = ((o[3], o[4]), (o[7], o[8]), (o[11], o[12]), (o[15], o[16]))
    wtok_all, wT_all, wz_all, wmg_all = _repack_call(
        w_in, (tok_plan, T_plan, z_plan, ((o[16], o[17]),)),
        (len(tok_specs) * CH, len(_T_KINDS) * CH, 4 * BRANCH_W, 4 * D_MODEL))

    wbr_all = w_branch.astype(BF16)
    wout_all = w_out.astype(BF16)

    for l in range(DEPTH):
        g = qk_g[l]
        plain = ((0,) * 4, (0,) * 4, (ones,) * 4)
        bq_p = ((1,) * 4, (1,) * 4, (g[0] * QSCALE2,) * 4)
        bk_p = ((1,) * 4, (1,) * 4, (g[1],) * 4)
        tokp = _tok_params([
            plain, bq_p, bk_p, plain,
            ((1,) * 4, (1,) * 4, (g[3],) * 4),
            ((0, 0, 1, 1), (1, 0, 1, 1), (ones, ones, g[6], g[7])),
            bq_p, bk_p, plain, bq_p, bk_p, plain])
        gT = jnp.concatenate([jnp.tile(ones, 8), jnp.tile(g[2], 4), jnp.tile(ones, 4), jnp.tile(g[4], 4),
                              jnp.tile(ones, 4)])[:, None]
        ng = norm_g[l][None, :]
        otok, og1, og2, okv, oT = _proj_call(x, ng, wtok_all, wT_all, tokp, gT, rtok, rT, G, layer=l,
                                             tok_specs=tok_specs, T_kinds=_T_KINDS, n_main=N_MAIN)

        ya = _sb_call(oT, otok, U, q_chunk=T_AQ, k_chunk=TOK_AK, v_chunk=T_AV)

        obs, lses = [], []
        for gi, (window, dil) in enumerate(DIL_PAIRS):
            if dil == 1:
                ob, lse = _band_call(otok, ind, dil=1, per_res=N_MAIN, qi=TOK_BQ, ki=TOK_BK, vi=TOK_BV,
                                     max_dist=window)
            else:
                ob, lse = _band_call((og1, og2)[gi - 1], ind, dil=dil, per_res=3, qi=0, ki=1, vi=2,
                                     max_dist=window // dil)
            obs.append(ob.reshape(T // dil, dil * BRANCH_W))
            lses.append(lse.reshape(T // dil, dil * BRANCH_W))

        yc = _moba_call(oT, otok, A, q_chunk=T_CQ, k_chunk=TOK_CK, v_chunk=T_CV)

        w1 = cmp_w1[l].reshape(2, NSA_CMP_LEN, HEAD_DIM, HEAD_DIM)
        z64 = jnp.zeros((NSA_CMP_STRIDE, HEAD_DIM, HEAD_DIM), F32)
        w1blk = jnp.concatenate([
            jnp.concatenate([w1[0, :16], w1[0, 16:], z64, z64], axis=2),
            jnp.concatenate([z64, z64, w1[1, :16], w1[1, 16:]], axis=2)], axis=1)
        pe2 = jnp.concatenate([cmp_pe[l].reshape(1, -1), jnp.zeros((7, 2 * NSA_CMP_LEN * HEAD_DIM), F32)], axis=0)
        zf = jnp.zeros((NSA_CMP_LEN * HEAD_DIM, HEAD_DIM), F32)
        w1f = jnp.concatenate([jnp.concatenate([cmp_w1[l, 0], zf], axis=1),
                               jnp.concatenate([zf, cmp_w1[l, 1]], axis=1)], axis=0)
        z2 = jnp.zeros((HEAD_DIM, HEAD_DIM), F32)
        w2blk = jnp.concatenate([jnp.concatenate([cmp_w2[l, 0], z2], axis=1),
                                 jnp.concatenate([z2, cmp_w2[l, 1]], axis=1)], axis=0)
        gk = jnp.concatenate([g[5], ones])[None, :]
        kc, kcT = _compress_call(okv, w1blk, pe2, w1f, w2blk, gk, G128)
        yd = _nsa_call(oT, otok, kc, kcT, ovT, q_chunk=T_DQ, kv_chunk=TOK_DKV, v_chunk=T_DX)

        x2 = _epi_call(x.reshape(T, D_MODEL), ng, ya.reshape(T, BRANCH_W), yc.reshape(T, BRANCH_W),
                       yd.reshape(T, BRANCH_W), obs, lses, wz_all, wmg_all, wbr_all, wout_all, layer=l)
        x = x2.reshape(B, S, D_MODEL)
    return x
```

```python
import functools
import math

import numpy as np
import jax
import jax.numpy as jnp
from jax import lax
from jax.experimental import pallas as pl
from jax.experimental.pallas import tpu as pltpu

F32 = jnp.float32
BF16 = jnp.bfloat16

D_MODEL = 1024
DEPTH = 4
HEAD_DIM = 64
ROT_DIM = 16
ROPE_THETA = 500000.0
EPS = 1e-6
NEG = -1e30
BIG = 1e9
BRANCH_W = 256
DIL_PAIRS = ((128, 1), (512, 4), (2048, 16))
MOBA_BLOCK = 256
MOBA_TOPK = 3
NSA_CMP_LEN = 32
NSA_CMP_STRIDE = 16
NSA_SEL_BLOCK = 64
NSA_SEL_TOPK = 16
NSA_WINDOW = 512
COL_SIZES = (256, 256, 256, 256, 768, 768, 768, 256, 256, 256, 256, 256, 256, 384, 12, 256, 4096)
COL_OFF = tuple(int(v) for v in np.concatenate([[0], np.cumsum(COL_SIZES)]))

LANE = 128
CH = 256
VMEM_LIMIT = 56 * 1024 * 1024


def _cparams(sem):
    return pltpu.CompilerParams(dimension_semantics=sem, vmem_limit_bytes=VMEM_LIMIT)


def _dot(a, b):
    return jnp.dot(a, b, preferred_element_type=F32)


def _dot_nt(a, b):
    return lax.dot_general(a, b, (((1,), (1,)), ((), ())), preferred_element_type=F32)


def _split2(x):
    hi = x.astype(BF16)
    lo = (x - hi.astype(F32)).astype(BF16)
    return hi, lo


def _split3(x):
    hi = x.astype(BF16)
    r = x - hi.astype(F32)
    mid = r.astype(BF16)
    lo = (r - mid.astype(F32)).astype(BF16)
    return hi, mid, lo


def _repack_kernel(w_ref, *out_refs, plans):
    for o_ref, ranges in zip(out_refs, plans):
        width = o_ref.shape[-1]
        parts = [w_ref[:, a:b] for a, b in ranges]
        used = sum(b - a for a, b in ranges)
        if used < width:
            parts.append(jnp.zeros((w_ref.shape[0], width - used), F32))
        o_ref[...] = (parts[0] if len(parts) == 1 else jnp.concatenate(parts, axis=1)).astype(o_ref.dtype)


def _repack_call(w_in, plans, widths, rows=128):
    depth, d, n_in = w_in.shape
    kern = functools.partial(_repack_kernel, plans=plans)
    return pl.pallas_call(
        kern,
        grid=(depth, d // rows),
        in_specs=[pl.BlockSpec((None, rows, n_in), lambda l, i: (l, i, 0))],
        out_specs=[pl.BlockSpec((None, rows, w), lambda l, i: (l, i, 0)) for w in widths],
        out_shape=[jax.ShapeDtypeStruct((depth, d, w), BF16) for w in widths],
        compiler_params=_cparams(("parallel", "parallel")),
        name="repack",
    )(w_in)


def _proj_kernel(x_ref, ng_ref, wtok_ref, wT_ref, tokp_ref, gT_ref, rtok_ref, rT_ref, G_ref,
                 omain_ref, og1_ref, og2_ref, okv_ref, oT_ref, stage_ref, *, tok_specs, T_kinds, tm):
    x = x_ref[...]
    ms = jnp.mean(x * x, axis=-1, keepdims=True)
    xn = (x * lax.rsqrt(ms + EPS) * ng_ref[...]).astype(BF16)

    dsts = (omain_ref, og1_ref, og2_ref, okv_ref)
    for c, (has_norm, has_rope, outs) in enumerate(tok_specs):
        y = _dot(xn, wtok_ref[:, c * CH:(c + 1) * CH])
        prm = tokp_ref[c]
        if has_norm:
            ss = _dot((y * y).astype(BF16), G_ref[...])
            inv = lax.rsqrt(ss * (1.0 / HEAD_DIM) + EPS)
            y = y * (jnp.where(prm[0:1, :] > 0.0, inv, 1.0) if has_norm == "mixed" else inv)
        y = y * prm[2:3, :]
        if has_rope:
            halves = []
            for hf in range(CH // LANE):
                yh = y[:, hf * LANE:(hf + 1) * LANE]
                cc, s1, s2 = rtok_ref[0], rtok_ref[1], rtok_ref[2]
                if has_rope == "mixed":
                    rf = prm[1:2, hf * LANE:(hf + 1) * LANE]
                    cc, s1, s2 = jnp.where(rf > 0.0, cc, 1.0), s1 * rf, s2 * rf
                halves.append(yh * cc + pltpu.roll(yh, 8, 1) * s1 + pltpu.roll(yh, LANE - 8, 1) * s2)
            y = jnp.concatenate(halves, axis=1)
        staged = False
        for dst, dil, col, lanes, rstride in outs:
            if dil == 1:
                dsts[dst][:, col:col + lanes] = y[:, :lanes].astype(BF16)
                continue
            if not staged:
                for hf in range(CH // LANE):
                    stage_ref[hf] = y[:, hf * LANE:(hf + 1) * LANE]
                staged = True
            for r in range(dil):
                for hf in range(lanes // LANE):
                    blk = stage_ref[hf, pl.ds(r, tm // dil, stride=dil), :]
                    c0 = r * rstride + col + hf * LANE
                    dsts[dst][:, c0:c0 + LANE] = blk.astype(BF16)

    cosT = rT_ref[0]
    sinT = rT_ref[1]
    for c, heads in enumerate(T_kinds):
        y = _dot(xn, wT_ref[:, c * CH:(c + 1) * CH]).T
        for h, (nrm, rope, scale, sigm) in enumerate(heads):
            r0 = c * CH + h * HEAD_DIM
            yh = y[h * HEAD_DIM:(h + 1) * HEAD_DIM, :]
            if nrm:
                msq = jnp.mean(yh * yh, axis=0, keepdims=True)
                yh = yh * lax.rsqrt(msq + EPS) * gT_ref[r0:r0 + HEAD_DIM, :]
            if rope:
                x1 = yh[0:8, :]
                x2 = yh[8:16, :]
                yh = jnp.concatenate([x1 * cosT - x2 * sinT, x2 * cosT + x1 * sinT, yh[16:, :]], axis=0)
            if scale != 1.0:
                yh = yh * scale
            if sigm:
                yh = jax.nn.sigmoid(yh)
            yb = yh.astype(BF16)
            for t in range(tm // LANE):
                oT_ref[t, r0:r0 + HEAD_DIM, :] = yb[:, t * LANE:(t + 1) * LANE]


def _proj_call(x, ng, wtok, wT, tokp, gT, rtok, rT, G, *, layer, tok_specs, T_kinds, n_main, tm=512):
    B, S, _ = x.shape
    ntok, nT = len(tok_specs), len(T_kinds)
    d1, d2 = DIL_PAIRS[1][1], DIL_PAIRS[2][1]
    kern = functools.partial(_proj_kernel, tok_specs=tok_specs, T_kinds=T_kinds, tm=tm)
    const2 = lambda b, n: (0, 0)
    return pl.pallas_call(
        kern,
        grid=(B, S // tm),
        in_specs=[
            pl.BlockSpec((None, tm, D_MODEL), lambda b, n: (b, n, 0)),
            pl.BlockSpec((1, D_MODEL), const2),
            pl.BlockSpec((None, D_MODEL, ntok * CH), lambda b, n: (layer, 0, 0)),
            pl.BlockSpec((None, D_MODEL, nT * CH), lambda b, n: (layer, 0, 0)),
            pl.BlockSpec((ntok, 8, CH), lambda b, n: (0, 0, 0)),
            pl.BlockSpec((nT * CH, 1), const2),
            pl.BlockSpec((3, tm, LANE), lambda b, n: (0, n, 0)),
            pl.BlockSpec((2, 8, tm), lambda b, n: (0, 0, n)),
            pl.BlockSpec((CH, CH), const2),
        ],
        out_specs=[
            pl.BlockSpec((None, tm, n_main * CH), lambda b, n: (b, n, 0)),
            pl.BlockSpec((None, tm // d1, d1 * 3 * CH), lambda b, n: (b, n, 0)),
            pl.BlockSpec((None, tm // d2, d2 * 3 * CH), lambda b, n: (b, n, 0)),
            pl.BlockSpec((None, tm // NSA_CMP_STRIDE, NSA_CMP_STRIDE * LANE), lambda b, n: (b, n, 0)),
            pl.BlockSpec((None, tm // LANE, nT * CH, LANE), lambda b, n: (b, n, 0, 0)),
        ],
        out_shape=[
            jax.ShapeDtypeStruct((B, S, n_main * CH), BF16),
            jax.ShapeDtypeStruct((B, S // d1, d1 * 3 * CH), BF16),
            jax.ShapeDtypeStruct((B, S // d2, d2 * 3 * CH), BF16),
            jax.ShapeDtypeStruct((B, S // NSA_CMP_STRIDE, NSA_CMP_STRIDE * LANE), BF16),
            jax.ShapeDtypeStruct((B, S // LANE, nT * CH, LANE), BF16),
        ],
        scratch_shapes=[pltpu.VMEM((CH // LANE, tm, LANE), F32)],
        compiler_params=_cparams(("parallel", "parallel")),
        name="proj",
    )(x, ng, wtok, wT, tokp, gT, rtok, rT, G)


def _pair_masked_q(q_pair, h):
    rid = lax.broadcasted_iota(jnp.int32, q_pair.shape, 0)
    lo = (h % 2) * HEAD_DIM
    keep = jnp.where(rid >= lo, jnp.where(rid < lo + HEAD_DIM, 1.0, 0.0), 0.0).astype(BF16)
    return q_pair * keep


def _lane_tiles(ref, t0, nt, r0, nr):
    return jnp.concatenate([ref[t0 + t, r0:r0 + nr, :] for t in range(nt)], axis=1)


BIGPOS = 1e30
ONES_ROWS = 16
HA = HEAD_DIM + ONES_ROWS
LOG2E = 1.4426950408889634


def _v_aug(vb):
    return jnp.concatenate([vb, jnp.ones((ONES_ROWS, vb.shape[1]), BF16)], axis=0)


def _online_cols(s, colsel, m_old, acc_old, vb_aug):
    tmax = jnp.max(s, axis=0, keepdims=True)
    m_new = jnp.where(colsel > 0.0, jnp.maximum(m_old, tmax), m_old)
    m_use = jnp.where(colsel > 0.0, m_new, BIGPOS)
    p = jnp.exp2(s - m_use).astype(BF16)
    acc = jnp.exp2(m_old - m_new) * acc_old + _dot(vb_aug, p)
    return m_new, acc


SB_LOG_CUTOFF = -144.0


def _sb_kernel(qT_ref, k_ref, vT_ref, U_ref, o_ref, qm_ref, acc_ref, carry_ref, *, tq):
    i = pl.program_id(1)
    nt = tq // LANE
    row = lax.broadcasted_iota(jnp.int32, (tq, tq), 0)
    col = lax.broadcasted_iota(jnp.int32, (tq, tq), 1)
    past = row < col
    for h in range(4):
        p = h // 2
        qm_ref[h] = _pair_masked_q(_lane_tiles(qT_ref, 0, nt, p * LANE, LANE), h)
    acc_ref[...] = jnp.zeros_like(acc_ref)
    carry_ref[...] = jnp.zeros_like(carry_ref)

    def tile(j, masked):
        scores, logsig, laters = [], [], []
        for h in range(4):
            p = h // 2
            kb = k_ref[pl.ds(pl.multiple_of(j * tq, tq), tq), p * LANE:(p + 1) * LANE]
            scores.append(_dot(kb, qm_ref[h]))
        worst = None
        for h in range(4):
            s = scores[h]
            sp = jnp.maximum(s, 0.0) + jnp.log2(1.0 + jnp.exp2(-jnp.abs(s)))
            lg = -sp
            if masked:
                lg = jnp.where(past, lg, 0.0)
            hi, lo = _split2(lg)
            carry = carry_ref[h:h + 1, :]
            laters.append(_dot(U_ref[...], hi) + _dot(U_ref[...], lo) + carry)
            logsig.append(s - sp)
            carry = carry + jnp.sum(lg, axis=0, keepdims=True)
            carry_ref[h:h + 1, :] = carry
            worst = carry if worst is None else jnp.maximum(worst, carry)
        for h in range(4):
            w = jnp.exp2(logsig[h] + laters[h])
            if masked:
                w = jnp.where(past, w, 0.0)
            vb = _lane_tiles(vT_ref, j * nt, nt, h * HEAD_DIM, HEAD_DIM)
            acc_ref[h * HEAD_DIM:(h + 1) * HEAD_DIM, :] += _dot(vb, w.astype(BF16))
        return jnp.max(worst)

    worst0 = tile(i, True)

    def cond(st):
        return jnp.logical_and(st[0] >= 0, st[1] > SB_LOG_CUTOFF)

    def body(st):
        return st[0] - 1, tile(st[0], False)

    lax.while_loop(cond, body, (i - 1, worst0))
    o_ref[...] = acc_ref[...].T.astype(o_ref.dtype)


def _attn_specs(S, tq, q_chunk, k_chunk, v_chunk):
    return [
        pl.BlockSpec((None, tq // LANE, CH, LANE), lambda b, i: (b, i, q_chunk, 0)),
        pl.BlockSpec((None, S, CH), lambda b, i: (b, 0, k_chunk)),
        pl.BlockSpec((None, S // LANE, CH, LANE), lambda b, i: (b, 0, v_chunk, 0)),
    ]


def _sb_call(oT, otok, U, *, q_chunk, k_chunk, v_chunk, tq=256):
    B, nlt, _, _ = oT.shape
    S = nlt * LANE
    kern = functools.partial(_sb_kernel, tq=tq)
    return pl.pallas_call(
        kern,
        grid=(B, S // tq),
        in_specs=_attn_specs(S, tq, q_chunk, k_chunk, v_chunk) + [pl.BlockSpec((tq, tq), lambda b, i: (0, 0))],
        out_specs=pl.BlockSpec((None, tq, BRANCH_W), lambda b, i: (b, i, 0)),
        out_shape=jax.ShapeDtypeStruct((B, S, BRANCH_W), BF16),
        scratch_shapes=[pltpu.VMEM((4, LANE, tq), BF16), pltpu.VMEM((CH, tq), F32), pltpu.VMEM((8, tq), F32)],
        compiler_params=_cparams(("parallel", "arbitrary")),
        name="stick_breaking",
    )(oT, otok, oT, U)


def _head_lanes(x, lane, h):
    lo = h * HEAD_DIM
    return jnp.where(lane >= lo, jnp.where(lane < lo + HEAD_DIM, x, jnp.zeros_like(x)), jnp.zeros_like(x))


def _band_kernel(q_ref, k_ref, v_ref, ind_ref, o_ref, lse_ref, *, tqb, max_dist):
    n = pl.program_id(2)
    n_prev = -(-max_dist // LANE)
    nkr = (n_prev + 1) * LANE
    nsub = tqb // LANE
    row = lax.broadcasted_iota(jnp.int32, (nkr, LANE), 0)
    col = lax.broadcasted_iota(jnp.int32, (nkr, LANE), 1)
    lane_q = lax.broadcasted_iota(jnp.int32, (LANE, LANE), 1)
    lane_v = lax.broadcasted_iota(jnp.int32, (nkr, CH), 1)
    for u in range(nsub):
        qt = n * nsub + u
        kt0 = jnp.maximum(qt - n_prev, 0)
        dist = (qt - kt0) * LANE + col - row
        bias = jnp.where(dist >= 0, jnp.where(dist <= max_dist, 0.0, NEG), NEG)
        k0 = pl.multiple_of(kt0 * LANE, LANE)
        kwin = k_ref[pl.ds(k0, nkr), :]
        vwin = v_ref[pl.ds(k0, nkr), :]
        qu = q_ref[u * LANE:(u + 1) * LANE, :]
        o_acc = None
        lses = []
        for h in range(4):
            p = h // 2
            qm = _head_lanes(qu[:, p * LANE:(p + 1) * LANE], lane_q, h % 2)
            s = _dot_nt(kwin[:, p * LANE:(p + 1) * LANE], qm) + bias
            m = jnp.max(s, axis=0, keepdims=True)
            e = jnp.exp2(s - m)
            den = jnp.sum(e, axis=0, keepdims=True)
            pn = e * (1.0 / den)
            contrib = _dot(pn.T.astype(BF16), _head_lanes(vwin, lane_v, h))
            o_acc = contrib if o_acc is None else o_acc + contrib
            lses.append(m + jnp.log(den) * LOG2E)
        o_ref[u * LANE:(u + 1) * LANE, :] = o_acc.astype(o_ref.dtype)
        ls = jnp.concatenate(lses + [jnp.zeros((LANE - 4, LANE), F32)], axis=0).T
        l1, l2, l3 = _split3(ls)
        lse_ref[u * LANE:(u + 1) * LANE, :] = (_dot(l1, ind_ref[...]) + _dot(l2, ind_ref[...])
                                               + _dot(l3, ind_ref[...]))


def _band_call(arr, ind, *, dil, per_res, qi, ki, vi, max_dist):
    B, L, _ = arr.shape
    tqb = min(1024, L)
    kern = functools.partial(_band_kernel, tqb=tqb, max_dist=max_dist)
    return pl.pallas_call(
        kern,
        grid=(B, dil, L // tqb),
        in_specs=[
            pl.BlockSpec((None, tqb, CH), lambda b, r, n: (b, n, r * per_res + qi)),
            pl.BlockSpec((None, L, CH), lambda b, r, n: (b, 0, r * per_res + ki)),
            pl.BlockSpec((None, L, CH), lambda b, r, n: (b, 0, r * per_res + vi)),
            pl.BlockSpec((LANE, CH), lambda b, r, n: (0, 0)),
        ],
        out_specs=[
            pl.BlockSpec((None, tqb, BRANCH_W), lambda b, r, n: (b, n, r)),
            pl.BlockSpec((None, tqb, BRANCH_W), lambda b, r, n: (b, n, r)),
        ],
        out_shape=[
            jax.ShapeDtypeStruct((B, L, dil * BRANCH_W), BF16),
            jax.ShapeDtypeStruct((B, L, dil * BRANCH_W), F32),
        ],
        compiler_params=_cparams(("parallel", "parallel", "parallel")),
        name="banded",
    )(arr, arr, arr, ind)


def _moba_kernel(qT_ref, k_ref, vT_ref, A_ref, o_ref, kmean_ref, sel_ref, qm_ref, acc_ref, ml_ref, s_ref, *, tq):
    i = pl.program_id(1)
    nt = tq // LANE
    nblk = A_ref.shape[0]

    @pl.when(i == 0)
    def _():
        kmean_ref[...] = _dot(A_ref[...], k_ref[...])

    blk = lax.broadcasted_iota(jnp.int32, (nblk, tq), 0)
    row = lax.broadcasted_iota(jnp.int32, (tq, tq), 0)
    col = lax.broadcasted_iota(jnp.int32, (tq, tq), 1)
    gates, diag = [], []
    for h in range(4):
        p = h // 2
        qm = _pair_masked_q(_lane_tiles(qT_ref, 0, nt, p * LANE, LANE), h)
        qm_ref[h] = qm
        k1, k2, k3 = _split3(kmean_ref[:, p * LANE:(p + 1) * LANE])
        gates.append(_dot(k1, qm) + _dot(k2, qm) + _dot(k3, qm))
        kb = k_ref[pl.ds(pl.multiple_of(i * tq, tq), tq), p * LANE:(p + 1) * LANE]
        diag.append(_dot(kb, qm))
    for h in range(4):
        g = jnp.where(blk < i, gates[h], NEG)
        sel = jnp.zeros((nblk, tq), F32)
        for _r in range(MOBA_TOPK):
            mx = jnp.max(g, axis=0, keepdims=True)
            idx = jnp.min(jnp.where(g == mx, blk, nblk), axis=0, keepdims=True)
            hit = blk == idx
            sel = jnp.where(hit, 1.0, sel)
            g = jnp.where(hit, -jnp.inf, g)
        sel_ref[h] = jnp.where(blk < i, sel, 0.0)

        s = jnp.where(row <= col, diag[h], NEG)
        m = jnp.max(s, axis=0, keepdims=True)
        ml_ref[h:h + 1, :] = m
        acc_ref[h * HA:(h + 1) * HA, :] = _dot(
            _v_aug(_lane_tiles(vT_ref, i * nt, nt, h * HEAD_DIM, HEAD_DIM)), jnp.exp2(s - m).astype(BF16))

    last = jnp.maximum(i - 1, 0)

    def qk(t, slot):
        for h in range(4):
            p = h // 2
            kbj = k_ref[pl.ds(pl.multiple_of(t * tq, tq), tq), p * LANE:(p + 1) * LANE]
            s_ref[slot, h] = _dot(kbj, qm_ref[h])

    def update(t, valid, slot):
        for h in range(4):
            srow = sel_ref[h, pl.ds(t, 1), :] * valid
            vb = _v_aug(_lane_tiles(vT_ref, t * nt, nt, h * HEAD_DIM, HEAD_DIM))
            rows = slice(h * HA, (h + 1) * HA)
            m, acc = _online_cols(s_ref[slot, h], srow, ml_ref[h:h + 1, :], acc_ref[rows, :], vb)
            ml_ref[h:h + 1, :] = m
            acc_ref[rows, :] = acc

    qk(0, 0)

    def body(jj, c):
        t0 = 2 * jj
        qk(jnp.minimum(t0 + 1, last), 1)
        update(t0, 1.0, 0)
        qk(jnp.minimum(t0 + 2, last), 0)
        update(jnp.minimum(t0 + 1, last), (t0 + 1 < i).astype(F32), 1)
        return c

    lax.fori_loop(0, (i + 1) // 2, body, 0)
    outs = [acc_ref[h * HA:h * HA + HEAD_DIM, :] / acc_ref[h * HA + HEAD_DIM:h * HA + HEAD_DIM + 1, :]
            for h in range(4)]
    o_ref[...] = jnp.concatenate(outs, axis=0).T.astype(o_ref.dtype)


def _moba_call(oT, otok, A, *, q_chunk, k_chunk, v_chunk):
    B, nlt, _, _ = oT.shape
    S = nlt * LANE
    tq = MOBA_BLOCK
    nblk = S // MOBA_BLOCK
    kern = functools.partial(_moba_kernel, tq=tq)
    return pl.pallas_call(
        kern,
        grid=(B, S // tq),
        in_specs=_attn_specs(S, tq, q_chunk, k_chunk, v_chunk) + [pl.BlockSpec((nblk, S), lambda b, i: (0, 0))],
        out_specs=pl.BlockSpec((None, tq, BRANCH_W), lambda b, i: (b, i, 0)),
        out_shape=jax.ShapeDtypeStruct((B, S, BRANCH_W), BF16),
        scratch_shapes=[pltpu.VMEM((nblk, CH), F32), pltpu.VMEM((4, nblk, tq), F32),
                        pltpu.VMEM((4, LANE, tq), BF16), pltpu.VMEM((4 * HA, tq), F32), pltpu.VMEM((8, tq), F32),
                        pltpu.VMEM((2, 4, tq, tq), F32)],
        compiler_params=_cparams(("parallel", "arbitrary")),
        name="moba",
    )(oT, otok, oT, A)


def _compress_kernel(x_ref, w1_ref, pe_ref, w1f_ref, w2_ref, gk_ref, G_ref, o_ref, oT_ref, acc_ref):
    l = pl.program_id(1)

    @pl.when(l == 0)
    def _():
        acc_ref[...] = jnp.zeros_like(acc_ref)

    x = x_ref[...]
    w_hi, w_lo = _split2(w1_ref[...])
    acc_ref[...] += _dot(x, w_hi) + _dot(x, w_lo)

    @pl.when(l == pl.num_programs(1) - 1)
    def _():
        nc = acc_ref.shape[0]
        r = acc_ref[...]
        bias = jnp.dot(pe_ref[...], w1f_ref[...], preferred_element_type=F32,
                       precision=lax.Precision.HIGHEST)
        p1 = jnp.concatenate([r[:, 0:64], r[:, 128:192]], axis=1)
        p2 = jnp.concatenate([r[:, 64:128], r[:, 192:256]], axis=1)
        hid = p1 + pltpu.roll(p2, nc - 1, 0) + bias[0:1, :]
        hid = hid * jax.nn.sigmoid(hid)
        comp = jnp.dot(hid, w2_ref[...], preferred_element_type=F32, precision=lax.Precision.HIGHEST)
        ss = _dot((comp * comp).astype(BF16), G_ref[...])
        lane = lax.broadcasted_iota(jnp.int32, comp.shape, 1)
        inv = jnp.where(lane < HEAD_DIM, lax.rsqrt(ss * (1.0 / HEAD_DIM) + EPS), 1.0)
        comp = comp * inv * gk_ref[...]
        rowi = lax.broadcasted_iota(jnp.int32, comp.shape, 0)
        comp = jnp.where(rowi < nc - 1, comp, 0.0)
        o_ref[...] = comp.astype(o_ref.dtype)
        oT_ref[...] = comp.T.astype(oT_ref.dtype)


def _compress_call(xv, w1blk, pe2, w1f, w2blk, gk, G128):
    B, nc, _ = xv.shape
    return pl.pallas_call(
        _compress_kernel,
        grid=(B, NSA_CMP_STRIDE),
        in_specs=[
            pl.BlockSpec((None, nc, LANE), lambda b, l: (b, 0, l)),
            pl.BlockSpec((None, LANE, CH), lambda b, l: (l, 0, 0)),
            pl.BlockSpec((8, 2 * NSA_CMP_LEN * HEAD_DIM), lambda b, l: (0, 0)),
            pl.BlockSpec((2 * NSA_CMP_LEN * HEAD_DIM, LANE), lambda b, l: (0, 0)),
            pl.BlockSpec((LANE, LANE), lambda b, l: (0, 0)),
            pl.BlockSpec((1, LANE), lambda b, l: (0, 0)),
            pl.BlockSpec((LANE, LANE), lambda b, l: (0, 0)),
        ],
        out_specs=[
            pl.BlockSpec((None, nc, LANE), lambda b, l: (b, 0, 0)),
            pl.BlockSpec((None, LANE, nc), lambda b, l: (b, 0, 0)),
        ],
        out_shape=[
            jax.ShapeDtypeStruct((B, nc, LANE), BF16),
            jax.ShapeDtypeStruct((B, LANE, nc), BF16),
        ],
        scratch_shapes=[pltpu.VMEM((nc, CH), F32)],
        compiler_params=_cparams(("parallel", "arbitrary")),
        name="nsa_compress",
    )(xv, w1blk, pe2, w1f, w2blk, gk, G128)


def _nsa_kernel(qT_ref, k_ref, vT_ref, kc_ref, kcT_ref, ov_ref, o_ref, sel_ref, qm_ref, acc_ref, ml_ref, s_ref,
                win_ref, *, tq):
    i = pl.program_id(1)
    nt = tq // LANE
    nq = 4 * tq
    nc = kc_ref.shape[0]
    nsel = ov_ref.shape[0]
    spb = tq // NSA_SEL_BLOCK

    zeros = jnp.zeros((HEAD_DIM, tq), BF16)
    q_heads = [_lane_tiles(qT_ref, 0, nt, h * HEAD_DIM, HEAD_DIM) for h in range(4)]
    qm_lo = jnp.concatenate([jnp.concatenate([q, zeros], axis=0) for q in q_heads], axis=1)
    qm_hi = jnp.concatenate([jnp.concatenate([zeros, q], axis=0) for q in q_heads], axis=1)
    qpos = i * tq + lax.broadcasted_iota(jnp.int32, (1, tq), 1)
    qpos4 = jnp.concatenate([qpos] * 4, axis=1)

    n_prev = -(-(NSA_WINDOW - 1) // tq)
    nk = n_prev + 1
    kt0 = jnp.maximum(i - n_prev, 0)
    kw = k_ref[pl.ds(pl.multiple_of(kt0 * tq, tq), nk * tq), LANE:2 * LANE]
    sw = _dot(kw, qm_hi)
    dist = qpos - (kt0 * tq + lax.broadcasted_iota(jnp.int32, (nk * tq, tq), 0))
    wbias = jnp.where(dist >= 0, jnp.where(dist <= NSA_WINDOW - 1, 0.0, NEG), NEG)
    sw = sw + jnp.concatenate([wbias] * 4, axis=1)
    mw = jnp.max(sw, axis=0, keepdims=True)
    ew = jnp.exp2(sw - mw)
    dw = jnp.sum(ew, axis=0, keepdims=True)
    vwin = _lane_tiles(vT_ref, kt0 * nt, nk * nt, HEAD_DIM, HEAD_DIM)
    win_ref[...] = _dot(vwin, ew.astype(BF16)) / dw

    zc = _dot(kc_ref[...], qm_lo)
    c_end = lax.broadcasted_iota(jnp.int32, (nc, tq), 0) * NSA_CMP_STRIDE + (NSA_CMP_LEN - 1)
    cbias = jnp.where(c_end <= qpos, 0.0, NEG)
    zc = zc + jnp.concatenate([cbias] * 4, axis=1)
    e = jnp.exp2(zc - jnp.max(zc, axis=0, keepdims=True))
    seen = jnp.where(qpos4 >= NSA_CMP_LEN - 1, 1.0, 0.0)
    pc = e * (seen / jnp.maximum(jnp.sum(e, axis=0, keepdims=True), 1.0))
    o_cmp = _dot(kcT_ref[HEAD_DIM:2 * HEAD_DIM, :], pc.astype(BF16))

    psum = pc[:, 0:tq] + pc[:, tq:2 * tq] + pc[:, 2 * tq:3 * tq] + pc[:, 3 * tq:4 * tq]
    p_hi, p_lo = _split2(psum)
    imp = _dot(ov_ref[...], p_hi) + _dot(ov_ref[...], p_lo)
    nid = lax.broadcasted_iota(jnp.int32, (nsel, tq), 0)
    cur = qpos // NSA_SEL_BLOCK
    imp = jnp.where(nid == 0, BIG, imp)
    imp = jnp.where(nid == cur, BIG, imp)
    imp = jnp.where(nid == cur - 1, BIG, imp)
    imp = jnp.where(nid > cur, NEG, imp)
    for _r in range(min(NSA_SEL_TOPK, nsel)):
        mx = jnp.max(imp, axis=0, keepdims=True)
        idx = jnp.min(jnp.where(imp == mx, nid, nsel), axis=0, keepdims=True)
        imp = jnp.where(nid == idx, -jnp.inf, imp)
    sel = jnp.where(nid <= cur, jnp.where(imp == -jnp.inf, 1.0, 0.0), 0.0)
    sel_ref[...] = sel
    for h in range(4):
        qm_ref[h] = qm_lo[:, h * tq:(h + 1) * tq]

    def qk(t, slot):
        kb = k_ref[pl.ds(pl.multiple_of(t * tq, tq), tq), LANE:2 * LANE]
        for h in range(4):
            s_ref[slot, h] = _dot(kb, qm_ref[h])

    kpos_d = i * tq + lax.broadcasted_iota(jnp.int32, (tq, tq), 0)
    qk(i, 0)
    vb_d = _v_aug(_lane_tiles(vT_ref, i * nt, nt, 0, HEAD_DIM))
    srows_d = [sel_ref[pl.ds(i * spb + u, 1), :] for u in range(spb)]
    for h in range(4):
        sc = s_ref[0, h]
        s = jnp.concatenate(
            [jnp.where(srows_d[u] > 0.0, sc[u * NSA_SEL_BLOCK:(u + 1) * NSA_SEL_BLOCK, :], NEG)
             for u in range(spb)], axis=0)
        s = jnp.where(kpos_d <= qpos, s, NEG)
        m = jnp.max(s, axis=0, keepdims=True)
        lanes = slice(h * tq, (h + 1) * tq)
        ml_ref[0:1, lanes] = m
        acc_ref[:, lanes] = _dot(vb_d, jnp.exp2(s - m).astype(BF16))

    last = jnp.maximum(i - 1, 0)

    def update(t, valid, slot):
        vb = _v_aug(_lane_tiles(vT_ref, t * nt, nt, 0, HEAD_DIM))
        srows = [sel_ref[pl.ds(t * spb + u, 1), :] * valid for u in range(spb)]
        for h in range(4):
            lanes = slice(h * tq, (h + 1) * tq)
            sc = s_ref[slot, h]
            subs = [sc[u * NSA_SEL_BLOCK:(u + 1) * NSA_SEL_BLOCK, :] for u in range(spb)]
            tmax = None
            for u in range(spb):
                mu = jnp.where(srows[u] > 0.0, jnp.max(subs[u], axis=0, keepdims=True), NEG)
                tmax = mu if tmax is None else jnp.maximum(tmax, mu)
            m_old = ml_ref[0:1, lanes]
            m_new = jnp.maximum(m_old, tmax)
            p = jnp.concatenate(
                [jnp.exp2(subs[u] - jnp.where(srows[u] > 0.0, m_new, BIGPOS)).astype(BF16) for u in range(spb)],
                axis=0)
            ml_ref[0:1, lanes] = m_new
            acc_ref[:, lanes] = jnp.exp2(m_old - m_new) * acc_ref[:, lanes] + _dot(vb, p)

    qk(0, 0)

    def body(jj, c):
        t0 = 2 * jj
        qk(jnp.minimum(t0 + 1, last), 1)
        update(t0, 1.0, 0)
        qk(jnp.minimum(t0 + 2, last), 0)
        update(jnp.minimum(t0 + 1, last), (t0 + 1 < i).astype(F32), 1)
        return c

    lax.fori_loop(0, (i + 1) // 2, body, 0)
    o_sel = acc_ref[0:HEAD_DIM, :] / acc_ref[HEAD_DIM:HEAD_DIM + 1, :]

    o_win = win_ref[...]

    gates = _lane_tiles(vT_ref, i * nt, nt, 2 * HEAD_DIM, 16).astype(F32)
    outs = []
    for h in range(4):
        sl = slice(h * tq, (h + 1) * tq)
        outs.append(gates[3 * h:3 * h + 1, :] * o_cmp[:, sl]
                    + gates[3 * h + 1:3 * h + 2, :] * o_sel[:, sl]
                    + gates[3 * h + 2:3 * h + 3, :] * o_win[:, sl])
    o_ref[...] = jnp.concatenate(outs, axis=0).T.astype(o_ref.dtype)


def _nsa_call(oT, otok, kc, kcT, ovT, *, q_chunk, kv_chunk, v_chunk):
    B, nlt, _, _ = oT.shape
    S = nlt * LANE
    tq = 256
    nc = S // NSA_CMP_STRIDE
    nsel = S // NSA_SEL_BLOCK
    kern = functools.partial(_nsa_kernel, tq=tq)
    return pl.pallas_call(
        kern,
        grid=(B, S // tq),
        in_specs=_attn_specs(S, tq, q_chunk, kv_chunk, v_chunk) + [
            pl.BlockSpec((None, nc, LANE), lambda b, i: (b, 0, 0)),
            pl.BlockSpec((None, LANE, nc), lambda b, i: (b, 0, 0)),
            pl.BlockSpec((nsel, nc), lambda b, i: (0, 0)),
        ],
        out_specs=pl.BlockSpec((None, tq, BRANCH_W), lambda b, i: (b, i, 0)),
        out_shape=jax.ShapeDtypeStruct((B, S, BRANCH_W), BF16),
        scratch_shapes=[pltpu.VMEM((nsel, tq), F32), pltpu.VMEM((4, LANE, tq), BF16),
                        pltpu.VMEM((HA, 4 * tq), F32), pltpu.VMEM((8, 4 * tq), F32),
                        pltpu.VMEM((2, 4, tq, tq), F32), pltpu.VMEM((HEAD_DIM, 4 * tq), F32)],
        compiler_params=_cparams(("parallel", "arbitrary")),
        name="nsa",
    )(oT, otok, oT, kc, kcT, ovT)


def _epi_kernel(x_ref, ng_ref, ya_ref, yc_ref, yd_ref, ob0_ref, ob1_ref, ob2_ref, l0_ref, l1_ref, l2_ref,
                wz_ref, wmg_ref, wbr_ref, wout_ref, o_ref, nat_ref, *, tm):
    x = x_ref[...]
    ms = jnp.mean(x * x, axis=-1, keepdims=True)
    xn = (x * lax.rsqrt(ms + EPS) * ng_ref[...]).astype(BF16)

    def natural(ref, slot, dil):
        nh = BRANCH_W // LANE
        for r in range(dil):
            for hf in range(nh):
                c0 = r * BRANCH_W + hf * LANE
                nat_ref[slot * nh + hf, pl.ds(r, tm // dil, stride=dil), :] = ref[:, c0:c0 + LANE].astype(F32)
        return jnp.concatenate([nat_ref[slot * nh + hf] for hf in range(nh)], axis=1)

    d1, d2 = DIL_PAIRS[1][1], DIL_PAIRS[2][1]
    l0, l1, l2 = l0_ref[...], natural(l1_ref, 0, d1), natural(l2_ref, 1, d2)
    mx = jnp.maximum(jnp.maximum(l0, l1), l2)
    e0, e1, e2 = jnp.exp2(l0 - mx), jnp.exp2(l1 - mx), jnp.exp2(l2 - mx)
    yb = (e0 * ob0_ref[...].astype(F32) + e1 * natural(ob1_ref, 2, d1)
          + e2 * natural(ob2_ref, 3, d2)) / (e0 + e1 + e2)

    ys = (ya_ref[...].astype(F32), yb, yc_ref[...].astype(F32), yd_ref[...].astype(F32))
    merged = None
    for i in range(4):
        z = _dot(xn, wz_ref[:, i * BRANCH_W:(i + 1) * BRANCH_W])
        gated = (ys[i] * (z * jax.nn.sigmoid(z))).astype(BF16)
        br = _dot(gated, wbr_ref[i])
        mg = _dot(xn, wmg_ref[:, i * D_MODEL:(i + 1) * D_MODEL])
        term = jax.nn.sigmoid(mg) * br
        merged = term if merged is None else merged + term
    o_ref[...] = x + _dot(merged.astype(BF16), wout_ref[...])


def _epi_call(x2, ng, ya, yc, yd, obs, lses, wz, wmg, wbr, wout, *, layer, tm=256):
    T = x2.shape[0]
    row = lambda i: (i, 0)
    full2 = lambda i: (0, 0)
    yspec = pl.BlockSpec((tm, BRANCH_W), row)
    d1, d2 = DIL_PAIRS[1][1], DIL_PAIRS[2][1]
    gspecs = [yspec, pl.BlockSpec((tm // d1, d1 * BRANCH_W), row), pl.BlockSpec((tm // d2, d2 * BRANCH_W), row)]
    return pl.pallas_call(
        functools.partial(_epi_kernel, tm=tm),
        grid=(T // tm,),
        in_specs=[pl.BlockSpec((tm, D_MODEL), row), pl.BlockSpec((1, D_MODEL), full2)]
        + [yspec] * 3 + gspecs + gspecs
        + [pl.BlockSpec((None, D_MODEL, 4 * BRANCH_W), lambda i: (layer, 0, 0)),
           pl.BlockSpec((None, D_MODEL, 4 * D_MODEL), lambda i: (layer, 0, 0)),
           pl.BlockSpec((None, 4, BRANCH_W, D_MODEL), lambda i: (layer, 0, 0, 0)),
           pl.BlockSpec((None, D_MODEL, D_MODEL), lambda i: (layer, 0, 0))],
        out_specs=pl.BlockSpec((tm, D_MODEL), row),
        out_shape=jax.ShapeDtypeStruct((T, D_MODEL), F32),
        scratch_shapes=[pltpu.VMEM((4 * (BRANCH_W // LANE), tm, LANE), F32)],
        compiler_params=_cparams(("parallel",)),
        name="epilogue",
    )(x2, ng, ya, yc, yd, *obs, *lses, wz, wmg, wbr, wout)


TOK_AK, TOK_BQ, TOK_BK, TOK_BV, TOK_CK, TOK_DKV = range(6)
N_MAIN = 6
T_AQ, T_AV, T_CQ, T_CV, T_DQ, T_DX = range(6)
SCALE = 1.0 / math.sqrt(HEAD_DIM)
QSCALE2 = SCALE * LOG2E

_PLAIN = (False, False, 1.0, False)
_T_KINDS = (
    ((False, False, QSCALE2, False),) * 4,
    (_PLAIN,) * 4,
    ((True, True, QSCALE2, False),) * 4,
    (_PLAIN,) * 4,
    ((True, True, QSCALE2, False),) * 4,
    (_PLAIN, _PLAIN, (False, False, 1.0, True), _PLAIN),
)


def _tok_specs():
    main = lambda j: (0, 1, j * CH, CH, 0)
    qkv = ((True, True), (True, True), (False, False))
    specs = []
    for j in range(N_MAIN):
        if j in (TOK_BQ, TOK_BK, TOK_CK):
            specs.append((True, True, (main(j),)))
        elif j == TOK_DKV:
            specs.append(("mixed", "mixed", (main(j), (3, NSA_CMP_STRIDE, 0, LANE, LANE))))
        else:
            specs.append((False, False, (main(j),)))
    for gi in (1, 2):
        dil = DIL_PAIRS[gi][1]
        for k, (nrm, rope) in enumerate(qkv):
            specs.append((nrm, rope, ((gi, dil, k * CH, CH, 3 * CH),)))
    return tuple(specs)


def _rope_tables(S, dil):
    L = S // dil
    pos = (jnp.arange(dil, dtype=jnp.int32)[:, None] + dil * jnp.arange(L, dtype=jnp.int32)[None, :]).astype(F32)
    inv = ROPE_THETA ** (-jnp.arange(0, ROT_DIM, 2, dtype=F32) / ROT_DIM)
    ang = pos[:, :, None] * inv[None, None, :]
    cos, sin = jnp.cos(ang), jnp.sin(ang)
    one = jnp.ones((dil, L, HEAD_DIM - ROT_DIM), F32)
    zero8 = jnp.zeros((dil, L, 8), F32)
    zero = jnp.zeros_like(one)
    c_head = jnp.concatenate([cos, cos, one], axis=-1)
    s1_head = jnp.concatenate([zero8, sin, zero], axis=-1)
    s2_head = jnp.concatenate([-sin, zero8, zero], axis=-1)
    rtok = jnp.stack([jnp.tile(t, (1, 1, LANE // HEAD_DIM)) for t in (c_head, s1_head, s2_head)], axis=1)
    rT = jnp.stack([cos.transpose(0, 2, 1), sin.transpose(0, 2, 1)], axis=1)
    return rtok, rT


def _tok_params(rows):
    out = []
    for nf, rf, gains in rows:
        nrow = jnp.concatenate([jnp.full((HEAD_DIM,), float(f), F32) for f in nf])
        rrow = jnp.concatenate([jnp.full((HEAD_DIM,), float(f), F32) for f in rf])
        grow = jnp.concatenate([g.astype(F32) for g in gains])
        out.append(jnp.concatenate([jnp.stack([nrow, rrow, grow]), jnp.zeros((5, CH), F32)], axis=0))
    return jnp.stack(out)


def kernel(x, norm_g, w_in, qk_g, cmp_pe, cmp_w1, cmp_w2, w_branch, w_out):
    B, S, _ = x.shape
    T = B * S
    o = COL_OFF
    ones = jnp.ones((HEAD_DIM,), F32)

    r = np.arange(CH)
    G = jnp.asarray((r[:, None] // HEAD_DIM == r[None, :] // HEAD_DIM).astype(np.float32), BF16)
    G128 = G[:LANE, :LANE]
    t = np.arange(256)
    U = jnp.asarray((t[None, :] > t[:, None]).astype(np.float32), BF16)
    nblk = S // MOBA_BLOCK
    A = jnp.asarray(np.repeat(np.eye(nblk, dtype=np.float32), MOBA_BLOCK, axis=1) / MOBA_BLOCK, BF16)
    nc, nsel = S // NSA_CMP_STRIDE, S // NSA_SEL_BLOCK
    cs = np.arange(nc) * NSA_CMP_STRIDE
    ss = np.arange(nsel) * NSA_SEL_BLOCK
    ov = ((cs[None, :] < ss[:, None] + NSA_SEL_BLOCK) & (cs[None, :] + NSA_CMP_LEN > ss[:, None]))
    ov[:, nc - 1] = False
    ovT = jnp.asarray(ov.astype(np.float32), BF16)
    hid = np.arange(LANE)[:, None]
    ind = jnp.asarray(((hid < 4) & (r[None, :] // HEAD_DIM == hid)).astype(np.float32), BF16)
    rtok4, rT4 = _rope_tables(S, 1)
    rtok, rT = rtok4[0], rT4[0]
    tok_specs = _tok_specs()

    dkv = o[13]
    bq = lambda gi: (o[4] + gi * CH, o[4] + (gi + 1) * CH)
    bk = lambda gi: (o[5] + gi * CH, o[5] + (gi + 1) * CH)
    bv = lambda gi: (o[6] + gi * CH, o[6] + (gi + 1) * CH)
    tok_plan = ((o[1], o[2]), bq(0), bk(0), bv(0), (o[9], o[10]),
                (dkv, dkv + 64), (dkv + 64, dkv + 128), (dkv + 128, dkv + 192), (dkv + 256, dkv + 320),
                bq(1), bk(1), bv(1), bq(2), bk(2), bv(2))
    T_plan = ((o[0], o[1]), (o[2], o[3]), (o[8], o[9]), (o[10], o[11]), (o[12], o[13]),
              (dkv + 192, dkv + 256), (dkv + 320, dkv + 384), (o[14], o[15]))
    z_plan = ((o[3], o[4]), (o[7], o[8]), (o[11], o[12]), (o[15], o[16]))
    wtok_all, wT_all, wz_all, wmg_all = _repack_call(
        w_in, (tok_plan, T_plan, z_plan, ((o[16], o[17]),)),
        (len(tok_specs) * CH, len(_T_KINDS) * CH, 4 * BRANCH_W, 4 * D_MODEL))

    wbr_all = w_branch.astype(BF16)
    wout_all = w_out.astype(BF16)

    for l in range(DEPTH):
        g = qk_g[l]
        plain = ((0,) * 4, (0,) * 4, (ones,) * 4)
        bq_p = ((1,) * 4, (1,) * 4, (g[0] * QSCALE2,) * 4)
        bk_p = ((1,) * 4, (1,) * 4, (g[1],) * 4)
        tokp = _tok_params([
            plain, bq_p, bk_p, plain,
            ((1,) * 4, (1,) * 4, (g[3],) * 4),
            ((0, 0, 1, 1), (1, 0, 1, 1), (ones, ones, g[6], g[7])),
            bq_p, bk_p, plain, bq_p, bk_p, plain])
        gT = jnp.concatenate([jnp.tile(ones, 8), jnp.tile(g[2], 4), jnp.tile(ones, 4), jnp.tile(g[4], 4),
                              jnp.tile(ones, 4)])[:, None]
        ng = norm_g[l][None, :]
        otok, og1, og2, okv, oT = _proj_call(x, ng, wtok_all, wT_all, tokp, gT, rtok, rT, G, layer=l,
                                             tok_specs=tok_specs, T_kinds=_T_KINDS, n_main=N_MAIN)

        ya = _sb_call(oT, otok, U, q_chunk=T_AQ, k_chunk=TOK_AK, v_chunk=T_AV)

        obs, lses = [], []
        for gi, (window, dil) in enumerate(DIL_PAIRS):
            if dil == 1:
                ob, lse = _band_call(otok, ind, dil=1, per_res=N_MAIN, qi=TOK_BQ, ki=TOK_BK, vi=TOK_BV,
                                     max_dist=window)
            else:
                ob, lse = _band_call((og1, og2)[gi - 1], ind, dil=dil, per_res=3, qi=0, ki=1, vi=2,
                                     max_dist=window // dil)
            obs.append(ob.reshape(T // dil, dil * BRANCH_W))
            lses.append(lse.reshape(T // dil, dil * BRANCH_W))

        yc = _moba_call(oT, otok, A, q_chunk=T_CQ, k_chunk=TOK_CK, v_chunk=T_CV)

        w1 = cmp_w1[l].reshape(2, NSA_CMP_LEN, HEAD_DIM, HEAD_DIM)
        z64 = jnp.zeros((NSA_CMP_STRIDE, HEAD_DIM, HEAD_DIM), F32)
        w1blk = jnp.concatenate([
            jnp.concatenate([w1[0, :16], w1[0, 16:], z64, z64], axis=2),
            jnp.concatenate([z64, z64, w1[1, :16], w1[1, 16:]], axis=2)], axis=1)
        pe2 = jnp.concatenate([cmp_pe[l].reshape(1, -1), jnp.zeros((7, 2 * NSA_CMP_LEN * HEAD_DIM), F32)], axis=0)
        zf = jnp.zeros((NSA_CMP_LEN * HEAD_DIM, HEAD_DIM), F32)
        w1f = jnp.concatenate([jnp.concatenate([cmp_w1[l, 0], zf], axis=1),
                               jnp.concatenate([zf, cmp_w1[l, 1]], axis=1)], axis=0)
        z2 = jnp.zeros((HEAD_DIM, HEAD_DIM), F32)
        w2blk = jnp.concatenate([jnp.concatenate([cmp_w2[l, 0], z2], axis=1),
                                 jnp.concatenate([z2, cmp_w2[l, 1]], axis=1)], axis=0)
        gk = jnp.concatenate([g[5], ones])[None, :]
        kc, kcT = _compress_call(okv, w1blk, pe2, w1f, w2blk, gk, G128)
        yd = _nsa_call(oT, otok, kc, kcT, ovT, q_chunk=T_DQ, kv_chunk=TOK_DKV, v_chunk=T_DX)

        x2 = _epi_call(x.reshape(T, D_MODEL), ng, ya.reshape(T, BRANCH_W), yc.reshape(T, BRANCH_W),
                       yd.reshape(T, BRANCH_W), obs, lses, wz_all, wmg_all, wbr_all, wout_all, layer=l)
        x = x2.reshape(B, S, D_MODEL)
    return x
```

```python
import functools
import math

import numpy as np
import jax
import jax.numpy as jnp
from jax import lax
from jax.experimental import pallas as pl
from jax.experimental.pallas import tpu as pltpu

F32 = jnp.float32
BF16 = jnp.bfloat16

D_MODEL = 1024
DEPTH = 4
HEAD_DIM = 64
ROT_DIM = 16
ROPE_THETA = 500000.0
EPS = 1e-6
NEG = -1e30
BIG = 1e9
BRANCH_W = 256
DIL_PAIRS = ((128, 1), (512, 4), (2048, 16))
MOBA_BLOCK = 256
MOBA_TOPK = 3
NSA_CMP_LEN = 32
NSA_CMP_STRIDE = 16
NSA_SEL_BLOCK = 64
NSA_SEL_TOPK = 16
NSA_WINDOW = 512
COL_SIZES = (256, 256, 256, 256, 768, 768, 768, 256, 256, 256, 256, 256, 256, 384, 12, 256, 4096)
COL_OFF = tuple(int(v) for v in np.concatenate([[0], np.cumsum(COL_SIZES)]))

LANE = 128
CH = 256
VMEM_LIMIT = 56 * 1024 * 1024


def _cparams(sem):
    return pltpu.CompilerParams(dimension_semantics=sem, vmem_limit_bytes=VMEM_LIMIT)


def _dot(a, b):
    return jnp.dot(a, b, preferred_element_type=F32)


def _dot_nt(a, b):
    return lax.dot_general(a, b, (((1,), (1,)), ((), ())), preferred_element_type=F32)


def _split2(x):
    hi = x.astype(BF16)
    lo = (x - hi.astype(F32)).astype(BF16)
    return hi, lo


def _split3(x):
    hi = x.astype(BF16)
    r = x - hi.astype(F32)
    mid = r.astype(BF16)
    lo = (r - mid.astype(F32)).astype(BF16)
    return hi, mid, lo


def _repack_kernel(w_ref, *out_refs, plans, transposed):
    for o_ref, ranges, tr in zip(out_refs, plans, transposed):
        width = o_ref.shape[0] if tr else o_ref.shape[-1]
        parts = [w_ref[:, a:b] for a, b in ranges]
        used = sum(b - a for a, b in ranges)
        if used < width:
            parts.append(jnp.zeros((w_ref.shape[0], width - used), F32))
        packed = parts[0] if len(parts) == 1 else jnp.concatenate(parts, axis=1)
        o_ref[...] = (packed.T if tr else packed).astype(o_ref.dtype)


def _repack_call(w_in, plans, widths, transposed, rows=128):
    depth, d, n_in = w_in.shape
    kern = functools.partial(_repack_kernel, plans=plans, transposed=transposed)
    return pl.pallas_call(
        kern,
        grid=(depth, d // rows),
        in_specs=[pl.BlockSpec((None, rows, n_in), lambda l, i: (l, i, 0))],
        out_specs=[pl.BlockSpec((None, w, rows), lambda l, i: (l, 0, i)) if tr
                   else pl.BlockSpec((None, rows, w), lambda l, i: (l, i, 0)) for w, tr in zip(widths, transposed)],
        out_shape=[jax.ShapeDtypeStruct((depth, w, d) if tr else (depth, d, w), BF16)
                   for w, tr in zip(widths, transposed)],
        compiler_params=_cparams(("parallel", "parallel")),
        name="repack",
    )(w_in)


def _proj_kernel(x_ref, ng_ref, wtok_ref, wT_ref, tokp_ref, gT_ref, rtok_ref, rT_ref, G_ref,
                 omain_ref, og1_ref, og2_ref, okv_ref, oT_ref, stage_ref, *, tok_specs, T_kinds, tm):
    x = x_ref[...]
    ms = jnp.mean(x * x, axis=-1, keepdims=True)
    xn = (x * lax.rsqrt(ms + EPS) * ng_ref[...]).astype(BF16)

    dsts = (omain_ref, og1_ref, og2_ref, okv_ref)
    for c, (has_norm, has_rope, outs) in enumerate(tok_specs):
        y = _dot(xn, wtok_ref[:, c * CH:(c + 1) * CH])
        prm = tokp_ref[c]
        if has_norm:
            ss = _dot((y * y).astype(BF16), G_ref[...])
            inv = lax.rsqrt(ss * (1.0 / HEAD_DIM) + EPS)
            y = y * (jnp.where(prm[0:1, :] > 0.0, inv, 1.0) if has_norm == "mixed" else inv)
        y = y * prm[2:3, :]
        if has_rope:
            halves = []
            for hf in range(CH // LANE):
                yh = y[:, hf * LANE:(hf + 1) * LANE]
                cc, s1, s2 = rtok_ref[0], rtok_ref[1], rtok_ref[2]
                if has_rope == "mixed":
                    rf = prm[1:2, hf * LANE:(hf + 1) * LANE]
                    cc, s1, s2 = jnp.where(rf > 0.0, cc, 1.0), s1 * rf, s2 * rf
                halves.append(yh * cc + pltpu.roll(yh, 8, 1) * s1 + pltpu.roll(yh, LANE - 8, 1) * s2)
            y = jnp.concatenate(halves, axis=1)
        staged = False
        for dst, dil, col, lanes, rstride in outs:
            if dil == 1:
                dsts[dst][:, col:col + lanes] = y[:, :lanes].astype(BF16)
                continue
            if not staged:
                for hf in range(CH // LANE):
                    stage_ref[hf] = y[:, hf * LANE:(hf + 1) * LANE]
                staged = True
            for r in range(dil):
                for hf in range(lanes // LANE):
                    blk = stage_ref[hf, pl.ds(r, tm // dil, stride=dil), :]
                    c0 = r * rstride + col + hf * LANE
                    dsts[dst][:, c0:c0 + LANE] = blk.astype(BF16)

    cosT = rT_ref[0]
    sinT = rT_ref[1]
    for c, heads in enumerate(T_kinds):
        y = _dot_nt(wT_ref[c * CH:(c + 1) * CH, :], xn)
        for h, (nrm, rope, scale, sigm) in enumerate(heads):
            r0 = c * CH + h * HEAD_DIM
            yh = y[h * HEAD_DIM:(h + 1) * HEAD_DIM, :]
            if nrm:
                msq = jnp.mean(yh * yh, axis=0, keepdims=True)
                yh = yh * lax.rsqrt(msq + EPS) * gT_ref[r0:r0 + HEAD_DIM, :]
            if rope:
                x1 = yh[0:8, :]
                x2 = yh[8:16, :]
                yh = jnp.concatenate([x1 * cosT - x2 * sinT, x2 * cosT + x1 * sinT, yh[16:, :]], axis=0)
            if scale != 1.0:
                yh = yh * scale
            if sigm:
                yh = jax.nn.sigmoid(yh)
            yb = yh.astype(BF16)
            for t in range(tm // LANE):
                oT_ref[t, r0:r0 + HEAD_DIM, :] = yb[:, t * LANE:(t + 1) * LANE]


def _proj_call(x, ng, wtok, wT, tokp, gT, rtok, rT, G, *, layer, tok_specs, T_kinds, n_main, tm=512):
    B, S, _ = x.shape
    ntok, nT = len(tok_specs), len(T_kinds)
    d1, d2 = DIL_PAIRS[1][1], DIL_PAIRS[2][1]
    kern = functools.partial(_proj_kernel, tok_specs=tok_specs, T_kinds=T_kinds, tm=tm)
    const2 = lambda b, n: (0, 0)
    return pl.pallas_call(
        kern,
        grid=(B, S // tm),
        in_specs=[
            pl.BlockSpec((None, tm, D_MODEL), lambda b, n: (b, n, 0)),
            pl.BlockSpec((1, D_MODEL), const2),
            pl.BlockSpec((None, D_MODEL, ntok * CH), lambda b, n: (layer, 0, 0)),
            pl.BlockSpec((None, nT * CH, D_MODEL), lambda b, n: (layer, 0, 0)),
            pl.BlockSpec((ntok, 8, CH), lambda b, n: (0, 0, 0)),
            pl.BlockSpec((nT * CH, 1), const2),
            pl.BlockSpec((3, tm, LANE), lambda b, n: (0, n, 0)),
            pl.BlockSpec((2, 8, tm), lambda b, n: (0, 0, n)),
            pl.BlockSpec((CH, CH), const2),
        ],
        out_specs=[
            pl.BlockSpec((None, tm, n_main * CH), lambda b, n: (b, n, 0)),
            pl.BlockSpec((None, tm // d1, d1 * 3 * CH), lambda b, n: (b, n, 0)),
            pl.BlockSpec((None, tm // d2, d2 * 3 * CH), lambda b, n: (b, n, 0)),
            pl.BlockSpec((None, tm // NSA_CMP_STRIDE, NSA_CMP_STRIDE * LANE), lambda b, n: (b, n, 0)),
            pl.BlockSpec((None, tm // LANE, nT * CH, LANE), lambda b, n: (b, n, 0, 0)),
        ],
        out_shape=[
            jax.ShapeDtypeStruct((B, S, n_main * CH), BF16),
            jax.ShapeDtypeStruct((B, S // d1, d1 * 3 * CH), BF16),
            jax.ShapeDtypeStruct((B, S // d2, d2 * 3 * CH), BF16),
            jax.ShapeDtypeStruct((B, S // NSA_CMP_STRIDE, NSA_CMP_STRIDE * LANE), BF16),
            jax.ShapeDtypeStruct((B, S // LANE, nT * CH, LANE), BF16),
        ],
        scratch_shapes=[pltpu.VMEM((CH // LANE, tm, LANE), F32)],
        compiler_params=_cparams(("parallel", "parallel")),
        name="proj",
    )(x, ng, wtok, wT, tokp, gT, rtok, rT, G)


def _pair_masked_q(q_pair, h):
    rid = lax.broadcasted_iota(jnp.int32, q_pair.shape, 0)
    lo = (h % 2) * HEAD_DIM
    keep = jnp.where(rid >= lo, jnp.where(rid < lo + HEAD_DIM, 1.0, 0.0), 0.0).astype(BF16)
    return q_pair * keep


def _lane_tiles(ref, t0, nt, r0, nr):
    return jnp.concatenate([ref[t0 + t, r0:r0 + nr, :] for t in range(nt)], axis=1)


BIGPOS = 1e30
ONES_ROWS = 16
HA = HEAD_DIM + ONES_ROWS
LOG2E = 1.4426950408889634


def _v_aug(vb):
    return jnp.concatenate([vb, jnp.ones((ONES_ROWS, vb.shape[1]), BF16)], axis=0)


def _online_cols(s, colsel, m_old, acc_old, vb_aug):
    tmax = jnp.max(s, axis=0, keepdims=True)
    m_new = jnp.where(colsel > 0.0, jnp.maximum(m_old, tmax), m_old)
    m_use = jnp.where(colsel > 0.0, m_new, BIGPOS)
    p = jnp.exp2(s - m_use).astype(BF16)
    acc = jnp.exp2(m_old - m_new) * acc_old + _dot(vb_aug, p)
    return m_new, acc


SB_LOG_CUTOFF = -144.0


def _sb_kernel(qT_ref, k_ref, vT_ref, U_ref, o_ref, qm_ref, acc_ref, carry_ref, *, tq):
    i = pl.program_id(1)
    nt = tq // LANE
    row = lax.broadcasted_iota(jnp.int32, (tq, tq), 0)
    col = lax.broadcasted_iota(jnp.int32, (tq, tq), 1)
    past = row < col
    for h in range(4):
        p = h // 2
        qm_ref[h] = _pair_masked_q(_lane_tiles(qT_ref, 0, nt, p * LANE, LANE), h)
    acc_ref[...] = jnp.zeros_like(acc_ref)
    carry_ref[...] = jnp.zeros_like(carry_ref)

    def tile(j, masked):
        scores, logsig, laters = [], [], []
        for h in range(4):
            p = h // 2
            kb = k_ref[pl.ds(pl.multiple_of(j * tq, tq), tq), p * LANE:(p + 1) * LANE]
            scores.append(_dot(kb, qm_ref[h]))
        worst = None
        for h in range(4):
            s = scores[h]
            sp = jnp.maximum(s, 0.0) + jnp.log2(1.0 + jnp.exp2(-jnp.abs(s)))
            lg = -sp
            if masked:
                lg = jnp.where(past, lg, 0.0)
            hi, lo = _split2(lg)
            carry = carry_ref[h:h + 1, :]
            laters.append(_dot(U_ref[...], hi) + _dot(U_ref[...], lo) + carry)
            logsig.append(s - sp)
            carry = carry + jnp.sum(lg, axis=0, keepdims=True)
            carry_ref[h:h + 1, :] = carry
            worst = carry if worst is None else jnp.maximum(worst, carry)
        for h in range(4):
            w = jnp.exp2(logsig[h] + laters[h])
            if masked:
                w = jnp.where(past, w, 0.0)
            vb = _lane_tiles(vT_ref, j * nt, nt, h * HEAD_DIM, HEAD_DIM)
            acc_ref[h * HEAD_DIM:(h + 1) * HEAD_DIM, :] += _dot(vb, w.astype(BF16))
        return jnp.max(worst)

    worst0 = tile(i, True)

    def cond(st):
        return jnp.logical_and(st[0] >= 0, st[1] > SB_LOG_CUTOFF)

    def body(st):
        return st[0] - 1, tile(st[0], False)

    lax.while_loop(cond, body, (i - 1, worst0))
    o_ref[...] = acc_ref[...].T.astype(o_ref.dtype)


def _attn_specs(S, tq, q_chunk, k_chunk, v_chunk):
    return [
        pl.BlockSpec((None, tq // LANE, CH, LANE), lambda b, i: (b, i, q_chunk, 0)),
        pl.BlockSpec((None, S, CH), lambda b, i: (b, 0, k_chunk)),
        pl.BlockSpec((None, S // LANE, CH, LANE), lambda b, i: (b, 0, v_chunk, 0)),
    ]


def _sb_call(oT, otok, U, *, q_chunk, k_chunk, v_chunk, tq=256):
    B, nlt, _, _ = oT.shape
    S = nlt * LANE
    kern = functools.partial(_sb_kernel, tq=tq)
    return pl.pallas_call(
        kern,
        grid=(B, S // tq),
        in_specs=_attn_specs(S, tq, q_chunk, k_chunk, v_chunk) + [pl.BlockSpec((tq, tq), lambda b, i: (0, 0))],
        out_specs=pl.BlockSpec((None, tq, BRANCH_W), lambda b, i: (b, i, 0)),
        out_shape=jax.ShapeDtypeStruct((B, S, BRANCH_W), BF16),
        scratch_shapes=[pltpu.VMEM((4, LANE, tq), BF16), pltpu.VMEM((CH, tq), F32), pltpu.VMEM((8, tq), F32)],
        compiler_params=_cparams(("parallel", "arbitrary")),
        name="stick_breaking",
    )(oT, otok, oT, U)


def _head_lanes(x, lane, h):
    lo = h * HEAD_DIM
    return jnp.where(lane >= lo, jnp.where(lane < lo + HEAD_DIM, x, jnp.zeros_like(x)), jnp.zeros_like(x))


def _band_kernel(q_ref, k_ref, v_ref, ind_ref, o_ref, lse_ref, *, tqb, max_dist):
    n = pl.program_id(2)
    n_prev = -(-max_dist // LANE)
    nkr = (n_prev + 1) * LANE
    nsub = tqb // LANE
    row = lax.broadcasted_iota(jnp.int32, (nkr, LANE), 0)
    col = lax.broadcasted_iota(jnp.int32, (nkr, LANE), 1)
    lane_q = lax.broadcasted_iota(jnp.int32, (LANE, LANE), 1)
    lane_v = lax.broadcasted_iota(jnp.int32, (nkr, CH), 1)
    for u in range(nsub):
        qt = n * nsub + u
        kt0 = jnp.maximum(qt - n_prev, 0)
        dist = (qt - kt0) * LANE + col - row
        bias = jnp.where(dist >= 0, jnp.where(dist <= max_dist, 0.0, NEG), NEG)
        k0 = pl.multiple_of(kt0 * LANE, LANE)
        kwin = k_ref[pl.ds(k0, nkr), :]
        vwin = v_ref[pl.ds(k0, nkr), :]
        qu = q_ref[u * LANE:(u + 1) * LANE, :]
        o_acc = None
        lses = []
        for h in range(4):
            p = h // 2
            qm = _head_lanes(qu[:, p * LANE:(p + 1) * LANE], lane_q, h % 2)
            s = _dot_nt(kwin[:, p * LANE:(p + 1) * LANE], qm) + bias
            m = jnp.max(s, axis=0, keepdims=True)
            e = jnp.exp2(s - m)
            den = jnp.sum(e, axis=0, keepdims=True)
            pn = e * (1.0 / den)
            contrib = _dot(pn.T.astype(BF16), _head_lanes(vwin, lane_v, h))
            o_acc = contrib if o_acc is None else o_acc + contrib
            lses.append(m + jnp.log(den) * LOG2E)
        o_ref[u * LANE:(u + 1) * LANE, :] = o_acc.astype(o_ref.dtype)
        ls = jnp.concatenate(lses + [jnp.zeros((LANE - 4, LANE), F32)], axis=0).T
        l1, l2, l3 = _split3(ls)
        lse_ref[u * LANE:(u + 1) * LANE, :] = (_dot(l1, ind_ref[...]) + _dot(l2, ind_ref[...])
                                               + _dot(l3, ind_ref[...]))


def _band_call(arr, ind, *, dil, per_res, qi, ki, vi, max_dist):
    B, L, _ = arr.shape
    tqb = min(1024, L)
    kern = functools.partial(_band_kernel, tqb=tqb, max_dist=max_dist)
    return pl.pallas_call(
        kern,
        grid=(B, dil, L // tqb),
        in_specs=[
            pl.BlockSpec((None, tqb, CH), lambda b, r, n: (b, n, r * per_res + qi)),
            pl.BlockSpec((None, L, CH), lambda b, r, n: (b, 0, r * per_res + ki)),
            pl.BlockSpec((None, L, CH), lambda b, r, n: (b, 0, r * per_res + vi)),
            pl.BlockSpec((LANE, CH), lambda b, r, n: (0, 0)),
        ],
        out_specs=[
            pl.BlockSpec((None, tqb, BRANCH_W), lambda b, r, n: (b, n, r)),
            pl.BlockSpec((None, tqb, BRANCH_W), lambda b, r, n: (b, n, r)),
        ],
        out_shape=[
            jax.ShapeDtypeStruct((B, L, dil * BRANCH_W), BF16),
            jax.ShapeDtypeStruct((B, L, dil * BRANCH_W), F32),
        ],
        compiler_params=_cparams(("parallel", "parallel", "parallel")),
        name="banded",
    )(arr, arr, arr, ind)


def _moba_kernel(qT_ref, k_ref, vT_ref, A_ref, o_ref, kmean_ref, sel_ref, qm_ref, acc_ref, ml_ref, s_ref, *, tq):
    i = pl.program_id(1)
    nt = tq // LANE
    nblk = A_ref.shape[0]

    @pl.when(i == 0)
    def _():
        kmean_ref[...] = _dot(A_ref[...], k_ref[...])

    blk = lax.broadcasted_iota(jnp.int32, (nblk, tq), 0)
    row = lax.broadcasted_iota(jnp.int32, (tq, tq), 0)
    col = lax.broadcasted_iota(jnp.int32, (tq, tq), 1)
    gates, diag = [], []
    for h in range(4):
        p = h // 2
        qm = _pair_masked_q(_lane_tiles(qT_ref, 0, nt, p * LANE, LANE), h)
        qm_ref[h] = qm
        k1, k2, k3 = _split3(kmean_ref[:, p * LANE:(p + 1) * LANE])
        gates.append(_dot(k1, qm) + _dot(k2, qm) + _dot(k3, qm))
        kb = k_ref[pl.ds(pl.multiple_of(i * tq, tq), tq), p * LANE:(p + 1) * LANE]
        diag.append(_dot(kb, qm))
    for h in range(4):
        g = jnp.where(blk < i, gates[h], NEG)
        sel = jnp.zeros((nblk, tq), F32)
        for _r in range(MOBA_TOPK):
            mx = jnp.max(g, axis=0, keepdims=True)
            idx = jnp.min(jnp.where(g == mx, blk, nblk), axis=0, keepdims=True)
            hit = blk == idx
            sel = jnp.where(hit, 1.0, sel)
            g = jnp.where(hit, -jnp.inf, g)
        sel_ref[h] = jnp.where(blk < i, sel, 0.0)

        s = jnp.where(row <= col, diag[h], NEG)
        m = jnp.max(s, axis=0, keepdims=True)
        ml_ref[h:h + 1, :] = m
        acc_ref[h * HA:(h + 1) * HA, :] = _dot(
            _v_aug(_lane_tiles(vT_ref, i * nt, nt, h * HEAD_DIM, HEAD_DIM)), jnp.exp2(s - m).astype(BF16))

    last = jnp.maximum(i - 1, 0)

    def qk(t, slot):
        for h in range(4):
            p = h // 2
            kbj = k_ref[pl.ds(pl.multiple_of(t * tq, tq), tq), p * LANE:(p + 1) * LANE]
            s_ref[slot, h] = _dot(kbj, qm_ref[h])

    def update(t, valid, slot):
        for h in range(4):
            srow = sel_ref[h, pl.ds(t, 1), :] * valid
            vb = _v_aug(_lane_tiles(vT_ref, t * nt, nt, h * HEAD_DIM, HEAD_DIM))
            rows = slice(h * HA, (h + 1) * HA)
            m, acc = _online_cols(s_ref[slot, h], srow, ml_ref[h:h + 1, :], acc_ref[rows, :], vb)
            ml_ref[h:h + 1, :] = m
            acc_ref[rows, :] = acc

    qk(0, 0)

    def body(jj, c):
        t0 = 2 * jj
        qk(jnp.minimum(t0 + 1, last), 1)
        update(t0, 1.0, 0)
        qk(jnp.minimum(t0 + 2, last), 0)
        update(jnp.minimum(t0 + 1, last), (t0 + 1 < i).astype(F32), 1)
        return c

    lax.fori_loop(0, (i + 1) // 2, body, 0)
    outs = [acc_ref[h * HA:h * HA + HEAD_DIM, :] / acc_ref[h * HA + HEAD_DIM:h * HA + HEAD_DIM + 1, :]
            for h in range(4)]
    o_ref[...] = jnp.concatenate(outs, axis=0).T.astype(o_ref.dtype)


def _moba_call(oT, otok, A, *, q_chunk, k_chunk, v_chunk):
    B, nlt, _, _ = oT.shape
    S = nlt * LANE
    tq = MOBA_BLOCK
    nblk = S // MOBA_BLOCK
    kern = functools.partial(_moba_kernel, tq=tq)
    return pl.pallas_call(
        kern,
        grid=(B, S // tq),
        in_specs=_attn_specs(S, tq, q_chunk, k_chunk, v_chunk) + [pl.BlockSpec((nblk, S), lambda b, i: (0, 0))],
        out_specs=pl.BlockSpec((None, tq, BRANCH_W), lambda b, i: (b, i, 0)),
        out_shape=jax.ShapeDtypeStruct((B, S, BRANCH_W), BF16),
        scratch_shapes=[pltpu.VMEM((nblk, CH), F32), pltpu.VMEM((4, nblk, tq), F32),
                        pltpu.VMEM((4, LANE, tq), BF16), pltpu.VMEM((4 * HA, tq), F32), pltpu.VMEM((8, tq), F32),
                        pltpu.VMEM((2, 4, tq, tq), F32)],
        compiler_params=_cparams(("parallel", "arbitrary")),
        name="moba",
    )(oT, otok, oT, A)


def _compress_kernel(x_ref, w1_ref, pe_ref, w1f_ref, w2_ref, gk_ref, G_ref, o_ref, oT_ref, acc_ref):
    l = pl.program_id(1)

    @pl.when(l == 0)
    def _():
        acc_ref[...] = jnp.zeros_like(acc_ref)

    x = x_ref[...]
    w_hi, w_lo = _split2(w1_ref[...])
    acc_ref[...] += _dot(x, w_hi) + _dot(x, w_lo)

    @pl.when(l == pl.num_programs(1) - 1)
    def _():
        nc = acc_ref.shape[0]
        r = acc_ref[...]
        bias = jnp.dot(pe_ref[...], w1f_ref[...], preferred_element_type=F32,
                       precision=lax.Precision.HIGHEST)
        p1 = jnp.concatenate([r[:, 0:64], r[:, 128:192]], axis=1)
        p2 = jnp.concatenate([r[:, 64:128], r[:, 192:256]], axis=1)
        hid = p1 + pltpu.roll(p2, nc - 1, 0) + bias[0:1, :]
        hid = hid * jax.nn.sigmoid(hid)
        comp = jnp.dot(hid, w2_ref[...], preferred_element_type=F32, precision=lax.Precision.HIGHEST)
        ss = _dot((comp * comp).astype(BF16), G_ref[...])
        lane = lax.broadcasted_iota(jnp.int32, comp.shape, 1)
        inv = jnp.where(lane < HEAD_DIM, lax.rsqrt(ss * (1.0 / HEAD_DIM) + EPS), 1.0)
        comp = comp * inv * gk_ref[...]
        rowi = lax.broadcasted_iota(jnp.int32, comp.shape, 0)
        comp = jnp.where(rowi < nc - 1, comp, 0.0)
        o_ref[...] = comp.astype(o_ref.dtype)
        oT_ref[...] = comp.T.astype(oT_ref.dtype)


def _compress_call(xv, w1blk, pe2, w1f, w2blk, gk, G128):
    B, nc, _ = xv.shape
    return pl.pallas_call(
        _compress_kernel,
        grid=(B, NSA_CMP_STRIDE),
        in_specs=[
            pl.BlockSpec((None, nc, LANE), lambda b, l: (b, 0, l)),
            pl.BlockSpec((None, LANE, CH), lambda b, l: (l, 0, 0)),
            pl.BlockSpec((8, 2 * NSA_CMP_LEN * HEAD_DIM), lambda b, l: (0, 0)),
            pl.BlockSpec((2 * NSA_CMP_LEN * HEAD_DIM, LANE), lambda b, l: (0, 0)),
            pl.BlockSpec((LANE, LANE), lambda b, l: (0, 0)),
            pl.BlockSpec((1, LANE), lambda b, l: (0, 0)),
            pl.BlockSpec((LANE, LANE), lambda b, l: (0, 0)),
        ],
        out_specs=[
            pl.BlockSpec((None, nc, LANE), lambda b, l: (b, 0, 0)),
            pl.BlockSpec((None, LANE, nc), lambda b, l: (b, 0, 0)),
        ],
        out_shape=[
            jax.ShapeDtypeStruct((B, nc, LANE), BF16),
            jax.ShapeDtypeStruct((B, LANE, nc), BF16),
        ],
        scratch_shapes=[pltpu.VMEM((nc, CH), F32)],
        compiler_params=_cparams(("parallel", "arbitrary")),
        name="nsa_compress",
    )(xv, w1blk, pe2, w1f, w2blk, gk, G128)


def _nsa_kernel(qT_ref, k_ref, vT_ref, kc_ref, kcT_ref, ov_ref, o_ref, sel_ref, qm_ref, acc_ref, ml_ref, s_ref,
                win_ref, *, tq):
    i = pl.program_id(1)
    nt = tq // LANE
    nq = 4 * tq
    nc = kc_ref.shape[0]
    nsel = ov_ref.shape[0]
    spb = tq // NSA_SEL_BLOCK

    zeros = jnp.zeros((HEAD_DIM, tq), BF16)
    q_heads = [_lane_tiles(qT_ref, 0, nt, h * HEAD_DIM, HEAD_DIM) for h in range(4)]
    qm_lo = jnp.concatenate([jnp.concatenate([q, zeros], axis=0) for q in q_heads], axis=1)
    qm_hi = jnp.concatenate([jnp.concatenate([zeros, q], axis=0) for q in q_heads], axis=1)
    qpos = i * tq + lax.broadcasted_iota(jnp.int32, (1, tq), 1)
    qpos4 = jnp.concatenate([qpos] * 4, axis=1)

    n_prev = -(-(NSA_WINDOW - 1) // tq)
    nk = n_prev + 1
    kt0 = jnp.maximum(i - n_prev, 0)
    kw = k_ref[pl.ds(pl.multiple_of(kt0 * tq, tq), nk * tq), LANE:2 * LANE]
    sw = _dot(kw, qm_hi)
    dist = qpos - (kt0 * tq + lax.broadcasted_iota(jnp.int32, (nk * tq, tq), 0))
    wbias = jnp.where(dist >= 0, jnp.where(dist <= NSA_WINDOW - 1, 0.0, NEG), NEG)
    sw = sw + jnp.concatenate([wbias] * 4, axis=1)
    mw = jnp.max(sw, axis=0, keepdims=True)
    ew = jnp.exp2(sw - mw)
    dw = jnp.sum(ew, axis=0, keepdims=True)
    vwin = _lane_tiles(vT_ref, kt0 * nt, nk * nt, HEAD_DIM, HEAD_DIM)
    win_ref[...] = _dot(vwin, ew.astype(BF16)) / dw

    zc = _dot(kc_ref[...], qm_lo)
    c_end = lax.broadcasted_iota(jnp.int32, (nc, tq), 0) * NSA_CMP_STRIDE + (NSA_CMP_LEN - 1)
    cbias = jnp.where(c_end <= qpos, 0.0, NEG)
    zc = zc + jnp.concatenate([cbias] * 4, axis=1)
    e = jnp.exp2(zc - jnp.max(zc, axis=0, keepdims=True))
    seen = jnp.where(qpos4 >= NSA_CMP_LEN - 1, 1.0, 0.0)
    pc = e * (seen / jnp.maximum(jnp.sum(e, axis=0, keepdims=True), 1.0))
    o_cmp = _dot(kcT_ref[HEAD_DIM:2 * HEAD_DIM, :], pc.astype(BF16))

    psum = pc[:, 0:tq] + pc[:, tq:2 * tq] + pc[:, 2 * tq:3 * tq] + pc[:, 3 * tq:4 * tq]
    p_hi, p_lo = _split2(psum)
    imp = _dot(ov_ref[...], p_hi) + _dot(ov_ref[...], p_lo)
    nid = lax.broadcasted_iota(jnp.int32, (nsel, tq), 0)
    cur = qpos // NSA_SEL_BLOCK
    imp = jnp.where(nid == 0, BIG, imp)
    imp = jnp.where(nid == cur, BIG, imp)
    imp = jnp.where(nid == cur - 1, BIG, imp)
    imp = jnp.where(nid > cur, NEG, imp)
    for _r in range(min(NSA_SEL_TOPK, nsel)):
        mx = jnp.max(imp, axis=0, keepdims=True)
        idx = jnp.min(jnp.where(imp == mx, nid, nsel), axis=0, keepdims=True)
        imp = jnp.where(nid == idx, -jnp.inf, imp)
    sel = jnp.where(nid <= cur, jnp.where(imp == -jnp.inf, 1.0, 0.0), 0.0)
    sel_ref[...] = sel
    for h in range(4):
        qm_ref[h] = qm_lo[:, h * tq:(h + 1) * tq]

    def qk(t, slot):
        kb = k_ref[pl.ds(pl.multiple_of(t * tq, tq), tq), LANE:2 * LANE]
        for h in range(4):
            s_ref[slot, h] = _dot(kb, qm_ref[h])

    kpos_d = i * tq + lax.broadcasted_iota(jnp.int32, (tq, tq), 0)
    qk(i, 0)
    vb_d = _v_aug(_lane_tiles(vT_ref, i * nt, nt, 0, HEAD_DIM))
    srows_d = [sel_ref[pl.ds(i * spb + u, 1), :] for u in range(spb)]
    for h in range(4):
        sc = s_ref[0, h]
        s = jnp.concatenate(
            [jnp.where(srows_d[u] > 0.0, sc[u * NSA_SEL_BLOCK:(u + 1) * NSA_SEL_BLOCK, :], NEG)
             for u in range(spb)], axis=0)
        s = jnp.where(kpos_d <= qpos, s, NEG)
        m = jnp.max(s, axis=0, keepdims=True)
        lanes = slice(h * tq, (h + 1) * tq)
        ml_ref[0:1, lanes] = m
        acc_ref[:, lanes] = _dot(vb_d, jnp.exp2(s - m).astype(BF16))

    last = jnp.maximum(i - 1, 0)

    def update(t, valid, slot):
        vb = _v_aug(_lane_tiles(vT_ref, t * nt, nt, 0, HEAD_DIM))
        srows = [sel_ref[pl.ds(t * spb + u, 1), :] * valid for u in range(spb)]
        for h in range(4):
            lanes = slice(h * tq, (h + 1) * tq)
            sc = s_ref[slot, h]
            subs = [sc[u * NSA_SEL_BLOCK:(u + 1) * NSA_SEL_BLOCK, :] for u in range(spb)]
            tmax = None
            for u in range(spb):
                mu = jnp.where(srows[u] > 0.0, jnp.max(subs[u], axis=0, keepdims=True), NEG)
                tmax = mu if tmax is None else jnp.maximum(tmax, mu)
            m_old = ml_ref[0:1, lanes]
            m_new = jnp.maximum(m_old, tmax)
            p = jnp.concatenate(
                [jnp.exp2(subs[u] - jnp.where(srows[u] > 0.0, m_new, BIGPOS)).astype(BF16) for u in range(spb)],
                axis=0)
            ml_ref[0:1, lanes] = m_new
            acc_ref[:, lanes] = jnp.exp2(m_old - m_new) * acc_ref[:, lanes] + _dot(vb, p)

    qk(0, 0)

    def body(jj, c):
        t0 = 2 * jj
        qk(jnp.minimum(t0 + 1, last), 1)
        update(t0, 1.0, 0)
        qk(jnp.minimum(t0 + 2, last), 0)
        update(jnp.minimum(t0 + 1, last), (t0 + 1 < i).astype(F32), 1)
        return c

    lax.fori_loop(0, (i + 1) // 2, body, 0)
    o_sel = acc_ref[0:HEAD_DIM, :] / acc_ref[HEAD_DIM:HEAD_DIM + 1, :]

    o_win = win_ref[...]

    gates = _lane_tiles(vT_ref, i * nt, nt, 2 * HEAD_DIM, 16).astype(F32)
    outs = []
    for h in range(4):
        sl = slice(h * tq, (h + 1) * tq)
        outs.append(gates[3 * h:3 * h + 1, :] * o_cmp[:, sl]
                    + gates[3 * h + 1:3 * h + 2, :] * o_sel[:, sl]
                    + gates[3 * h + 2:3 * h + 3, :] * o_win[:, sl])
    o_ref[...] = jnp.concatenate(outs, axis=0).T.astype(o_ref.dtype)


def _nsa_call(oT, otok, kc, kcT, ovT, *, q_chunk, kv_chunk, v_chunk):
    B, nlt, _, _ = oT.shape
    S = nlt * LANE
    tq = 256
    nc = S // NSA_CMP_STRIDE
    nsel = S // NSA_SEL_BLOCK
    kern = functools.partial(_nsa_kernel, tq=tq)
    return pl.pallas_call(
        kern,
        grid=(B, S // tq),
        in_specs=_attn_specs(S, tq, q_chunk, kv_chunk, v_chunk) + [
            pl.BlockSpec((None, nc, LANE), lambda b, i: (b, 0, 0)),
            pl.BlockSpec((None, LANE, nc), lambda b, i: (b, 0, 0)),
            pl.BlockSpec((nsel, nc), lambda b, i: (0, 0)),
        ],
        out_specs=pl.BlockSpec((None, tq, BRANCH_W), lambda b, i: (b, i, 0)),
        out_shape=jax.ShapeDtypeStruct((B, S, BRANCH_W), BF16),
        scratch_shapes=[pltpu.VMEM((nsel, tq), F32), pltpu.VMEM((4, LANE, tq), BF16),
                        pltpu.VMEM((HA, 4 * tq), F32), pltpu.VMEM((8, 4 * tq), F32),
                        pltpu.VMEM((2, 4, tq, tq), F32), pltpu.VMEM((HEAD_DIM, 4 * tq), F32)],
        compiler_params=_cparams(("parallel", "arbitrary")),
        name="nsa",
    )(oT, otok, oT, kc, kcT, ovT)


def _epi_kernel(x_ref, ng_ref, ya_ref, yc_ref, yd_ref, ob0_ref, ob1_ref, ob2_ref, l0_ref, l1_ref, l2_ref,
                wz_ref, wmg_ref, wbr_ref, wout_ref, o_ref, nat_ref, *, tm):
    x = x_ref[...]
    ms = jnp.mean(x * x, axis=-1, keepdims=True)
    xn = (x * lax.rsqrt(ms + EPS) * ng_ref[...]).astype(BF16)

    def natural(ref, slot, dil):
        nh = BRANCH_W // LANE
        for r in range(dil):
            for hf in range(nh):
                c0 = r * BRANCH_W + hf * LANE
                nat_ref[slot * nh + hf, pl.ds(r, tm // dil, stride=dil), :] = ref[:, c0:c0 + LANE].astype(F32)
        return jnp.concatenate([nat_ref[slot * nh + hf] for hf in range(nh)], axis=1)

    d1, d2 = DIL_PAIRS[1][1], DIL_PAIRS[2][1]
    l0, l1, l2 = l0_ref[...], natural(l1_ref, 0, d1), natural(l2_ref, 1, d2)
    mx = jnp.maximum(jnp.maximum(l0, l1), l2)
    e0, e1, e2 = jnp.exp2(l0 - mx), jnp.exp2(l1 - mx), jnp.exp2(l2 - mx)
    yb = (e0 * ob0_ref[...].astype(F32) + e1 * natural(ob1_ref, 2, d1)
          + e2 * natural(ob2_ref, 3, d2)) / (e0 + e1 + e2)

    ys = (ya_ref[...].astype(F32), yb, yc_ref[...].astype(F32), yd_ref[...].astype(F32))
    merged = None
    for i in range(4):
        z = _dot(xn, wz_ref[:, i * BRANCH_W:(i + 1) * BRANCH_W])
        gated = (ys[i] * (z * jax.nn.sigmoid(z))).astype(BF16)
        br = _dot(gated, wbr_ref[i])
        mg = _dot(xn, wmg_ref[:, i * D_MODEL:(i + 1) * D_MODEL])
        term = jax.nn.sigmoid(mg) * br
        merged = term if merged is None else merged + term
    o_ref[...] = x + _dot(merged.astype(BF16), wout_ref[...])


def _epi_call(x2, ng, ya, yc, yd, obs, lses, wz, wmg, wbr, wout, *, layer, tm=256):
    T = x2.shape[0]
    row = lambda i: (i, 0)
    full2 = lambda i: (0, 0)
    yspec = pl.BlockSpec((tm, BRANCH_W), row)
    d1, d2 = DIL_PAIRS[1][1], DIL_PAIRS[2][1]
    gspecs = [yspec, pl.BlockSpec((tm // d1, d1 * BRANCH_W), row), pl.BlockSpec((tm // d2, d2 * BRANCH_W), row)]
    return pl.pallas_call(
        functools.partial(_epi_kernel, tm=tm),
        grid=(T // tm,),
        in_specs=[pl.BlockSpec((tm, D_MODEL), row), pl.BlockSpec((1, D_MODEL), full2)]
        + [yspec] * 3 + gspecs + gspecs
        + [pl.BlockSpec((None, D_MODEL, 4 * BRANCH_W), lambda i: (layer, 0, 0)),
           pl.BlockSpec((None, D_MODEL, 4 * D_MODEL), lambda i: (layer, 0, 0)),
           pl.BlockSpec((None, 4, BRANCH_W, D_MODEL), lambda i: (layer, 0, 0, 0)),
           pl.BlockSpec((None, D_MODEL, D_MODEL), lambda i: (layer, 0, 0))],
        out_specs=pl.BlockSpec((tm, D_MODEL), row),
        out_shape=jax.ShapeDtypeStruct((T, D_MODEL), F32),
        scratch_shapes=[pltpu.VMEM((4 * (BRANCH_W // LANE), tm, LANE), F32)],
        compiler_params=_cparams(("parallel",)),
        name="epilogue",
    )(x2, ng, ya, yc, yd, *obs, *lses, wz, wmg, wbr, wout)


TOK_AK, TOK_BQ, TOK_BK, TOK_BV, TOK_CK, TOK_DKV = range(6)
N_MAIN = 6
T_AQ, T_AV, T_CQ, T_CV, T_DQ, T_DX = range(6)
SCALE = 1.0 / math.sqrt(HEAD_DIM)
QSCALE2 = SCALE * LOG2E

_PLAIN = (False, False, 1.0, False)
_T_KINDS = (
    ((False, False, QSCALE2, False),) * 4,
    (_PLAIN,) * 4,
    ((True, True, QSCALE2, False),) * 4,
    (_PLAIN,) * 4,
    ((True, True, QSCALE2, False),) * 4,
    (_PLAIN, _PLAIN, (False, False, 1.0, True), _PLAIN),
)


def _tok_specs():
    main = lambda j: (0, 1, j * CH, CH, 0)
    qkv = ((True, True), (True, True), (False, False))
    specs = []
    for j in range(N_MAIN):
        if j in (TOK_BQ, TOK_BK, TOK_CK):
            specs.append((True, True, (main(j),)))
        elif j == TOK_DKV:
            specs.append(("mixed", "mixed", (main(j), (3, NSA_CMP_STRIDE, 0, LANE, LANE))))
        else:
            specs.append((False, False, (main(j),)))
    for gi in (1, 2):
        dil = DIL_PAIRS[gi][1]
        for k, (nrm, rope) in enumerate(qkv):
            specs.append((nrm, rope, ((gi, dil, k * CH, CH, 3 * CH),)))
    return tuple(specs)


def _rope_tables(S, dil):
    L = S // dil
    pos = (jnp.arange(dil, dtype=jnp.int32)[:, None] + dil * jnp.arange(L, dtype=jnp.int32)[None, :]).astype(F32)
    inv = ROPE_THETA ** (-jnp.arange(0, ROT_DIM, 2, dtype=F32) / ROT_DIM)
    ang = pos[:, :, None] * inv[None, None, :]
    cos, sin = jnp.cos(ang), jnp.sin(ang)
    one = jnp.ones((dil, L, HEAD_DIM - ROT_DIM), F32)
    zero8 = jnp.zeros((dil, L, 8), F32)
    zero = jnp.zeros_like(one)
    c_head = jnp.concatenate([cos, cos, one], axis=-1)
    s1_head = jnp.concatenate([zero8, sin, zero], axis=-1)
    s2_head = jnp.concatenate([-sin, zero8, zero], axis=-1)
    rtok = jnp.stack([jnp.tile(t, (1, 1, LANE // HEAD_DIM)) for t in (c_head, s1_head, s2_head)], axis=1)
    rT = jnp.stack([cos.transpose(0, 2, 1), sin.transpose(0, 2, 1)], axis=1)
    return rtok, rT


def _tok_params(rows):
    out = []
    for nf, rf, gains in rows:
        nrow = jnp.concatenate([jnp.full((HEAD_DIM,), float(f), F32) for f in nf])
        rrow = jnp.concatenate([jnp.full((HEAD_DIM,), float(f), F32) for f in rf])
        grow = jnp.concatenate([g.astype(F32) for g in gains])
        out.append(jnp.concatenate([jnp.stack([nrow, rrow, grow]), jnp.zeros((5, CH), F32)], axis=0))
    return jnp.stack(out)


def kernel(x, norm_g, w_in, qk_g, cmp_pe, cmp_w1, cmp_w2, w_branch, w_out):
    B, S, _ = x.shape
    T = B * S
    o = COL_OFF
    ones = jnp.ones((HEAD_DIM,), F32)

    r = np.arange(CH)
    G = jnp.asarray((r[:, None] // HEAD_DIM == r[None, :] // HEAD_DIM).astype(np.float32), BF16)
    G128 = G[:LANE, :LANE]
    t = np.arange(256)
    U = jnp.asarray((t[None, :] > t[:, None]).astype(np.float32), BF16)
    nblk = S // MOBA_BLOCK
    A = jnp.asarray(np.repeat(np.eye(nblk, dtype=np.float32), MOBA_BLOCK, axis=1) / MOBA_BLOCK, BF16)
    nc, nsel = S // NSA_CMP_STRIDE, S // NSA_SEL_BLOCK
    cs = np.arange(nc) * NSA_CMP_STRIDE
    ss = np.arange(nsel) * NSA_SEL_BLOCK
    ov = ((cs[None, :] < ss[:, None] + NSA_SEL_BLOCK) & (cs[None, :] + NSA_CMP_LEN > ss[:, None]))
    ov[:, nc - 1] = False
    ovT = jnp.asarray(ov.astype(np.float32), BF16)
    hid = np.arange(LANE)[:, None]
    ind = jnp.asarray(((hid < 4) & (r[None, :] // HEAD_DIM == hid)).astype(np.float32), BF16)
    rtok4, rT4 = _rope_tables(S, 1)
    rtok, rT = rtok4[0], rT4[0]
    tok_specs = _tok_specs()

    dkv = o[13]
    bq = lambda gi: (o[4] + gi * CH, o[4] + (gi + 1) * CH)
    bk = lambda gi: (o[5] + gi * CH, o[5] + (gi + 1) * CH)
    bv = lambda gi: (o[6] + gi * CH, o[6] + (gi + 1) * CH)
    tok_plan = ((o[1], o[2]), bq(0), bk(0), bv(0), (o[9], o[10]),
                (dkv, dkv + 64), (dkv + 64, dkv + 128), (dkv + 128, dkv + 192), (dkv + 256, dkv + 320),
                bq(1), bk(1), bv(1), bq(2), bk(2), bv(2))
    T_plan = ((o[0], o[1]), (o[2], o[3]), (o[8], o[9]), (o[10], o[11]), (o[12], o[13]),
              (dkv + 192, dkv + 256), (dkv + 320, dkv + 384), (o[14], o[15]))
    z_plan = ((o[3], o[4]), (o[7], o[8]), (o[11], o[12]), (o[15], o[16]))
    wtok_all, wT_all, wz_all, wmg_all = _repack_call(
        w_in, (tok_plan, T_plan, z_plan, ((o[16], o[17]),)),
        (len(tok_specs) * CH, len(_T_KINDS) * CH, 4 * BRANCH_W, 4 * D_MODEL), (False, True, False, False))

    wbr_all = w_branch.astype(BF16)
    wout_all = w_out.astype(BF16)

    for l in range(DEPTH):
        g = qk_g[l]
        plain = ((0,) * 4, (0,) * 4, (ones,) * 4)
        bq_p = ((1,) * 4, (1,) * 4, (g[0] * QSCALE2,) * 4)
        bk_p = ((1,) * 4, (1,) * 4, (g[1],) * 4)
        tokp = _tok_params([
            plain, bq_p, bk_p, plain,
            ((1,) * 4, (1,) * 4, (g[3],) * 4),
            ((0, 0, 1, 1), (1, 0, 1, 1), (ones, ones, g[6], g[7])),
            bq_p, bk_p, plain, bq_p, bk_p, plain])
        gT = jnp.concatenate([jnp.tile(ones, 8), jnp.tile(g[2], 4), jnp.tile(ones, 4), jnp.tile(g[4], 4),
                              jnp.tile(ones, 4)])[:, None]
        ng = norm_g[l][None, :]
        otok, og1, og2, okv, oT = _proj_call(x, ng, wtok_all, wT_all, tokp, gT, rtok, rT, G, layer=l,
                                             tok_specs=tok_specs, T_kinds=_T_KINDS, n_main=N_MAIN)

        ya = _sb_call(oT, otok, U, q_chunk=T_AQ, k_chunk=TOK_AK, v_chunk=T_AV)

        obs, lses = [], []
        for gi, (window, dil) in enumerate(DIL_PAIRS):
            if dil == 1:
                ob, lse = _band_call(otok, ind, dil=1, per_res=N_MAIN, qi=TOK_BQ, ki=TOK_BK, vi=TOK_BV,
                                     max_dist=window)
            else:
                ob, lse = _band_call((og1, og2)[gi - 1], ind, dil=dil, per_res=3, qi=0, ki=1, vi=2,
                                     max_dist=window // dil)
            obs.append(ob.reshape(T // dil, dil * BRANCH_W))
            lses.append(lse.reshape(T // dil, dil * BRANCH_W))

        yc = _moba_call(oT, otok, A, q_chunk=T_CQ, k_chunk=TOK_CK, v_chunk=T_CV)

        w1 = cmp_w1[l].reshape(2, NSA_CMP_LEN, HEAD_DIM, HEAD_DIM)
        z64 = jnp.zeros((NSA_CMP_STRIDE, HEAD_DIM, HEAD_DIM), F32)
        w1blk = jnp.concatenate([
            jnp.concatenate([w1[0, :16], w1[0, 16:], z64, z64], axis=2),
            jnp.concatenate([z64, z64, w1[1, :16], w1[1, 16:]], axis=2)], axis=1)
        pe2 = jnp.concatenate([cmp_pe[l].reshape(1, -1), jnp.zeros((7, 2 * NSA_CMP_LEN * HEAD_DIM), F32)], axis=0)
        zf = jnp.zeros((NSA_CMP_LEN * HEAD_DIM, HEAD_DIM), F32)
        w1f = jnp.concatenate([jnp.concatenate([cmp_w1[l, 0], zf], axis=1),
                               jnp.concatenate([zf, cmp_w1[l, 1]], axis=1)], axis=0)
        z2 = jnp.zeros((HEAD_DIM, HEAD_DIM), F32)
        w2blk = jnp.concatenate([jnp.concatenate([cmp_w2[l, 0], z2], axis=1),
                                 jnp.concatenate([z2, cmp_w2[l, 1]], axis=1)], axis=0)
        gk = jnp.concatenate([g[5], ones])[None, :]
        kc, kcT = _compress_call(okv, w1blk, pe2, w1f, w2blk, gk, G128)
        yd = _nsa_call(oT, otok, kc, kcT, ovT, q_chunk=T_DQ, kv_chunk=TOK_DKV, v_chunk=T_DX)

        x2 = _epi_call(x.reshape(T, D_MODEL), ng, ya.reshape(T, BRANCH_W), yc.reshape(T, BRANCH_W),
                       yd.reshape(T, BRANCH_W), obs, lses, wz_all, wmg_all, wbr_all, wout_all, layer=l)
        x = x2.reshape(B, S, D_MODEL)
    return x
```

```python
import functools
import math

import numpy as np
import jax
import jax.numpy as jnp
from jax import lax
from jax.experimental import pallas as pl
from jax.experimental.pallas import tpu as pltpu

F32 = jnp.float32
BF16 = jnp.bfloat16

D_MODEL = 1024
DEPTH = 4
HEAD_DIM = 64
ROT_DIM = 16
ROPE_THETA = 500000.0
EPS = 1e-6
NEG = -1e30
BIG = 1e9
BRANCH_W = 256
DIL_PAIRS = ((128, 1), (512, 4), (2048, 16))
MOBA_BLOCK = 256
MOBA_TOPK = 3
NSA_CMP_LEN = 32
NSA_CMP_STRIDE = 16
NSA_SEL_BLOCK = 64
NSA_SEL_TOPK = 16
NSA_WINDOW = 512
COL_SIZES = (256, 256, 256, 256, 768, 768, 768, 256, 256, 256, 256, 256, 256, 384, 12, 256, 4096)
COL_OFF = tuple(int(v) for v in np.concatenate([[0], np.cumsum(COL_SIZES)]))

LANE = 128
CH = 256
VMEM_LIMIT = 56 * 1024 * 1024


def _cparams(sem):
    return pltpu.CompilerParams(dimension_semantics=sem, vmem_limit_bytes=VMEM_LIMIT)


def _dot(a, b):
    return jnp.dot(a, b, preferred_element_type=F32)


def _dot_nt(a, b):
    return lax.dot_general(a, b, (((1,), (1,)), ((), ())), preferred_element_type=F32)


def _split2(x):
    hi = x.astype(BF16)
    lo = (x - hi.astype(F32)).astype(BF16)
    return hi, lo


def _split3(x):
    hi = x.astype(BF16)
    r = x - hi.astype(F32)
    mid = r.astype(BF16)
    lo = (r - mid.astype(F32)).astype(BF16)
    return hi, mid, lo


def _repack_kernel(w_ref, *out_refs, plans):
    for o_ref, ranges in zip(out_refs, plans):
        width = o_ref.shape[-1]
        parts = [w_ref[:, a:b] for a, b in ranges]
        used = sum(b - a for a, b in ranges)
        if used < width:
            parts.append(jnp.zeros((w_ref.shape[0], width - used), F32))
        o_ref[...] = (parts[0] if len(parts) == 1 else jnp.concatenate(parts, axis=1)).astype(o_ref.dtype)


def _repack_call(w_in, plans, widths, rows=128):
    depth, d, n_in = w_in.shape
    kern = functools.partial(_repack_kernel, plans=plans)
    return pl.pallas_call(
        kern,
        grid=(depth, d // rows),
        in_specs=[pl.BlockSpec((None, rows, n_in), lambda l, i: (l, i, 0))],
        out_specs=[pl.BlockSpec((None, rows, w), lambda l, i: (l, i, 0)) for w in widths],
        out_shape=[jax.ShapeDtypeStruct((depth, d, w), BF16) for w in widths],
        compiler_params=_cparams(("parallel", "parallel")),
        name="repack",
    )(w_in)


def _proj_kernel(x_ref, ng_ref, wtok_ref, wT_ref, tokp_ref, gT_ref, rtok_ref, rT_ref, G_ref,
                 omain_ref, og1_ref, og2_ref, okv_ref, oT_ref, stage_ref, *, tok_specs, T_kinds, tm):
    x = x_ref[...]
    ms = jnp.mean(x * x, axis=-1, keepdims=True)
    xn = (x * lax.rsqrt(ms + EPS) * ng_ref[...]).astype(BF16)

    dsts = (omain_ref, og1_ref, og2_ref, okv_ref)
    for c, (has_norm, has_rope, outs) in enumerate(tok_specs):
        y = _dot(xn, wtok_ref[:, c * CH:(c + 1) * CH])
        prm = tokp_ref[c]
        if has_norm:
            ss = _dot((y * y).astype(BF16), G_ref[...])
            inv = lax.rsqrt(ss * (1.0 / HEAD_DIM) + EPS)
            y = y * (jnp.where(prm[0:1, :] > 0.0, inv, 1.0) if has_norm == "mixed" else inv)
        y = y * prm[2:3, :]
        if has_rope:
            halves = []
            for hf in range(CH // LANE):
                yh = y[:, hf * LANE:(hf + 1) * LANE]
                cc, s1, s2 = rtok_ref[0], rtok_ref[1], rtok_ref[2]
                if has_rope == "mixed":
                    rf = prm[1:2, hf * LANE:(hf + 1) * LANE]
                    cc, s1, s2 = jnp.where(rf > 0.0, cc, 1.0), s1 * rf, s2 * rf
                halves.append(yh * cc + pltpu.roll(yh, 8, 1) * s1 + pltpu.roll(yh, LANE - 8, 1) * s2)
            y = jnp.concatenate(halves, axis=1)
        staged = False
        for dst, dil, col, lanes, rstride in outs:
            if dil == 1:
                dsts[dst][:, col:col + lanes] = y[:, :lanes].astype(BF16)
                continue
            if not staged:
                for hf in range(CH // LANE):
                    stage_ref[hf] = y[:, hf * LANE:(hf + 1) * LANE]
                staged = True
            for r in range(dil):
                for hf in range(lanes // LANE):
                    blk = stage_ref[hf, pl.ds(r, tm // dil, stride=dil), :]
                    c0 = r * rstride + col + hf * LANE
                    dsts[dst][:, c0:c0 + LANE] = blk.astype(BF16)

    cosT = rT_ref[0]
    sinT = rT_ref[1]
    for c, heads in enumerate(T_kinds):
        y = _dot(xn, wT_ref[:, c * CH:(c + 1) * CH]).T
        for h, (nrm, rope, scale, sigm) in enumerate(heads):
            r0 = c * CH + h * HEAD_DIM
            yh = y[h * HEAD_DIM:(h + 1) * HEAD_DIM, :]
            if nrm:
                msq = jnp.mean(yh * yh, axis=0, keepdims=True)
                yh = yh * lax.rsqrt(msq + EPS) * gT_ref[r0:r0 + HEAD_DIM, :]
            if rope:
                x1 = yh[0:8, :]
                x2 = yh[8:16, :]
                yh = jnp.concatenate([x1 * cosT - x2 * sinT, x2 * cosT + x1 * sinT, yh[16:, :]], axis=0)
            if scale != 1.0:
                yh = yh * scale
            if sigm:
                yh = jax.nn.sigmoid(yh)
            yb = yh.astype(BF16)
            for t in range(tm // LANE):
                oT_ref[t, r0:r0 + HEAD_DIM, :] = yb[:, t * LANE:(t + 1) * LANE]


def _proj_call(x, ng, wtok, wT, tokp, gT, rtok, rT, G, *, layer, tok_specs, T_kinds, n_main, tm=512):
    B, S, _ = x.shape
    ntok, nT = len(tok_specs), len(T_kinds)
    d1, d2 = DIL_PAIRS[1][1], DIL_PAIRS[2][1]
    kern = functools.partial(_proj_kernel, tok_specs=tok_specs, T_kinds=T_kinds, tm=tm)
    const2 = lambda b, n: (0, 0)
    return pl.pallas_call(
        kern,
        grid=(B, S // tm),
        in_specs=[
            pl.BlockSpec((None, tm, D_MODEL), lambda b, n: (b, n, 0)),
            pl.BlockSpec((1, D_MODEL), const2),
            pl.BlockSpec((None, D_MODEL, ntok * CH), lambda b, n: (layer, 0, 0)),
            pl.BlockSpec((None, D_MODEL, nT * CH), lambda b, n: (layer, 0, 0)),
            pl.BlockSpec((ntok, 8, CH), lambda b, n: (0, 0, 0)),
            pl.BlockSpec((nT * CH, 1), const2),
            pl.BlockSpec((3, tm, LANE), lambda b, n: (0, n, 0)),
            pl.BlockSpec((2, 8, tm), lambda b, n: (0, 0, n)),
            pl.BlockSpec((CH, CH), const2),
        ],
        out_specs=[
            pl.BlockSpec((None, tm, n_main * CH), lambda b, n: (b, n, 0)),
            pl.BlockSpec((None, tm // d1, d1 * 3 * CH), lambda b, n: (b, n, 0)),
            pl.BlockSpec((None, tm // d2, d2 * 3 * CH), lambda b, n: (b, n, 0)),
            pl.BlockSpec((None, tm // NSA_CMP_STRIDE, NSA_CMP_STRIDE * LANE), lambda b, n: (b, n, 0)),
            pl.BlockSpec((None, tm // LANE, nT * CH, LANE), lambda b, n: (b, n, 0, 0)),
        ],
        out_shape=[
            jax.ShapeDtypeStruct((B, S, n_main * CH), BF16),
            jax.ShapeDtypeStruct((B, S // d1, d1 * 3 * CH), BF16),
            jax.ShapeDtypeStruct((B, S // d2, d2 * 3 * CH), BF16),
            jax.ShapeDtypeStruct((B, S // NSA_CMP_STRIDE, NSA_CMP_STRIDE * LANE), BF16),
            jax.ShapeDtypeStruct((B, S // LANE, nT * CH, LANE), BF16),
        ],
        scratch_shapes=[pltpu.VMEM((CH // LANE, tm, LANE), F32)],
        compiler_params=_cparams(("parallel", "parallel")),
        name="proj",
    )(x, ng, wtok, wT, tokp, gT, rtok, rT, G)


def _pair_masked_q(q_pair, h):
    rid = lax.broadcasted_iota(jnp.int32, q_pair.shape, 0)
    lo = (h % 2) * HEAD_DIM
    keep = jnp.where(rid >= lo, jnp.where(rid < lo + HEAD_DIM, 1.0, 0.0), 0.0).astype(BF16)
    return q_pair * keep


def _lane_tiles(ref, t0, nt, r0, nr):
    return jnp.concatenate([ref[t0 + t, r0:r0 + nr, :] for t in range(nt)], axis=1)


BIGPOS = 1e30
ONES_ROWS = 16
HA = HEAD_DIM + ONES_ROWS
LOG2E = 1.4426950408889634


def _v_aug(vb):
    return jnp.concatenate([vb, jnp.ones((ONES_ROWS, vb.shape[1]), BF16)], axis=0)


def _online_cols(s, colsel, m_old, acc_old, vb_aug):
    tmax = jnp.max(s, axis=0, keepdims=True)
    m_new = jnp.where(colsel > 0.0, jnp.maximum(m_old, tmax), m_old)
    m_use = jnp.where(colsel > 0.0, m_new, BIGPOS)
    p = jnp.exp2(s - m_use).astype(BF16)
    acc = jnp.exp2(m_old - m_new) * acc_old + _dot(vb_aug, p)
    return m_new, acc


SB_LOG_CUTOFF = -144.0


def _sb_kernel(qT_ref, k_ref, vT_ref, U_ref, o_ref, qm_ref, acc_ref, carry_ref, *, tq):
    i = pl.program_id(1)
    nt = tq // LANE
    row = lax.broadcasted_iota(jnp.int32, (tq, tq), 0)
    col = lax.broadcasted_iota(jnp.int32, (tq, tq), 1)
    past = row < col
    for h in range(4):
        p = h // 2
        qm_ref[h] = _pair_masked_q(_lane_tiles(qT_ref, 0, nt, p * LANE, LANE), h)
    acc_ref[...] = jnp.zeros_like(acc_ref)
    carry_ref[...] = jnp.zeros_like(carry_ref)

    def tile(j, masked):
        scores, logsig, laters = [], [], []
        for h in range(4):
            p = h // 2
            kb = k_ref[pl.ds(pl.multiple_of(j * tq, tq), tq), p * LANE:(p + 1) * LANE]
            scores.append(_dot(kb, qm_ref[h]))
        worst = None
        for h in range(4):
            s = scores[h]
            sp = jnp.maximum(s, 0.0) + jnp.log2(1.0 + jnp.exp2(-jnp.abs(s)))
            lg = -sp
            if masked:
                lg = jnp.where(past, lg, 0.0)
            hi, lo = _split2(lg)
            carry = carry_ref[h:h + 1, :]
            laters.append(_dot(U_ref[...], hi) + _dot(U_ref[...], lo) + carry)
            logsig.append(s - sp)
            carry = carry + jnp.sum(lg, axis=0, keepdims=True)
            carry_ref[h:h + 1, :] = carry
            worst = carry if worst is None else jnp.maximum(worst, carry)
        for h in range(4):
            w = jnp.exp2(logsig[h] + laters[h])
            if masked:
                w = jnp.where(past, w, 0.0)
            vb = _lane_tiles(vT_ref, j * nt, nt, h * HEAD_DIM, HEAD_DIM)
            acc_ref[h * HEAD_DIM:(h + 1) * HEAD_DIM, :] += _dot(vb, w.astype(BF16))
        return jnp.max(worst)

    worst0 = tile(i, True)

    def cond(st):
        return jnp.logical_and(st[0] >= 0, st[1] > SB_LOG_CUTOFF)

    def body(st):
        return st[0] - 1, tile(st[0], False)

    lax.while_loop(cond, body, (i - 1, worst0))
    o_ref[...] = acc_ref[...].T.astype(o_ref.dtype)


def _attn_specs(S, tq, q_chunk, k_chunk, v_chunk):
    return [
        pl.BlockSpec((None, tq // LANE, CH, LANE), lambda b, i: (b, i, q_chunk, 0)),
        pl.BlockSpec((None, S, CH), lambda b, i: (b, 0, k_chunk)),
        pl.BlockSpec((None, S // LANE, CH, LANE), lambda b, i: (b, 0, v_chunk, 0)),
    ]


def _sb_call(oT, otok, U, *, q_chunk, k_chunk, v_chunk, tq=256):
    B, nlt, _, _ = oT.shape
    S = nlt * LANE
    kern = functools.partial(_sb_kernel, tq=tq)
    return pl.pallas_call(
        kern,
        grid=(B, S // tq),
        in_specs=_attn_specs(S, tq, q_chunk, k_chunk, v_chunk) + [pl.BlockSpec((tq, tq), lambda b, i: (0, 0))],
        out_specs=pl.BlockSpec((None, tq, BRANCH_W), lambda b, i: (b, i, 0)),
        out_shape=jax.ShapeDtypeStruct((B, S, BRANCH_W), BF16),
        scratch_shapes=[pltpu.VMEM((4, LANE, tq), BF16), pltpu.VMEM((CH, tq), F32), pltpu.VMEM((8, tq), F32)],
        compiler_params=_cparams(("parallel", "arbitrary")),
        name="stick_breaking",
    )(oT, otok, oT, U)


def _head_lanes(x, lane, h):
    lo = h * HEAD_DIM
    return jnp.where(lane >= lo, jnp.where(lane < lo + HEAD_DIM, x, jnp.zeros_like(x)), jnp.zeros_like(x))


def _band_kernel(q_ref, k_ref, v_ref, ind_ref, o_ref, lse_ref, *, tqb, max_dist):
    n = pl.program_id(2)
    n_prev = -(-max_dist // LANE)
    nkr = (n_prev + 1) * LANE
    nsub = tqb // LANE
    row = lax.broadcasted_iota(jnp.int32, (nkr, LANE), 0)
    col = lax.broadcasted_iota(jnp.int32, (nkr, LANE), 1)
    lane_q = lax.broadcasted_iota(jnp.int32, (LANE, LANE), 1)
    lane_v = lax.broadcasted_iota(jnp.int32, (nkr, CH), 1)
    for u in range(nsub):
        qt = n * nsub + u
        kt0 = jnp.maximum(qt - n_prev, 0)
        dist = (qt - kt0) * LANE + col - row
        bias = jnp.where(dist >= 0, jnp.where(dist <= max_dist, 0.0, NEG), NEG)
        k0 = pl.multiple_of(kt0 * LANE, LANE)
        kwin = k_ref[pl.ds(k0, nkr), :]
        vwin = v_ref[pl.ds(k0, nkr), :]
        qu = q_ref[u * LANE:(u + 1) * LANE, :]
        o_acc = None
        lses = []
        for h in range(4):
            p = h // 2
            qm = _head_lanes(qu[:, p * LANE:(p + 1) * LANE], lane_q, h % 2)
            s = _dot_nt(kwin[:, p * LANE:(p + 1) * LANE], qm) + bias
            m = jnp.max(s, axis=0, keepdims=True)
            e = jnp.exp2(s - m)
            den = jnp.sum(e, axis=0, keepdims=True)
            pn = e * (1.0 / den)
            contrib = _dot(pn.T.astype(BF16), _head_lanes(vwin, lane_v, h))
            o_acc = contrib if o_acc is None else o_acc + contrib
            lses.append(m + jnp.log(den) * LOG2E)
        o_ref[u * LANE:(u + 1) * LANE, :] = o_acc.astype(o_ref.dtype)
        ls = jnp.concatenate(lses + [jnp.zeros((LANE - 4, LANE), F32)], axis=0).T
        l1, l2, l3 = _split3(ls)
        lse_ref[u * LANE:(u + 1) * LANE, :] = (_dot(l1, ind_ref[...]) + _dot(l2, ind_ref[...])
                                               + _dot(l3, ind_ref[...]))


def _band_call(arr, ind, *, dil, per_res, qi, ki, vi, max_dist):
    B, L, _ = arr.shape
    tqb = min(1024, L)
    kern = functools.partial(_band_kernel, tqb=tqb, max_dist=max_dist)
    return pl.pallas_call(
        kern,
        grid=(B, dil, L // tqb),
        in_specs=[
            pl.BlockSpec((None, tqb, CH), lambda b, r, n: (b, n, r * per_res + qi)),
            pl.BlockSpec((None, L, CH), lambda b, r, n: (b, 0, r * per_res + ki)),
            pl.BlockSpec((None, L, CH), lambda b, r, n: (b, 0, r * per_res + vi)),
            pl.BlockSpec((LANE, CH), lambda b, r, n: (0, 0)),
        ],
        out_specs=[
            pl.BlockSpec((None, tqb, BRANCH_W), lambda b, r, n: (b, n, r)),
            pl.BlockSpec((None, tqb, BRANCH_W), lambda b, r, n: (b, n, r)),
        ],
        out_shape=[
            jax.ShapeDtypeStruct((B, L, dil * BRANCH_W), BF16),
            jax.ShapeDtypeStruct((B, L, dil * BRANCH_W), F32),
        ],
        compiler_params=_cparams(("parallel", "parallel", "parallel")),
        name="banded",
    )(arr, arr, arr, ind)


def _moba_kernel(qT_ref, k_ref, vT_ref, A_ref, o_ref, kmean_ref, sel_ref, qm_ref, acc_ref, ml_ref, s_ref, *, tq):
    i = pl.program_id(1)
    nt = tq // LANE
    nblk = A_ref.shape[0]

    @pl.when(i == 0)
    def _():
        kmean_ref[...] = _dot(A_ref[...], k_ref[...])

    blk = lax.broadcasted_iota(jnp.int32, (nblk, tq), 0)
    row = lax.broadcasted_iota(jnp.int32, (tq, tq), 0)
    col = lax.broadcasted_iota(jnp.int32, (tq, tq), 1)
    gates, diag = [], []
    for h in range(4):
        p = h // 2
        qm = _pair_masked_q(_lane_tiles(qT_ref, 0, nt, p * LANE, LANE), h)
        qm_ref[h] = qm
        k1, k2, k3 = _split3(kmean_ref[:, p * LANE:(p + 1) * LANE])
        gates.append(_dot(k1, qm) + _dot(k2, qm) + _dot(k3, qm))
        kb = k_ref[pl.ds(pl.multiple_of(i * tq, tq), tq), p * LANE:(p + 1) * LANE]
        diag.append(_dot(kb, qm))
    for h in range(4):
        g = jnp.where(blk < i, gates[h], NEG)
        sel = jnp.zeros((nblk, tq), F32)
        for _r in range(MOBA_TOPK):
            mx = jnp.max(g, axis=0, keepdims=True)
            idx = jnp.min(jnp.where(g == mx, blk, nblk), axis=0, keepdims=True)
            hit = blk == idx
            sel = jnp.where(hit, 1.0, sel)
            g = jnp.where(hit, -jnp.inf, g)
        sel_ref[h] = jnp.where(blk < i, sel, 0.0)

        s = jnp.where(row <= col, diag[h], NEG)
        m = jnp.max(s, axis=0, keepdims=True)
        ml_ref[h:h + 1, :] = m
        acc_ref[h * HA:(h + 1) * HA, :] = _dot(
            _v_aug(_lane_tiles(vT_ref, i * nt, nt, h * HEAD_DIM, HEAD_DIM)), jnp.exp2(s - m).astype(BF16))

    last = jnp.maximum(i - 1, 0)

    def qk(t, slot):
        for h in range(4):
            p = h // 2
            kbj = k_ref[pl.ds(pl.multiple_of(t * tq, tq), tq), p * LANE:(p + 1) * LANE]
            s_ref[slot, h] = _dot(kbj, qm_ref[h])

    def update(t, valid, slot):
        for h in range(4):
            srow = sel_ref[h, pl.ds(t, 1), :] * valid
            vb = _v_aug(_lane_tiles(vT_ref, t * nt, nt, h * HEAD_DIM, HEAD_DIM))
            rows = slice(h * HA, (h + 1) * HA)
            m, acc = _online_cols(s_ref[slot, h], srow, ml_ref[h:h + 1, :], acc_ref[rows, :], vb)
            ml_ref[h:h + 1, :] = m
            acc_ref[rows, :] = acc

    qk(0, 0)

    def body(jj, c):
        t0 = 2 * jj
        qk(jnp.minimum(t0 + 1, last), 1)
        update(t0, 1.0, 0)
        qk(jnp.minimum(t0 + 2, last), 0)
        update(jnp.minimum(t0 + 1, last), (t0 + 1 < i).astype(F32), 1)
        return c

    lax.fori_loop(0, (i + 1) // 2, body, 0)
    outs = [acc_ref[h * HA:h * HA + HEAD_DIM, :] / acc_ref[h * HA + HEAD_DIM:h * HA + HEAD_DIM + 1, :]
            for h in range(4)]
    o_ref[...] = jnp.concatenate(outs, axis=0).T.astype(o_ref.dtype)


def _moba_call(oT, otok, A, *, q_chunk, k_chunk, v_chunk):
    B, nlt, _, _ = oT.shape
    S = nlt * LANE
    tq = MOBA_BLOCK
    nblk = S // MOBA_BLOCK
    kern = functools.partial(_moba_kernel, tq=tq)
    return pl.pallas_call(
        kern,
        grid=(B, S // tq),
        in_specs=_attn_specs(S, tq, q_chunk, k_chunk, v_chunk) + [pl.BlockSpec((nblk, S), lambda b, i: (0, 0))],
        out_specs=pl.BlockSpec((None, tq, BRANCH_W), lambda b, i: (b, i, 0)),
        out_shape=jax.ShapeDtypeStruct((B, S, BRANCH_W), BF16),
        scratch_shapes=[pltpu.VMEM((nblk, CH), F32), pltpu.VMEM((4, nblk, tq), F32),
                        pltpu.VMEM((4, LANE, tq), BF16), pltpu.VMEM((4 * HA, tq), F32), pltpu.VMEM((8, tq), F32),
                        pltpu.VMEM((2, 4, tq, tq), F32)],
        compiler_params=_cparams(("parallel", "arbitrary")),
        name="moba",
    )(oT, otok, oT, A)


def _compress_kernel(x_ref, w1_ref, pe_ref, w1f_ref, w2_ref, gk_ref, G_ref, o_ref, oT_ref, acc_ref):
    l = pl.program_id(1)

    @pl.when(l == 0)
    def _():
        acc_ref[...] = jnp.zeros_like(acc_ref)

    x = x_ref[...]
    w_hi, w_lo = _split2(w1_ref[...])
    acc_ref[...] += _dot(x, w_hi) + _dot(x, w_lo)

    @pl.when(l == pl.num_programs(1) - 1)
    def _():
        nc = acc_ref.shape[0]
        r = acc_ref[...]
        bias = jnp.dot(pe_ref[...], w1f_ref[...], preferred_element_type=F32,
                       precision=lax.Precision.HIGHEST)
        p1 = jnp.concatenate([r[:, 0:64], r[:, 128:192]], axis=1)
        p2 = jnp.concatenate([r[:, 64:128], r[:, 192:256]], axis=1)
        hid = p1 + pltpu.roll(p2, nc - 1, 0) + bias[0:1, :]
        hid = hid * jax.nn.sigmoid(hid)
        comp = jnp.dot(hid, w2_ref[...], preferred_element_type=F32, precision=lax.Precision.HIGHEST)
        ss = _dot((comp * comp).astype(BF16), G_ref[...])
        lane = lax.broadcasted_iota(jnp.int32, comp.shape, 1)
        inv = jnp.where(lane < HEAD_DIM, lax.rsqrt(ss * (1.0 / HEAD_DIM) + EPS), 1.0)
        comp = comp * inv * gk_ref[...]
        rowi = lax.broadcasted_iota(jnp.int32, comp.shape, 0)
        comp = jnp.where(rowi < nc - 1, comp, 0.0)
        o_ref[...] = comp.astype(o_ref.dtype)
        oT_ref[...] = comp.T.astype(oT_ref.dtype)


def _compress_call(xv, w1blk, pe2, w1f, w2blk, gk, G128):
    B, nc, _ = xv.shape
    return pl.pallas_call(
        _compress_kernel,
        grid=(B, NSA_CMP_STRIDE),
        in_specs=[
            pl.BlockSpec((None, nc, LANE), lambda b, l: (b, 0, l)),
            pl.BlockSpec((None, LANE, CH), lambda b, l: (l, 0, 0)),
            pl.BlockSpec((8, 2 * NSA_CMP_LEN * HEAD_DIM), lambda b, l: (0, 0)),
            pl.BlockSpec((2 * NSA_CMP_LEN * HEAD_DIM, LANE), lambda b, l: (0, 0)),
            pl.BlockSpec((LANE, LANE), lambda b, l: (0, 0)),
            pl.BlockSpec((1, LANE), lambda b, l: (0, 0)),
            pl.BlockSpec((LANE, LANE), lambda b, l: (0, 0)),
        ],
        out_specs=[
            pl.BlockSpec((None, nc, LANE), lambda b, l: (b, 0, 0)),
            pl.BlockSpec((None, LANE, nc), lambda b, l: (b, 0, 0)),
        ],
        out_shape=[
            jax.ShapeDtypeStruct((B, nc, LANE), BF16),
            jax.ShapeDtypeStruct((B, LANE, nc), BF16),
        ],
        scratch_shapes=[pltpu.VMEM((nc, CH), F32)],
        compiler_params=_cparams(("parallel", "arbitrary")),
        name="nsa_compress",
    )(xv, w1blk, pe2, w1f, w2blk, gk, G128)


def _nsa_kernel(qT_ref, k_ref, vT_ref, kc_ref, kcT_ref, ov_ref, o_ref, sel_ref, qm_ref, acc_ref, ml_ref, s_ref,
                win_ref, *, tq):
    i = pl.program_id(1)
    nt = tq // LANE
    nq = 4 * tq
    nc = kc_ref.shape[0]
    nsel = ov_ref.shape[0]
    spb = tq // NSA_SEL_BLOCK

    zeros = jnp.zeros((HEAD_DIM, tq), BF16)
    q_heads = [_lane_tiles(qT_ref, 0, nt, h * HEAD_DIM, HEAD_DIM) for h in range(4)]
    qm_lo = jnp.concatenate([jnp.concatenate([q, zeros], axis=0) for q in q_heads], axis=1)
    qm_hi = jnp.concatenate([jnp.concatenate([zeros, q], axis=0) for q in q_heads], axis=1)
    qpos = i * tq + lax.broadcasted_iota(jnp.int32, (1, tq), 1)
    qpos4 = jnp.concatenate([qpos] * 4, axis=1)

    n_prev = -(-(NSA_WINDOW - 1) // tq)
    nk = n_prev + 1
    kt0 = jnp.maximum(i - n_prev, 0)
    kw = k_ref[pl.ds(pl.multiple_of(kt0 * tq, tq), nk * tq), LANE:2 * LANE]
    sw = _dot(kw, qm_hi)
    dist = qpos - (kt0 * tq + lax.broadcasted_iota(jnp.int32, (nk * tq, tq), 0))
    wbias = jnp.where(dist >= 0, jnp.where(dist <= NSA_WINDOW - 1, 0.0, NEG), NEG)
    sw = sw + jnp.concatenate([wbias] * 4, axis=1)
    mw = jnp.max(sw, axis=0, keepdims=True)
    ew = jnp.exp2(sw - mw)
    dw = jnp.sum(ew, axis=0, keepdims=True)
    vwin = _lane_tiles(vT_ref, kt0 * nt, nk * nt, HEAD_DIM, HEAD_DIM)
    win_ref[...] = _dot(vwin, ew.astype(BF16)) / dw

    zc = _dot(kc_ref[...], qm_lo)
    c_end = lax.broadcasted_iota(jnp.int32, (nc, tq), 0) * NSA_CMP_STRIDE + (NSA_CMP_LEN - 1)
    cbias = jnp.where(c_end <= qpos, 0.0, NEG)
    zc = zc + jnp.concatenate([cbias] * 4, axis=1)
    e = jnp.exp2(zc - jnp.max(zc, axis=0, keepdims=True))
    seen = jnp.where(qpos4 >= NSA_CMP_LEN - 1, 1.0, 0.0)
    pc = e * (seen / jnp.maximum(jnp.sum(e, axis=0, keepdims=True), 1.0))
    o_cmp = _dot(kcT_ref[HEAD_DIM:2 * HEAD_DIM, :], pc.astype(BF16))

    psum = pc[:, 0:tq] + pc[:, tq:2 * tq] + pc[:, 2 * tq:3 * tq] + pc[:, 3 * tq:4 * tq]
    p_hi, p_lo = _split2(psum)
    imp = _dot(ov_ref[...], p_hi) + _dot(ov_ref[...], p_lo)
    nid = lax.broadcasted_iota(jnp.int32, (nsel, tq), 0)
    cur = qpos // NSA_SEL_BLOCK
    imp = jnp.where(nid == 0, BIG, imp)
    imp = jnp.where(nid == cur, BIG, imp)
    imp = jnp.where(nid == cur - 1, BIG, imp)
    imp = jnp.where(nid > cur, NEG, imp)
    for _r in range(min(NSA_SEL_TOPK, nsel)):
        mx = jnp.max(imp, axis=0, keepdims=True)
        idx = jnp.min(jnp.where(imp == mx, nid, nsel), axis=0, keepdims=True)
        imp = jnp.where(nid == idx, -jnp.inf, imp)
    sel = jnp.where(nid <= cur, jnp.where(imp == -jnp.inf, 1.0, 0.0), 0.0)
    sel_ref[...] = sel
    for h in range(4):
        qm_ref[h] = qm_lo[:, h * tq:(h + 1) * tq]

    def qk(t, slot):
        kb = k_ref[pl.ds(pl.multiple_of(t * tq, tq), tq), LANE:2 * LANE]
        for h in range(4):
            s_ref[slot, h] = _dot(kb, qm_ref[h])

    kpos_d = i * tq + lax.broadcasted_iota(jnp.int32, (tq, tq), 0)
    qk(i, 0)
    vb_d = _v_aug(_lane_tiles(vT_ref, i * nt, nt, 0, HEAD_DIM))
    srows_d = [sel_ref[pl.ds(i * spb + u, 1), :] for u in range(spb)]
    for h in range(4):
        sc = s_ref[0, h]
        s = jnp.concatenate(
            [jnp.where(srows_d[u] > 0.0, sc[u * NSA_SEL_BLOCK:(u + 1) * NSA_SEL_BLOCK, :], NEG)
             for u in range(spb)], axis=0)
        s = jnp.where(kpos_d <= qpos, s, NEG)
        m = jnp.max(s, axis=0, keepdims=True)
        lanes = slice(h * tq, (h + 1) * tq)
        ml_ref[0:1, lanes] = m
        acc_ref[:, lanes] = _dot(vb_d, jnp.exp2(s - m).astype(BF16))

    last = jnp.maximum(i - 1, 0)

    def update(t, valid, slot):
        vb = _v_aug(_lane_tiles(vT_ref, t * nt, nt, 0, HEAD_DIM))
        srows = [sel_ref[pl.ds(t * spb + u, 1), :] * valid for u in range(spb)]
        for h in range(4):
            lanes = slice(h * tq, (h + 1) * tq)
            sc = s_ref[slot, h]
            subs = [sc[u * NSA_SEL_BLOCK:(u + 1) * NSA_SEL_BLOCK, :] for u in range(spb)]
            tmax = None
            for u in range(spb):
                mu = jnp.where(srows[u] > 0.0, jnp.max(subs[u], axis=0, keepdims=True), NEG)
                tmax = mu if tmax is None else jnp.maximum(tmax, mu)
            m_old = ml_ref[0:1, lanes]
            m_new = jnp.maximum(m_old, tmax)
            p = jnp.concatenate(
                [jnp.exp2(subs[u] - jnp.where(srows[u] > 0.0, m_new, BIGPOS)).astype(BF16) for u in range(spb)],
                axis=0)
            ml_ref[0:1, lanes] = m_new
            acc_ref[:, lanes] = jnp.exp2(m_old - m_new) * acc_ref[:, lanes] + _dot(vb, p)

    qk(0, 0)

    def body(jj, c):
        t0 = 2 * jj
        qk(jnp.minimum(t0 + 1, last), 1)
        update(t0, 1.0, 0)
        qk(jnp.minimum(t0 + 2, last), 0)
        update(jnp.minimum(t0 + 1, last), (t0 + 1 < i).astype(F32), 1)
        return c

    lax.fori_loop(0, (i + 1) // 2, body, 0)
    o_sel = acc_ref[0:HEAD_DIM, :] / acc_ref[HEAD_DIM:HEAD_DIM + 1, :]

    o_win = win_ref[...]

    gates = _lane_tiles(vT_ref, i * nt, nt, 2 * HEAD_DIM, 16).astype(F32)
    outs = []
    for h in range(4):
        sl = slice(h * tq, (h + 1) * tq)
        outs.append(gates[3 * h:3 * h + 1, :] * o_cmp[:, sl]
                    + gates[3 * h + 1:3 * h + 2, :] * o_sel[:, sl]
                    + gates[3 * h + 2:3 * h + 3, :] * o_win[:, sl])
    o_ref[...] = jnp.concatenate(outs, axis=0).T.astype(o_ref.dtype)


def _nsa_call(oT, otok, kc, kcT, ovT, *, q_chunk, kv_chunk, v_chunk):
    B, nlt, _, _ = oT.shape
    S = nlt * LANE
    tq = 256
    nc = S // NSA_CMP_STRIDE
    nsel = S // NSA_SEL_BLOCK
    kern = functools.partial(_nsa_kernel, tq=tq)
    return pl.pallas_call(
        kern,
        grid=(B, S // tq),
        in_specs=_attn_specs(S, tq, q_chunk, kv_chunk, v_chunk) + [
            pl.BlockSpec((None, nc, LANE), lambda b, i: (b, 0, 0)),
            pl.BlockSpec((None, LANE, nc), lambda b, i: (b, 0, 0)),
            pl.BlockSpec((nsel, nc), lambda b, i: (0, 0)),
        ],
        out_specs=pl.BlockSpec((None, tq, BRANCH_W), lambda b, i: (b, i, 0)),
        out_shape=jax.ShapeDtypeStruct((B, S, BRANCH_W), BF16),
        scratch_shapes=[pltpu.VMEM((nsel, tq), F32), pltpu.VMEM((4, LANE, tq), BF16),
                        pltpu.VMEM((HA, 4 * tq), F32), pltpu.VMEM((8, 4 * tq), F32),
                        pltpu.VMEM((2, 4, tq, tq), F32), pltpu.VMEM((HEAD_DIM, 4 * tq), F32)],
        compiler_params=_cparams(("parallel", "arbitrary")),
        name="nsa",
    )(oT, otok, oT, kc, kcT, ovT)


def _epi_kernel(x_ref, ng_ref, ya_ref, yc_ref, yd_ref, ob0_ref, ob1_ref, ob2_ref, l0_ref, l1_ref, l2_ref,
                wz_ref, wmg_ref, wbr_ref, wout_ref, o_ref, nat_ref, *, tm):
    x = x_ref[...]
    ms = jnp.mean(x * x, axis=-1, keepdims=True)
    xn = (x * lax.rsqrt(ms + EPS) * ng_ref[...]).astype(BF16)

    def natural(ref, slot, dil):
        nh = BRANCH_W // LANE
        for r in range(dil):
            for hf in range(nh):
                c0 = r * BRANCH_W + hf * LANE
                nat_ref[slot * nh + hf, pl.ds(r, tm // dil, stride=dil), :] = ref[:, c0:c0 + LANE].astype(F32)
        return jnp.concatenate([nat_ref[slot * nh + hf] for hf in range(nh)], axis=1)

    d1, d2 = DIL_PAIRS[1][1], DIL_PAIRS[2][1]
    l0, l1, l2 = l0_ref[...], natural(l1_ref, 0, d1), natural(l2_ref, 1, d2)
    mx = jnp.maximum(jnp.maximum(l0, l1), l2)
    e0, e1, e2 = jnp.exp2(l0 - mx), jnp.exp2(l1 - mx), jnp.exp2(l2 - mx)
    yb = (e0 * ob0_ref[...].astype(F32) + e1 * natural(ob1_ref, 2, d1)
          + e2 * natural(ob2_ref, 3, d2)) / (e0 + e1 + e2)

    ys = (ya_ref[...].astype(F32), yb, yc_ref[...].astype(F32), yd_ref[...].astype(F32))
    merged = None
    for i in range(4):
        z = _dot(xn, wz_ref[:, i * BRANCH_W:(i + 1) * BRANCH_W])
        gated = (ys[i] * (z * jax.nn.sigmoid(z))).astype(BF16)
        br = _dot(gated, wbr_ref[i])
        mg = _dot(xn, wmg_ref[:, i * D_MODEL:(i + 1) * D_MODEL])
        term = jax.nn.sigmoid(mg) * br
        merged = term if merged is None else merged + term
    o_ref[...] = x + _dot(merged.astype(BF16), wout_ref[...])


def _epi_call(x2, ng, ya, yc, yd, obs, lses, wz, wmg, wbr, wout, *, layer, tm=512):
    T = x2.shape[0]
    row = lambda i: (i, 0)
    full2 = lambda i: (0, 0)
    yspec = pl.BlockSpec((tm, BRANCH_W), row)
    d1, d2 = DIL_PAIRS[1][1], DIL_PAIRS[2][1]
    gspecs = [yspec, pl.BlockSpec((tm // d1, d1 * BRANCH_W), row), pl.BlockSpec((tm // d2, d2 * BRANCH_W), row)]
    return pl.pallas_call(
        functools.partial(_epi_kernel, tm=tm),
        grid=(T // tm,),
        in_specs=[pl.BlockSpec((tm, D_MODEL), row), pl.BlockSpec((1, D_MODEL), full2)]
        + [yspec] * 3 + gspecs + gspecs
        + [pl.BlockSpec((None, D_MODEL, 4 * BRANCH_W), lambda i: (layer, 0, 0)),
           pl.BlockSpec((None, D_MODEL, 4 * D_MODEL), lambda i: (layer, 0, 0)),
           pl.BlockSpec((None, 4, BRANCH_W, D_MODEL), lambda i: (layer, 0, 0, 0)),
           pl.BlockSpec((None, D_MODEL, D_MODEL), lambda i: (layer, 0, 0))],
        out_specs=pl.BlockSpec((tm, D_MODEL), row),
        out_shape=jax.ShapeDtypeStruct((T, D_MODEL), F32),
        scratch_shapes=[pltpu.VMEM((4 * (BRANCH_W // LANE), tm, LANE), F32)],
        compiler_params=_cparams(("parallel",)),
        name="epilogue",
    )(x2, ng, ya, yc, yd, *obs, *lses, wz, wmg, wbr, wout)


TOK_AK, TOK_BQ, TOK_BK, TOK_BV, TOK_CK, TOK_DKV = range(6)
N_MAIN = 6
T_AQ, T_AV, T_CQ, T_CV, T_DQ, T_DX = range(6)
SCALE = 1.0 / math.sqrt(HEAD_DIM)
QSCALE2 = SCALE * LOG2E

_PLAIN = (False, False, 1.0, False)
_T_KINDS = (
    ((False, False, QSCALE2, False),) * 4,
    (_PLAIN,) * 4,
    ((True, True, QSCALE2, False),) * 4,
    (_PLAIN,) * 4,
    ((True, True, QSCALE2, False),) * 4,
    (_PLAIN, _PLAIN, (False, False, 1.0, True), _PLAIN),
)


def _tok_specs():
    main = lambda j: (0, 1, j * CH, CH, 0)
    qkv = ((True, True), (True, True), (False, False))
    specs = []
    for j in range(N_MAIN):
        if j in (TOK_BQ, TOK_BK, TOK_CK):
            specs.append((True, True, (main(j),)))
        elif j == TOK_DKV:
            specs.append(("mixed", "mixed", (main(j), (3, NSA_CMP_STRIDE, 0, LANE, LANE))))
        else:
            specs.append((False, False, (main(j),)))
    for gi in (1, 2):
        dil = DIL_PAIRS[gi][1]
        for k, (nrm, rope) in enumerate(qkv):
            specs.append((nrm, rope, ((gi, dil, k * CH, CH, 3 * CH),)))
    return tuple(specs)


def _rope_tables(S, dil):
    L = S // dil
    pos = (jnp.arange(dil, dtype=jnp.int32)[:, None] + dil * jnp.arange(L, dtype=jnp.int32)[None, :]).astype(F32)
    inv = ROPE_THETA ** (-jnp.arange(0, ROT_DIM, 2, dtype=F32) / ROT_DIM)
    ang = pos[:, :, None] * inv[None, None, :]
    cos, sin = jnp.cos(ang), jnp.sin(ang)
    one = jnp.ones((dil, L, HEAD_DIM - ROT_DIM), F32)
    zero8 = jnp.zeros((dil, L, 8), F32)
    zero = jnp.zeros_like(one)
    c_head = jnp.concatenate([cos, cos, one], axis=-1)
    s1_head = jnp.concatenate([zero8, sin, zero], axis=-1)
    s2_head = jnp.concatenate([-sin, zero8, zero], axis=-1)
    rtok = jnp.stack([jnp.tile(t, (1, 1, LANE // HEAD_DIM)) for t in (c_head, s1_head, s2_head)], axis=1)
    rT = jnp.stack([cos.transpose(0, 2, 1), sin.transpose(0, 2, 1)], axis=1)
    return rtok, rT


def _tok_params(rows):
    out = []
    for nf, rf, gains in rows:
        nrow = jnp.concatenate([jnp.full((HEAD_DIM,), float(f), F32) for f in nf])
        rrow = jnp.concatenate([jnp.full((HEAD_DIM,), float(f), F32) for f in rf])
        grow = jnp.concatenate([g.astype(F32) for g in gains])
        out.append(jnp.concatenate([jnp.stack([nrow, rrow, grow]), jnp.zeros((5, CH), F32)], axis=0))
    return jnp.stack(out)


def kernel(x, norm_g, w_in, qk_g, cmp_pe, cmp_w1, cmp_w2, w_branch, w_out):
    B, S, _ = x.shape
    T = B * S
    o = COL_OFF
    ones = jnp.ones((HEAD_DIM,), F32)

    r = np.arange(CH)
    G = jnp.asarray((r[:, None] // HEAD_DIM == r[None, :] // HEAD_DIM).astype(np.float32), BF16)
    G128 = G[:LANE, :LANE]
    t = np.arange(256)
    U = jnp.asarray((t[None, :] > t[:, None]).astype(np.float32), BF16)
    nblk = S // MOBA_BLOCK
    A = jnp.asarray(np.repeat(np.eye(nblk, dtype=np.float32), MOBA_BLOCK, axis=1) / MOBA_BLOCK, BF16)
    nc, nsel = S // NSA_CMP_STRIDE, S // NSA_SEL_BLOCK
    cs = np.arange(nc) * NSA_CMP_STRIDE
    ss = np.arange(nsel) * NSA_SEL_BLOCK
    ov = ((cs[None, :] < ss[:, None] + NSA_SEL_BLOCK) & (cs[None, :] + NSA_CMP_LEN > ss[:, None]))
    ov[:, nc - 1] = False
    ovT = jnp.asarray(ov.astype(np.float32), BF16)
    hid = np.arange(LANE)[:, None]
    ind = jnp.asarray(((hid < 4) & (r[None, :] // HEAD_DIM == hid)).astype(np.float32), BF16)
    rtok4, rT4 = _rope_tables(S, 1)
    rtok, rT = rtok4[0], rT4[0]
    tok_specs = _tok_specs()

    dkv = o[13]
    bq = lambda gi: (o[4] + gi * CH, o[4] + (gi + 1) * CH)
    bk = lambda gi: (o[5] + gi * CH, o[5] + (gi + 1) * CH)
    bv = lambda gi: (o[6] + gi * CH, o[6] + (gi + 1) * CH)
    tok_plan = ((o[1], o[2]), bq(0), bk(0), bv(0), (o[9], o[10]),
                (dkv, dkv + 64), (dkv + 64, dkv + 128), (dkv + 128, dkv + 192), (dkv + 256, dkv + 320),
                bq(1), bk(1), bv(1), bq(2), bk(2), bv(2))
    T_plan = ((o[0], o[1]), (o[2], o[3]), (o[8], o[9]), (o[10], o[11]), (o[12], o[13]),
              (dkv + 192, dkv + 256), (dkv + 320, dkv + 384), (o[14], o[15]))
    z_plan = ((o[3], o[4]), (o[7], o[8]), (o[11], o[12]), (o[15], o[16]))
    wtok_all, wT_all, wz_all, wmg_all = _repack_call(
        w_in, (tok_plan, T_plan, z_plan, ((o[16], o[17]),)),
        (len(tok_specs) * CH, len(_T_KINDS) * CH, 4 * BRANCH_W, 4 * D_MODEL))

    wbr_all = w_branch.astype(BF16)
    wout_all = w_out.astype(BF16)

    for l in range(DEPTH):
        g = qk_g[l]
        plain = ((0,) * 4, (0,) * 4, (ones,) * 4)
        bq_p = ((1,) * 4, (1,) * 4, (g[0] * QSCALE2,) * 4)
        bk_p = ((1,) * 4, (1,) * 4, (g[1],) * 4)
        tokp = _tok_params([
            plain, bq_p, bk_p, plain,
            ((1,) * 4, (1,) * 4, (g[3],) * 4),
            ((0, 0, 1, 1), (1, 0, 1, 1), (ones, ones, g[6], g[7])),
            bq_p, bk_p, plain, bq_p, bk_p, plain])
        gT = jnp.concatenate([jnp.tile(ones, 8), jnp.tile(g[2], 4), jnp.tile(ones, 4), jnp.tile(g[4], 4),
                              jnp.tile(ones, 4)])[:, None]
        ng = norm_g[l][None, :]
        otok, og1, og2, okv, oT = _proj_call(x, ng, wtok_all, wT_all, tokp, gT, rtok, rT, G, layer=l,
                                             tok_specs=tok_specs, T_kinds=_T_KINDS, n_main=N_MAIN)

        ya = _sb_call(oT, otok, U, q_chunk=T_AQ, k_chunk=TOK_AK, v_chunk=T_AV)

        obs, lses = [], []
        for gi, (window, dil) in enumerate(DIL_PAIRS):
            if dil == 1:
                ob, lse = _band_call(otok, ind, dil=1, per_res=N_MAIN, qi=TOK_BQ, ki=TOK_BK, vi=TOK_BV,
                                     max_dist=window)
            else:
                ob, lse = _band_call((og1, og2)[gi - 1], ind, dil=dil, per_res=3, qi=0, ki=1, vi=2,
                                     max_dist=window // dil)
            obs.append(ob.reshape(T // dil, dil * BRANCH_W))
            lses.append(lse.reshape(T // dil, dil * BRANCH_W))

        yc = _moba_call(oT, otok, A, q_chunk=T_CQ, k_chunk=TOK_CK, v_chunk=T_CV)

        w1 = cmp_w1[l].reshape(2, NSA_CMP_LEN, HEAD_DIM, HEAD_DIM)
        z64 = jnp.zeros((NSA_CMP_STRIDE, HEAD_DIM, HEAD_DIM), F32)
        w1blk = jnp.concatenate([
            jnp.concatenate([w1[0, :16], w1[0, 16:], z64, z64], axis=2),
            jnp.concatenate([z64, z64, w1[1, :16], w1[1, 16:]], axis=2)], axis=1)
        pe2 = jnp.concatenate([cmp_pe[l].reshape(1, -1), jnp.zeros((7, 2 * NSA_CMP_LEN * HEAD_DIM), F32)], axis=0)
        zf = jnp.zeros((NSA_CMP_LEN * HEAD_DIM, HEAD_DIM), F32)
        w1f = jnp.concatenate([jnp.concatenate([cmp_w1[l, 0], zf], axis=1),
                               jnp.concatenate([zf, cmp_w1[l, 1]], axis=1)], axis=0)
        z2 = jnp.zeros((HEAD_DIM, HEAD_DIM), F32)
        w2blk = jnp.concatenate([jnp.concatenate([cmp_w2[l, 0], z2], axis=1),
                                 jnp.concatenate([z2, cmp_w2[l, 1]], axis=1)], axis=0)
        gk = jnp.concatenate([g[5], ones])[None, :]
        kc, kcT = _compress_call(okv, w1blk, pe2, w1f, w2blk, gk, G128)
        yd = _nsa_call(oT, otok, kc, kcT, ovT, q_chunk=T_DQ, kv_chunk=TOK_DKV, v_chunk=T_DX)

        x2 = _epi_call(x.reshape(T, D_MODEL), ng, ya.reshape(T, BRANCH_W), yc.reshape(T, BRANCH_W),
                       yd.reshape(T, BRANCH_W), obs, lses, wz_all, wmg_all, wbr_all, wout_all, layer=l)
        x = x2.reshape(B, S, D_MODEL)
    return x
```

```python
import functools
import math

import numpy as np
import jax
import jax.numpy as jnp
from jax import lax
from jax.experimental import pallas as pl
from jax.experimental.pallas import tpu as pltpu

F32 = jnp.float32
BF16 = jnp.bfloat16

D_MODEL = 1024
DEPTH = 4
HEAD_DIM = 64
ROT_DIM = 16
ROPE_THETA = 500000.0
EPS = 1e-6
NEG = -1e30
BIG = 1e9
BRANCH_W = 256
DIL_PAIRS = ((128, 1), (512, 4), (2048, 16))
MOBA_BLOCK = 256
MOBA_TOPK = 3
NSA_CMP_LEN = 32
NSA_CMP_STRIDE = 16
NSA_SEL_BLOCK = 64
NSA_SEL_TOPK = 16
NSA_WINDOW = 512
COL_SIZES = (256, 256, 256, 256, 768, 768, 768, 256, 256, 256, 256, 256, 256, 384, 12, 256, 4096)
COL_OFF = tuple(int(v) for v in np.concatenate([[0], np.cumsum(COL_SIZES)]))

LANE = 128
CH = 256
VMEM_LIMIT = 56 * 1024 * 1024


def _cparams(sem):
    return pltpu.CompilerParams(dimension_semantics=sem, vmem_limit_bytes=VMEM_LIMIT)


def _dot(a, b):
    return jnp.dot(a, b, preferred_element_type=F32)


def _dot_nt(a, b):
    return lax.dot_general(a, b, (((1,), (1,)), ((), ())), preferred_element_type=F32)


def _split2(x):
    hi = x.astype(BF16)
    lo = (x - hi.astype(F32)).astype(BF16)
    return hi, lo


def _split3(x):
    hi = x.astype(BF16)
    r = x - hi.astype(F32)
    mid = r.astype(BF16)
    lo = (r - mid.astype(F32)).astype(BF16)
    return hi, mid, lo


def _repack_kernel(w_ref, *out_refs, plans):
    for o_ref, ranges in zip(out_refs, plans):
        width = o_ref.shape[-1]
        parts = [w_ref[:, a:b] for a, b in ranges]
        used = sum(b - a for a, b in ranges)
        if used < width:
            parts.append(jnp.zeros((w_ref.shape[0], width - used), F32))
        o_ref[...] = (parts[0] if len(parts) == 1 else jnp.concatenate(parts, axis=1)).astype(o_ref.dtype)


def _repack_call(w_in, plans, widths, rows=128):
    depth, d, n_in = w_in.shape
    kern = functools.partial(_repack_kernel, plans=plans)
    return pl.pallas_call(
        kern,
        grid=(depth, d // rows),
        in_specs=[pl.BlockSpec((None, rows, n_in), lambda l, i: (l, i, 0))],
        out_specs=[pl.BlockSpec((None, rows, w), lambda l, i: (l, i, 0)) for w in widths],
        out_shape=[jax.ShapeDtypeStruct((depth, d, w), BF16) for w in widths],
        compiler_params=_cparams(("parallel", "parallel")),
        name="repack",
    )(w_in)


def _proj_kernel(x_ref, ng_ref, wtok_ref, wT_ref, tokp_ref, gT_ref, rtok_ref, rT_ref, G_ref,
                 omain_ref, og1_ref, og2_ref, okv_ref, oT_ref, stage_ref, *, tok_specs, T_kinds, tm):
    x = x_ref[...]
    ms = jnp.mean(x * x, axis=-1, keepdims=True)
    xn = (x * lax.rsqrt(ms + EPS) * ng_ref[...]).astype(BF16)

    dsts = (omain_ref, og1_ref, og2_ref, okv_ref)
    for c, (has_norm, has_rope, outs) in enumerate(tok_specs):
        y = _dot(xn, wtok_ref[:, c * CH:(c + 1) * CH])
        prm = tokp_ref[c]
        if has_norm:
            ss = _dot((y * y).astype(BF16), G_ref[...])
            inv = lax.rsqrt(ss * (1.0 / HEAD_DIM) + EPS)
            y = y * (jnp.where(prm[0:1, :] > 0.0, inv, 1.0) if has_norm == "mixed" else inv)
        y = y * prm[2:3, :]
        if has_rope:
            halves = []
            for hf in range(CH // LANE):
                yh = y[:, hf * LANE:(hf + 1) * LANE]
                cc, s1, s2 = rtok_ref[0], rtok_ref[1], rtok_ref[2]
                if has_rope == "mixed":
                    rf = prm[1:2, hf * LANE:(hf + 1) * LANE]
                    cc, s1, s2 = jnp.where(rf > 0.0, cc, 1.0), s1 * rf, s2 * rf
                halves.append(yh * cc + pltpu.roll(yh, 8, 1) * s1 + pltpu.roll(yh, LANE - 8, 1) * s2)
            y = jnp.concatenate(halves, axis=1)
        staged = False
        for dst, dil, col, lanes, rstride in outs:
            if dil == 1:
                dsts[dst][:, col:col + lanes] = y[:, :lanes].astype(BF16)
                continue
            if not staged:
                for hf in range(CH // LANE):
                    stage_ref[hf] = y[:, hf * LANE:(hf + 1) * LANE]
                staged = True
            for r in range(dil):
                for hf in range(lanes // LANE):
                    blk = stage_ref[hf, pl.ds(r, tm // dil, stride=dil), :]
                    c0 = r * rstride + col + hf * LANE
                    dsts[dst][:, c0:c0 + LANE] = blk.astype(BF16)

    cosT = rT_ref[0]
    sinT = rT_ref[1]
    for c, heads in enumerate(T_kinds):
        y = _dot(xn, wT_ref[:, c * CH:(c + 1) * CH]).T
        for h, (nrm, rope, scale, sigm) in enumerate(heads):
            r0 = c * CH + h * HEAD_DIM
            yh = y[h * HEAD_DIM:(h + 1) * HEAD_DIM, :]
            if nrm:
                msq = jnp.mean(yh * yh, axis=0, keepdims=True)
                yh = yh * lax.rsqrt(msq + EPS) * gT_ref[r0:r0 + HEAD_DIM, :]
            if rope:
                x1 = yh[0:8, :]
                x2 = yh[8:16, :]
                yh = jnp.concatenate([x1 * cosT - x2 * sinT, x2 * cosT + x1 * sinT, yh[16:, :]], axis=0)
            if scale != 1.0:
                yh = yh * scale
            if sigm:
                yh = jax.nn.sigmoid(yh)
            yb = yh.astype(BF16)
            for t in range(tm // LANE):
                oT_ref[t, r0:r0 + HEAD_DIM, :] = yb[:, t * LANE:(t + 1) * LANE]


def _proj_call(x, ng, wtok, wT, tokp, gT, rtok, rT, G, *, layer, tok_specs, T_kinds, n_main, tm=512):
    B, S, _ = x.shape
    ntok, nT = len(tok_specs), len(T_kinds)
    d1, d2 = DIL_PAIRS[1][1], DIL_PAIRS[2][1]
    kern = functools.partial(_proj_kernel, tok_specs=tok_specs, T_kinds=T_kinds, tm=tm)
    const2 = lambda b, n: (0, 0)
    return pl.pallas_call(
        kern,
        grid=(B, S // tm),
        in_specs=[
            pl.BlockSpec((None, tm, D_MODEL), lambda b, n: (b, n, 0)),
            pl.BlockSpec((1, D_MODEL), const2),
            pl.BlockSpec((None, D_MODEL, ntok * CH), lambda b, n: (layer, 0, 0)),
            pl.BlockSpec((None, D_MODEL, nT * CH), lambda b, n: (layer, 0, 0)),
            pl.BlockSpec((ntok, 8, CH), lambda b, n: (0, 0, 0)),
            pl.BlockSpec((nT * CH, 1), const2),
            pl.BlockSpec((3, tm, LANE), lambda b, n: (0, n, 0)),
            pl.BlockSpec((2, 8, tm), lambda b, n: (0, 0, n)),
            pl.BlockSpec((CH, CH), const2),
        ],
        out_specs=[
            pl.BlockSpec((None, tm, n_main * CH), lambda b, n: (b, n, 0)),
            pl.BlockSpec((None, tm // d1, d1 * 3 * CH), lambda b, n: (b, n, 0)),
            pl.BlockSpec((None, tm // d2, d2 * 3 * CH), lambda b, n: (b, n, 0)),
            pl.BlockSpec((None, tm // NSA_CMP_STRIDE, NSA_CMP_STRIDE * LANE), lambda b, n: (b, n, 0)),
            pl.BlockSpec((None, tm // LANE, nT * CH, LANE), lambda b, n: (b, n, 0, 0)),
        ],
        out_shape=[
            jax.ShapeDtypeStruct((B, S, n_main * CH), BF16),
            jax.ShapeDtypeStruct((B, S // d1, d1 * 3 * CH), BF16),
            jax.ShapeDtypeStruct((B, S // d2, d2 * 3 * CH), BF16),
            jax.ShapeDtypeStruct((B, S // NSA_CMP_STRIDE, NSA_CMP_STRIDE * LANE), BF16),
            jax.ShapeDtypeStruct((B, S // LANE, nT * CH, LANE), BF16),
        ],
        scratch_shapes=[pltpu.VMEM((CH // LANE, tm, LANE), F32)],
        compiler_params=_cparams(("parallel", "parallel")),
        name="proj",
    )(x, ng, wtok, wT, tokp, gT, rtok, rT, G)


def _pair_masked_q(q_pair, h):
    rid = lax.broadcasted_iota(jnp.int32, q_pair.shape, 0)
    lo = (h % 2) * HEAD_DIM
    keep = jnp.where(rid >= lo, jnp.where(rid < lo + HEAD_DIM, 1.0, 0.0), 0.0).astype(BF16)
    return q_pair * keep


def _lane_tiles(ref, t0, nt, r0, nr):
    return jnp.concatenate([ref[t0 + t, r0:r0 + nr, :] for t in range(nt)], axis=1)


BIGPOS = 1e30
ONES_ROWS = 16
HA = HEAD_DIM + ONES_ROWS
LOG2E = 1.4426950408889634


def _v_aug(vb):
    return jnp.concatenate([vb, jnp.ones((ONES_ROWS, vb.shape[1]), BF16)], axis=0)


def _online_cols(s, colsel, m_old, acc_old, vb_aug):
    tmax = jnp.max(s, axis=0, keepdims=True)
    m_new = jnp.where(colsel > 0.0, jnp.maximum(m_old, tmax), m_old)
    m_use = jnp.where(colsel > 0.0, m_new, BIGPOS)
    p = jnp.exp2(s - m_use).astype(BF16)
    acc = jnp.exp2(m_old - m_new) * acc_old + _dot(vb_aug, p)
    return m_new, acc


SB_LOG_CUTOFF = -144.0


def _sb_kernel(qT_ref, k_ref, vT_ref, U_ref, o_ref, qm_ref, acc_ref, carry_ref, *, tq):
    i = pl.program_id(1)
    nt = tq // LANE
    row = lax.broadcasted_iota(jnp.int32, (tq, tq), 0)
    col = lax.broadcasted_iota(jnp.int32, (tq, tq), 1)
    past = row < col
    for h in range(4):
        p = h // 2
        qm_ref[h] = _pair_masked_q(_lane_tiles(qT_ref, 0, nt, p * LANE, LANE), h)
    acc_ref[...] = jnp.zeros_like(acc_ref)
    carry_ref[...] = jnp.zeros_like(carry_ref)

    def tile(j, masked):
        scores, logsig, laters = [], [], []
        for h in range(4):
            p = h // 2
            kb = k_ref[pl.ds(pl.multiple_of(j * tq, tq), tq), p * LANE:(p + 1) * LANE]
            scores.append(_dot(kb, qm_ref[h]))
        worst = None
        for h in range(4):
            s = scores[h]
            sp = jnp.maximum(s, 0.0) + jnp.log2(1.0 + jnp.exp2(-jnp.abs(s)))
            lg = -sp
            if masked:
                lg = jnp.where(past, lg, 0.0)
            hi, lo = _split2(lg)
            carry = carry_ref[h:h + 1, :]
            laters.append(_dot(U_ref[...], hi) + _dot(U_ref[...], lo) + carry)
            logsig.append(s - sp)
            carry = carry + jnp.sum(lg, axis=0, keepdims=True)
            carry_ref[h:h + 1, :] = carry
            worst = carry if worst is None else jnp.maximum(worst, carry)
        for h in range(4):
            w = jnp.exp2(logsig[h] + laters[h])
            if masked:
                w = jnp.where(past, w, 0.0)
            vb = _lane_tiles(vT_ref, j * nt, nt, h * HEAD_DIM, HEAD_DIM)
            acc_ref[h * HEAD_DIM:(h + 1) * HEAD_DIM, :] += _dot(vb, w.astype(BF16))
        return jnp.max(worst)

    worst0 = tile(i, True)

    def cond(st):
        return jnp.logical_and(st[0] >= 0, st[1] > SB_LOG_CUTOFF)

    def body(st):
        return st[0] - 1, tile(st[0], False)

    lax.while_loop(cond, body, (i - 1, worst0))
    o_ref[...] = acc_ref[...].T.astype(o_ref.dtype)


def _attn_specs(S, tq, q_chunk, k_chunk, v_chunk):
    return [
        pl.BlockSpec((None, tq // LANE, CH, LANE), lambda b, i: (b, i, q_chunk, 0)),
        pl.BlockSpec((None, S, CH), lambda b, i: (b, 0, k_chunk)),
        pl.BlockSpec((None, S // LANE, CH, LANE), lambda b, i: (b, 0, v_chunk, 0)),
    ]


def _sb_call(oT, otok, U, *, q_chunk, k_chunk, v_chunk, tq=256):
    B, nlt, _, _ = oT.shape
    S = nlt * LANE
    kern = functools.partial(_sb_kernel, tq=tq)
    return pl.pallas_call(
        kern,
        grid=(B, S // tq),
        in_specs=_attn_specs(S, tq, q_chunk, k_chunk, v_chunk) + [pl.BlockSpec((tq, tq), lambda b, i: (0, 0))],
        out_specs=pl.BlockSpec((None, tq, BRANCH_W), lambda b, i: (b, i, 0)),
        out_shape=jax.ShapeDtypeStruct((B, S, BRANCH_W), BF16),
        scratch_shapes=[pltpu.VMEM((4, LANE, tq), BF16), pltpu.VMEM((CH, tq), F32), pltpu.VMEM((8, tq), F32)],
        compiler_params=_cparams(("parallel", "arbitrary")),
        name="stick_breaking",
    )(oT, otok, oT, U)


def _head_lanes(x, lane, h):
    lo = h * HEAD_DIM
    return jnp.where(lane >= lo, jnp.where(lane < lo + HEAD_DIM, x, jnp.zeros_like(x)), jnp.zeros_like(x))


def _band_kernel(q_ref, k_ref, v_ref, ind_ref, o_ref, lse_ref, *, tqb, max_dist):
    n = pl.program_id(2)
    n_prev = -(-max_dist // LANE)
    nkr = (n_prev + 1) * LANE
    nsub = tqb // LANE
    row = lax.broadcasted_iota(jnp.int32, (nkr, LANE), 0)
    col = lax.broadcasted_iota(jnp.int32, (nkr, LANE), 1)
    lane_q = lax.broadcasted_iota(jnp.int32, (LANE, LANE), 1)
    lane_v = lax.broadcasted_iota(jnp.int32, (nkr, CH), 1)
    for u in range(nsub):
        qt = n * nsub + u
        kt0 = jnp.maximum(qt - n_prev, 0)
        dist = (qt - kt0) * LANE + col - row
        bias = jnp.where(dist >= 0, jnp.where(dist <= max_dist, 0.0, NEG), NEG)
        k0 = pl.multiple_of(kt0 * LANE, LANE)
        kwin = k_ref[pl.ds(k0, nkr), :]
        vwin = v_ref[pl.ds(k0, nkr), :]
        qu = q_ref[u * LANE:(u + 1) * LANE, :]
        o_acc = None
        lses = []
        for h in range(4):
            p = h // 2
            qm = _head_lanes(qu[:, p * LANE:(p + 1) * LANE], lane_q, h % 2)
            s = _dot_nt(kwin[:, p * LANE:(p + 1) * LANE], qm) + bias
            m = jnp.max(s, axis=0, keepdims=True)
            e = jnp.exp2(s - m)
            den = jnp.sum(e, axis=0, keepdims=True)
            pn = e * (1.0 / den)
            contrib = _dot(pn.T.astype(BF16), _head_lanes(vwin, lane_v, h))
            o_acc = contrib if o_acc is None else o_acc + contrib
            lses.append(m + jnp.log(den) * LOG2E)
        o_ref[u * LANE:(u + 1) * LANE, :] = o_acc.astype(o_ref.dtype)
        ls = jnp.concatenate(lses + [jnp.zeros((LANE - 4, LANE), F32)], axis=0).T
        l1, l2, l3 = _split3(ls)
        lse_ref[u * LANE:(u + 1) * LANE, :] = (_dot(l1, ind_ref[...]) + _dot(l2, ind_ref[...])
                                               + _dot(l3, ind_ref[...]))


def _band_call(arr, ind, *, dil, per_res, qi, ki, vi, max_dist):
    B, L, _ = arr.shape
    tqb = min(1024, L)
    kern = functools.partial(_band_kernel, tqb=tqb, max_dist=max_dist)
    return pl.pallas_call(
        kern,
        grid=(B, dil, L // tqb),
        in_specs=[
            pl.BlockSpec((None, tqb, CH), lambda b, r, n: (b, n, r * per_res + qi)),
            pl.BlockSpec((None, L, CH), lambda b, r, n: (b, 0, r * per_res + ki)),
            pl.BlockSpec((None, L, CH), lambda b, r, n: (b, 0, r * per_res + vi)),
            pl.BlockSpec((LANE, CH), lambda b, r, n: (0, 0)),
        ],
        out_specs=[
            pl.BlockSpec((None, tqb, BRANCH_W), lambda b, r, n: (b, n, r)),
            pl.BlockSpec((None, tqb, BRANCH_W), lambda b, r, n: (b, n, r)),
        ],
        out_shape=[
            jax.ShapeDtypeStruct((B, L, dil * BRANCH_W), BF16),
            jax.ShapeDtypeStruct((B, L, dil * BRANCH_W), F32),
        ],
        compiler_params=_cparams(("parallel", "parallel", "parallel")),
        name="banded",
    )(arr, arr, arr, ind)


def _moba_kernel(qT_ref, k_ref, vT_ref, A_ref, o_ref, kmean_ref, sel_ref, qm_ref, acc_ref, ml_ref, s_ref, *, tq):
    i = pl.program_id(1)
    nt = tq // LANE
    nblk = A_ref.shape[0]

    @pl.when(i == 0)
    def _():
        kmean_ref[...] = _dot(A_ref[...], k_ref[...])

    blk = lax.broadcasted_iota(jnp.int32, (nblk, tq), 0)
    row = lax.broadcasted_iota(jnp.int32, (tq, tq), 0)
    col = lax.broadcasted_iota(jnp.int32, (tq, tq), 1)
    gates, diag = [], []
    for h in range(4):
        p = h // 2
        qm = _pair_masked_q(_lane_tiles(qT_ref, 0, nt, p * LANE, LANE), h)
        qm_ref[h] = qm
        k1, k2, k3 = _split3(kmean_ref[:, p * LANE:(p + 1) * LANE])
        gates.append(_dot(k1, qm) + _dot(k2, qm) + _dot(k3, qm))
        kb = k_ref[pl.ds(pl.multiple_of(i * tq, tq), tq), p * LANE:(p + 1) * LANE]
        diag.append(_dot(kb, qm))
    for h in range(4):
        g = jnp.where(blk < i, gates[h], NEG)
        sel = jnp.zeros((nblk, tq), F32)
        for _r in range(MOBA_TOPK):
            mx = jnp.max(g, axis=0, keepdims=True)
            idx = jnp.min(jnp.where(g == mx, blk, nblk), axis=0, keepdims=True)
            hit = blk == idx
            sel = jnp.where(hit, 1.0, sel)
            g = jnp.where(hit, -jnp.inf, g)
        sel_ref[h] = jnp.where(blk < i, sel, 0.0)

        s = jnp.where(row <= col, diag[h], NEG)
        m = jnp.max(s, axis=0, keepdims=True)
        ml_ref[h:h + 1, :] = m
        acc_ref[h * HA:(h + 1) * HA, :] = _dot(
            _v_aug(_lane_tiles(vT_ref, i * nt, nt, h * HEAD_DIM, HEAD_DIM)), jnp.exp2(s - m).astype(BF16))

    last = jnp.maximum(i - 1, 0)

    def qk(t, slot):
        for h in range(4):
            p = h // 2
            kbj = k_ref[pl.ds(pl.multiple_of(t * tq, tq), tq), p * LANE:(p + 1) * LANE]
            s_ref[slot, h] = _dot(kbj, qm_ref[h])

    def update(t, valid, slot):
        for h in range(4):
            srow = sel_ref[h, pl.ds(t, 1), :] * valid
            vb = _v_aug(_lane_tiles(vT_ref, t * nt, nt, h * HEAD_DIM, HEAD_DIM))
            rows = slice(h * HA, (h + 1) * HA)
            m, acc = _online_cols(s_ref[slot, h], srow, ml_ref[h:h + 1, :], acc_ref[rows, :], vb)
            ml_ref[h:h + 1, :] = m
            acc_ref[rows, :] = acc

    qk(0, 0)

    def body(jj, c):
        t0 = 2 * jj
        qk(t0 + 1, 1)
        update(t0, 1.0, 0)
        qk(jnp.minimum(t0 + 2, last), 0)
        update(t0 + 1, 1.0, 1)
        return c

    lax.fori_loop(0, i // 2, body, 0)

    @pl.when(i % 2 == 1)
    def _():
        update(last, 1.0, 0)
    outs = [acc_ref[h * HA:h * HA + HEAD_DIM, :] / acc_ref[h * HA + HEAD_DIM:h * HA + HEAD_DIM + 1, :]
            for h in range(4)]
    o_ref[...] = jnp.concatenate(outs, axis=0).T.astype(o_ref.dtype)


def _moba_call(oT, otok, A, *, q_chunk, k_chunk, v_chunk):
    B, nlt, _, _ = oT.shape
    S = nlt * LANE
    tq = MOBA_BLOCK
    nblk = S // MOBA_BLOCK
    kern = functools.partial(_moba_kernel, tq=tq)
    return pl.pallas_call(
        kern,
        grid=(B, S // tq),
        in_specs=_attn_specs(S, tq, q_chunk, k_chunk, v_chunk) + [pl.BlockSpec((nblk, S), lambda b, i: (0, 0))],
        out_specs=pl.BlockSpec((None, tq, BRANCH_W), lambda b, i: (b, i, 0)),
        out_shape=jax.ShapeDtypeStruct((B, S, BRANCH_W), BF16),
        scratch_shapes=[pltpu.VMEM((nblk, CH), F32), pltpu.VMEM((4, nblk, tq), F32),
                        pltpu.VMEM((4, LANE, tq), BF16), pltpu.VMEM((4 * HA, tq), F32), pltpu.VMEM((8, tq), F32),
                        pltpu.VMEM((2, 4, tq, tq), F32)],
        compiler_params=_cparams(("parallel", "arbitrary")),
        name="moba",
    )(oT, otok, oT, A)


def _compress_kernel(x_ref, w1_ref, pe_ref, w1f_ref, w2_ref, gk_ref, G_ref, o_ref, oT_ref, acc_ref):
    l = pl.program_id(1)

    @pl.when(l == 0)
    def _():
        acc_ref[...] = jnp.zeros_like(acc_ref)

    x = x_ref[...]
    w_hi, w_lo = _split2(w1_ref[...])
    acc_ref[...] += _dot(x, w_hi) + _dot(x, w_lo)

    @pl.when(l == pl.num_programs(1) - 1)
    def _():
        nc = acc_ref.shape[0]
        r = acc_ref[...]
        bias = jnp.dot(pe_ref[...], w1f_ref[...], preferred_element_type=F32,
                       precision=lax.Precision.HIGHEST)
        p1 = jnp.concatenate([r[:, 0:64], r[:, 128:192]], axis=1)
        p2 = jnp.concatenate([r[:, 64:128], r[:, 192:256]], axis=1)
        hid = p1 + pltpu.roll(p2, nc - 1, 0) + bias[0:1, :]
        hid = hid * jax.nn.sigmoid(hid)
        comp = jnp.dot(hid, w2_ref[...], preferred_element_type=F32, precision=lax.Precision.HIGHEST)
        ss = _dot((comp * comp).astype(BF16), G_ref[...])
        lane = lax.broadcasted_iota(jnp.int32, comp.shape, 1)
        inv = jnp.where(lane < HEAD_DIM, lax.rsqrt(ss * (1.0 / HEAD_DIM) + EPS), 1.0)
        comp = comp * inv * gk_ref[...]
        rowi = lax.broadcasted_iota(jnp.int32, comp.shape, 0)
        comp = jnp.where(rowi < nc - 1, comp, 0.0)
        o_ref[...] = comp.astype(o_ref.dtype)
        oT_ref[...] = comp.T.astype(oT_ref.dtype)


def _compress_call(xv, w1blk, pe2, w1f, w2blk, gk, G128):
    B, nc, _ = xv.shape
    return pl.pallas_call(
        _compress_kernel,
        grid=(B, NSA_CMP_STRIDE),
        in_specs=[
            pl.BlockSpec((None, nc, LANE), lambda b, l: (b, 0, l)),
            pl.BlockSpec((None, LANE, CH), lambda b, l: (l, 0, 0)),
            pl.BlockSpec((8, 2 * NSA_CMP_LEN * HEAD_DIM), lambda b, l: (0, 0)),
            pl.BlockSpec((2 * NSA_CMP_LEN * HEAD_DIM, LANE), lambda b, l: (0, 0)),
            pl.BlockSpec((LANE, LANE), lambda b, l: (0, 0)),
            pl.BlockSpec((1, LANE), lambda b, l: (0, 0)),
            pl.BlockSpec((LANE, LANE), lambda b, l: (0, 0)),
        ],
        out_specs=[
            pl.BlockSpec((None, nc, LANE), lambda b, l: (b, 0, 0)),
            pl.BlockSpec((None, LANE, nc), lambda b, l: (b, 0, 0)),
        ],
        out_shape=[
            jax.ShapeDtypeStruct((B, nc, LANE), BF16),
            jax.ShapeDtypeStruct((B, LANE, nc), BF16),
        ],
        scratch_shapes=[pltpu.VMEM((nc, CH), F32)],
        compiler_params=_cparams(("parallel", "arbitrary")),
        name="nsa_compress",
    )(xv, w1blk, pe2, w1f, w2blk, gk, G128)


def _nsa_kernel(qT_ref, k_ref, vT_ref, kc_ref, kcT_ref, ov_ref, o_ref, sel_ref, qm_ref, acc_ref, ml_ref, s_ref,
                win_ref, *, tq):
    i = pl.program_id(1)
    nt = tq // LANE
    nq = 4 * tq
    nc = kc_ref.shape[0]
    nsel = ov_ref.shape[0]
    spb = tq // NSA_SEL_BLOCK

    zeros = jnp.zeros((HEAD_DIM, tq), BF16)
    q_heads = [_lane_tiles(qT_ref, 0, nt, h * HEAD_DIM, HEAD_DIM) for h in range(4)]
    qm_lo = jnp.concatenate([jnp.concatenate([q, zeros], axis=0) for q in q_heads], axis=1)
    qm_hi = jnp.concatenate([jnp.concatenate([zeros, q], axis=0) for q in q_heads], axis=1)
    qpos = i * tq + lax.broadcasted_iota(jnp.int32, (1, tq), 1)
    qpos4 = jnp.concatenate([qpos] * 4, axis=1)

    n_prev = -(-(NSA_WINDOW - 1) // tq)
    nk = n_prev + 1
    kt0 = jnp.maximum(i - n_prev, 0)
    kw = k_ref[pl.ds(pl.multiple_of(kt0 * tq, tq), nk * tq), LANE:2 * LANE]
    sw = _dot(kw, qm_hi)
    dist = qpos - (kt0 * tq + lax.broadcasted_iota(jnp.int32, (nk * tq, tq), 0))
    wbias = jnp.where(dist >= 0, jnp.where(dist <= NSA_WINDOW - 1, 0.0, NEG), NEG)
    sw = sw + jnp.concatenate([wbias] * 4, axis=1)
    mw = jnp.max(sw, axis=0, keepdims=True)
    ew = jnp.exp2(sw - mw)
    dw = jnp.sum(ew, axis=0, keepdims=True)
    vwin = _lane_tiles(vT_ref, kt0 * nt, nk * nt, HEAD_DIM, HEAD_DIM)
    win_ref[...] = _dot(vwin, ew.astype(BF16)) / dw

    zc = _dot(kc_ref[...], qm_lo)
    c_end = lax.broadcasted_iota(jnp.int32, (nc, tq), 0) * NSA_CMP_STRIDE + (NSA_CMP_LEN - 1)
    cbias = jnp.where(c_end <= qpos, 0.0, NEG)
    zc = zc + jnp.concatenate([cbias] * 4, axis=1)
    e = jnp.exp2(zc - jnp.max(zc, axis=0, keepdims=True))
    seen = jnp.where(qpos4 >= NSA_CMP_LEN - 1, 1.0, 0.0)
    pc = e * (seen / jnp.maximum(jnp.sum(e, axis=0, keepdims=True), 1.0))
    o_cmp = _dot(kcT_ref[HEAD_DIM:2 * HEAD_DIM, :], pc.astype(BF16))

    psum = pc[:, 0:tq] + pc[:, tq:2 * tq] + pc[:, 2 * tq:3 * tq] + pc[:, 3 * tq:4 * tq]
    p_hi, p_lo = _split2(psum)
    imp = _dot(ov_ref[...], p_hi) + _dot(ov_ref[...], p_lo)
    nid = lax.broadcasted_iota(jnp.int32, (nsel, tq), 0)
    cur = qpos // NSA_SEL_BLOCK
    imp = jnp.where(nid == 0, BIG, imp)
    imp = jnp.where(nid == cur, BIG, imp)
    imp = jnp.where(nid == cur - 1, BIG, imp)
    imp = jnp.where(nid > cur, NEG, imp)
    for _r in range(min(NSA_SEL_TOPK, nsel)):
        mx = jnp.max(imp, axis=0, keepdims=True)
        idx = jnp.min(jnp.where(imp == mx, nid, nsel), axis=0, keepdims=True)
        imp = jnp.where(nid == idx, -jnp.inf, imp)
    sel = jnp.where(nid <= cur, jnp.where(imp == -jnp.inf, 1.0, 0.0), 0.0)
    sel_ref[...] = sel
    for h in range(4):
        qm_ref[h] = qm_lo[:, h * tq:(h + 1) * tq]

    def qk(t, slot):
        kb = k_ref[pl.ds(pl.multiple_of(t * tq, tq), tq), LANE:2 * LANE]
        for h in range(4):
            s_ref[slot, h] = _dot(kb, qm_ref[h])

    kpos_d = i * tq + lax.broadcasted_iota(jnp.int32, (tq, tq), 0)
    qk(i, 0)
    vb_d = _v_aug(_lane_tiles(vT_ref, i * nt, nt, 0, HEAD_DIM))
    srows_d = [sel_ref[pl.ds(i * spb + u, 1), :] for u in range(spb)]
    for h in range(4):
        sc = s_ref[0, h]
        s = jnp.concatenate(
            [jnp.where(srows_d[u] > 0.0, sc[u * NSA_SEL_BLOCK:(u + 1) * NSA_SEL_BLOCK, :], NEG)
             for u in range(spb)], axis=0)
        s = jnp.where(kpos_d <= qpos, s, NEG)
        m = jnp.max(s, axis=0, keepdims=True)
        lanes = slice(h * tq, (h + 1) * tq)
        ml_ref[0:1, lanes] = m
        acc_ref[:, lanes] = _dot(vb_d, jnp.exp2(s - m).astype(BF16))

    last = jnp.maximum(i - 1, 0)

    def update(t, valid, slot):
        vb = _v_aug(_lane_tiles(vT_ref, t * nt, nt, 0, HEAD_DIM))
        srows = [sel_ref[pl.ds(t * spb + u, 1), :] * valid for u in range(spb)]
        for h in range(4):
            lanes = slice(h * tq, (h + 1) * tq)
            sc = s_ref[slot, h]
            subs = [sc[u * NSA_SEL_BLOCK:(u + 1) * NSA_SEL_BLOCK, :] for u in range(spb)]
            tmax = None
            for u in range(spb):
                mu = jnp.where(srows[u] > 0.0, jnp.max(subs[u], axis=0, keepdims=True), NEG)
                tmax = mu if tmax is None else jnp.maximum(tmax, mu)
            m_old = ml_ref[0:1, lanes]
            m_new = jnp.maximum(m_old, tmax)
            p = jnp.concatenate(
                [jnp.exp2(subs[u] - jnp.where(srows[u] > 0.0, m_new, BIGPOS)).astype(BF16) for u in range(spb)],
                axis=0)
            ml_ref[0:1, lanes] = m_new
            acc_ref[:, lanes] = jnp.exp2(m_old - m_new) * acc_ref[:, lanes] + _dot(vb, p)

    qk(0, 0)

    def body(jj, c):
        t0 = 2 * jj
        qk(t0 + 1, 1)
        update(t0, 1.0, 0)
        qk(jnp.minimum(t0 + 2, last), 0)
        update(t0 + 1, 1.0, 1)
        return c

    lax.fori_loop(0, i // 2, body, 0)

    @pl.when(i % 2 == 1)
    def _():
        update(last, 1.0, 0)
    o_sel = acc_ref[0:HEAD_DIM, :] / acc_ref[HEAD_DIM:HEAD_DIM + 1, :]

    o_win = win_ref[...]

    gates = _lane_tiles(vT_ref, i * nt, nt, 2 * HEAD_DIM, 16).astype(F32)
    outs = []
    for h in range(4):
        sl = slice(h * tq, (h + 1) * tq)
        outs.append(gates[3 * h:3 * h + 1, :] * o_cmp[:, sl]
                    + gates[3 * h + 1:3 * h + 2, :] * o_sel[:, sl]
                    + gates[3 * h + 2:3 * h + 3, :] * o_win[:, sl])
    o_ref[...] = jnp.concatenate(outs, axis=0).T.astype(o_ref.dtype)


def _nsa_call(oT, otok, kc, kcT, ovT, *, q_chunk, kv_chunk, v_chunk):
    B, nlt, _, _ = oT.shape
    S = nlt * LANE
    tq = 256
    nc = S // NSA_CMP_STRIDE
    nsel = S // NSA_SEL_BLOCK
    kern = functools.partial(_nsa_kernel, tq=tq)
    return pl.pallas_call(
        kern,
        grid=(B, S // tq),
        in_specs=_attn_specs(S, tq, q_chunk, kv_chunk, v_chunk) + [
            pl.BlockSpec((None, nc, LANE), lambda b, i: (b, 0, 0)),
            pl.BlockSpec((None, LANE, nc), lambda b, i: (b, 0, 0)),
            pl.BlockSpec((nsel, nc), lambda b, i: (0, 0)),
        ],
        out_specs=pl.BlockSpec((None, tq, BRANCH_W), lambda b, i: (b, i, 0)),
        out_shape=jax.ShapeDtypeStruct((B, S, BRANCH_W), BF16),
        scratch_shapes=[pltpu.VMEM((nsel, tq), F32), pltpu.VMEM((4, LANE, tq), BF16),
                        pltpu.VMEM((HA, 4 * tq), F32), pltpu.VMEM((8, 4 * tq), F32),
                        pltpu.VMEM((2, 4, tq, tq), F32), pltpu.VMEM((HEAD_DIM, 4 * tq), F32)],
        compiler_params=_cparams(("parallel", "arbitrary")),
        name="nsa",
    )(oT, otok, oT, kc, kcT, ovT)


def _epi_kernel(x_ref, ng_ref, ya_ref, yc_ref, yd_ref, ob0_ref, ob1_ref, ob2_ref, l0_ref, l1_ref, l2_ref,
                wz_ref, wmg_ref, wbr_ref, wout_ref, o_ref, nat_ref, *, tm):
    x = x_ref[...]
    ms = jnp.mean(x * x, axis=-1, keepdims=True)
    xn = (x * lax.rsqrt(ms + EPS) * ng_ref[...]).astype(BF16)

    def natural(ref, slot, dil):
        nh = BRANCH_W // LANE
        for r in range(dil):
            for hf in range(nh):
                c0 = r * BRANCH_W + hf * LANE
                nat_ref[slot * nh + hf, pl.ds(r, tm // dil, stride=dil), :] = ref[:, c0:c0 + LANE].astype(F32)
        return jnp.concatenate([nat_ref[slot * nh + hf] for hf in range(nh)], axis=1)

    d1, d2 = DIL_PAIRS[1][1], DIL_PAIRS[2][1]
    l0, l1, l2 = l0_ref[...], natural(l1_ref, 0, d1), natural(l2_ref, 1, d2)
    mx = jnp.maximum(jnp.maximum(l0, l1), l2)
    e0, e1, e2 = jnp.exp2(l0 - mx), jnp.exp2(l1 - mx), jnp.exp2(l2 - mx)
    yb = (e0 * ob0_ref[...].astype(F32) + e1 * natural(ob1_ref, 2, d1)
          + e2 * natural(ob2_ref, 3, d2)) / (e0 + e1 + e2)

    ys = (ya_ref[...].astype(F32), yb, yc_ref[...].astype(F32), yd_ref[...].astype(F32))
    merged = None
    for i in range(4):
        z = _dot(xn, wz_ref[:, i * BRANCH_W:(i + 1) * BRANCH_W])
        gated = (ys[i] * (z * jax.nn.sigmoid(z))).astype(BF16)
        br = _dot(gated, wbr_ref[i])
        mg = _dot(xn, wmg_ref[:, i * D_MODEL:(i + 1) * D_MODEL])
        term = jax.nn.sigmoid(mg) * br
        merged = term if merged is None else merged + term
    o_ref[...] = x + _dot(merged.astype(BF16), wout_ref[...])


def _epi_call(x2, ng, ya, yc, yd, obs, lses, wz, wmg, wbr, wout, *, layer, tm=256):
    T = x2.shape[0]
    row = lambda i: (i, 0)
    full2 = lambda i: (0, 0)
    yspec = pl.BlockSpec((tm, BRANCH_W), row)
    d1, d2 = DIL_PAIRS[1][1], DIL_PAIRS[2][1]
    gspecs = [yspec, pl.BlockSpec((tm // d1, d1 * BRANCH_W), row), pl.BlockSpec((tm // d2, d2 * BRANCH_W), row)]
    return pl.pallas_call(
        functools.partial(_epi_kernel, tm=tm),
        grid=(T // tm,),
        in_specs=[pl.BlockSpec((tm, D_MODEL), row), pl.BlockSpec((1, D_MODEL), full2)]
        + [yspec] * 3 + gspecs + gspecs
        + [pl.BlockSpec((None, D_MODEL, 4 * BRANCH_W), lambda i: (layer, 0, 0)),
           pl.BlockSpec((None, D_MODEL, 4 * D_MODEL), lambda i: (layer, 0, 0)),
           pl.BlockSpec((None, 4, BRANCH_W, D_MODEL), lambda i: (layer, 0, 0, 0)),
           pl.BlockSpec((None, D_MODEL, D_MODEL), lambda i: (layer, 0, 0))],
        out_specs=pl.BlockSpec((tm, D_MODEL), row),
        out_shape=jax.ShapeDtypeStruct((T, D_MODEL), F32),
        scratch_shapes=[pltpu.VMEM((4 * (BRANCH_W // LANE), tm, LANE), F32)],
        compiler_params=_cparams(("parallel",)),
        name="epilogue",
    )(x2, ng, ya, yc, yd, *obs, *lses, wz, wmg, wbr, wout)


TOK_AK, TOK_BQ, TOK_BK, TOK_BV, TOK_CK, TOK_DKV = range(6)
N_MAIN = 6
T_AQ, T_AV, T_CQ, T_CV, T_DQ, T_DX = range(6)
SCALE = 1.0 / math.sqrt(HEAD_DIM)
QSCALE2 = SCALE * LOG2E

_PLAIN = (False, False, 1.0, False)
_T_KINDS = (
    ((False, False, QSCALE2, False),) * 4,
    (_PLAIN,) * 4,
    ((True, True, QSCALE2, False),) * 4,
    (_PLAIN,) * 4,
    ((True, True, QSCALE2, False),) * 4,
    (_PLAIN, _PLAIN, (False, False, 1.0, True), _PLAIN),
)


def _tok_specs():
    main = lambda j: (0, 1, j * CH, CH, 0)
    qkv = ((True, True), (True, True), (False, False))
    specs = []
    for j in range(N_MAIN):
        if j in (TOK_BQ, TOK_BK, TOK_CK):
            specs.append((True, True, (main(j),)))
        elif j == TOK_DKV:
            specs.append(("mixed", "mixed", (main(j), (3, NSA_CMP_STRIDE, 0, LANE, LANE))))
        else:
            specs.append((False, False, (main(j),)))
    for gi in (1, 2):
        dil = DIL_PAIRS[gi][1]
        for k, (nrm, rope) in enumerate(qkv):
            specs.append((nrm, rope, ((gi, dil, k * CH, CH, 3 * CH),)))
    return tuple(specs)


def _rope_tables(S, dil):
    L = S // dil
    pos = (jnp.arange(dil, dtype=jnp.int32)[:, None] + dil * jnp.arange(L, dtype=jnp.int32)[None, :]).astype(F32)
    inv = ROPE_THETA ** (-jnp.arange(0, ROT_DIM, 2, dtype=F32) / ROT_DIM)
    ang = pos[:, :, None] * inv[None, None, :]
    cos, sin = jnp.cos(ang), jnp.sin(ang)
    one = jnp.ones((dil, L, HEAD_DIM - ROT_DIM), F32)
    zero8 = jnp.zeros((dil, L, 8), F32)
    zero = jnp.zeros_like(one)
    c_head = jnp.concatenate([cos, cos, one], axis=-1)
    s1_head = jnp.concatenate([zero8, sin, zero], axis=-1)
    s2_head = jnp.concatenate([-sin, zero8, zero], axis=-1)
    rtok = jnp.stack([jnp.tile(t, (1, 1, LANE // HEAD_DIM)) for t in (c_head, s1_head, s2_head)], axis=1)
    rT = jnp.stack([cos.transpose(0, 2, 1), sin.transpose(0, 2, 1)], axis=1)
    return rtok, rT


def _tok_params(rows):
    out = []
    for nf, rf, gains in rows:
        nrow = jnp.concatenate([jnp.full((HEAD_DIM,), float(f), F32) for f in nf])
        rrow = jnp.concatenate([jnp.full((HEAD_DIM,), float(f), F32) for f in rf])
        grow = jnp.concatenate([g.astype(F32) for g in gains])
        out.append(jnp.concatenate([jnp.stack([nrow, rrow, grow]), jnp.zeros((5, CH), F32)], axis=0))
    return jnp.stack(out)


def kernel(x, norm_g, w_in, qk_g, cmp_pe, cmp_w1, cmp_w2, w_branch, w_out):
    B, S, _ = x.shape
    T = B * S
    o = COL_OFF
    ones = jnp.ones((HEAD_DIM,), F32)

    r = np.arange(CH)
    G = jnp.asarray((r[:, None] // HEAD_DIM == r[None, :] // HEAD_DIM).astype(np.float32), BF16)
    G128 = G[:LANE, :LANE]
    t = np.arange(256)
    U = jnp.asarray((t[None, :] > t[:, None]).astype(np.float32), BF16)
    nblk = S // MOBA_BLOCK
    A = jnp.asarray(np.repeat(np.eye(nblk, dtype=np.float32), MOBA_BLOCK, axis=1) / MOBA_BLOCK, BF16)
    nc, nsel = S // NSA_CMP_STRIDE, S // NSA_SEL_BLOCK
    cs = np.arange(nc) * NSA_CMP_STRIDE
    ss = np.arange(nsel) * NSA_SEL_BLOCK
    ov = ((cs[None, :] < ss[:, None] + NSA_SEL_BLOCK) & (cs[None, :] + NSA_CMP_LEN > ss[:, None]))
    ov[:, nc - 1] = False
    ovT = jnp.asarray(ov.astype(np.float32), BF16)
    hid = np.arange(LANE)[:, None]
    ind = jnp.asarray(((hid < 4) & (r[None, :] // HEAD_DIM == hid)).astype(np.float32), BF16)
    rtok4, rT4 = _rope_tables(S, 1)
    rtok, rT = rtok4[0], rT4[0]
    tok_specs = _tok_specs()

    dkv = o[13]
    bq = lambda gi: (o[4] + gi * CH, o[4] + (gi + 1) * CH)
    bk = lambda gi: (o[5] + gi * CH, o[5] + (gi + 1) * CH)
    bv = lambda gi: (o[6] + gi * CH, o[6] + (gi + 1) * CH)
    tok_plan = ((o[1], o[2]), bq(0), bk(0), bv(0), (o[9], o[10]),
                (dkv, dkv + 64), (dkv + 64, dkv + 128), (dkv + 128, dkv + 192), (dkv + 256, dkv + 320),
                bq(1), bk(1), bv(1), bq(2), bk(2), bv(2))
    T_plan = ((o[0], o[1]), (o[2], o[3]), (o[8], o[9]), (o[10], o[11]), (o[12], o[13]),
              (dkv + 192, dkv + 256), (dkv + 320, dkv + 384), (o[14], o[15]))
    z_plan = ((o[3], o[4]), (o[7], o[8]), (o[11], o[12]), (o[15], o[16]))
    wtok_all, wT_all, wz_all, wmg_all = _repack_call(
        w_in, (tok_plan, T_plan, z_plan, ((o[16], o[17]),)),
        (len(tok_specs) * CH, len(_T_KINDS) * CH, 4 * BRANCH_W, 4 * D_MODEL))

    wbr_all = w_branch.astype(BF16)
    wout_all = w_out.astype(BF16)

    for l in range(DEPTH):
        g = qk_g[l]
        plain = ((0,) * 4, (0,) * 4, (ones,) * 4)
        bq_p = ((1,) * 4, (1,) * 4, (g[0] * QSCALE2,) * 4)
        bk_p = ((1,) * 4, (1,) * 4, (g[1],) * 4)
        tokp = _tok_params([
            plain, bq_p, bk_p, plain,
            ((1,) * 4, (1,) * 4, (g[3],) * 4),
            ((0, 0, 1, 1), (1, 0, 1, 1), (ones, ones, g[6], g[7])),
            bq_p, bk_p, plain, bq_p, bk_p, plain])
        gT = jnp.concatenate([jnp.tile(ones, 8), jnp.tile(g[2], 4), jnp.tile(ones, 4), jnp.tile(g[4], 4),
                              jnp.tile(ones, 4)])[:, None]
        ng = norm_g[l][None, :]
        otok, og1, og2, okv, oT = _proj_call(x, ng, wtok_all, wT_all, tokp, gT, rtok, rT, G, layer=l,
                                             tok_specs=tok_specs, T_kinds=_T_KINDS, n_main=N_MAIN)

        ya = _sb_call(oT, otok, U, q_chunk=T_AQ, k_chunk=TOK_AK, v_chunk=T_AV)

        obs, lses = [], []
        for gi, (window, dil) in enumerate(DIL_PAIRS):
            if dil == 1:
                ob, lse = _band_call(otok, ind, dil=1, per_res=N_MAIN, qi=TOK_BQ, ki=TOK_BK, vi=TOK_BV,
                                     max_dist=window)
            else:
                ob, lse = _band_call((og1, og2)[gi - 1], ind, dil=dil, per_res=3, qi=0, ki=1, vi=2,
                                     max_dist=window // dil)
            obs.append(ob.reshape(T // dil, dil * BRANCH_W))
            lses.append(lse.reshape(T // dil, dil * BRANCH_W))

        yc = _moba_call(oT, otok, A, q_chunk=T_CQ, k_chunk=TOK_CK, v_chunk=T_CV)

        w1 = cmp_w1[l].reshape(2, NSA_CMP_LEN, HEAD_DIM, HEAD_DIM)
        z64 = jnp.zeros((NSA_CMP_STRIDE, HEAD_DIM, HEAD_DIM), F32)
        w1blk = jnp.concatenate([
            jnp.concatenate([w1[0, :16], w1[0, 16:], z64, z64], axis=2),
            jnp.concatenate([z64, z64, w1[1, :16], w1[1, 16:]], axis=2)], axis=1)
        pe2 = jnp.concatenate([cmp_pe[l].reshape(1, -1), jnp.zeros((7, 2 * NSA_CMP_LEN * HEAD_DIM), F32)], axis=0)
        zf = jnp.zeros((NSA_CMP_LEN * HEAD_DIM, HEAD_DIM), F32)
        w1f = jnp.concatenate([jnp.concatenate([cmp_w1[l, 0], zf], axis=1),
                               jnp.concatenate([zf, cmp_w1[l, 1]], axis=1)], axis=0)
        z2 = jnp.zeros((HEAD_DIM, HEAD_DIM), F32)
        w2blk = jnp.concatenate([jnp.concatenate([cmp_w2[l, 0], z2], axis=1),
                                 jnp.concatenate([z2, cmp_w2[l, 1]], axis=1)], axis=0)
        gk = jnp.concatenate([g[5], ones])[None, :]
        kc, kcT = _compress_call(okv, w1blk, pe2, w1f, w2blk, gk, G128)
        yd = _nsa_call(oT, otok, kc, kcT, ovT, q_chunk=T_DQ, kv_chunk=TOK_DKV, v_chunk=T_DX)

        x2 = _epi_call(x.reshape(T, D_MODEL), ng, ya.reshape(T, BRANCH_W), yc.reshape(T, BRANCH_W),
                       yd.reshape(T, BRANCH_W), obs, lses, wz_all, wmg_all, wbr_all, wout_all, layer=l)
        x = x2.reshape(B, S, D_MODEL)
    return x
```

```python
import functools
import math

import numpy as np
import jax
import jax.numpy as jnp
from jax import lax
from jax.experimental import pallas as pl
from jax.experimental.pallas import tpu as pltpu

F32 = jnp.float32
BF16 = jnp.bfloat16

D_MODEL = 1024
DEPTH = 4
HEAD_DIM = 64
ROT_DIM = 16
ROPE_THETA = 500000.0
EPS = 1e-6
NEG = -1e30
BIG = 1e9
BRANCH_W = 256
DIL_PAIRS = ((128, 1), (512, 4), (2048, 16))
MOBA_BLOCK = 256
MOBA_TOPK = 3
NSA_CMP_LEN = 32
NSA_CMP_STRIDE = 16
NSA_SEL_BLOCK = 64
NSA_SEL_TOPK = 16
NSA_WINDOW = 512
COL_SIZES = (256, 256, 256, 256, 768, 768, 768, 256, 256, 256, 256, 256, 256, 384, 12, 256, 4096)
COL_OFF = tuple(int(v) for v in np.concatenate([[0], np.cumsum(COL_SIZES)]))

LANE = 128
CH = 256
VMEM_LIMIT = 56 * 1024 * 1024


def _cparams(sem):
    return pltpu.CompilerParams(dimension_semantics=sem, vmem_limit_bytes=VMEM_LIMIT)


def _dot(a, b):
    return jnp.dot(a, b, preferred_element_type=F32)


def _dot_nt(a, b):
    return lax.dot_general(a, b, (((1,), (1,)), ((), ())), preferred_element_type=F32)


def _split2(x):
    hi = x.astype(BF16)
    lo = (x - hi.astype(F32)).astype(BF16)
    return hi, lo


def _split3(x):
    hi = x.astype(BF16)
    r = x - hi.astype(F32)
    mid = r.astype(BF16)
    lo = (r - mid.astype(F32)).astype(BF16)
    return hi, mid, lo


def _repack_kernel(w_ref, *out_refs, plans):
    for o_ref, ranges in zip(out_refs, plans):
        width = o_ref.shape[-1]
        parts = [w_ref[:, a:b] for a, b in ranges]
        used = sum(b - a for a, b in ranges)
        if used < width:
            parts.append(jnp.zeros((w_ref.shape[0], width - used), F32))
        o_ref[...] = (parts[0] if len(parts) == 1 else jnp.concatenate(parts, axis=1)).astype(o_ref.dtype)


def _repack_call(w_in, plans, widths, rows=128):
    depth, d, n_in = w_in.shape
    kern = functools.partial(_repack_kernel, plans=plans)
    return pl.pallas_call(
        kern,
        grid=(depth, d // rows),
        in_specs=[pl.BlockSpec((None, rows, n_in), lambda l, i: (l, i, 0))],
        out_specs=[pl.BlockSpec((None, rows, w), lambda l, i: (l, i, 0)) for w in widths],
        out_shape=[jax.ShapeDtypeStruct((depth, d, w), BF16) for w in widths],
        compiler_params=_cparams(("parallel", "parallel")),
        name="repack",
    )(w_in)


def _proj_kernel(x_ref, ng_ref, wtok_ref, wT_ref, tokp_ref, gT_ref, rtok_ref, rT_ref, G_ref,
                 omain_ref, og1_ref, og2_ref, okv_ref, oT_ref, stage_ref, *, tok_specs, T_kinds, tm):
    x = x_ref[...]
    ms = jnp.mean(x * x, axis=-1, keepdims=True)
    xn = (x * lax.rsqrt(ms + EPS) * ng_ref[...]).astype(BF16)

    dsts = (omain_ref, og1_ref, og2_ref, okv_ref)
    for c, (has_norm, has_rope, outs) in enumerate(tok_specs):
        y = _dot(xn, wtok_ref[:, c * CH:(c + 1) * CH])
        prm = tokp_ref[c]
        if has_norm:
            ss = _dot((y * y).astype(BF16), G_ref[...])
            inv = lax.rsqrt(ss * (1.0 / HEAD_DIM) + EPS)
            y = y * (jnp.where(prm[0:1, :] > 0.0, inv, 1.0) if has_norm == "mixed" else inv)
        y = y * prm[2:3, :]
        if has_rope:
            halves = []
            for hf in range(CH // LANE):
                yh = y[:, hf * LANE:(hf + 1) * LANE]
                cc, s1, s2 = rtok_ref[0], rtok_ref[1], rtok_ref[2]
                if has_rope == "mixed":
                    rf = prm[1:2, hf * LANE:(hf + 1) * LANE]
                    cc, s1, s2 = jnp.where(rf > 0.0, cc, 1.0), s1 * rf, s2 * rf
                halves.append(yh * cc + pltpu.roll(yh, 8, 1) * s1 + pltpu.roll(yh, LANE - 8, 1) * s2)
            y = jnp.concatenate(halves, axis=1)
        staged = False
        for dst, dil, col, lanes, rstride in outs:
            if dil == 1:
                dsts[dst][:, col:col + lanes] = y[:, :lanes].astype(BF16)
                continue
            if not staged:
                for hf in range(CH // LANE):
                    stage_ref[hf] = y[:, hf * LANE:(hf + 1) * LANE]
                staged = True
            for r in range(dil):
                for hf in range(lanes // LANE):
                    blk = stage_ref[hf, pl.ds(r, tm // dil, stride=dil), :]
                    c0 = r * rstride + col + hf * LANE
                    dsts[dst][:, c0:c0 + LANE] = blk.astype(BF16)

    cosT = rT_ref[0]
    sinT = rT_ref[1]
    for c, heads in enumerate(T_kinds):
        y = _dot(xn, wT_ref[:, c * CH:(c + 1) * CH]).T
        for h, (nrm, rope, scale, sigm) in enumerate(heads):
            r0 = c * CH + h * HEAD_DIM
            yh = y[h * HEAD_DIM:(h + 1) * HEAD_DIM, :]
            if nrm:
                msq = jnp.mean(yh * yh, axis=0, keepdims=True)
                yh = yh * lax.rsqrt(msq + EPS) * gT_ref[r0:r0 + HEAD_DIM, :]
            if rope:
                x1 = yh[0:8, :]
                x2 = yh[8:16, :]
                yh = jnp.concatenate([x1 * cosT - x2 * sinT, x2 * cosT + x1 * sinT, yh[16:, :]], axis=0)
            if scale != 1.0:
                yh = yh * scale
            if sigm:
                yh = jax.nn.sigmoid(yh)
            yb = yh.astype(BF16)
            for t in range(tm // LANE):
                oT_ref[t, r0:r0 + HEAD_DIM, :] = yb[:, t * LANE:(t + 1) * LANE]


def _proj_call(x, ng, wtok, wT, tokp, gT, rtok, rT, G, *, layer, tok_specs, T_kinds, n_main, tm=1024):
    B, S, _ = x.shape
    ntok, nT = len(tok_specs), len(T_kinds)
    d1, d2 = DIL_PAIRS[1][1], DIL_PAIRS[2][1]
    kern = functools.partial(_proj_kernel, tok_specs=tok_specs, T_kinds=T_kinds, tm=tm)
    const2 = lambda b, n: (0, 0)
    return pl.pallas_call(
        kern,
        grid=(B, S // tm),
        in_specs=[
            pl.BlockSpec((None, tm, D_MODEL), lambda b, n: (b, n, 0)),
            pl.BlockSpec((1, D_MODEL), const2),
            pl.BlockSpec((None, D_MODEL, ntok * CH), lambda b, n: (layer, 0, 0)),
            pl.BlockSpec((None, D_MODEL, nT * CH), lambda b, n: (layer, 0, 0)),
            pl.BlockSpec((ntok, 8, CH), lambda b, n: (0, 0, 0)),
            pl.BlockSpec((nT * CH, 1), const2),
            pl.BlockSpec((3, tm, LANE), lambda b, n: (0, n, 0)),
            pl.BlockSpec((2, 8, tm), lambda b, n: (0, 0, n)),
            pl.BlockSpec((CH, CH), const2),
        ],
        out_specs=[
            pl.BlockSpec((None, tm, n_main * CH), lambda b, n: (b, n, 0)),
            pl.BlockSpec((None, tm // d1, d1 * 3 * CH), lambda b, n: (b, n, 0)),
            pl.BlockSpec((None, tm // d2, d2 * 3 * CH), lambda b, n: (b, n, 0)),
            pl.BlockSpec((None, tm // NSA_CMP_STRIDE, NSA_CMP_STRIDE * LANE), lambda b, n: (b, n, 0)),
            pl.BlockSpec((None, tm // LANE, nT * CH, LANE), lambda b, n: (b, n, 0, 0)),
        ],
        out_shape=[
            jax.ShapeDtypeStruct((B, S, n_main * CH), BF16),
            jax.ShapeDtypeStruct((B, S // d1, d1 * 3 * CH), BF16),
            jax.ShapeDtypeStruct((B, S // d2, d2 * 3 * CH), BF16),
            jax.ShapeDtypeStruct((B, S // NSA_CMP_STRIDE, NSA_CMP_STRIDE * LANE), BF16),
            jax.ShapeDtypeStruct((B, S // LANE, nT * CH, LANE), BF16),
        ],
        scratch_shapes=[pltpu.VMEM((CH // LANE, tm, LANE), F32)],
        compiler_params=_cparams(("parallel", "parallel")),
        name="proj",
    )(x, ng, wtok, wT, tokp, gT, rtok, rT, G)


def _pair_masked_q(q_pair, h):
    rid = lax.broadcasted_iota(jnp.int32, q_pair.shape, 0)
    lo = (h % 2) * HEAD_DIM
    keep = jnp.where(rid >= lo, jnp.where(rid < lo + HEAD_DIM, 1.0, 0.0), 0.0).astype(BF16)
    return q_pair * keep


def _lane_tiles(ref, t0, nt, r0, nr):
    return jnp.concatenate([ref[t0 + t, r0:r0 + nr, :] for t in range(nt)], axis=1)


BIGPOS = 1e30
ONES_ROWS = 16
HA = HEAD_DIM + ONES_ROWS
LOG2E = 1.4426950408889634


def _v_aug(vb):
    return jnp.concatenate([vb, jnp.ones((ONES_ROWS, vb.shape[1]), BF16)], axis=0)


def _online_cols(s, colsel, m_old, acc_old, vb_aug):
    tmax = jnp.max(s, axis=0, keepdims=True)
    m_new = jnp.where(colsel > 0.0, jnp.maximum(m_old, tmax), m_old)
    m_use = jnp.where(colsel > 0.0, m_new, BIGPOS)
    p = jnp.exp2(s - m_use).astype(BF16)
    acc = jnp.exp2(m_old - m_new) * acc_old + _dot(vb_aug, p)
    return m_new, acc


SB_LOG_CUTOFF = -144.0


def _sb_kernel(qT_ref, k_ref, vT_ref, U_ref, o_ref, qm_ref, acc_ref, carry_ref, *, tq):
    i = pl.program_id(1)
    nt = tq // LANE
    row = lax.broadcasted_iota(jnp.int32, (tq, tq), 0)
    col = lax.broadcasted_iota(jnp.int32, (tq, tq), 1)
    past = row < col
    for h in range(4):
        p = h // 2
        qm_ref[h] = _pair_masked_q(_lane_tiles(qT_ref, 0, nt, p * LANE, LANE), h)
    acc_ref[...] = jnp.zeros_like(acc_ref)
    carry_ref[...] = jnp.zeros_like(carry_ref)

    def tile(j, masked):
        scores, logsig, laters = [], [], []
        for h in range(4):
            p = h // 2
            kb = k_ref[pl.ds(pl.multiple_of(j * tq, tq), tq), p * LANE:(p + 1) * LANE]
            scores.append(_dot(kb, qm_ref[h]))
        worst = None
        for h in range(4):
            s = scores[h]
            sp = jnp.maximum(s, 0.0) + jnp.log2(1.0 + jnp.exp2(-jnp.abs(s)))
            lg = -sp
            if masked:
                lg = jnp.where(past, lg, 0.0)
            hi, lo = _split2(lg)
            carry = carry_ref[h:h + 1, :]
            laters.append(_dot(U_ref[...], hi) + _dot(U_ref[...], lo) + carry)
            logsig.append(s - sp)
            carry = carry + jnp.sum(lg, axis=0, keepdims=True)
            carry_ref[h:h + 1, :] = carry
            worst = carry if worst is None else jnp.maximum(worst, carry)
        for h in range(4):
            w = jnp.exp2(logsig[h] + laters[h])
            if masked:
                w = jnp.where(past, w, 0.0)
            vb = _lane_tiles(vT_ref, j * nt, nt, h * HEAD_DIM, HEAD_DIM)
            acc_ref[h * HEAD_DIM:(h + 1) * HEAD_DIM, :] += _dot(vb, w.astype(BF16))
        return jnp.max(worst)

    worst0 = tile(i, True)

    def cond(st):
        return jnp.logical_and(st[0] >= 0, st[1] > SB_LOG_CUTOFF)

    def body(st):
        return st[0] - 1, tile(st[0], False)

    lax.while_loop(cond, body, (i - 1, worst0))
    o_ref[...] = acc_ref[...].T.astype(o_ref.dtype)


def _attn_specs(S, tq, q_chunk, k_chunk, v_chunk):
    return [
        pl.BlockSpec((None, tq // LANE, CH, LANE), lambda b, i: (b, i, q_chunk, 0)),
        pl.BlockSpec((None, S, CH), lambda b, i: (b, 0, k_chunk)),
        pl.BlockSpec((None, S // LANE, CH, LANE), lambda b, i: (b, 0, v_chunk, 0)),
    ]


def _sb_call(oT, otok, U, *, q_chunk, k_chunk, v_chunk, tq=256):
    B, nlt, _, _ = oT.shape
    S = nlt * LANE
    kern = functools.partial(_sb_kernel, tq=tq)
    return pl.pallas_call(
        kern,
        grid=(B, S // tq),
        in_specs=_attn_specs(S, tq, q_chunk, k_chunk, v_chunk) + [pl.BlockSpec((tq, tq), lambda b, i: (0, 0))],
        out_specs=pl.BlockSpec((None, tq, BRANCH_W), lambda b, i: (b, i, 0)),
        out_shape=jax.ShapeDtypeStruct((B, S, BRANCH_W), BF16),
        scratch_shapes=[pltpu.VMEM((4, LANE, tq), BF16), pltpu.VMEM((CH, tq), F32), pltpu.VMEM((8, tq), F32)],
        compiler_params=_cparams(("parallel", "arbitrary")),
        name="stick_breaking",
    )(oT, otok, oT, U)


def _head_lanes(x, lane, h):
    lo = h * HEAD_DIM
    return jnp.where(lane >= lo, jnp.where(lane < lo + HEAD_DIM, x, jnp.zeros_like(x)), jnp.zeros_like(x))


def _band_kernel(q_ref, k_ref, v_ref, ind_ref, o_ref, lse_ref, *, tqb, max_dist):
    n = pl.program_id(2)
    n_prev = -(-max_dist // LANE)
    nkr = (n_prev + 1) * LANE
    nsub = tqb // LANE
    row = lax.broadcasted_iota(jnp.int32, (nkr, LANE), 0)
    col = lax.broadcasted_iota(jnp.int32, (nkr, LANE), 1)
    lane_q = lax.broadcasted_iota(jnp.int32, (LANE, LANE), 1)
    lane_v = lax.broadcasted_iota(jnp.int32, (nkr, CH), 1)
    for u in range(nsub):
        qt = n * nsub + u
        kt0 = jnp.maximum(qt - n_prev, 0)
        dist = (qt - kt0) * LANE + col - row
        bias = jnp.where(dist >= 0, jnp.where(dist <= max_dist, 0.0, NEG), NEG)
        k0 = pl.multiple_of(kt0 * LANE, LANE)
        kwin = k_ref[pl.ds(k0, nkr), :]
        vwin = v_ref[pl.ds(k0, nkr), :]
        qu = q_ref[u * LANE:(u + 1) * LANE, :]
        o_acc = None
        lses = []
        for h in range(4):
            p = h // 2
            qm = _head_lanes(qu[:, p * LANE:(p + 1) * LANE], lane_q, h % 2)
            s = _dot_nt(kwin[:, p * LANE:(p + 1) * LANE], qm) + bias
            m = jnp.max(s, axis=0, keepdims=True)
            e = jnp.exp2(s - m)
            den = jnp.sum(e, axis=0, keepdims=True)
            pn = e * (1.0 / den)
            contrib = _dot(pn.T.astype(BF16), _head_lanes(vwin, lane_v, h))
            o_acc = contrib if o_acc is None else o_acc + contrib
            lses.append(m + jnp.log(den) * LOG2E)
        o_ref[u * LANE:(u + 1) * LANE, :] = o_acc.astype(o_ref.dtype)
        ls = jnp.concatenate(lses + [jnp.zeros((LANE - 4, LANE), F32)], axis=0).T
        l1, l2, l3 = _split3(ls)
        lse_ref[u * LANE:(u + 1) * LANE, :] = (_dot(l1, ind_ref[...]) + _dot(l2, ind_ref[...])
                                               + _dot(l3, ind_ref[...]))


def _band_call(arr, ind, *, dil, per_res, qi, ki, vi, max_dist):
    B, L, _ = arr.shape
    tqb = min(1024, L)
    kern = functools.partial(_band_kernel, tqb=tqb, max_dist=max_dist)
    return pl.pallas_call(
        kern,
        grid=(B, dil, L // tqb),
        in_specs=[
            pl.BlockSpec((None, tqb, CH), lambda b, r, n: (b, n, r * per_res + qi)),
            pl.BlockSpec((None, L, CH), lambda b, r, n: (b, 0, r * per_res + ki)),
            pl.BlockSpec((None, L, CH), lambda b, r, n: (b, 0, r * per_res + vi)),
            pl.BlockSpec((LANE, CH), lambda b, r, n: (0, 0)),
        ],
        out_specs=[
            pl.BlockSpec((None, tqb, BRANCH_W), lambda b, r, n: (b, n, r)),
            pl.BlockSpec((None, tqb, BRANCH_W), lambda b, r, n: (b, n, r)),
        ],
        out_shape=[
            jax.ShapeDtypeStruct((B, L, dil * BRANCH_W), BF16),
            jax.ShapeDtypeStruct((B, L, dil * BRANCH_W), F32),
        ],
        compiler_params=_cparams(("parallel", "parallel", "parallel")),
        name="banded",
    )(arr, arr, arr, ind)


def _moba_kernel(qT_ref, k_ref, vT_ref, A_ref, o_ref, kmean_ref, sel_ref, qm_ref, acc_ref, ml_ref, s_ref, *, tq):
    i = pl.program_id(1)
    nt = tq // LANE
    nblk = A_ref.shape[0]

    @pl.when(i == 0)
    def _():
        kmean_ref[...] = _dot(A_ref[...], k_ref[...])

    blk = lax.broadcasted_iota(jnp.int32, (nblk, tq), 0)
    row = lax.broadcasted_iota(jnp.int32, (tq, tq), 0)
    col = lax.broadcasted_iota(jnp.int32, (tq, tq), 1)
    gates, diag = [], []
    for h in range(4):
        p = h // 2
        qm = _pair_masked_q(_lane_tiles(qT_ref, 0, nt, p * LANE, LANE), h)
        qm_ref[h] = qm
        k1, k2, k3 = _split3(kmean_ref[:, p * LANE:(p + 1) * LANE])
        gates.append(_dot(k1, qm) + _dot(k2, qm) + _dot(k3, qm))
        kb = k_ref[pl.ds(pl.multiple_of(i * tq, tq), tq), p * LANE:(p + 1) * LANE]
        diag.append(_dot(kb, qm))
    for h in range(4):
        g = jnp.where(blk < i, gates[h], NEG)
        sel = jnp.zeros((nblk, tq), F32)
        for _r in range(MOBA_TOPK):
            mx = jnp.max(g, axis=0, keepdims=True)
            idx = jnp.min(jnp.where(g == mx, blk, nblk), axis=0, keepdims=True)
            hit = blk == idx
            sel = jnp.where(hit, 1.0, sel)
            g = jnp.where(hit, -jnp.inf, g)
        sel_ref[h] = jnp.where(blk < i, sel, 0.0)

        s = jnp.where(row <= col, diag[h], NEG)
        m = jnp.max(s, axis=0, keepdims=True)
        ml_ref[h:h + 1, :] = m
        acc_ref[h * HA:(h + 1) * HA, :] = _dot(
            _v_aug(_lane_tiles(vT_ref, i * nt, nt, h * HEAD_DIM, HEAD_DIM)), jnp.exp2(s - m).astype(BF16))

    last = jnp.maximum(i - 1, 0)

    def qk(t, slot):
        for h in range(4):
            p = h // 2
            kbj = k_ref[pl.ds(pl.multiple_of(t * tq, tq), tq), p * LANE:(p + 1) * LANE]
            s_ref[slot, h] = _dot(kbj, qm_ref[h])

    def update(t, valid, slot):
        for h in range(4):
            srow = sel_ref[h, pl.ds(t, 1), :] * valid
            vb = _v_aug(_lane_tiles(vT_ref, t * nt, nt, h * HEAD_DIM, HEAD_DIM))
            rows = slice(h * HA, (h + 1) * HA)
            m, acc = _online_cols(s_ref[slot, h], srow, ml_ref[h:h + 1, :], acc_ref[rows, :], vb)
            ml_ref[h:h + 1, :] = m
            acc_ref[rows, :] = acc

    qk(0, 0)

    def body(jj, c):
        t0 = 2 * jj
        qk(t0 + 1, 1)
        update(t0, 1.0, 0)
        qk(jnp.minimum(t0 + 2, last), 0)
        update(t0 + 1, 1.0, 1)
        return c

    lax.fori_loop(0, i // 2, body, 0)

    @pl.when(i % 2 == 1)
    def _():
        update(last, 1.0, 0)
    outs = [acc_ref[h * HA:h * HA + HEAD_DIM, :] / acc_ref[h * HA + HEAD_DIM:h * HA + HEAD_DIM + 1, :]
            for h in range(4)]
    o_ref[...] = jnp.concatenate(outs, axis=0).T.astype(o_ref.dtype)


def _moba_call(oT, otok, A, *, q_chunk, k_chunk, v_chunk):
    B, nlt, _, _ = oT.shape
    S = nlt * LANE
    tq = MOBA_BLOCK
    nblk = S // MOBA_BLOCK
    kern = functools.partial(_moba_kernel, tq=tq)
    return pl.pallas_call(
        kern,
        grid=(B, S // tq),
        in_specs=_attn_specs(S, tq, q_chunk, k_chunk, v_chunk) + [pl.BlockSpec((nblk, S), lambda b, i: (0, 0))],
        out_specs=pl.BlockSpec((None, tq, BRANCH_W), lambda b, i: (b, i, 0)),
        out_shape=jax.ShapeDtypeStruct((B, S, BRANCH_W), BF16),
        scratch_shapes=[pltpu.VMEM((nblk, CH), F32), pltpu.VMEM((4, nblk, tq), F32),
                        pltpu.VMEM((4, LANE, tq), BF16), pltpu.VMEM((4 * HA, tq), F32), pltpu.VMEM((8, tq), F32),
                        pltpu.VMEM((2, 4, tq, tq), F32)],
        compiler_params=_cparams(("parallel", "arbitrary")),
        name="moba",
    )(oT, otok, oT, A)


def _compress_kernel(x_ref, w1_ref, pe_ref, w1f_ref, w2_ref, gk_ref, G_ref, o_ref, oT_ref, acc_ref):
    l = pl.program_id(1)

    @pl.when(l == 0)
    def _():
        acc_ref[...] = jnp.zeros_like(acc_ref)

    x = x_ref[...]
    w_hi, w_lo = _split2(w1_ref[...])
    acc_ref[...] += _dot(x, w_hi) + _dot(x, w_lo)

    @pl.when(l == pl.num_programs(1) - 1)
    def _():
        nc = acc_ref.shape[0]
        r = acc_ref[...]
        bias = jnp.dot(pe_ref[...], w1f_ref[...], preferred_element_type=F32,
                       precision=lax.Precision.HIGHEST)
        p1 = jnp.concatenate([r[:, 0:64], r[:, 128:192]], axis=1)
        p2 = jnp.concatenate([r[:, 64:128], r[:, 192:256]], axis=1)
        hid = p1 + pltpu.roll(p2, nc - 1, 0) + bias[0:1, :]
        hid = hid * jax.nn.sigmoid(hid)
        comp = jnp.dot(hid, w2_ref[...], preferred_element_type=F32, precision=lax.Precision.HIGHEST)
        ss = _dot((comp * comp).astype(BF16), G_ref[...])
        lane = lax.broadcasted_iota(jnp.int32, comp.shape, 1)
        inv = jnp.where(lane < HEAD_DIM, lax.rsqrt(ss * (1.0 / HEAD_DIM) + EPS), 1.0)
        comp = comp * inv * gk_ref[...]
        rowi = lax.broadcasted_iota(jnp.int32, comp.shape, 0)
        comp = jnp.where(rowi < nc - 1, comp, 0.0)
        o_ref[...] = comp.astype(o_ref.dtype)
        oT_ref[...] = comp.T.astype(oT_ref.dtype)


def _compress_call(xv, w1blk, pe2, w1f, w2blk, gk, G128):
    B, nc, _ = xv.shape
    return pl.pallas_call(
        _compress_kernel,
        grid=(B, NSA_CMP_STRIDE),
        in_specs=[
            pl.BlockSpec((None, nc, LANE), lambda b, l: (b, 0, l)),
            pl.BlockSpec((None, LANE, CH), lambda b, l: (l, 0, 0)),
            pl.BlockSpec((8, 2 * NSA_CMP_LEN * HEAD_DIM), lambda b, l: (0, 0)),
            pl.BlockSpec((2 * NSA_CMP_LEN * HEAD_DIM, LANE), lambda b, l: (0, 0)),
            pl.BlockSpec((LANE, LANE), lambda b, l: (0, 0)),
            pl.BlockSpec((1, LANE), lambda b, l: (0, 0)),
            pl.BlockSpec((LANE, LANE), lambda b, l: (0, 0)),
        ],
        out_specs=[
            pl.BlockSpec((None, nc, LANE), lambda b, l: (b, 0, 0)),
            pl.BlockSpec((None, LANE, nc), lambda b, l: (b, 0, 0)),
        ],
        out_shape=[
            jax.ShapeDtypeStruct((B, nc, LANE), BF16),
            jax.ShapeDtypeStruct((B, LANE, nc), BF16),
        ],
        scratch_shapes=[pltpu.VMEM((nc, CH), F32)],
        compiler_params=_cparams(("parallel", "arbitrary")),
        name="nsa_compress",
    )(xv, w1blk, pe2, w1f, w2blk, gk, G128)


def _nsa_kernel(qT_ref, k_ref, vT_ref, kc_ref, kcT_ref, ov_ref, o_ref, sel_ref, qm_ref, acc_ref, ml_ref, s_ref,
                win_ref, *, tq):
    i = pl.program_id(1)
    nt = tq // LANE
    nq = 4 * tq
    nc = kc_ref.shape[0]
    nsel = ov_ref.shape[0]
    spb = tq // NSA_SEL_BLOCK

    zeros = jnp.zeros((HEAD_DIM, tq), BF16)
    q_heads = [_lane_tiles(qT_ref, 0, nt, h * HEAD_DIM, HEAD_DIM) for h in range(4)]
    qm_lo = jnp.concatenate([jnp.concatenate([q, zeros], axis=0) for q in q_heads], axis=1)
    qm_hi = jnp.concatenate([jnp.concatenate([zeros, q], axis=0) for q in q_heads], axis=1)
    qpos = i * tq + lax.broadcasted_iota(jnp.int32, (1, tq), 1)
    qpos4 = jnp.concatenate([qpos] * 4, axis=1)

    n_prev = -(-(NSA_WINDOW - 1) // tq)
    nk = n_prev + 1
    kt0 = jnp.maximum(i - n_prev, 0)
    kw = k_ref[pl.ds(pl.multiple_of(kt0 * tq, tq), nk * tq), LANE:2 * LANE]
    sw = _dot(kw, qm_hi)
    dist = qpos - (kt0 * tq + lax.broadcasted_iota(jnp.int32, (nk * tq, tq), 0))
    wbias = jnp.where(dist >= 0, jnp.where(dist <= NSA_WINDOW - 1, 0.0, NEG), NEG)
    sw = sw + jnp.concatenate([wbias] * 4, axis=1)
    mw = jnp.max(sw, axis=0, keepdims=True)
    ew = jnp.exp2(sw - mw)
    dw = jnp.sum(ew, axis=0, keepdims=True)
    vwin = _lane_tiles(vT_ref, kt0 * nt, nk * nt, HEAD_DIM, HEAD_DIM)
    win_ref[...] = _dot(vwin, ew.astype(BF16)) / dw

    zc = _dot(kc_ref[...], qm_lo)
    c_end = lax.broadcasted_iota(jnp.int32, (nc, tq), 0) * NSA_CMP_STRIDE + (NSA_CMP_LEN - 1)
    cbias = jnp.where(c_end <= qpos, 0.0, NEG)
    zc = zc + jnp.concatenate([cbias] * 4, axis=1)
    e = jnp.exp2(zc - jnp.max(zc, axis=0, keepdims=True))
    seen = jnp.where(qpos4 >= NSA_CMP_LEN - 1, 1.0, 0.0)
    pc = e * (seen / jnp.maximum(jnp.sum(e, axis=0, keepdims=True), 1.0))
    o_cmp = _dot(kcT_ref[HEAD_DIM:2 * HEAD_DIM, :], pc.astype(BF16))

    psum = pc[:, 0:tq] + pc[:, tq:2 * tq] + pc[:, 2 * tq:3 * tq] + pc[:, 3 * tq:4 * tq]
    p_hi, p_lo = _split2(psum)
    imp = _dot(ov_ref[...], p_hi) + _dot(ov_ref[...], p_lo)
    nid = lax.broadcasted_iota(jnp.int32, (nsel, tq), 0)
    cur = qpos // NSA_SEL_BLOCK
    imp = jnp.where(nid == 0, BIG, imp)
    imp = jnp.where(nid == cur, BIG, imp)
    imp = jnp.where(nid == cur - 1, BIG, imp)
    imp = jnp.where(nid > cur, NEG, imp)
    for _r in range(min(NSA_SEL_TOPK, nsel)):
        mx = jnp.max(imp, axis=0, keepdims=True)
        idx = jnp.min(jnp.where(imp == mx, nid, nsel), axis=0, keepdims=True)
        imp = jnp.where(nid == idx, -jnp.inf, imp)
    sel = jnp.where(nid <= cur, jnp.where(imp == -jnp.inf, 1.0, 0.0), 0.0)
    sel_ref[...] = sel
    for h in range(4):
        qm_ref[h] = qm_lo[:, h * tq:(h + 1) * tq]

    def qk(t, slot):
        kb = k_ref[pl.ds(pl.multiple_of(t * tq, tq), tq), LANE:2 * LANE]
        for h in range(4):
            s_ref[slot, h] = _dot(kb, qm_ref[h])

    kpos_d = i * tq + lax.broadcasted_iota(jnp.int32, (tq, tq), 0)
    qk(i, 0)
    vb_d = _v_aug(_lane_tiles(vT_ref, i * nt, nt, 0, HEAD_DIM))
    srows_d = [sel_ref[pl.ds(i * spb + u, 1), :] for u in range(spb)]
    for h in range(4):
        sc = s_ref[0, h]
        s = jnp.concatenate(
            [jnp.where(srows_d[u] > 0.0, sc[u * NSA_SEL_BLOCK:(u + 1) * NSA_SEL_BLOCK, :], NEG)
             for u in range(spb)], axis=0)
        s = jnp.where(kpos_d <= qpos, s, NEG)
        m = jnp.max(s, axis=0, keepdims=True)
        lanes = slice(h * tq, (h + 1) * tq)
        ml_ref[0:1, lanes] = m
        acc_ref[:, lanes] = _dot(vb_d, jnp.exp2(s - m).astype(BF16))

    last = jnp.maximum(i - 1, 0)

    def update(t, valid, slot):
        vb = _v_aug(_lane_tiles(vT_ref, t * nt, nt, 0, HEAD_DIM))
        srows = [sel_ref[pl.ds(t * spb + u, 1), :] * valid for u in range(spb)]
        for h in range(4):
            lanes = slice(h * tq, (h + 1) * tq)
            sc = s_ref[slot, h]
            subs = [sc[u * NSA_SEL_BLOCK:(u + 1) * NSA_SEL_BLOCK, :] for u in range(spb)]
            tmax = None
            for u in range(spb):
                mu = jnp.where(srows[u] > 0.0, jnp.max(subs[u], axis=0, keepdims=True), NEG)
                tmax = mu if tmax is None else jnp.maximum(tmax, mu)
            m_old = ml_ref[0:1, lanes]
            m_new = jnp.maximum(m_old, tmax)
            p = jnp.concatenate(
                [jnp.exp2(subs[u] - jnp.where(srows[u] > 0.0, m_new, BIGPOS)).astype(BF16) for u in range(spb)],
                axis=0)
            ml_ref[0:1, lanes] = m_new
            acc_ref[:, lanes] = jnp.exp2(m_old - m_new) * acc_ref[:, lanes] + _dot(vb, p)

    qk(0, 0)

    def body(jj, c):
        t0 = 2 * jj
        qk(t0 + 1, 1)
        update(t0, 1.0, 0)
        qk(jnp.minimum(t0 + 2, last), 0)
        update(t0 + 1, 1.0, 1)
        return c

    lax.fori_loop(0, i // 2, body, 0)

    @pl.when(i % 2 == 1)
    def _():
        update(last, 1.0, 0)
    o_sel = acc_ref[0:HEAD_DIM, :] / acc_ref[HEAD_DIM:HEAD_DIM + 1, :]

    o_win = win_ref[...]

    gates = _lane_tiles(vT_ref, i * nt, nt, 2 * HEAD_DIM, 16).astype(F32)
    outs = []
    for h in range(4):
        sl = slice(h * tq, (h + 1) * tq)
        outs.append(gates[3 * h:3 * h + 1, :] * o_cmp[:, sl]
                    + gates[3 * h + 1:3 * h + 2, :] * o_sel[:, sl]
                    + gates[3 * h + 2:3 * h + 3, :] * o_win[:, sl])
    o_ref[...] = jnp.concatenate(outs, axis=0).T.astype(o_ref.dtype)


def _nsa_call(oT, otok, kc, kcT, ovT, *, q_chunk, kv_chunk, v_chunk):
    B, nlt, _, _ = oT.shape
    S = nlt * LANE
    tq = 256
    nc = S // NSA_CMP_STRIDE
    nsel = S // NSA_SEL_BLOCK
    kern = functools.partial(_nsa_kernel, tq=tq)
    return pl.pallas_call(
        kern,
        grid=(B, S // tq),
        in_specs=_attn_specs(S, tq, q_chunk, kv_chunk, v_chunk) + [
            pl.BlockSpec((None, nc, LANE), lambda b, i: (b, 0, 0)),
            pl.BlockSpec((None, LANE, nc), lambda b, i: (b, 0, 0)),
            pl.BlockSpec((nsel, nc), lambda b, i: (0, 0)),
        ],
        out_specs=pl.BlockSpec((None, tq, BRANCH_W), lambda b, i: (b, i, 0)),
        out_shape=jax.ShapeDtypeStruct((B, S, BRANCH_W), BF16),
        scratch_shapes=[pltpu.VMEM((nsel, tq), F32), pltpu.VMEM((4, LANE, tq), BF16),
                        pltpu.VMEM((HA, 4 * tq), F32), pltpu.VMEM((8, 4 * tq), F32),
                        pltpu.VMEM((2, 4, tq, tq), F32), pltpu.VMEM((HEAD_DIM, 4 * tq), F32)],
        compiler_params=_cparams(("parallel", "arbitrary")),
        name="nsa",
    )(oT, otok, oT, kc, kcT, ovT)


def _epi_kernel(x_ref, ng_ref, ya_ref, yc_ref, yd_ref, ob0_ref, ob1_ref, ob2_ref, l0_ref, l1_ref, l2_ref,
                wz_ref, wmg_ref, wbr_ref, wout_ref, o_ref, nat_ref, *, tm):
    x = x_ref[...]
    ms = jnp.mean(x * x, axis=-1, keepdims=True)
    xn = (x * lax.rsqrt(ms + EPS) * ng_ref[...]).astype(BF16)

    def natural(ref, slot, dil):
        nh = BRANCH_W // LANE
        for r in range(dil):
            for hf in range(nh):
                c0 = r * BRANCH_W + hf * LANE
                nat_ref[slot * nh + hf, pl.ds(r, tm // dil, stride=dil), :] = ref[:, c0:c0 + LANE].astype(F32)
        return jnp.concatenate([nat_ref[slot * nh + hf] for hf in range(nh)], axis=1)

    d1, d2 = DIL_PAIRS[1][1], DIL_PAIRS[2][1]
    l0, l1, l2 = l0_ref[...], natural(l1_ref, 0, d1), natural(l2_ref, 1, d2)
    mx = jnp.maximum(jnp.maximum(l0, l1), l2)
    e0, e1, e2 = jnp.exp2(l0 - mx), jnp.exp2(l1 - mx), jnp.exp2(l2 - mx)
    yb = (e0 * ob0_ref[...].astype(F32) + e1 * natural(ob1_ref, 2, d1)
          + e2 * natural(ob2_ref, 3, d2)) / (e0 + e1 + e2)

    ys = (ya_ref[...].astype(F32), yb, yc_ref[...].astype(F32), yd_ref[...].astype(F32))
    merged = None
    for i in range(4):
        z = _dot(xn, wz_ref[:, i * BRANCH_W:(i + 1) * BRANCH_W])
        gated = (ys[i] * (z * jax.nn.sigmoid(z))).astype(BF16)
        br = _dot(gated, wbr_ref[i])
        mg = _dot(xn, wmg_ref[:, i * D_MODEL:(i + 1) * D_MODEL])
        term = jax.nn.sigmoid(mg) * br
        merged = term if merged is None else merged + term
    o_ref[...] = x + _dot(merged.astype(BF16), wout_ref[...])


def _epi_call(x2, ng, ya, yc, yd, obs, lses, wz, wmg, wbr, wout, *, layer, tm=256):
    T = x2.shape[0]
    row = lambda i: (i, 0)
    full2 = lambda i: (0, 0)
    yspec = pl.BlockSpec((tm, BRANCH_W), row)
    d1, d2 = DIL_PAIRS[1][1], DIL_PAIRS[2][1]
    gspecs = [yspec, pl.BlockSpec((tm // d1, d1 * BRANCH_W), row), pl.BlockSpec((tm // d2, d2 * BRANCH_W), row)]
    return pl.pallas_call(
        functools.partial(_epi_kernel, tm=tm),
        grid=(T // tm,),
        in_specs=[pl.BlockSpec((tm, D_MODEL), row), pl.BlockSpec((1, D_MODEL), full2)]
        + [yspec] * 3 + gspecs + gspecs
        + [pl.BlockSpec((None, D_MODEL, 4 * BRANCH_W), lambda i: (layer, 0, 0)),
           pl.BlockSpec((None, D_MODEL, 4 * D_MODEL), lambda i: (layer, 0, 0)),
           pl.BlockSpec((None, 4, BRANCH_W, D_MODEL), lambda i: (layer, 0, 0, 0)),
           pl.BlockSpec((None, D_MODEL, D_MODEL), lambda i: (layer, 0, 0))],
        out_specs=pl.BlockSpec((tm, D_MODEL), row),
        out_shape=jax.ShapeDtypeStruct((T, D_MODEL), F32),
        scratch_shapes=[pltpu.VMEM((4 * (BRANCH_W // LANE), tm, LANE), F32)],
        compiler_params=_cparams(("parallel",)),
        name="epilogue",
    )(x2, ng, ya, yc, yd, *obs, *lses, wz, wmg, wbr, wout)


TOK_AK, TOK_BQ, TOK_BK, TOK_BV, TOK_CK, TOK_DKV = range(6)
N_MAIN = 6
T_AQ, T_AV, T_CQ, T_CV, T_DQ, T_DX = range(6)
SCALE = 1.0 / math.sqrt(HEAD_DIM)
QSCALE2 = SCALE * LOG2E

_PLAIN = (False, False, 1.0, False)
_T_KINDS = (
    ((False, False, QSCALE2, False),) * 4,
    (_PLAIN,) * 4,
    ((True, True, QSCALE2, False),) * 4,
    (_PLAIN,) * 4,
    ((True, True, QSCALE2, False),) * 4,
    (_PLAIN, _PLAIN, (False, False, 1.0, True), _PLAIN),
)


def _tok_specs():
    main = lambda j: (0, 1, j * CH, CH, 0)
    qkv = ((True, True), (True, True), (False, False))
    specs = []
    for j in range(N_MAIN):
        if j in (TOK_BQ, TOK_BK, TOK_CK):
            specs.append((True, True, (main(j),)))
        elif j == TOK_DKV:
            specs.append(("mixed", "mixed", (main(j), (3, NSA_CMP_STRIDE, 0, LANE, LANE))))
        else:
            specs.append((False, False, (main(j),)))
    for gi in (1, 2):
        dil = DIL_PAIRS[gi][1]
        for k, (nrm, rope) in enumerate(qkv):
            specs.append((nrm, rope, ((gi, dil, k * CH, CH, 3 * CH),)))
    return tuple(specs)


def _rope_tables(S, dil):
    L = S // dil
    pos = (jnp.arange(dil, dtype=jnp.int32)[:, None] + dil * jnp.arange(L, dtype=jnp.int32)[None, :]).astype(F32)
    inv = ROPE_THETA ** (-jnp.arange(0, ROT_DIM, 2, dtype=F32) / ROT_DIM)
    ang = pos[:, :, None] * inv[None, None, :]
    cos, sin = jnp.cos(ang), jnp.sin(ang)
    one = jnp.ones((dil, L, HEAD_DIM - ROT_DIM), F32)
    zero8 = jnp.zeros((dil, L, 8), F32)
    zero = jnp.zeros_like(one)
    c_head = jnp.concatenate([cos, cos, one], axis=-1)
    s1_head = jnp.concatenate([zero8, sin, zero], axis=-1)
    s2_head = jnp.concatenate([-sin, zero8, zero], axis=-1)
    rtok = jnp.stack([jnp.tile(t, (1, 1, LANE // HEAD_DIM)) for t in (c_head, s1_head, s2_head)], axis=1)
    rT = jnp.stack([cos.transpose(0, 2, 1), sin.transpose(0, 2, 1)], axis=1)
    return rtok, rT


def _tok_params(rows):
    out = []
    for nf, rf, gains in rows:
        nrow = jnp.concatenate([jnp.full((HEAD_DIM,), float(f), F32) for f in nf])
        rrow = jnp.concatenate([jnp.full((HEAD_DIM,), float(f), F32) for f in rf])
        grow = jnp.concatenate([g.astype(F32) for g in gains])
        out.append(jnp.concatenate([jnp.stack([nrow, rrow, grow]), jnp.zeros((5, CH), F32)], axis=0))
    return jnp.stack(out)


def kernel(x, norm_g, w_in, qk_g, cmp_pe, cmp_w1, cmp_w2, w_branch, w_out):
    B, S, _ = x.shape
    T = B * S
    o = COL_OFF
    ones = jnp.ones((HEAD_DIM,), F32)

    r = np.arange(CH)
    G = jnp.asarray((r[:, None] // HEAD_DIM == r[None, :] // HEAD_DIM).astype(np.float32), BF16)
    G128 = G[:LANE, :LANE]
    t = np.arange(256)
    U = jnp.asarray((t[None, :] > t[:, None]).astype(np.float32), BF16)
    nblk = S // MOBA_BLOCK
    A = jnp.asarray(np.repeat(np.eye(nblk, dtype=np.float32), MOBA_BLOCK, axis=1) / MOBA_BLOCK, BF16)
    nc, nsel = S // NSA_CMP_STRIDE, S // NSA_SEL_BLOCK
    cs = np.arange(nc) * NSA_CMP_STRIDE
    ss = np.arange(nsel) * NSA_SEL_BLOCK
    ov = ((cs[None, :] < ss[:, None] + NSA_SEL_BLOCK) & (cs[None, :] + NSA_CMP_LEN > ss[:, None]))
    ov[:, nc - 1] = False
    ovT = jnp.asarray(ov.astype(np.float32), BF16)
    hid = np.arange(LANE)[:, None]
    ind = jnp.asarray(((hid < 4) & (r[None, :] // HEAD_DIM == hid)).astype(np.float32), BF16)
    rtok4, rT4 = _rope_tables(S, 1)
    rtok, rT = rtok4[0], rT4[0]
    tok_specs = _tok_specs()

    dkv = o[13]
    bq = lambda gi: (o[4] + gi * CH, o[4] + (gi + 1) * CH)
    bk = lambda gi: (o[5] + gi * CH, o[5] + (gi + 1) * CH)
    bv = lambda gi: (o[6] + gi * CH, o[6] + (gi + 1) * CH)
    tok_plan = ((o[1], o[2]), bq(0), bk(0), bv(0), (o[9], o[10]),
                (dkv, dkv + 64), (dkv + 64, dkv + 128), (dkv + 128, dkv + 192), (dkv + 256, dkv + 320),
                bq(1), bk(1), bv(1), bq(2), bk(2), bv(2))
    T_plan = ((o[0], o[1]), (o[2], o[3]), (o[8], o[9]), (o[10], o[11]), (o[12], o[13]),
              (dkv + 192, dkv + 256), (dkv + 320, dkv + 384), (o[14], o[15]))
    z_plan = ((o[3], o[4]), (o[7], o[8]), (o[11], o[12]), (o[15], o[16]))
    wtok_all, wT_all, wz_all, wmg_all = _repack_call(
        w_in, (tok_plan, T_plan, z_plan, ((o[16], o[17]),)),
        (len(tok_specs) * CH, len(_T_KINDS) * CH, 4 * BRANCH_W, 4 * D_MODEL))

    wbr_all = w_branch.astype(BF16)
    wout_all = w_out.astype(BF16)

    for l in range(DEPTH):
        g = qk_g[l]
        plain = ((0,) * 4, (0,) * 4, (ones,) * 4)
        bq_p = ((1,) * 4, (1,) * 4, (g[0] * QSCALE2,) * 4)
        bk_p = ((1,) * 4, (1,) * 4, (g[1],) * 4)
        tokp = _tok_params([
            plain, bq_p, bk_p, plain,
            ((1,) * 4, (1,) * 4, (g[3],) * 4),
            ((0, 0, 1, 1), (1, 0, 1, 1), (ones, ones, g[6], g[7])),
            bq_p, bk_p, plain, bq_p, bk_p, plain])
        gT = jnp.concatenate([jnp.tile(ones, 8), jnp.tile(g[2], 4), jnp.tile(ones, 4), jnp.tile(g[4], 4),
                              jnp.tile(ones, 4)])[:, None]
        ng = norm_g[l][None, :]
        otok, og1, og2, okv, oT = _proj_call(x, ng, wtok_all, wT_all, tokp, gT, rtok, rT, G, layer=l,
                                             tok_specs=tok_specs, T_kinds=_T_KINDS, n_main=N_MAIN)

        ya = _sb_call(oT, otok, U, q_chunk=T_AQ, k_chunk=TOK_AK, v_chunk=T_AV)

        obs, lses = [], []
        for gi, (window, dil) in enumerate(DIL_PAIRS):
            if dil == 1:
                ob, lse = _band_call(otok, ind, dil=1, per_res=N_MAIN, qi=TOK_BQ, ki=TOK_BK, vi=TOK_BV,
                                     max_dist=window)
            else:
                ob, lse = _band_call((og1, og2)[gi - 1], ind, dil=dil, per_res=3, qi=0, ki=1, vi=2,
                                     max_dist=window // dil)
            obs.append(ob.reshape(T // dil, dil * BRANCH_W))
            lses.append(lse.reshape(T // dil, dil * BRANCH_W))

        yc = _moba_call(oT, otok, A, q_chunk=T_CQ, k_chunk=TOK_CK, v_chunk=T_CV)

        w1 = cmp_w1[l].reshape(2, NSA_CMP_LEN, HEAD_DIM, HEAD_DIM)
        z64 = jnp.zeros((NSA_CMP_STRIDE, HEAD_DIM, HEAD_DIM), F32)
        w1blk = jnp.concatenate([
            jnp.concatenate([w1[0, :16], w1[0, 16:], z64, z64], axis=2),
            jnp.concatenate([z64, z64, w1[1, :16], w1[1, 16:]], axis=2)], axis=1)
        pe2 = jnp.concatenate([cmp_pe[l].reshape(1, -1), jnp.zeros((7, 2 * NSA_CMP_LEN * HEAD_DIM), F32)], axis=0)
        zf = jnp.zeros((NSA_CMP_LEN * HEAD_DIM, HEAD_DIM), F32)
        w1f = jnp.concatenate([jnp.concatenate([cmp_w1[l, 0], zf], axis=1),
                               jnp.concatenate([zf, cmp_w1[l, 1]], axis=1)], axis=0)
        z2 = jnp.zeros((HEAD_DIM, HEAD_DIM), F32)
        w2blk = jnp.concatenate([jnp.concatenate([cmp_w2[l, 0], z2], axis=1),
                                 jnp.concatenate([z2, cmp_w2[l, 1]], axis=1)], axis=0)
        gk = jnp.concatenate([g[5], ones])[None, :]
        kc, kcT = _compress_call(okv, w1blk, pe2, w1f, w2blk, gk, G128)
        yd = _nsa_call(oT, otok, kc, kcT, ovT, q_chunk=T_DQ, kv_chunk=TOK_DKV, v_chunk=T_DX)

        x2 = _epi_call(x.reshape(T, D_MODEL), ng, ya.reshape(T, BRANCH_W), yc.reshape(T, BRANCH_W),
                       yd.reshape(T, BRANCH_W), obs, lses, wz_all, wmg_all, wbr_all, wout_all, layer=l)
        x = x2.reshape(B, S, D_MODEL)
    return x
```

```python
import functools
import math

import numpy as np
import jax
import jax.numpy as jnp
from jax import lax
from jax.experimental import pallas as pl
from jax.experimental.pallas import tpu as pltpu

F32 = jnp.float32
BF16 = jnp.bfloat16

D_MODEL = 1024
DEPTH = 4
HEAD_DIM = 64
ROT_DIM = 16
ROPE_THETA = 500000.0
EPS = 1e-6
NEG = -1e30
BIG = 1e9
BRANCH_W = 256
DIL_PAIRS = ((128, 1), (512, 4), (2048, 16))
MOBA_BLOCK = 256
MOBA_TOPK = 3
NSA_CMP_LEN = 32
NSA_CMP_STRIDE = 16
NSA_SEL_BLOCK = 64
NSA_SEL_TOPK = 16
NSA_WINDOW = 512
COL_SIZES = (256, 256, 256, 256, 768, 768, 768, 256, 256, 256, 256, 256, 256, 384, 12, 256, 4096)
COL_OFF = tuple(int(v) for v in np.concatenate([[0], np.cumsum(COL_SIZES)]))

LANE = 128
CH = 256
VMEM_LIMIT = 56 * 1024 * 1024


def _cparams(sem):
    return pltpu.CompilerParams(dimension_semantics=sem, vmem_limit_bytes=VMEM_LIMIT)


def _dot(a, b):
    return jnp.dot(a, b, preferred_element_type=F32)


def _dot_nt(a, b):
    return lax.dot_general(a, b, (((1,), (1,)), ((), ())), preferred_element_type=F32)


def _split2(x):
    hi = x.astype(BF16)
    lo = (x - hi.astype(F32)).astype(BF16)
    return hi, lo


def _split3(x):
    hi = x.astype(BF16)
    r = x - hi.astype(F32)
    mid = r.astype(BF16)
    lo = (r - mid.astype(F32)).astype(BF16)
    return hi, mid, lo


def _repack_kernel(w_ref, *out_refs, plans):
    for o_ref, ranges in zip(out_refs, plans):
        width = o_ref.shape[-1]
        parts = [w_ref[:, a:b] for a, b in ranges]
        used = sum(b - a for a, b in ranges)
        if used < width:
            parts.append(jnp.zeros((w_ref.shape[0], width - used), F32))
        o_ref[...] = (parts[0] if len(parts) == 1 else jnp.concatenate(parts, axis=1)).astype(o_ref.dtype)


def _repack_call(w_in, plans, widths, rows=128):
    depth, d, n_in = w_in.shape
    kern = functools.partial(_repack_kernel, plans=plans)
    return pl.pallas_call(
        kern,
        grid=(depth, d // rows),
        in_specs=[pl.BlockSpec((None, rows, n_in), lambda l, i: (l, i, 0))],
        out_specs=[pl.BlockSpec((None, rows, w), lambda l, i: (l, i, 0)) for w in widths],
        out_shape=[jax.ShapeDtypeStruct((depth, d, w), BF16) for w in widths],
        compiler_params=_cparams(("parallel", "parallel")),
        name="repack",
    )(w_in)


def _proj_kernel(x_ref, ng_ref, wtok_ref, wT_ref, tokp_ref, gT_ref, rtok_ref, rT_ref, G_ref,
                 omain_ref, og1_ref, og2_ref, okv_ref, oT_ref, stage_ref, *, tok_specs, T_kinds, tm):
    x = x_ref[...]
    ms = jnp.mean(x * x, axis=-1, keepdims=True)
    xn = (x * lax.rsqrt(ms + EPS) * ng_ref[...]).astype(BF16)

    dsts = (omain_ref, og1_ref, og2_ref, okv_ref)
    for c, (has_norm, has_rope, outs) in enumerate(tok_specs):
        y = _dot(xn, wtok_ref[:, c * CH:(c + 1) * CH])
        prm = tokp_ref[c]
        if has_norm:
            ss = _dot((y * y).astype(BF16), G_ref[...])
            inv = lax.rsqrt(ss * (1.0 / HEAD_DIM) + EPS)
            y = y * (jnp.where(prm[0:1, :] > 0.0, inv, 1.0) if has_norm == "mixed" else inv)
        y = y * prm[2:3, :]
        if has_rope:
            halves = []
            for hf in range(CH // LANE):
                yh = y[:, hf * LANE:(hf + 1) * LANE]
                cc, s1, s2 = rtok_ref[0], rtok_ref[1], rtok_ref[2]
                if has_rope == "mixed":
                    rf = prm[1:2, hf * LANE:(hf + 1) * LANE]
                    cc, s1, s2 = jnp.where(rf > 0.0, cc, 1.0), s1 * rf, s2 * rf
                halves.append(yh * cc + pltpu.roll(yh, 8, 1) * s1 + pltpu.roll(yh, LANE - 8, 1) * s2)
            y = jnp.concatenate(halves, axis=1)
        staged = False
        for dst, dil, col, lanes, rstride in outs:
            if dil == 1:
                dsts[dst][:, col:col + lanes] = y[:, :lanes].astype(BF16)
                continue
            if not staged:
                for hf in range(CH // LANE):
                    stage_ref[hf] = y[:, hf * LANE:(hf + 1) * LANE]
                staged = True
            for r in range(dil):
                for hf in range(lanes // LANE):
                    blk = stage_ref[hf, pl.ds(r, tm // dil, stride=dil), :]
                    c0 = r * rstride + col + hf * LANE
                    dsts[dst][:, c0:c0 + LANE] = blk.astype(BF16)

    cosT = rT_ref[0]
    sinT = rT_ref[1]
    for c, heads in enumerate(T_kinds):
        y = _dot(xn, wT_ref[:, c * CH:(c + 1) * CH]).T
        for h, (nrm, rope, scale, sigm) in enumerate(heads):
            r0 = c * CH + h * HEAD_DIM
            yh = y[h * HEAD_DIM:(h + 1) * HEAD_DIM, :]
            if nrm:
                msq = jnp.mean(yh * yh, axis=0, keepdims=True)
                yh = yh * lax.rsqrt(msq + EPS) * gT_ref[r0:r0 + HEAD_DIM, :]
            if rope:
                x1 = yh[0:8, :]
                x2 = yh[8:16, :]
                yh = jnp.concatenate([x1 * cosT - x2 * sinT, x2 * cosT + x1 * sinT, yh[16:, :]], axis=0)
            if scale != 1.0:
                yh = yh * scale
            if sigm:
                yh = jax.nn.sigmoid(yh)
            yb = yh.astype(BF16)
            for t in range(tm // LANE):
                oT_ref[t, r0:r0 + HEAD_DIM, :] = yb[:, t * LANE:(t + 1) * LANE]


def _proj_call(x, ng, wtok, wT, tokp, gT, rtok, rT, G, *, layer, tok_specs, T_kinds, n_main, tm=1024):
    B, S, _ = x.shape
    ntok, nT = len(tok_specs), len(T_kinds)
    d1, d2 = DIL_PAIRS[1][1], DIL_PAIRS[2][1]
    kern = functools.partial(_proj_kernel, tok_specs=tok_specs, T_kinds=T_kinds, tm=tm)
    const2 = lambda b, n: (0, 0)
    return pl.pallas_call(
        kern,
        grid=(B, S // tm),
        in_specs=[
            pl.BlockSpec((None, tm, D_MODEL), lambda b, n: (b, n, 0)),
            pl.BlockSpec((1, D_MODEL), const2),
            pl.BlockSpec((None, D_MODEL, ntok * CH), lambda b, n: (layer, 0, 0), pipeline_mode=pl.Buffered(1)),
            pl.BlockSpec((None, D_MODEL, nT * CH), lambda b, n: (layer, 0, 0), pipeline_mode=pl.Buffered(1)),
            pl.BlockSpec((ntok, 8, CH), lambda b, n: (0, 0, 0)),
            pl.BlockSpec((nT * CH, 1), const2),
            pl.BlockSpec((3, tm, LANE), lambda b, n: (0, n, 0)),
            pl.BlockSpec((2, 8, tm), lambda b, n: (0, 0, n)),
            pl.BlockSpec((CH, CH), const2),
        ],
        out_specs=[
            pl.BlockSpec((None, tm, n_main * CH), lambda b, n: (b, n, 0)),
            pl.BlockSpec((None, tm // d1, d1 * 3 * CH), lambda b, n: (b, n, 0)),
            pl.BlockSpec((None, tm // d2, d2 * 3 * CH), lambda b, n: (b, n, 0)),
            pl.BlockSpec((None, tm // NSA_CMP_STRIDE, NSA_CMP_STRIDE * LANE), lambda b, n: (b, n, 0)),
            pl.BlockSpec((None, tm // LANE, nT * CH, LANE), lambda b, n: (b, n, 0, 0)),
        ],
        out_shape=[
            jax.ShapeDtypeStruct((B, S, n_main * CH), BF16),
            jax.ShapeDtypeStruct((B, S // d1, d1 * 3 * CH), BF16),
            jax.ShapeDtypeStruct((B, S // d2, d2 * 3 * CH), BF16),
            jax.ShapeDtypeStruct((B, S // NSA_CMP_STRIDE, NSA_CMP_STRIDE * LANE), BF16),
            jax.ShapeDtypeStruct((B, S // LANE, nT * CH, LANE), BF16),
        ],
        scratch_shapes=[pltpu.VMEM((CH // LANE, tm, LANE), F32)],
        compiler_params=_cparams(("parallel", "parallel")),
        name="proj",
    )(x, ng, wtok, wT, tokp, gT, rtok, rT, G)


def _pair_masked_q(q_pair, h):
    rid = lax.broadcasted_iota(jnp.int32, q_pair.shape, 0)
    lo = (h % 2) * HEAD_DIM
    keep = jnp.where(rid >= lo, jnp.where(rid < lo + HEAD_DIM, 1.0, 0.0), 0.0).astype(BF16)
    return q_pair * keep


def _lane_tiles(ref, t0, nt, r0, nr):
    return jnp.concatenate([ref[t0 + t, r0:r0 + nr, :] for t in range(nt)], axis=1)


BIGPOS = 1e30
ONES_ROWS = 16
HA = HEAD_DIM + ONES_ROWS
LOG2E = 1.4426950408889634


def _v_aug(vb):
    return jnp.concatenate([vb, jnp.ones((ONES_ROWS, vb.shape[1]), BF16)], axis=0)


def _online_cols(s, colsel, m_old, acc_old, vb_aug):
    tmax = jnp.max(s, axis=0, keepdims=True)
    m_new = jnp.where(colsel > 0.0, jnp.maximum(m_old, tmax), m_old)
    m_use = jnp.where(colsel > 0.0, m_new, BIGPOS)
    p = jnp.exp2(s - m_use).astype(BF16)
    acc = jnp.exp2(m_old - m_new) * acc_old + _dot(vb_aug, p)
    return m_new, acc


SB_LOG_CUTOFF = -144.0


def _sb_kernel(qT_ref, k_ref, vT_ref, U_ref, o_ref, qm_ref, acc_ref, carry_ref, *, tq):
    i = pl.program_id(1)
    nt = tq // LANE
    row = lax.broadcasted_iota(jnp.int32, (tq, tq), 0)
    col = lax.broadcasted_iota(jnp.int32, (tq, tq), 1)
    past = row < col
    for h in range(4):
        p = h // 2
        qm_ref[h] = _pair_masked_q(_lane_tiles(qT_ref, 0, nt, p * LANE, LANE), h)
    acc_ref[...] = jnp.zeros_like(acc_ref)
    carry_ref[...] = jnp.zeros_like(carry_ref)

    def tile(j, masked):
        scores, logsig, laters = [], [], []
        for h in range(4):
            p = h // 2
            kb = k_ref[pl.ds(pl.multiple_of(j * tq, tq), tq), p * LANE:(p + 1) * LANE]
            scores.append(_dot(kb, qm_ref[h]))
        worst = None
        for h in range(4):
            s = scores[h]
            sp = jnp.maximum(s, 0.0) + jnp.log2(1.0 + jnp.exp2(-jnp.abs(s)))
            lg = -sp
            if masked:
                lg = jnp.where(past, lg, 0.0)
            hi, lo = _split2(lg)
            carry = carry_ref[h:h + 1, :]
            laters.append(_dot(U_ref[...], hi) + _dot(U_ref[...], lo) + carry)
            logsig.append(s - sp)
            carry = carry + jnp.sum(lg, axis=0, keepdims=True)
            carry_ref[h:h + 1, :] = carry
            worst = carry if worst is None else jnp.maximum(worst, carry)
        for h in range(4):
            w = jnp.exp2(logsig[h] + laters[h])
            if masked:
                w = jnp.where(past, w, 0.0)
            vb = _lane_tiles(vT_ref, j * nt, nt, h * HEAD_DIM, HEAD_DIM)
            acc_ref[h * HEAD_DIM:(h + 1) * HEAD_DIM, :] += _dot(vb, w.astype(BF16))
        return jnp.max(worst)

    worst0 = tile(i, True)

    def cond(st):
        return jnp.logical_and(st[0] >= 0, st[1] > SB_LOG_CUTOFF)

    def body(st):
        return st[0] - 1, tile(st[0], False)

    lax.while_loop(cond, body, (i - 1, worst0))
    o_ref[...] = acc_ref[...].T.astype(o_ref.dtype)


def _attn_specs(S, tq, q_chunk, k_chunk, v_chunk):
    return [
        pl.BlockSpec((None, tq // LANE, CH, LANE), lambda b, i: (b, i, q_chunk, 0)),
        pl.BlockSpec((None, S, CH), lambda b, i: (b, 0, k_chunk)),
        pl.BlockSpec((None, S // LANE, CH, LANE), lambda b, i: (b, 0, v_chunk, 0)),
    ]


def _sb_call(oT, otok, U, *, q_chunk, k_chunk, v_chunk, tq=256):
    B, nlt, _, _ = oT.shape
    S = nlt * LANE
    kern = functools.partial(_sb_kernel, tq=tq)
    return pl.pallas_call(
        kern,
        grid=(B, S // tq),
        in_specs=_attn_specs(S, tq, q_chunk, k_chunk, v_chunk) + [pl.BlockSpec((tq, tq), lambda b, i: (0, 0))],
        out_specs=pl.BlockSpec((None, tq, BRANCH_W), lambda b, i: (b, i, 0)),
        out_shape=jax.ShapeDtypeStruct((B, S, BRANCH_W), BF16),
        scratch_shapes=[pltpu.VMEM((4, LANE, tq), BF16), pltpu.VMEM((CH, tq), F32), pltpu.VMEM((8, tq), F32)],
        compiler_params=_cparams(("parallel", "arbitrary")),
        name="stick_breaking",
    )(oT, otok, oT, U)


def _head_lanes(x, lane, h):
    lo = h * HEAD_DIM
    return jnp.where(lane >= lo, jnp.where(lane < lo + HEAD_DIM, x, jnp.zeros_like(x)), jnp.zeros_like(x))


def _band_kernel(q_ref, k_ref, v_ref, ind_ref, o_ref, lse_ref, *, tqb, max_dist):
    n = pl.program_id(2)
    n_prev = -(-max_dist // LANE)
    nkr = (n_prev + 1) * LANE
    nsub = tqb // LANE
    row = lax.broadcasted_iota(jnp.int32, (nkr, LANE), 0)
    col = lax.broadcasted_iota(jnp.int32, (nkr, LANE), 1)
    lane_q = lax.broadcasted_iota(jnp.int32, (LANE, LANE), 1)
    lane_v = lax.broadcasted_iota(jnp.int32, (nkr, CH), 1)
    for u in range(nsub):
        qt = n * nsub + u
        kt0 = jnp.maximum(qt - n_prev, 0)
        dist = (qt - kt0) * LANE + col - row
        bias = jnp.where(dist >= 0, jnp.where(dist <= max_dist, 0.0, NEG), NEG)
        k0 = pl.multiple_of(kt0 * LANE, LANE)
        kwin = k_ref[pl.ds(k0, nkr), :]
        vwin = v_ref[pl.ds(k0, nkr), :]
        qu = q_ref[u * LANE:(u + 1) * LANE, :]
        o_acc = None
        lses = []
        for h in range(4):
            p = h // 2
            qm = _head_lanes(qu[:, p * LANE:(p + 1) * LANE], lane_q, h % 2)
            s = _dot_nt(kwin[:, p * LANE:(p + 1) * LANE], qm) + bias
            m = jnp.max(s, axis=0, keepdims=True)
            e = jnp.exp2(s - m)
            den = jnp.sum(e, axis=0, keepdims=True)
            pn = e * (1.0 / den)
            contrib = _dot(pn.T.astype(BF16), _head_lanes(vwin, lane_v, h))
            o_acc = contrib if o_acc is None else o_acc + contrib
            lses.append(m + jnp.log(den) * LOG2E)
        o_ref[u * LANE:(u + 1) * LANE, :] = o_acc.astype(o_ref.dtype)
        ls = jnp.concatenate(lses + [jnp.zeros((LANE - 4, LANE), F32)], axis=0).T
        l1, l2, l3 = _split3(ls)
        lse_ref[u * LANE:(u + 1) * LANE, :] = (_dot(l1, ind_ref[...]) + _dot(l2, ind_ref[...])
                                               + _dot(l3, ind_ref[...]))


def _band_call(arr, ind, *, dil, per_res, qi, ki, vi, max_dist):
    B, L, _ = arr.shape
    tqb = min(1024, L)
    kern = functools.partial(_band_kernel, tqb=tqb, max_dist=max_dist)
    return pl.pallas_call(
        kern,
        grid=(B, dil, L // tqb),
        in_specs=[
            pl.BlockSpec((None, tqb, CH), lambda b, r, n: (b, n, r * per_res + qi)),
            pl.BlockSpec((None, L, CH), lambda b, r, n: (b, 0, r * per_res + ki)),
            pl.BlockSpec((None, L, CH), lambda b, r, n: (b, 0, r * per_res + vi)),
            pl.BlockSpec((LANE, CH), lambda b, r, n: (0, 0)),
        ],
        out_specs=[
            pl.BlockSpec((None, tqb, BRANCH_W), lambda b, r, n: (b, n, r)),
            pl.BlockSpec((None, tqb, BRANCH_W), lambda b, r, n: (b, n, r)),
        ],
        out_shape=[
            jax.ShapeDtypeStruct((B, L, dil * BRANCH_W), BF16),
            jax.ShapeDtypeStruct((B, L, dil * BRANCH_W), F32),
        ],
        compiler_params=_cparams(("parallel", "parallel", "parallel")),
        name="banded",
    )(arr, arr, arr, ind)


def _moba_kernel(qT_ref, k_ref, vT_ref, A_ref, o_ref, kmean_ref, sel_ref, qm_ref, acc_ref, ml_ref, s_ref, *, tq):
    i = pl.program_id(1)
    nt = tq // LANE
    nblk = A_ref.shape[0]

    @pl.when(i == 0)
    def _():
        kmean_ref[...] = _dot(A_ref[...], k_ref[...])

    blk = lax.broadcasted_iota(jnp.int32, (nblk, tq), 0)
    row = lax.broadcasted_iota(jnp.int32, (tq, tq), 0)
    col = lax.broadcasted_iota(jnp.int32, (tq, tq), 1)
    gates, diag = [], []
    for h in range(4):
        p = h // 2
        qm = _pair_masked_q(_lane_tiles(qT_ref, 0, nt, p * LANE, LANE), h)
        qm_ref[h] = qm
        k1, k2, k3 = _split3(kmean_ref[:, p * LANE:(p + 1) * LANE])
        gates.append(_dot(k1, qm) + _dot(k2, qm) + _dot(k3, qm))
        kb = k_ref[pl.ds(pl.multiple_of(i * tq, tq), tq), p * LANE:(p + 1) * LANE]
        diag.append(_dot(kb, qm))
    for h in range(4):
        g = jnp.where(blk < i, gates[h], NEG)
        sel = jnp.zeros((nblk, tq), F32)
        for _r in range(MOBA_TOPK):
            mx = jnp.max(g, axis=0, keepdims=True)
            idx = jnp.min(jnp.where(g == mx, blk, nblk), axis=0, keepdims=True)
            hit = blk == idx
            sel = jnp.where(hit, 1.0, sel)
            g = jnp.where(hit, -jnp.inf, g)
        sel_ref[h] = jnp.where(blk < i, sel, 0.0)

        s = jnp.where(row <= col, diag[h], NEG)
        m = jnp.max(s, axis=0, keepdims=True)
        ml_ref[h:h + 1, :] = m
        acc_ref[h * HA:(h + 1) * HA, :] = _dot(
            _v_aug(_lane_tiles(vT_ref, i * nt, nt, h * HEAD_DIM, HEAD_DIM)), jnp.exp2(s - m).astype(BF16))

    last = jnp.maximum(i - 1, 0)

    def qk(t, slot):
        for h in range(4):
            p = h // 2
            kbj = k_ref[pl.ds(pl.multiple_of(t * tq, tq), tq), p * LANE:(p + 1) * LANE]
            s_ref[slot, h] = _dot(kbj, qm_ref[h])

    def update(t, valid, slot):
        for h in range(4):
            srow = sel_ref[h, pl.ds(t, 1), :] * valid
            vb = _v_aug(_lane_tiles(vT_ref, t * nt, nt, h * HEAD_DIM, HEAD_DIM))
            rows = slice(h * HA, (h + 1) * HA)
            m, acc = _online_cols(s_ref[slot, h], srow, ml_ref[h:h + 1, :], acc_ref[rows, :], vb)
            ml_ref[h:h + 1, :] = m
            acc_ref[rows, :] = acc

    qk(0, 0)

    def body(jj, c):
        t0 = 2 * jj
        qk(t0 + 1, 1)
        update(t0, 1.0, 0)
        qk(jnp.minimum(t0 + 2, last), 0)
        update(t0 + 1, 1.0, 1)
        return c

    lax.fori_loop(0, i // 2, body, 0)

    @pl.when(i % 2 == 1)
    def _():
        update(last, 1.0, 0)
    outs = [acc_ref[h * HA:h * HA + HEAD_DIM, :] / acc_ref[h * HA + HEAD_DIM:h * HA + HEAD_DIM + 1, :]
            for h in range(4)]
    o_ref[...] = jnp.concatenate(outs, axis=0).T.astype(o_ref.dtype)


def _moba_call(oT, otok, A, *, q_chunk, k_chunk, v_chunk):
    B, nlt, _, _ = oT.shape
    S = nlt * LANE
    tq = MOBA_BLOCK
    nblk = S // MOBA_BLOCK
    kern = functools.partial(_moba_kernel, tq=tq)
    return pl.pallas_call(
        kern,
        grid=(B, S // tq),
        in_specs=_attn_specs(S, tq, q_chunk, k_chunk, v_chunk) + [pl.BlockSpec((nblk, S), lambda b, i: (0, 0))],
        out_specs=pl.BlockSpec((None, tq, BRANCH_W), lambda b, i: (b, i, 0)),
        out_shape=jax.ShapeDtypeStruct((B, S, BRANCH_W), BF16),
        scratch_shapes=[pltpu.VMEM((nblk, CH), F32), pltpu.VMEM((4, nblk, tq), F32),
                        pltpu.VMEM((4, LANE, tq), BF16), pltpu.VMEM((4 * HA, tq), F32), pltpu.VMEM((8, tq), F32),
                        pltpu.VMEM((2, 4, tq, tq), F32)],
        compiler_params=_cparams(("parallel", "arbitrary")),
        name="moba",
    )(oT, otok, oT, A)


def _compress_kernel(x_ref, w1_ref, pe_ref, w1f_ref, w2_ref, gk_ref, G_ref, o_ref, oT_ref, acc_ref):
    l = pl.program_id(1)

    @pl.when(l == 0)
    def _():
        acc_ref[...] = jnp.zeros_like(acc_ref)

    x = x_ref[...]
    w_hi, w_lo = _split2(w1_ref[...])
    acc_ref[...] += _dot(x, w_hi) + _dot(x, w_lo)

    @pl.when(l == pl.num_programs(1) - 1)
    def _():
        nc = acc_ref.shape[0]
        r = acc_ref[...]
        bias = jnp.dot(pe_ref[...], w1f_ref[...], preferred_element_type=F32,
                       precision=lax.Precision.HIGHEST)
        p1 = jnp.concatenate([r[:, 0:64], r[:, 128:192]], axis=1)
        p2 = jnp.concatenate([r[:, 64:128], r[:, 192:256]], axis=1)
        hid = p1 + pltpu.roll(p2, nc - 1, 0) + bias[0:1, :]
        hid = hid * jax.nn.sigmoid(hid)
        comp = jnp.dot(hid, w2_ref[...], preferred_element_type=F32, precision=lax.Precision.HIGHEST)
        ss = _dot((comp * comp).astype(BF16), G_ref[...])
        lane = lax.broadcasted_iota(jnp.int32, comp.shape, 1)
        inv = jnp.where(lane < HEAD_DIM, lax.rsqrt(ss * (1.0 / HEAD_DIM) + EPS), 1.0)
        comp = comp * inv * gk_ref[...]
        rowi = lax.broadcasted_iota(jnp.int32, comp.shape, 0)
        comp = jnp.where(rowi < nc - 1, comp, 0.0)
        o_ref[...] = comp.astype(o_ref.dtype)
        oT_ref[...] = comp.T.astype(oT_ref.dtype)


def _compress_call(xv, w1blk, pe2, w1f, w2blk, gk, G128):
    B, nc, _ = xv.shape
    return pl.pallas_call(
        _compress_kernel,
        grid=(B, NSA_CMP_STRIDE),
        in_specs=[
            pl.BlockSpec((None, nc, LANE), lambda b, l: (b, 0, l)),
            pl.BlockSpec((None, LANE, CH), lambda b, l: (l, 0, 0)),
            pl.BlockSpec((8, 2 * NSA_CMP_LEN * HEAD_DIM), lambda b, l: (0, 0)),
            pl.BlockSpec((2 * NSA_CMP_LEN * HEAD_DIM, LANE), lambda b, l: (0, 0)),
            pl.BlockSpec((LANE, LANE), lambda b, l: (0, 0)),
            pl.BlockSpec((1, LANE), lambda b, l: (0, 0)),
            pl.BlockSpec((LANE, LANE), lambda b, l: (0, 0)),
        ],
        out_specs=[
            pl.BlockSpec((None, nc, LANE), lambda b, l: (b, 0, 0)),
            pl.BlockSpec((None, LANE, nc), lambda b, l: (b, 0, 0)),
        ],
        out_shape=[
            jax.ShapeDtypeStruct((B, nc, LANE), BF16),
            jax.ShapeDtypeStruct((B, LANE, nc), BF16),
        ],
        scratch_shapes=[pltpu.VMEM((nc, CH), F32)],
        compiler_params=_cparams(("parallel", "arbitrary")),
        name="nsa_compress",
    )(xv, w1blk, pe2, w1f, w2blk, gk, G128)


def _nsa_kernel(qT_ref, k_ref, vT_ref, kc_ref, kcT_ref, ov_ref, o_ref, sel_ref, qm_ref, acc_ref, ml_ref, s_ref,
                win_ref, *, tq):
    i = pl.program_id(1)
    nt = tq // LANE
    nq = 4 * tq
    nc = kc_ref.shape[0]
    nsel = ov_ref.shape[0]
    spb = tq // NSA_SEL_BLOCK

    zeros = jnp.zeros((HEAD_DIM, tq), BF16)
    q_heads = [_lane_tiles(qT_ref, 0, nt, h * HEAD_DIM, HEAD_DIM) for h in range(4)]
    qm_lo = jnp.concatenate([jnp.concatenate([q, zeros], axis=0) for q in q_heads], axis=1)
    qm_hi = jnp.concatenate([jnp.concatenate([zeros, q], axis=0) for q in q_heads], axis=1)
    qpos = i * tq + lax.broadcasted_iota(jnp.int32, (1, tq), 1)
    qpos4 = jnp.concatenate([qpos] * 4, axis=1)

    n_prev = -(-(NSA_WINDOW - 1) // tq)
    nk = n_prev + 1
    kt0 = jnp.maximum(i - n_prev, 0)
    kw = k_ref[pl.ds(pl.multiple_of(kt0 * tq, tq), nk * tq), LANE:2 * LANE]
    sw = _dot(kw, qm_hi)
    dist = qpos - (kt0 * tq + lax.broadcasted_iota(jnp.int32, (nk * tq, tq), 0))
    wbias = jnp.where(dist >= 0, jnp.where(dist <= NSA_WINDOW - 1, 0.0, NEG), NEG)
    sw = sw + jnp.concatenate([wbias] * 4, axis=1)
    mw = jnp.max(sw, axis=0, keepdims=True)
    ew = jnp.exp2(sw - mw)
    dw = jnp.sum(ew, axis=0, keepdims=True)
    vwin = _lane_tiles(vT_ref, kt0 * nt, nk * nt, HEAD_DIM, HEAD_DIM)
    win_ref[...] = _dot(vwin, ew.astype(BF16)) / dw

    zc = _dot(kc_ref[...], qm_lo)
    c_end = lax.broadcasted_iota(jnp.int32, (nc, tq), 0) * NSA_CMP_STRIDE + (NSA_CMP_LEN - 1)
    cbias = jnp.where(c_end <= qpos, 0.0, NEG)
    zc = zc + jnp.concatenate([cbias] * 4, axis=1)
    e = jnp.exp2(zc - jnp.max(zc, axis=0, keepdims=True))
    seen = jnp.where(qpos4 >= NSA_CMP_LEN - 1, 1.0, 0.0)
    pc = e * (seen / jnp.maximum(jnp.sum(e, axis=0, keepdims=True), 1.0))
    o_cmp = _dot(kcT_ref[HEAD_DIM:2 * HEAD_DIM, :], pc.astype(BF16))

    psum = pc[:, 0:tq] + pc[:, tq:2 * tq] + pc[:, 2 * tq:3 * tq] + pc[:, 3 * tq:4 * tq]
    p_hi, p_lo = _split2(psum)
    imp = _dot(ov_ref[...], p_hi) + _dot(ov_ref[...], p_lo)
    nid = lax.broadcasted_iota(jnp.int32, (nsel, tq), 0)
    cur = qpos // NSA_SEL_BLOCK
    imp = jnp.where(nid == 0, BIG, imp)
    imp = jnp.where(nid == cur, BIG, imp)
    imp = jnp.where(nid == cur - 1, BIG, imp)
    imp = jnp.where(nid > cur, NEG, imp)
    for _r in range(min(NSA_SEL_TOPK, nsel)):
        mx = jnp.max(imp, axis=0, keepdims=True)
        idx = jnp.min(jnp.where(imp == mx, nid, nsel), axis=0, keepdims=True)
        imp = jnp.where(nid == idx, -jnp.inf, imp)
    sel = jnp.where(nid <= cur, jnp.where(imp == -jnp.inf, 1.0, 0.0), 0.0)
    sel_ref[...] = sel
    for h in range(4):
        qm_ref[h] = qm_lo[:, h * tq:(h + 1) * tq]

    def qk(t, slot):
        kb = k_ref[pl.ds(pl.multiple_of(t * tq, tq), tq), LANE:2 * LANE]
        for h in range(4):
            s_ref[slot, h] = _dot(kb, qm_ref[h])

    kpos_d = i * tq + lax.broadcasted_iota(jnp.int32, (tq, tq), 0)
    qk(i, 0)
    vb_d = _v_aug(_lane_tiles(vT_ref, i * nt, nt, 0, HEAD_DIM))
    srows_d = [sel_ref[pl.ds(i * spb + u, 1), :] for u in range(spb)]
    for h in range(4):
        sc = s_ref[0, h]
        s = jnp.concatenate(
            [jnp.where(srows_d[u] > 0.0, sc[u * NSA_SEL_BLOCK:(u + 1) * NSA_SEL_BLOCK, :], NEG)
             for u in range(spb)], axis=0)
        s = jnp.where(kpos_d <= qpos, s, NEG)
        m = jnp.max(s, axis=0, keepdims=True)
        lanes = slice(h * tq, (h + 1) * tq)
        ml_ref[0:1, lanes] = m
        acc_ref[:, lanes] = _dot(vb_d, jnp.exp2(s - m).astype(BF16))

    last = jnp.maximum(i - 1, 0)

    def update(t, valid, slot):
        vb = _v_aug(_lane_tiles(vT_ref, t * nt, nt, 0, HEAD_DIM))
        srows = [sel_ref[pl.ds(t * spb + u, 1), :] * valid for u in range(spb)]
        for h in range(4):
            lanes = slice(h * tq, (h + 1) * tq)
            sc = s_ref[slot, h]
            subs = [sc[u * NSA_SEL_BLOCK:(u + 1) * NSA_SEL_BLOCK, :] for u in range(spb)]
            tmax = None
            for u in range(spb):
                mu = jnp.where(srows[u] > 0.0, jnp.max(subs[u], axis=0, keepdims=True), NEG)
                tmax = mu if tmax is None else jnp.maximum(tmax, mu)
            m_old = ml_ref[0:1, lanes]
            m_new = jnp.maximum(m_old, tmax)
            p = jnp.concatenate(
                [jnp.exp2(subs[u] - jnp.where(srows[u] > 0.0, m_new, BIGPOS)).astype(BF16) for u in range(spb)],
                axis=0)
            ml_ref[0:1, lanes] = m_new
            acc_ref[:, lanes] = jnp.exp2(m_old - m_new) * acc_ref[:, lanes] + _dot(vb, p)

    qk(0, 0)

    def body(jj, c):
        t0 = 2 * jj
        qk(t0 + 1, 1)
        update(t0, 1.0, 0)
        qk(jnp.minimum(t0 + 2, last), 0)
        update(t0 + 1, 1.0, 1)
        return c

    lax.fori_loop(0, i // 2, body, 0)

    @pl.when(i % 2 == 1)
    def _():
        update(last, 1.0, 0)
    o_sel = acc_ref[0:HEAD_DIM, :] / acc_ref[HEAD_DIM:HEAD_DIM + 1, :]

    o_win = win_ref[...]

    gates = _lane_tiles(vT_ref, i * nt, nt, 2 * HEAD_DIM, 16).astype(F32)
    outs = []
    for h in range(4):
        sl = slice(h * tq, (h + 1) * tq)
        outs.append(gates[3 * h:3 * h + 1, :] * o_cmp[:, sl]
                    + gates[3 * h + 1:3 * h + 2, :] * o_sel[:, sl]
                    + gates[3 * h + 2:3 * h + 3, :] * o_win[:, sl])
    o_ref[...] = jnp.concatenate(outs, axis=0).T.astype(o_ref.dtype)


def _nsa_call(oT, otok, kc, kcT, ovT, *, q_chunk, kv_chunk, v_chunk):
    B, nlt, _, _ = oT.shape
    S = nlt * LANE
    tq = 256
    nc = S // NSA_CMP_STRIDE
    nsel = S // NSA_SEL_BLOCK
    kern = functools.partial(_nsa_kernel, tq=tq)
    return pl.pallas_call(
        kern,
        grid=(B, S // tq),
        in_specs=_attn_specs(S, tq, q_chunk, kv_chunk, v_chunk) + [
            pl.BlockSpec((None, nc, LANE), lambda b, i: (b, 0, 0)),
            pl.BlockSpec((None, LANE, nc), lambda b, i: (b, 0, 0)),
            pl.BlockSpec((nsel, nc), lambda b, i: (0, 0)),
        ],
        out_specs=pl.BlockSpec((None, tq, BRANCH_W), lambda b, i: (b, i, 0)),
        out_shape=jax.ShapeDtypeStruct((B, S, BRANCH_W), BF16),
        scratch_shapes=[pltpu.VMEM((nsel, tq), F32), pltpu.VMEM((4, LANE, tq), BF16),
                        pltpu.VMEM((HA, 4 * tq), F32), pltpu.VMEM((8, 4 * tq), F32),
                        pltpu.VMEM((2, 4, tq, tq), F32), pltpu.VMEM((HEAD_DIM, 4 * tq), F32)],
        compiler_params=_cparams(("parallel", "arbitrary")),
        name="nsa",
    )(oT, otok, oT, kc, kcT, ovT)


def _epi_kernel(x_ref, ng_ref, ya_ref, yc_ref, yd_ref, ob0_ref, ob1_ref, ob2_ref, l0_ref, l1_ref, l2_ref,
                wz_ref, wmg_ref, wbr_ref, wout_ref, o_ref, nat_ref, *, tm):
    x = x_ref[...]
    ms = jnp.mean(x * x, axis=-1, keepdims=True)
    xn = (x * lax.rsqrt(ms + EPS) * ng_ref[...]).astype(BF16)

    def natural(ref, slot, dil):
        nh = BRANCH_W // LANE
        for r in range(dil):
            for hf in range(nh):
                c0 = r * BRANCH_W + hf * LANE
                nat_ref[slot * nh + hf, pl.ds(r, tm // dil, stride=dil), :] = ref[:, c0:c0 + LANE].astype(F32)
        return jnp.concatenate([nat_ref[slot * nh + hf] for hf in range(nh)], axis=1)

    d1, d2 = DIL_PAIRS[1][1], DIL_PAIRS[2][1]
    l0, l1, l2 = l0_ref[...], natural(l1_ref, 0, d1), natural(l2_ref, 1, d2)
    mx = jnp.maximum(jnp.maximum(l0, l1), l2)
    e0, e1, e2 = jnp.exp2(l0 - mx), jnp.exp2(l1 - mx), jnp.exp2(l2 - mx)
    yb = (e0 * ob0_ref[...].astype(F32) + e1 * natural(ob1_ref, 2, d1)
          + e2 * natural(ob2_ref, 3, d2)) / (e0 + e1 + e2)

    ys = (ya_ref[...].astype(F32), yb, yc_ref[...].astype(F32), yd_ref[...].astype(F32))
    merged = None
    for i in range(4):
        z = _dot(xn, wz_ref[:, i * BRANCH_W:(i + 1) * BRANCH_W])
        gated = (ys[i] * (z * jax.nn.sigmoid(z))).astype(BF16)
        br = _dot(gated, wbr_ref[i])
        mg = _dot(xn, wmg_ref[:, i * D_MODEL:(i + 1) * D_MODEL])
        term = jax.nn.sigmoid(mg) * br
        merged = term if merged is None else merged + term
    o_ref[...] = x + _dot(merged.astype(BF16), wout_ref[...])


def _epi_call(x2, ng, ya, yc, yd, obs, lses, wz, wmg, wbr, wout, *, layer, tm=256):
    T = x2.shape[0]
    row = lambda i: (i, 0)
    full2 = lambda i: (0, 0)
    yspec = pl.BlockSpec((tm, BRANCH_W), row)
    d1, d2 = DIL_PAIRS[1][1], DIL_PAIRS[2][1]
    gspecs = [yspec, pl.BlockSpec((tm // d1, d1 * BRANCH_W), row), pl.BlockSpec((tm // d2, d2 * BRANCH_W), row)]
    return pl.pallas_call(
        functools.partial(_epi_kernel, tm=tm),
        grid=(T // tm,),
        in_specs=[pl.BlockSpec((tm, D_MODEL), row), pl.BlockSpec((1, D_MODEL), full2)]
        + [yspec] * 3 + gspecs + gspecs
        + [pl.BlockSpec((None, D_MODEL, 4 * BRANCH_W), lambda i: (layer, 0, 0), pipeline_mode=pl.Buffered(1)),
           pl.BlockSpec((None, D_MODEL, 4 * D_MODEL), lambda i: (layer, 0, 0), pipeline_mode=pl.Buffered(1)),
           pl.BlockSpec((None, 4, BRANCH_W, D_MODEL), lambda i: (layer, 0, 0, 0), pipeline_mode=pl.Buffered(1)),
           pl.BlockSpec((None, D_MODEL, D_MODEL), lambda i: (layer, 0, 0), pipeline_mode=pl.Buffered(1))],
        out_specs=pl.BlockSpec((tm, D_MODEL), row),
        out_shape=jax.ShapeDtypeStruct((T, D_MODEL), F32),
        scratch_shapes=[pltpu.VMEM((4 * (BRANCH_W // LANE), tm, LANE), F32)],
        compiler_params=_cparams(("parallel",)),
        name="epilogue",
    )(x2, ng, ya, yc, yd, *obs, *lses, wz, wmg, wbr, wout)


TOK_AK, TOK_BQ, TOK_BK, TOK_BV, TOK_CK, TOK_DKV = range(6)
N_MAIN = 6
T_AQ, T_AV, T_CQ, T_CV, T_DQ, T_DX = range(6)
SCALE = 1.0 / math.sqrt(HEAD_DIM)
QSCALE2 = SCALE * LOG2E

_PLAIN = (False, False, 1.0, False)
_T_KINDS = (
    ((False, False, QSCALE2, False),) * 4,
    (_PLAIN,) * 4,
    ((True, True, QSCALE2, False),) * 4,
    (_PLAIN,) * 4,
    ((True, True, QSCALE2, False),) * 4,
    (_PLAIN, _PLAIN, (False, False, 1.0, True), _PLAIN),
)


def _tok_specs():
    main = lambda j: (0, 1, j * CH, CH, 0)
    qkv = ((True, True), (True, True), (False, False))
    specs = []
    for j in range(N_MAIN):
        if j in (TOK_BQ, TOK_BK, TOK_CK):
            specs.append((True, True, (main(j),)))
        elif j == TOK_DKV:
            specs.append(("mixed", "mixed", (main(j), (3, NSA_CMP_STRIDE, 0, LANE, LANE))))
        else:
            specs.append((False, False, (main(j),)))
    for gi in (1, 2):
        dil = DIL_PAIRS[gi][1]
        for k, (nrm, rope) in enumerate(qkv):
            specs.append((nrm, rope, ((gi, dil, k * CH, CH, 3 * CH),)))
    return tuple(specs)


def _rope_tables(S, dil):
    L = S // dil
    pos = (jnp.arange(dil, dtype=jnp.int32)[:, None] + dil * jnp.arange(L, dtype=jnp.int32)[None, :]).astype(F32)
    inv = ROPE_THETA ** (-jnp.arange(0, ROT_DIM, 2, dtype=F32) / ROT_DIM)
    ang = pos[:, :, None] * inv[None, None, :]
    cos, sin = jnp.cos(ang), jnp.sin(ang)
    one = jnp.ones((dil, L, HEAD_DIM - ROT_DIM), F32)
    zero8 = jnp.zeros((dil, L, 8), F32)
    zero = jnp.zeros_like(one)
    c_head = jnp.concatenate([cos, cos, one], axis=-1)
    s1_head = jnp.concatenate([zero8, sin, zero], axis=-1)
    s2_head = jnp.concatenate([-sin, zero8, zero], axis=-1)
    rtok = jnp.stack([jnp.tile(t, (1, 1, LANE // HEAD_DIM)) for t in (c_head, s1_head, s2_head)], axis=1)
    rT = jnp.stack([cos.transpose(0, 2, 1), sin.transpose(0, 2, 1)], axis=1)
    return rtok, rT


def _tok_params(rows):
    out = []
    for nf, rf, gains in rows:
        nrow = jnp.concatenate([jnp.full((HEAD_DIM,), float(f), F32) for f in nf])
        rrow = jnp.concatenate([jnp.full((HEAD_DIM,), float(f), F32) for f in rf])
        grow = jnp.concatenate([g.astype(F32) for g in gains])
        out.append(jnp.concatenate([jnp.stack([nrow, rrow, grow]), jnp.zeros((5, CH), F32)], axis=0))
    return jnp.stack(out)


def kernel(x, norm_g, w_in, qk_g, cmp_pe, cmp_w1, cmp_w2, w_branch, w_out):
    B, S, _ = x.shape
    T = B * S
    o = COL_OFF
    ones = jnp.ones((HEAD_DIM,), F32)

    r = np.arange(CH)
    G = jnp.asarray((r[:, None] // HEAD_DIM == r[None, :] // HEAD_DIM).astype(np.float32), BF16)
    G128 = G[:LANE, :LANE]
    t = np.arange(256)
    U = jnp.asarray((t[None, :] > t[:, None]).astype(np.float32), BF16)
    nblk = S // MOBA_BLOCK
    A = jnp.asarray(np.repeat(np.eye(nblk, dtype=np.float32), MOBA_BLOCK, axis=1) / MOBA_BLOCK, BF16)
    nc, nsel = S // NSA_CMP_STRIDE, S // NSA_SEL_BLOCK
    cs = np.arange(nc) * NSA_CMP_STRIDE
    ss = np.arange(nsel) * NSA_SEL_BLOCK
    ov = ((cs[None, :] < ss[:, None] + NSA_SEL_BLOCK) & (cs[None, :] + NSA_CMP_LEN > ss[:, None]))
    ov[:, nc - 1] = False
    ovT = jnp.asarray(ov.astype(np.float32), BF16)
    hid = np.arange(LANE)[:, None]
    ind = jnp.asarray(((hid < 4) & (r[None, :] // HEAD_DIM == hid)).astype(np.float32), BF16)
    rtok4, rT4 = _rope_tables(S, 1)
    rtok, rT = rtok4[0], rT4[0]
    tok_specs = _tok_specs()

    dkv = o[13]
    bq = lambda gi: (o[4] + gi * CH, o[4] + (gi + 1) * CH)
    bk = lambda gi: (o[5] + gi * CH, o[5] + (gi + 1) * CH)
    bv = lambda gi: (o[6] + gi * CH, o[6] + (gi + 1) * CH)
    tok_plan = ((o[1], o[2]), bq(0), bk(0), bv(0), (o[9], o[10]),
                (dkv, dkv + 64), (dkv + 64, dkv + 128), (dkv + 128, dkv + 192), (dkv + 256, dkv + 320),
                bq(1), bk(1), bv(1), bq(2), bk(2), bv(2))
    T_plan = ((o[0], o[1]), (o[2], o[3]), (o[8], o[9]), (o[10], o[11]), (o[12], o[13]),
              (dkv + 192, dkv + 256), (dkv + 320, dkv + 384), (o[14], o[15]))
    z_plan = ((o[3], o[4]), (o[7], o[8]), (o[11], o[12]), (o[15], o[16]))
    wtok_all, wT_all, wz_all, wmg_all = _repack_call(
        w_in, (tok_plan, T_plan, z_plan, ((o[16], o[17]),)),
        (len(tok_specs) * CH, len(_T_KINDS) * CH, 4 * BRANCH_W, 4 * D_MODEL))

    wbr_all = w_branch.astype(BF16)
    wout_all = w_out.astype(BF16)

    for l in range(DEPTH):
        g = qk_g[l]
        plain = ((0,) * 4, (0,) * 4, (ones,) * 4)
        bq_p = ((1,) * 4, (1,) * 4, (g[0] * QSCALE2,) * 4)
        bk_p = ((1,) * 4, (1,) * 4, (g[1],) * 4)
        tokp = _tok_params([
            plain, bq_p, bk_p, plain,
            ((1,) * 4, (1,) * 4, (g[3],) * 4),
            ((0, 0, 1, 1), (1, 0, 1, 1), (ones, ones, g[6], g[7])),
            bq_p, bk_p, plain, bq_p, bk_p, plain])
        gT = jnp.concatenate([jnp.tile(ones, 8), jnp.tile(g[2], 4), jnp.tile(ones, 4), jnp.tile(g[4], 4),
                              jnp.tile(ones, 4)])[:, None]
        ng = norm_g[l][None, :]
        otok, og1, og2, okv, oT = _proj_call(x, ng, wtok_all, wT_all, tokp, gT, rtok, rT, G, layer=l,
                                             tok_specs=tok_specs, T_kinds=_T_KINDS, n_main=N_MAIN)

        ya = _sb_call(oT, otok, U, q_chunk=T_AQ, k_chunk=TOK_AK, v_chunk=T_AV)

        obs, lses = [], []
        for gi, (window, dil) in enumerate(DIL_PAIRS):
            if dil == 1:
                ob, lse = _band_call(otok, ind, dil=1, per_res=N_MAIN, qi=TOK_BQ, ki=TOK_BK, vi=TOK_BV,
                                     max_dist=window)
            else:
                ob, lse = _band_call((og1, og2)[gi - 1], ind, dil=dil, per_res=3, qi=0, ki=1, vi=2,
                                     max_dist=window // dil)
            obs.append(ob.reshape(T // dil, dil * BRANCH_W))
            lses.append(lse.reshape(T // dil, dil * BRANCH_W))

        yc = _moba_call(oT, otok, A, q_chunk=T_CQ, k_chunk=TOK_CK, v_chunk=T_CV)

        w1 = cmp_w1[l].reshape(2, NSA_CMP_LEN, HEAD_DIM, HEAD_DIM)
        z64 = jnp.zeros((NSA_CMP_STRIDE, HEAD_DIM, HEAD_DIM), F32)
        w1blk = jnp.concatenate([
            jnp.concatenate([w1[0, :16], w1[0, 16:], z64, z64], axis=2),
            jnp.concatenate([z64, z64, w1[1, :16], w1[1, 16:]], axis=2)], axis=1)
        pe2 = jnp.concatenate([cmp_pe[l].reshape(1, -1), jnp.zeros((7, 2 * NSA_CMP_LEN * HEAD_DIM), F32)], axis=0)
        zf = jnp.zeros((NSA_CMP_LEN * HEAD_DIM, HEAD_DIM), F32)
        w1f = jnp.concatenate([jnp.concatenate([cmp_w1[l, 0], zf], axis=1),
                               jnp.concatenate([zf, cmp_w1[l, 1]], axis=1)], axis=0)
        z2 = jnp.zeros((HEAD_DIM, HEAD_DIM), F32)
        w2blk = jnp.concatenate([jnp.concatenate([cmp_w2[l, 0], z2], axis=1),
                                 jnp.concatenate([z2, cmp_w2[l, 1]], axis=1)], axis=0)
        gk = jnp.concatenate([g[5], ones])[None, :]
        kc, kcT = _compress_call(okv, w1blk, pe2, w1f, w2blk, gk, G128)
        yd = _nsa_call(oT, otok, kc, kcT, ovT, q_chunk=T_DQ, kv_chunk=TOK_DKV, v_chunk=T_DX)

        x2 = _epi_call(x.reshape(T, D_MODEL), ng, ya.reshape(T, BRANCH_W), yc.reshape(T, BRANCH_W),
                       yd.reshape(T, BRANCH_W), obs, lses, wz_all, wmg_all, wbr_all, wout_all, layer=l)
        x = x2.reshape(B, S, D_MODEL)
    return x
```
